```python
import math
import jax, jax.numpy as jnp
from jax import lax
import numpy as np

D_MODEL = 1024
BATCH = 16
SEQ = 4096
DEPTH = 4

NORM_EPS = 1e-6
N_BRANCHES = 4
BRANCH_WIDTH = 512
D_FF = 4 * D_MODEL
Q_BLOCK = 128

DN_HEADS = 4
DN_HEAD_DIM = 128
DN_CONV = 4
DN_CHUNK = 64

MLA_HEADS = 4
MLA_Q_RANK = 256
MLA_KV_RANK = 128
MLA_NOPE = 128
MLA_ROPE = 64
MLA_V = 128
MLA_QK_DIM = MLA_NOPE + MLA_ROPE
ROPE_THETA = 10000.0

SG_GROUPS = 4
SG_GROUP_DIM = 128
SG_CHUNK = 128

FOX_HEADS = 4
FOX_HEAD_DIM = 128

IN_SPLITS = (
    3 * DN_HEADS * DN_HEAD_DIM,
    DN_HEADS * DN_HEAD_DIM,
    DN_HEADS,
    DN_HEADS,
    MLA_Q_RANK,
    MLA_KV_RANK,
    MLA_ROPE,
    SG_GROUPS * SG_GROUP_DIM,
    SG_GROUPS * SG_GROUP_DIM,
    3 * FOX_HEADS * FOX_HEAD_DIM,
    FOX_HEADS,
    N_BRANCHES * D_MODEL,
)
D_IN = sum(IN_SPLITS)

kernel_name = "hybrid_gdn_mla_sgmlp_fox_trunk"


def rms_norm(x, g, eps=NORM_EPS):
    xf = x.astype(jnp.float32)
    y = xf * lax.rsqrt(jnp.mean(xf * xf, axis=-1, keepdims=True) + eps)
    return (y * g.astype(jnp.float32)).astype(x.dtype)


def l2_norm(x, eps=NORM_EPS):
    return x * lax.rsqrt(jnp.sum(x * x, axis=-1, keepdims=True) + eps)


def causal_depthwise_conv(x, w):
    K = w.shape[0]
    S = x.shape[1]
    xp = jnp.pad(x, ((0, 0), (K - 1, 0), (0, 0)))
    out = xp[:, 0:S, :] * w[0]
    for k in range(1, K):
        out = out + xp[:, k:k + S, :] * w[k]
    return out


def rope_tables(positions):
    inv_freq = ROPE_THETA ** (-jnp.arange(0, MLA_ROPE, 2, dtype=jnp.float32) / MLA_ROPE)
    ang = positions.astype(jnp.float32)[..., None] * inv_freq
    return jnp.cos(ang)[:, :, None, :], jnp.sin(ang)[:, :, None, :]


def apply_rope(x, cos, sin):
    xf = x.astype(jnp.float32)
    x1, x2 = jnp.split(xf, 2, axis=-1)
    return jnp.concatenate([x1 * cos - x2 * sin, x2 * cos + x1 * sin], axis=-1).astype(x.dtype)


def block_causal_attention(q, k, v, log_decay=None):
    B, H, S, dk = q.shape
    nb = S // Q_BLOCK
    scale = dk ** -0.5
    k_pos = jnp.arange(S)

    def to_blocks(a):
        return jnp.moveaxis(a.reshape(B, H, nb, Q_BLOCK, *a.shape[3:]), 2, 0)

    xs = {"idx": jnp.arange(nb), "q": to_blocks(q)}
    if log_decay is not None:
        xs["d"] = to_blocks(log_decay)

    def attend(blk):
        q_pos = blk["idx"] * Q_BLOCK + jnp.arange(Q_BLOCK)
        s = jnp.einsum("bhqd,bhkd->bhqk", blk["q"], k).astype(jnp.float32) * scale
        if log_decay is not None:
            s = s + (blk["d"][..., :, None] - log_decay[..., None, :])
        s = jnp.where(k_pos[None, :] <= q_pos[:, None], s, -jnp.inf)
        p = jax.nn.softmax(s, axis=-1).astype(v.dtype)
        return jnp.einsum("bhqk,bhkd->bhqd", p, v)

    out = lax.map(attend, xs)
    return jnp.moveaxis(out, 0, 2).reshape(B, H, S, v.shape[-1])


def gated_deltanet(qkv, z, a_logit, b_logit, conv_w, a_log, dt_bias, out_g):
    B, S, _ = qkv.shape
    H, dk, C = DN_HEADS, DN_HEAD_DIM, DN_CHUNK
    N = S // C
    qkv = jax.nn.silu(causal_depthwise_conv(qkv, conv_w)).astype(jnp.float32)
    q, k, v = jnp.split(qkv, 3, axis=-1)

    def heads(t):
        return t.reshape(B, S, H, dk).transpose(0, 2, 1, 3)

    q = l2_norm(heads(q)) * dk ** -0.5
    k = l2_norm(heads(k))
    v = heads(v)
    beta = jax.nn.sigmoid(b_logit.astype(jnp.float32)).transpose(0, 2, 1)
    g = (-jnp.exp(a_log.astype(jnp.float32))
         * jax.nn.softplus(a_logit.astype(jnp.float32) + dt_bias.astype(jnp.float32))).transpose(0, 2, 1)

    def chunk(t):
        return t.reshape(B, H, N, C, *t.shape[3:])

    q, k, v, beta, g = chunk(q), chunk(k), chunk(v), chunk(beta), chunk(g)
    gc = jnp.cumsum(g, axis=-1)
    tril = jnp.tril(jnp.ones((C, C), dtype=bool))
    strict = jnp.tril(jnp.ones((C, C), dtype=bool), -1)
    diff = gc[..., :, None] - gc[..., None, :]
    decay = jnp.where(tril, jnp.exp(jnp.where(tril, diff, 0.0)), 0.0)

    kb = k * beta[..., None]
    lower = jnp.where(strict, jnp.einsum("bhnid,bhnjd->bhnij", kb, k) * decay, 0.0)
    rhs = jnp.concatenate([v * beta[..., None], kb * jnp.exp(gc)[..., None]], axis=-1)
    sol = lax.linalg.triangular_solve(lower + jnp.eye(C, dtype=jnp.float32), rhs,
                                      left_side=True, lower=True, unit_diagonal=True)
    u, w = jnp.split(sol, 2, axis=-1)

    a_qk = jnp.einsum("bhnid,bhnjd->bhnij", q, k) * decay
    q_dec = q * jnp.exp(gc)[..., None]
    k_dec = k * jnp.exp(gc[..., -1:] - gc)[..., None]
    g_last = jnp.exp(gc[..., -1])

    def step(state, xs):
        q_i, k_i, u_i, w_i, a_i, gl_i = xs
        v_new = u_i - jnp.einsum("bhcd,bhde->bhce", w_i, state)
        o_i = jnp.einsum("bhcd,bhde->bhce", q_i, state) + jnp.einsum("bhij,bhje->bhie", a_i, v_new)
        state = state * gl_i[..., None, None] + jnp.einsum("bhcd,bhce->bhde", k_i, v_new)
        return state, o_i

    xs = tuple(jnp.moveaxis(t, 2, 0) for t in (q_dec, k_dec, u, w, a_qk, g_last))
    _, o = lax.scan(step, jnp.zeros((B, H, dk, dk), jnp.float32), xs)
    o = jnp.moveaxis(o, 0, 2).reshape(B, H, S, dk)
    o = rms_norm(o, out_g).transpose(0, 2, 1, 3).reshape(B, S, H * dk)
    return (o * jax.nn.silu(z.astype(jnp.float32))).astype(z.dtype)


def latent_attention(c_q, c_kv, k_rope, positions, q_norm_g, kv_norm_g, w_uq, w_ukv, qk_q_g, qk_k_g):
    B, S, _ = c_q.shape
    H = MLA_HEADS
    q = (rms_norm(c_q, q_norm_g) @ w_uq).reshape(B, S, H, MLA_QK_DIM)
    kv = (rms_norm(c_kv, kv_norm_g) @ w_ukv).reshape(B, S, H, MLA_NOPE + MLA_V)
    k_nope, v = jnp.split(kv, [MLA_NOPE], axis=-1)
    k = jnp.concatenate([k_nope, jnp.broadcast_to(k_rope[:, :, None, :], (B, S, H, MLA_ROPE))], axis=-1)
    q = rms_norm(q, qk_q_g)
    k = rms_norm(k, qk_k_g)
    cos, sin = rope_tables(positions)
    q = jnp.concatenate([q[..., :MLA_NOPE], apply_rope(q[..., MLA_NOPE:], cos, sin)], axis=-1)
    k = jnp.concatenate([k[..., :MLA_NOPE], apply_rope(k[..., MLA_NOPE:], cos, sin)], axis=-1)
    o = block_causal_attention(q.transpose(0, 2, 1, 3), k.transpose(0, 2, 1, 3), v.transpose(0, 2, 1, 3))
    return o.transpose(0, 2, 1, 3).reshape(B, S, H * MLA_V)


def spatial_gating(u, v, norm_g, w_s, b_s):
    B, S, _ = u.shape
    G, Cg, T = SG_GROUPS, SG_GROUP_DIM, SG_CHUNK
    N = S // T
    u = jax.nn.gelu(u)
    v = rms_norm(jax.nn.gelu(v).reshape(B, S, G, Cg), norm_g).reshape(B, N, T, G, Cg)
    w_causal = jnp.where(jnp.tril(jnp.ones((T, T), dtype=bool)), w_s, 0.0)
    mixed = jnp.einsum("gts,bnsgc->bntgc", w_causal, v) + b_s.T[None, None, :, :, None]
    return u * mixed.reshape(B, S, G * Cg)


def forgetting_attention(qkv, f_logit, f_bias, q_g, k_g):
    B, S, _ = qkv.shape
    H, dh = FOX_HEADS, FOX_HEAD_DIM
    q, k, v = jnp.split(qkv, 3, axis=-1)

    def heads(t):
        return t.reshape(B, S, H, dh).transpose(0, 2, 1, 3)

    q = rms_norm(heads(q), q_g)
    k = rms_norm(heads(k), k_g)
    log_f = jax.nn.log_sigmoid(f_logit.astype(jnp.float32) + f_bias.astype(jnp.float32))
    cum = jnp.cumsum(log_f, axis=1).transpose(0, 2, 1)
    o = block_causal_attention(q, k, heads(v), cum)
    return o.transpose(0, 2, 1, 3).reshape(B, S, H * dh)


def setup_inputs(seed: int = 0) -> dict:
    key = jax.random.key(seed)
    ks = jax.random.split(key, 26)
    L = DEPTH
    f32 = jnp.float32

    def nrm(k, shape, fan_in):
        return jax.random.normal(k, shape, f32) * fan_in ** -0.5

    def gain(k, shape):
        return 1.0 + 0.1 * jax.random.normal(k, shape, f32)

    dt = jnp.exp(jax.random.uniform(ks[4], (L, DN_HEADS), f32, math.log(1e-3), math.log(1e-1)))
    return {
        "x": jax.random.normal(ks[0], (BATCH, SEQ, D_MODEL), f32),
        "positions": jnp.broadcast_to(jnp.arange(SEQ, dtype=jnp.int32)[None, :], (BATCH, SEQ)),
        "norm1_g": gain(ks[1], (L, D_MODEL)),
        "w_in": nrm(ks[2], (L, D_MODEL, D_IN), D_MODEL),
        "dn_conv_w": nrm(ks[3], (L, DN_CONV, 3 * DN_HEADS * DN_HEAD_DIM), DN_CONV),
        "dn_a_log": jnp.log(jax.random.uniform(ks[5], (L, DN_HEADS), f32, 1.0, 16.0)),
        "dn_dt_bias": dt + jnp.log(-jnp.expm1(-dt)),
        "dn_out_norm_g": gain(ks[6], (L, DN_HEAD_DIM)),
        "mla_q_norm_g": gain(ks[7], (L, MLA_Q_RANK)),
        "mla_kv_norm_g": gain(ks[8], (L, MLA_KV_RANK)),
        "mla_w_uq": nrm(ks[9], (L, MLA_Q_RANK, MLA_HEADS * MLA_QK_DIM), MLA_Q_RANK),
        "mla_w_ukv": nrm(ks[10], (L, MLA_KV_RANK, MLA_HEADS * (MLA_NOPE + MLA_V)), MLA_KV_RANK),
        "mla_qk_q_g": gain(ks[11], (L, MLA_QK_DIM)),
        "mla_qk_k_g": gain(ks[12], (L, MLA_QK_DIM)),
        "sg_v_norm_g": gain(ks[13], (L, SG_GROUPS, SG_GROUP_DIM)),
        "sg_w_s": nrm(ks[14], (L, SG_GROUPS, SG_CHUNK, SG_CHUNK), SG_CHUNK),
        "sg_b_s": gain(ks[15], (L, SG_GROUPS, SG_CHUNK)),
        "fox_q_norm_g": gain(ks[16], (L, FOX_HEAD_DIM)),
        "fox_k_norm_g": gain(ks[17], (L, FOX_HEAD_DIM)),
        "fox_f_bias": 4.0 + jax.random.normal(ks[18], (L, FOX_HEADS), f32),
        "w_branch": nrm(ks[19], (L, N_BRANCHES, BRANCH_WIDTH, D_MODEL), BRANCH_WIDTH),
        "w_out": nrm(ks[20], (L, D_MODEL, D_MODEL), D_MODEL),
        "norm2_g": gain(ks[21], (L, D_MODEL)),
        "w_ff1": nrm(ks[22], (L, D_MODEL, D_FF), D_MODEL),
        "w_ff2": nrm(ks[23], (L, D_FF, D_MODEL), D_FF),
    }


def reference(x, positions, norm1_g, w_in, dn_conv_w, dn_a_log, dn_dt_bias, dn_out_norm_g,
              mla_q_norm_g, mla_kv_norm_g, mla_w_uq, mla_w_ukv, mla_qk_q_g, mla_qk_k_g,
              sg_v_norm_g, sg_w_s, sg_b_s, fox_q_norm_g, fox_k_norm_g, fox_f_bias,
              w_branch, w_out, norm2_g, w_ff1, w_ff2):
    B, S, _ = x.shape
    split_points = np.cumsum(IN_SPLITS)[:-1]
    for l in range(DEPTH):
        h = rms_norm(x, norm1_g[l])
        proj = h @ w_in[l]
        (dn_qkv, dn_z, dn_a, dn_b, mla_cq, mla_ckv, mla_kr,
         sg_u, sg_v, fox_qkv, fox_f, gate_logits) = jnp.split(proj, split_points, axis=-1)

        o_a = gated_deltanet(dn_qkv, dn_z, dn_a, dn_b, dn_conv_w[l], dn_a_log[l],
                             dn_dt_bias[l], dn_out_norm_g[l])
        o_b = latent_attention(mla_cq, mla_ckv, mla_kr, positions, mla_q_norm_g[l], mla_kv_norm_g[l],
                               mla_w_uq[l], mla_w_ukv[l], mla_qk_q_g[l], mla_qk_k_g[l])
        o_c = spatial_gating(sg_u, sg_v, sg_v_norm_g[l], sg_w_s[l], sg_b_s[l])
        o_d = forgetting_attention(fox_qkv, fox_f, fox_f_bias[l], fox_q_norm_g[l], fox_k_norm_g[l])

        branches = jnp.stack([o_a.astype(x.dtype), o_b.astype(x.dtype),
                              o_c.astype(x.dtype), o_d.astype(x.dtype)], axis=2)
        projected = jnp.einsum("bsiw,iwd->bsid", branches, w_branch[l])
        gates = jax.nn.sigmoid(gate_logits.reshape(B, S, N_BRANCHES, D_MODEL))
        merged = jnp.sum(gates * projected, axis=2)
        x = x + merged @ w_out[l]

        h2 = rms_norm(x, norm2_g[l])
        x = x + jnp.square(jax.nn.relu(h2 @ w_ff1[l])) @ w_ff2[l]
    return x
```

```python
import functools
import math

import jax
import jax.numpy as jnp
from jax import lax
from jax.experimental import pallas as pl
from jax.experimental.pallas import tpu as pltpu

F32 = jnp.float32
BF16 = jnp.bfloat16

D_MODEL = 1024
NORM_EPS = 1e-6
N_BRANCHES = 4
BRANCH_WIDTH = 512
D_FF = 4 * D_MODEL
HEADS = 4
HEAD_DIM = 128
DN_CONV = 4
DN_CHUNK = 128
MLA_Q_RANK = 256
MLA_KV_RANK = 128
MLA_NOPE = 128
MLA_ROPE = 64
MLA_QK_DIM = MLA_NOPE + MLA_ROPE
ROPE_THETA = 10000.0
SG_CHUNK = 128
QK_PAD = 256
LANES = 128
NEG_BIG = -1e30

PROJ_GATES = 0
PROJ_DN_Q = 4096
PROJ_SG_U = 6144
PROJ_SG_V = 6656
PROJ_FOX_Q = 7168
PROJ_MLA_CQ = 8704
PROJ_MLA_CKV = 8960
PROJ_MLA_KR = 9088
PROJ_WIDTH = 9216

_SPLITS = (1536, 512, 4, 4, 256, 128, 64, 512, 512, 1536, 4, 4096)
_OFF = [0]
for _s in _SPLITS:
    _OFF.append(_OFF[-1] + _s)
(O_DN_QKV, O_DN_Z, O_DN_A, O_DN_B, O_MLA_CQ, O_MLA_CKV, O_MLA_KR, O_SG_U, O_SG_V, O_FOX_QKV, O_FOX_F,
 O_GATES, O_END) = _OFF

VMEM_LIMIT = 56 * 1024 * 1024


def _cparams(sem):
    return pltpu.CompilerParams(dimension_semantics=sem, vmem_limit_bytes=VMEM_LIMIT)


def _dot(a, b):
    return jnp.dot(a, b, preferred_element_type=F32)


def _dot_nt(a, b):
    return lax.dot_general(a, b, (((1,), (1,)), ((), ())), preferred_element_type=F32)


def _dot_f32(a, b):
    return jnp.dot(a, b, preferred_element_type=F32, precision=lax.Precision.HIGHEST)


def _sigmoid(x):
    return 1.0 / (1.0 + jnp.exp(-x))


def _softplus(x):
    return jnp.maximum(x, 0.0) + jnp.log1p(jnp.exp(-jnp.abs(x)))


def _gelu_tanh(x):
    return 0.5 * x * (1.0 + jnp.tanh(math.sqrt(2.0 / math.pi) * (x + 0.044715 * (x * x * x))))


def _tri_masks(n):
    r = lax.broadcasted_iota(jnp.int32, (n, n), 0)
    c = lax.broadcasted_iota(jnp.int32, (n, n), 1)
    return r >= c, r > c


def _rope_kernel(pos_ref, freq_ref, sign_ref, cos_ref, sin_ref):
    ang = pos_ref[...].astype(F32) * freq_ref[...]
    lane = lax.broadcasted_iota(jnp.int32, ang.shape, 1)
    live = lane < MLA_ROPE
    cos_ref[...] = jnp.where(live, jnp.cos(ang), 0.0)
    sin_ref[...] = jnp.where(live, jnp.sin(ang) * sign_ref[...], 0.0)


def _rope_tables(positions, tm):
    T = positions.size
    half = MLA_ROPE // 2
    inv_freq = ROPE_THETA ** (-jnp.arange(0, MLA_ROPE, 2, dtype=F32) / MLA_ROPE)
    zeros = jnp.zeros((LANES - MLA_ROPE,), F32)
    freq = jnp.concatenate([inv_freq, inv_freq, zeros]).reshape(1, LANES)
    sign = jnp.concatenate([-jnp.ones((half,), F32), jnp.ones((half,), F32), zeros]).reshape(1, LANES)
    row = pl.BlockSpec((1, LANES), lambda i: (0, 0))
    tab = pl.BlockSpec((tm, LANES), lambda i: (i, 0))
    return pl.pallas_call(
        _rope_kernel,
        out_shape=(jax.ShapeDtypeStruct((T, LANES), F32), jax.ShapeDtypeStruct((T, LANES), F32)),
        grid=(T // tm,),
        in_specs=[pl.BlockSpec((tm, 1), lambda i: (i, 0)), row, row],
        out_specs=(tab, tab),
        compiler_params=_cparams(("parallel",)),
        name="rope_tables",
    )(positions.reshape(T, 1), freq, sign)


def _in_proj_kernel(x_ref, g_ref, w_ref, ws_ref, proj_ref, small_ref, h_scr):
    @pl.when(pl.program_id(1) == 0)
    def _():
        x = x_ref[...]
        ms = jnp.mean(x * x, axis=-1, keepdims=True)
        h = (x * lax.rsqrt(ms + NORM_EPS) * g_ref[...]).astype(BF16)
        h_scr[...] = h
        small_ref[...] = _dot(h, ws_ref[...])

    proj_ref[...] = _dot(h_scr[...], w_ref[...]).astype(BF16)


def _in_proj(x2, g, w_p, w_small, layer, tm, tn):
    T = x2.shape[0]
    return pl.pallas_call(
        _in_proj_kernel,
        out_shape=(jax.ShapeDtypeStruct((T, PROJ_WIDTH), BF16), jax.ShapeDtypeStruct((T, LANES), F32)),
        grid=(T // tm, PROJ_WIDTH // tn),
        in_specs=[
            pl.BlockSpec((tm, D_MODEL), lambda i, j: (i, 0)),
            pl.BlockSpec((None, 1, D_MODEL), lambda i, j: (layer, 0, 0)),
            pl.BlockSpec((None, D_MODEL, tn), lambda i, j: (layer, 0, j)),
            pl.BlockSpec((None, D_MODEL, LANES), lambda i, j: (layer, 0, 0)),
        ],
        out_specs=(pl.BlockSpec((tm, tn), lambda i, j: (i, j)),
                   pl.BlockSpec((tm, LANES), lambda i, j: (i, 0))),
        scratch_shapes=[pltpu.VMEM((tm, D_MODEL), BF16)],
        compiler_params=_cparams(("parallel", "arbitrary")),
        name="in_proj",
    )(x2, g, w_p, w_small)


INV_BASE = 8


def _inverse_masks(n):
    r = lax.broadcasted_iota(jnp.int32, (n, n), 0)
    c = lax.broadcasted_iota(jnp.int32, (n, n), 1)
    same = lambda s: (r // s) == (c // s)
    levels = []
    s = INV_BASE
    while s < n:
        levels.append(same(2 * s) & jnp.logical_not(same(s)))
        s *= 2
    return (r == c).astype(F32), same(INV_BASE), levels


def _unit_lower_inverse(nmat, masks):
    eye, base, levels = masks
    nb = jnp.where(base, nmat, 0.0)
    x = eye - nb
    p = nb.astype(BF16)
    for _ in range(int(math.log2(INV_BASE)) - 1):
        p = _dot(p, p).astype(BF16)
        x = x + _dot(x.astype(BF16), p)
    for off in levels:
        x16 = x.astype(BF16)
        low = jnp.where(off, nmat, 0.0).astype(BF16)
        x = x - _dot(x16, _dot(low, x16).astype(BF16))
    return x


def _deltanet_kernel(q_ref, k_ref, v_ref, z_ref, small_ref, cw_ref, alog_ref, dt_ref, og_ref, o_ref,
                     qbuf, kbuf, vbuf, qs, ks, vs, gsc, bsc, state, *, rows):
    C = DN_CHUNK
    W = HEADS * HEAD_DIM

    @pl.when(pl.program_id(1) == 0)
    def _():
        state[...] = jnp.zeros_like(state)
        for buf in (qbuf, kbuf, vbuf):
            buf[0:8, :] = jnp.zeros((8, W), F32)

    def conv_silu(buf, x_ref, w, dst):
        buf[8:8 + rows, :] = x_ref[...].astype(F32)
        acc = buf[8:8 + rows, :] * w[DN_CONV - 1:DN_CONV, :]
        for s in range(1, DN_CONV):
            acc = acc + buf[8 - s:8 - s + rows, :] * w[DN_CONV - 1 - s:DN_CONV - s, :]
        buf[0:8, :] = buf[rows:rows + 8, :]
        dst[...] = acc * _sigmoid(acc)

    cw = cw_ref[...]
    conv_silu(qbuf, q_ref, cw[:, 0:W], qs)
    conv_silu(kbuf, k_ref, cw[:, W:2 * W], ks)
    conv_silu(vbuf, v_ref, cw[:, 2 * W:3 * W], vs)

    small = small_ref[...]
    gsc[...] = -jnp.exp(alog_ref[...]) * _softplus(small + dt_ref[...])
    bsc[...] = _sigmoid(small)

    tril, strict = _tri_masks(C)
    tril_f = tril.astype(F32)
    inv_masks = _inverse_masks(C)
    og = og_ref[...]

    def chunk(c, carry):
        r0 = pl.multiple_of(c * C, C)
        rs = pl.ds(r0, C)
        gc = _dot_f32(tril_f, gsc[rs, :])
        gct = gc.T
        beta = bsc[rs, :]
        for h in range(HEADS):
            hs = slice(h * HEAD_DIM, (h + 1) * HEAD_DIM)
            q = qs[rs, hs]
            k = ks[rs, hs]
            v = vs[rs, hs]
            q = q * lax.rsqrt(jnp.sum(q * q, axis=-1, keepdims=True) + NORM_EPS) * (HEAD_DIM ** -0.5)
            k = k * lax.rsqrt(jnp.sum(k * k, axis=-1, keepdims=True) + NORM_EPS)
            gcol = gc[:, h:h + 1]
            grow = gct[h:h + 1, :]
            bcol = beta[:, HEADS + h:HEADS + h + 1]
            decay = jnp.where(tril, jnp.exp(jnp.where(tril, gcol - grow, 0.0)), 0.0)
            kb = k * bcol
            k16 = k.astype(BF16)
            qk_kk = _dot_nt(jnp.concatenate([kb, q], axis=0).astype(BF16), k16)
            nmat = jnp.where(strict, qk_kk[:C] * decay, 0.0)
            a_qk = qk_kk[C:] * decay
            eg = jnp.exp(gcol)
            rhs = jnp.concatenate([v * bcol, kb * eg], axis=-1)
            sol = _dot(_unit_lower_inverse(nmat, inv_masks).astype(BF16), rhs.astype(BF16))
            u = sol[:, :HEAD_DIM]
            w = sol[:, HEAD_DIM:]
            glast = gc[C - 1:C, h:h + 1]
            k_dec = k * jnp.exp(glast - gcol)
            st = state[h]
            wq = jnp.concatenate([w, q * eg], axis=0).astype(BF16)
            r = _dot(wq, st.astype(BF16))
            v_new = u - r[:C]
            v16 = v_new.astype(BF16)
            o = r[C:] + _dot(a_qk.astype(BF16), v16)
            state[h] = st * jnp.exp(glast) + _dot(k_dec.T.astype(BF16), v16)
            o = o * lax.rsqrt(jnp.mean(o * o, axis=-1, keepdims=True) + NORM_EPS) * og
            z = z_ref[rs, hs].astype(F32)
            o_ref[rs, hs] = (o * (z * _sigmoid(z))).astype(BF16)
        return carry

    lax.fori_loop(0, rows // C, chunk, 0)


def _deltanet(proj, small, conv_w, a_log_row, dt_row, out_g, layer, B, S, rows):
    T = B * S
    W = HEADS * HEAD_DIM
    nb = S // rows
    blk = lambda col: pl.BlockSpec((rows, W), lambda b, i, col=col: (b * nb + i, col))
    prow = pl.BlockSpec((None, 1, LANES), lambda b, i: (layer, 0, 0))
    return pl.pallas_call(
        functools.partial(_deltanet_kernel, rows=rows),
        out_shape=jax.ShapeDtypeStruct((T, W), BF16),
        grid=(B, nb),
        in_specs=[
            blk(PROJ_DN_Q // W), blk(PROJ_DN_Q // W + 1), blk(PROJ_DN_Q // W + 2), blk(PROJ_DN_Q // W + 3),
            pl.BlockSpec((rows, LANES), lambda b, i: (b * nb + i, 0)),
            pl.BlockSpec((None, DN_CONV, 3 * W), lambda b, i: (layer, 0, 0)),
            prow, prow, prow,
        ],
        out_specs=pl.BlockSpec((rows, W), lambda b, i: (b * nb + i, 0)),
        scratch_shapes=[pltpu.VMEM((rows + 8, W), F32)] * 3 + [pltpu.VMEM((rows, W), F32)] * 3
        + [pltpu.VMEM((rows, LANES), F32)] * 2 + [pltpu.VMEM((HEADS, HEAD_DIM, HEAD_DIM), F32)],
        compiler_params=_cparams(("parallel", "arbitrary")),
        name="deltanet",
    )(proj, proj, proj, proj, small, conv_w, a_log_row, dt_row, out_g)


def _mla_prep_kernel(cq_ref, ckv_ref, kr_ref, cos_ref, sin_ref, qn_ref, kvn_ref, wq_ref, wkv_ref,
                     gq_ref, gk_ref, q_out, k_out, vt_out):
    cq = cq_ref[...].astype(F32)
    ckv = ckv_ref[...].astype(F32)
    cq = cq * lax.rsqrt(jnp.mean(cq * cq, axis=-1, keepdims=True) + NORM_EPS) * qn_ref[...]
    ckv = ckv * lax.rsqrt(jnp.mean(ckv * ckv, axis=-1, keepdims=True) + NORM_EPS) * kvn_ref[...]
    q_all = _dot(cq.astype(BF16), wq_ref[...])
    kv_all = _dot(ckv.astype(BF16), wkv_ref[...])
    cos = cos_ref[...]
    sin = sin_ref[...]
    gq = gq_ref[...]
    gk = gk_ref[...]
    lane = lax.broadcasted_iota(jnp.int32, cos.shape, 1)
    live = lane < MLA_ROPE
    scale = MLA_QK_DIM ** -0.5

    def rope(xr):
        return xr * cos + pltpu.roll(xr, MLA_ROPE, 1) * sin

    kr = kr_ref[...].astype(F32)
    kr_ss = jnp.sum(jnp.where(live, kr * kr, 0.0), axis=-1, keepdims=True)
    for h in range(HEADS):
        base = h * QK_PAD
        qn = q_all[:, base:base + MLA_NOPE]
        qr = q_all[:, base + MLA_NOPE:base + QK_PAD]
        ss = jnp.sum(qn * qn, axis=-1, keepdims=True) + jnp.sum(jnp.where(live, qr * qr, 0.0), axis=-1, keepdims=True)
        rinv = lax.rsqrt(ss * (1.0 / MLA_QK_DIM) + NORM_EPS) * scale
        q_out[:, base:base + MLA_NOPE] = (qn * rinv * gq[:, :MLA_NOPE]).astype(BF16)
        q_out[:, base + MLA_NOPE:base + QK_PAD] = rope(qr * rinv * gq[:, MLA_NOPE:]).astype(BF16)

        kn = kv_all[:, base:base + MLA_NOPE]
        v = kv_all[:, base + MLA_NOPE:base + QK_PAD]
        ssk = jnp.sum(kn * kn, axis=-1, keepdims=True) + kr_ss
        rk = lax.rsqrt(ssk * (1.0 / MLA_QK_DIM) + NORM_EPS)
        k_out[:, base:base + MLA_NOPE] = (kn * rk * gk[:, :MLA_NOPE]).astype(BF16)
        k_out[:, base + MLA_NOPE:base + QK_PAD] = rope(kr * rk * gk[:, MLA_NOPE:]).astype(BF16)
        vt_out[h * HEAD_DIM:(h + 1) * HEAD_DIM, :] = v.T.astype(BF16)


def _mla_prep(proj, cos_t, sin_t, qn_g, kvn_g, w_uq_p, w_ukv, gq_p, gk_p, layer, B, S, tm):
    T = B * S
    nb = S // tm
    HW = HEADS * QK_PAD
    tab = pl.BlockSpec((tm, LANES), lambda i: (i, 0))
    return pl.pallas_call(
        _mla_prep_kernel,
        out_shape=(jax.ShapeDtypeStruct((T, HW), BF16), jax.ShapeDtypeStruct((T, HW), BF16),
                   jax.ShapeDtypeStruct((B, nb, HEADS * HEAD_DIM, tm), BF16)),
        grid=(T // tm,),
        in_specs=[
            pl.BlockSpec((tm, MLA_Q_RANK), lambda i: (i, PROJ_MLA_CQ // MLA_Q_RANK)),
            pl.BlockSpec((tm, MLA_KV_RANK), lambda i: (i, PROJ_MLA_CKV // MLA_KV_RANK)),
            pl.BlockSpec((tm, LANES), lambda i: (i, PROJ_MLA_KR // LANES)),
            tab, tab,
            pl.BlockSpec((None, 1, MLA_Q_RANK), lambda i: (layer, 0, 0)),
            pl.BlockSpec((None, 1, MLA_KV_RANK), lambda i: (layer, 0, 0)),
            pl.BlockSpec((None, MLA_Q_RANK, HW), lambda i: (layer, 0, 0)),
            pl.BlockSpec((None, MLA_KV_RANK, HW), lambda i: (layer, 0, 0)),
            pl.BlockSpec((None, 1, QK_PAD), lambda i: (layer, 0, 0)),
            pl.BlockSpec((None, 1, QK_PAD), lambda i: (layer, 0, 0)),
        ],
        out_specs=(pl.BlockSpec((tm, HW), lambda i: (i, 0)), pl.BlockSpec((tm, HW), lambda i: (i, 0)),
                   pl.BlockSpec((None, None, HEADS * HEAD_DIM, tm), lambda i: (i // nb, i % nb, 0, 0))),
        compiler_params=_cparams(("parallel",)),
        name="mla_prep",
    )(proj, proj, proj, cos_t, sin_t, qn_g, kvn_g, w_uq_p, w_ukv, gq_p, gk_p)


def _split3(c):
    hi = c.astype(BF16).astype(F32)
    r = c - hi
    mid = r.astype(BF16).astype(F32)
    lo = (r - mid).astype(BF16).astype(F32)
    return hi, mid, lo


def _fox_prep_kernel(q_ref, k_ref, v_ref, small_ref, fb_ref, gq_ref, gk_ref, q_out, k_out, vt_out, carry, *, tm):
    @pl.when(pl.program_id(1) == 0)
    def _():
        carry[...] = jnp.zeros_like(carry)

    C = LANES
    tril, _ = _tri_masks(C)
    tril_f = tril.astype(F32)
    logf = -_softplus(-(small_ref[...] + fb_ref[...]))
    gq = gq_ref[...]
    gk = gk_ref[...]
    scale = HEAD_DIM ** -0.5
    lane = lax.broadcasted_iota(jnp.int32, (C, LANES), 1)
    for c in range(tm // C):
        rs = slice(c * C, (c + 1) * C)
        cum = _dot_f32(tril_f, logf[rs, :]) + carry[...]
        carry[...] = cum[C - 1:C, :]
        for h in range(HEADS):
            hs = slice(h * HEAD_DIM, (h + 1) * HEAD_DIM)
            q = q_ref[rs, hs].astype(F32)
            k = k_ref[rs, hs].astype(F32)
            q = q * lax.rsqrt(jnp.mean(q * q, axis=-1, keepdims=True) + NORM_EPS) * gq * scale
            k = k * lax.rsqrt(jnp.mean(k * k, axis=-1, keepdims=True) + NORM_EPS) * gk
            hi, mid, lo = _split3(cum[:, 2 * HEADS + h:2 * HEADS + h + 1])
            one = jnp.where(lane < 6, 1.0, 0.0)
            parts = jnp.where(lane == 0, hi, jnp.where(lane == 1, mid, jnp.where(lane == 2, lo, 0.0)))
            q_aug = jnp.where(lane < 3, parts, jnp.where(lane < 6, one, 0.0))
            nparts = jnp.where(lane == 3, -hi, jnp.where(lane == 4, -mid, jnp.where(lane == 5, -lo, 0.0)))
            k_aug = jnp.where(lane < 3, one, nparts)
            base = h * QK_PAD
            q_out[rs, base:base + HEAD_DIM] = q.astype(BF16)
            q_out[rs, base + HEAD_DIM:base + QK_PAD] = q_aug.astype(BF16)
            k_out[rs, base:base + HEAD_DIM] = k.astype(BF16)
            k_out[rs, base + HEAD_DIM:base + QK_PAD] = k_aug.astype(BF16)
            vt_out[hs, rs] = v_ref[rs, hs].astype(F32).T.astype(BF16)


def _fox_prep(proj, small, f_bias_row, gq, gk, layer, B, S, tm):
    T = B * S
    nb = S // tm
    W = HEADS * HEAD_DIM
    HW = HEADS * QK_PAD
    blk = lambda col: pl.BlockSpec((tm, W), lambda b, i, col=col: (b * nb + i, col))
    prow = pl.BlockSpec((None, 1, LANES), lambda b, i: (layer, 0, 0))
    return pl.pallas_call(
        functools.partial(_fox_prep_kernel, tm=tm),
        out_shape=(jax.ShapeDtypeStruct((T, HW), BF16), jax.ShapeDtypeStruct((T, HW), BF16),
                   jax.ShapeDtypeStruct((B, nb, W, tm), BF16)),
        grid=(B, nb),
        in_specs=[blk(PROJ_FOX_Q // W), blk(PROJ_FOX_Q // W + 1), blk(PROJ_FOX_Q // W + 2),
                  pl.BlockSpec((tm, LANES), lambda b, i: (b * nb + i, 0)), prow, prow, prow],
        out_specs=(pl.BlockSpec((tm, HW), lambda b, i: (b * nb + i, 0)),
                   pl.BlockSpec((tm, HW), lambda b, i: (b * nb + i, 0)),
                   pl.BlockSpec((None, None, W, tm), lambda b, i: (b, i, 0, 0))),
        scratch_shapes=[pltpu.VMEM((1, LANES), F32)],
        compiler_params=_cparams(("parallel", "arbitrary")),
        name="fox_prep",
    )(proj, proj, proj, small, f_bias_row, gq, gk)


def _attn_kernel(q_ref, k_ref, vt_ref, o_ref, m_scr, l_scr, acc_scr, *, blk):
    i = pl.program_id(2)
    q = q_ref[...]
    m_scr[...] = jnp.full_like(m_scr, NEG_BIG)
    l_scr[...] = jnp.zeros_like(l_scr)
    acc_scr[...] = jnp.zeros_like(acc_scr)

    def step(j, masked):
        kb = k_ref[pl.ds(pl.multiple_of(j * blk, blk), blk), :]
        st = _dot_nt(kb, q)
        if masked:
            r = lax.broadcasted_iota(jnp.int32, st.shape, 0)
            c = lax.broadcasted_iota(jnp.int32, st.shape, 1)
            st = jnp.where(r <= c, st, NEG_BIG)
        m_old = m_scr[...]
        m_new = jnp.maximum(m_old, jnp.max(st, axis=0, keepdims=True))
        alpha = jnp.exp(m_old - m_new)
        p = jnp.exp(st - m_new)
        l_scr[...] = alpha * l_scr[...] + jnp.sum(p, axis=0, keepdims=True)
        acc_scr[...] = alpha * acc_scr[...] + _dot(vt_ref[j], p.astype(BF16))
        m_scr[...] = m_new

    def body(j, carry):
        step(j, False)
        return carry

    lax.fori_loop(0, i, body, 0)
    step(i, True)
    o_ref[...] = (acc_scr[...] / l_scr[...]).T.astype(BF16)


def _attention(q, k, vt, B, S, blk):
    T = B * S
    nb = S // blk
    W = HEADS * HEAD_DIM
    return pl.pallas_call(
        functools.partial(_attn_kernel, blk=blk),
        out_shape=jax.ShapeDtypeStruct((T, W), BF16),
        grid=(B, HEADS, nb),
        in_specs=[
            pl.BlockSpec((blk, QK_PAD), lambda b, h, i: (b * nb + i, h)),
            pl.BlockSpec((S, QK_PAD), lambda b, h, i: (b, h)),
            pl.BlockSpec((None, nb, HEAD_DIM, blk), lambda b, h, i: (b, 0, h, 0)),
        ],
        out_specs=pl.BlockSpec((blk, HEAD_DIM), lambda b, h, i: (b * nb + i, h)),
        scratch_shapes=[pltpu.VMEM((1, blk), F32), pltpu.VMEM((1, blk), F32), pltpu.VMEM((HEAD_DIM, blk), F32)],
        compiler_params=_cparams(("parallel", "parallel", "arbitrary")),
        name="causal_attention",
    )(q, k, vt)


def _sg_kernel(u_ref, v_ref, g_ref, ws_ref, b_ref, o_ref, *, rows):
    Tn = SG_CHUNK
    tril, _ = _tri_masks(Tn)
    g = g_ref[...]
    bias = b_ref[...]
    for gi in range(HEADS):
        hs = slice(gi * HEAD_DIM, (gi + 1) * HEAD_DIM)
        w = jnp.where(tril, ws_ref[gi], 0.0).astype(BF16)
        for n in range(rows // Tn):
            rs = slice(n * Tn, (n + 1) * Tn)
            v = _gelu_tanh(v_ref[rs, hs].astype(F32))
            v = v * lax.rsqrt(jnp.mean(v * v, axis=-1, keepdims=True) + NORM_EPS) * g[:, hs]
            mixed = _dot(w, v.astype(BF16)) + bias[:, hs]
            u = _gelu_tanh(u_ref[rs, hs].astype(F32))
            o_ref[rs, hs] = (u * mixed).astype(BF16)


def _spatial_gating(proj, norm_g, w_s, bias_full, layer, T, rows):
    W = HEADS * HEAD_DIM
    return pl.pallas_call(
        functools.partial(_sg_kernel, rows=rows),
        out_shape=jax.ShapeDtypeStruct((T, W), BF16),
        grid=(T // rows,),
        in_specs=[
            pl.BlockSpec((rows, W), lambda i: (i, PROJ_SG_U // W)),
            pl.BlockSpec((rows, W), lambda i: (i, PROJ_SG_V // W)),
            pl.BlockSpec((None, 1, W), lambda i: (layer, 0, 0)),
            pl.BlockSpec((None, HEADS, SG_CHUNK, SG_CHUNK), lambda i: (layer, 0, 0, 0)),
            pl.BlockSpec((None, SG_CHUNK, W), lambda i: (layer, 0, 0)),
        ],
        out_specs=pl.BlockSpec((rows, W), lambda i: (i, 0)),
        compiler_params=_cparams(("parallel",)),
        name="spatial_gating",
    )(proj, proj, norm_g, w_s, bias_full)


def _merge_kernel(oa, ob, oc, od, g0, g1, g2, g3, x_ref, wb_ref, wo_ref, out_ref):
    merged = None
    for i, (o, g) in enumerate(((oa, g0), (ob, g1), (oc, g2), (od, g3))):
        term = _sigmoid(g[...].astype(F32)) * _dot(o[...], wb_ref[i])
        merged = term if merged is None else merged + term
    out_ref[...] = x_ref[...] + _dot(merged.astype(BF16), wo_ref[...])


def _merge(branches, proj, x2, w_branch, w_out, layer, tm):
    T = x2.shape[0]
    ob = pl.BlockSpec((tm, BRANCH_WIDTH), lambda i: (i, 0))
    gate = lambda n: pl.BlockSpec((tm, D_MODEL), lambda i, n=n: (i, n))
    xs = pl.BlockSpec((tm, D_MODEL), lambda i: (i, 0))
    return pl.pallas_call(
        _merge_kernel,
        out_shape=jax.ShapeDtypeStruct((T, D_MODEL), F32),
        grid=(T // tm,),
        in_specs=[ob, ob, ob, ob, gate(0), gate(1), gate(2), gate(3), xs,
                  pl.BlockSpec((None, N_BRANCHES, BRANCH_WIDTH, D_MODEL), lambda i: (layer, 0, 0, 0)),
                  pl.BlockSpec((None, D_MODEL, D_MODEL), lambda i: (layer, 0, 0))],
        out_specs=xs,
        compiler_params=_cparams(("parallel",)),
        name="merge_out_proj",
    )(*branches, proj, proj, proj, proj, x2, w_branch, w_out)


def _ffn_kernel(x_ref, g_ref, w1_ref, w2_ref, out_ref, h_scr):
    f = pl.program_id(1)

    @pl.when(f == 0)
    def _():
        x = x_ref[...]
        ms = jnp.mean(x * x, axis=-1, keepdims=True)
        h_scr[...] = (x * lax.rsqrt(ms + NORM_EPS) * g_ref[...]).astype(BF16)
        out_ref[...] = x

    a = jnp.maximum(_dot(h_scr[...], w1_ref[...]), 0.0)
    out_ref[...] += _dot((a * a).astype(BF16), w2_ref[...])


def _ffn(x2, g, w1, w2, layer, tm, tf):
    T = x2.shape[0]
    xs = pl.BlockSpec((tm, D_MODEL), lambda i, f: (i, 0))
    return pl.pallas_call(
        _ffn_kernel,
        out_shape=jax.ShapeDtypeStruct((T, D_MODEL), F32),
        grid=(T // tm, D_FF // tf),
        in_specs=[xs,
                  pl.BlockSpec((None, 1, D_MODEL), lambda i, f: (layer, 0, 0)),
                  pl.BlockSpec((None, D_MODEL, tf), lambda i, f: (layer, 0, f)),
                  pl.BlockSpec((None, tf, D_MODEL), lambda i, f: (layer, f, 0))],
        out_specs=xs,
        scratch_shapes=[pltpu.VMEM((tm, D_MODEL), BF16)],
        compiler_params=_cparams(("parallel", "arbitrary")),
        name="relu2_mlp",
    )(x2, g, w1, w2)


def _pad_lanes(a, width):
    return jnp.pad(a, [(0, 0)] * (a.ndim - 1) + [(0, width - a.shape[-1])])


def _swap_halves(a):
    h = a.shape[-1] // 2
    return jnp.concatenate([a[..., h:], a[..., :h]], axis=-1)


def _prep_params(norm1_g, w_in, dn_a_log, dn_dt_bias, dn_out_norm_g, mla_q_norm_g, mla_kv_norm_g, mla_w_uq,
                 mla_w_ukv, mla_qk_q_g, mla_qk_k_g, sg_v_norm_g, sg_b_s, fox_q_norm_g, fox_k_norm_g, fox_f_bias,
                 w_branch, w_out, norm2_g, w_ff1, w_ff2):
    L = w_in.shape[0]
    col = lambda a, b: w_in[:, :, a:b]
    kr = col(O_MLA_KR, O_MLA_KR + MLA_ROPE)
    w_p = jnp.concatenate([
        col(O_GATES, O_END), col(O_DN_QKV, O_DN_Z), col(O_DN_Z, O_DN_A), col(O_SG_U, O_SG_V),
        col(O_SG_V, O_FOX_QKV), col(O_FOX_QKV, O_FOX_F), col(O_MLA_CQ, O_MLA_CKV), col(O_MLA_CKV, O_MLA_KR),
        kr, _swap_halves(kr)], axis=-1).astype(BF16)
    assert w_p.shape[-1] == PROJ_WIDTH
    w_small = _pad_lanes(jnp.concatenate([col(O_DN_A, O_DN_B), col(O_DN_B, O_MLA_CQ), col(O_FOX_F, O_GATES)],
                                         axis=-1), LANES).astype(BF16)

    def row(a, width=LANES, offset=0):
        a = a.reshape(L, 1, -1)
        return jnp.pad(a, ((0, 0), (0, 0), (offset, width - offset - a.shape[-1])))

    wq = mla_w_uq.reshape(L, MLA_Q_RANK, HEADS, MLA_QK_DIM)
    wq_r = wq[..., MLA_NOPE:]
    w_uq_p = jnp.concatenate([wq, _swap_halves(wq_r)], axis=-1).reshape(L, MLA_Q_RANK, HEADS * QK_PAD).astype(BF16)

    def qk_gain(g):
        return jnp.concatenate([g, _swap_halves(g[:, MLA_NOPE:])], axis=-1).reshape(L, 1, QK_PAD)

    bias_full = jnp.repeat(jnp.swapaxes(sg_b_s, 1, 2), HEAD_DIM, axis=-1)
    return dict(
        norm1_g=norm1_g.reshape(L, 1, D_MODEL), w_p=w_p, w_small=w_small,
        a_log=row(dn_a_log), dt=row(dn_dt_bias), dn_og=dn_out_norm_g.reshape(L, 1, HEAD_DIM),
        mla_qn=mla_q_norm_g.reshape(L, 1, MLA_Q_RANK), mla_kvn=mla_kv_norm_g.reshape(L, 1, MLA_KV_RANK),
        w_uq_p=w_uq_p, w_ukv=mla_w_ukv.astype(BF16), mla_gq=qk_gain(mla_qk_q_g), mla_gk=qk_gain(mla_qk_k_g),
        sg_g=sg_v_norm_g.reshape(L, 1, HEADS * HEAD_DIM), sg_bias=bias_full,
        fox_fb=row(fox_f_bias, offset=2 * HEADS), fox_gq=fox_q_norm_g.reshape(L, 1, HEAD_DIM),
        fox_gk=fox_k_norm_g.reshape(L, 1, HEAD_DIM),
        w_branch=w_branch.astype(BF16), w_out=w_out.astype(BF16), norm2_g=norm2_g.reshape(L, 1, D_MODEL),
        w_ff1=w_ff1.astype(BF16), w_ff2=w_ff2.astype(BF16),
    )


def _tiles(S):
    pick = lambda want: min(want, S)
    return dict(in_tm=pick(1024), in_tn=1024, dn_rows=pick(512), prep_tm=pick(512), attn_blk=pick(512),
                sg_rows=pick(512), merge_tm=pick(512), ffn_tm=pick(1024), ffn_tf=1024, rope_tm=pick(1024))


def kernel(x, positions, norm1_g, w_in, dn_conv_w, dn_a_log, dn_dt_bias, dn_out_norm_g, mla_q_norm_g, mla_kv_norm_g, mla_w_uq, mla_w_ukv, mla_qk_q_g, mla_qk_k_g, sg_v_norm_g, sg_w_s, sg_b_s, fox_q_norm_g, fox_k_norm_g, fox_f_bias, w_branch, w_out, norm2_g, w_ff1, w_ff2):
    B, S, D = x.shape
    assert D == D_MODEL and S % LANES == 0
    T = B * S
    depth = w_in.shape[0]
    t = _tiles(S)
    p = _prep_params(norm1_g, w_in, dn_a_log, dn_dt_bias, dn_out_norm_g, mla_q_norm_g, mla_kv_norm_g, mla_w_uq,
                     mla_w_ukv, mla_qk_q_g, mla_qk_k_g, sg_v_norm_g, sg_b_s, fox_q_norm_g, fox_k_norm_g,
                     fox_f_bias, w_branch, w_out, norm2_g, w_ff1, w_ff2)
    cos_t, sin_t = _rope_tables(positions, t["rope_tm"])
    x2 = x.reshape(T, D)
    for l in range(depth):
        proj, small = _in_proj(x2, p["norm1_g"], p["w_p"], p["w_small"], l, t["in_tm"], t["in_tn"])
        o_a = _deltanet(proj, small, dn_conv_w, p["a_log"], p["dt"], p["dn_og"], l, B, S, t["dn_rows"])
        qb, kb, vtb = _mla_prep(proj, cos_t, sin_t, p["mla_qn"], p["mla_kvn"], p["w_uq_p"], p["w_ukv"],
                                p["mla_gq"], p["mla_gk"], l, B, S, t["prep_tm"])
        o_b = _attention(qb, kb, vtb, B, S, t["attn_blk"])
        o_c = _spatial_gating(proj, p["sg_g"], sg_w_s, p["sg_bias"], l, T, t["sg_rows"])
        qd, kd, vtd = _fox_prep(proj, small, p["fox_fb"], p["fox_gq"], p["fox_gk"], l, B, S, t["prep_tm"])
        o_d = _attention(qd, kd, vtd, B, S, t["attn_blk"])
        x2 = _merge((o_a, o_b, o_c, o_d), proj, x2, p["w_branch"], p["w_out"], l, t["merge_tm"])
        x2 = _ffn(x2, p["norm2_g"], p["w_ff1"], p["w_ff2"], l, t["ffn_tm"], t["ffn_tf"])
    return x2.reshape(B, S, D)
```

```python
import functools
import math

import jax
import jax.numpy as jnp
from jax import lax
from jax.experimental import pallas as pl
from jax.experimental.pallas import tpu as pltpu

F32 = jnp.float32
BF16 = jnp.bfloat16

D_MODEL = 1024
NORM_EPS = 1e-6
N_BRANCHES = 4
BRANCH_WIDTH = 512
D_FF = 4 * D_MODEL
HEADS = 4
HEAD_DIM = 128
DN_CONV = 4
DN_CHUNK = 128
MLA_Q_RANK = 256
MLA_KV_RANK = 128
MLA_NOPE = 128
MLA_ROPE = 64
MLA_QK_DIM = MLA_NOPE + MLA_ROPE
ROPE_THETA = 10000.0
SG_CHUNK = 128
QK_PAD = 256
LANES = 128
NEG_BIG = -1e30

PROJ_GATES = 0
PROJ_DN_Q = 4096
PROJ_SG_U = 6144
PROJ_SG_V = 6656
PROJ_FOX_Q = 7168
PROJ_MLA_CQ = 8704
PROJ_MLA_CKV = 8960
PROJ_MLA_KR = 9088
PROJ_WIDTH = 9216

_SPLITS = (1536, 512, 4, 4, 256, 128, 64, 512, 512, 1536, 4, 4096)
_OFF = [0]
for _s in _SPLITS:
    _OFF.append(_OFF[-1] + _s)
(O_DN_QKV, O_DN_Z, O_DN_A, O_DN_B, O_MLA_CQ, O_MLA_CKV, O_MLA_KR, O_SG_U, O_SG_V, O_FOX_QKV, O_FOX_F,
 O_GATES, O_END) = _OFF

VMEM_LIMIT = 56 * 1024 * 1024


def _cparams(sem):
    return pltpu.CompilerParams(dimension_semantics=sem, vmem_limit_bytes=VMEM_LIMIT)


def _dot(a, b):
    return jnp.dot(a, b, preferred_element_type=F32)


def _dot_nt(a, b):
    return lax.dot_general(a, b, (((1,), (1,)), ((), ())), preferred_element_type=F32)


def _dot_f32(a, b):
    return jnp.dot(a, b, preferred_element_type=F32, precision=lax.Precision.HIGHEST)


def _sigmoid(x):
    return 1.0 / (1.0 + jnp.exp(-x))


def _softplus(x):
    return jnp.maximum(x, 0.0) + jnp.log1p(jnp.exp(-jnp.abs(x)))


def _gelu_tanh(x):
    return 0.5 * x * (1.0 + jnp.tanh(math.sqrt(2.0 / math.pi) * (x + 0.044715 * (x * x * x))))


def _tri_masks(n):
    r = lax.broadcasted_iota(jnp.int32, (n, n), 0)
    c = lax.broadcasted_iota(jnp.int32, (n, n), 1)
    return r >= c, r > c


def _rope_kernel(pos_ref, freq_ref, sign_ref, cos_ref, sin_ref):
    ang = pos_ref[...].astype(F32) * freq_ref[...]
    lane = lax.broadcasted_iota(jnp.int32, ang.shape, 1)
    live = lane < MLA_ROPE
    cos_ref[...] = jnp.where(live, jnp.cos(ang), 0.0)
    sin_ref[...] = jnp.where(live, jnp.sin(ang) * sign_ref[...], 0.0)


def _rope_tables(positions, tm):
    T = positions.size
    half = MLA_ROPE // 2
    inv_freq = ROPE_THETA ** (-jnp.arange(0, MLA_ROPE, 2, dtype=F32) / MLA_ROPE)
    zeros = jnp.zeros((LANES - MLA_ROPE,), F32)
    freq = jnp.concatenate([inv_freq, inv_freq, zeros]).reshape(1, LANES)
    sign = jnp.concatenate([-jnp.ones((half,), F32), jnp.ones((half,), F32), zeros]).reshape(1, LANES)
    row = pl.BlockSpec((1, LANES), lambda i: (0, 0))
    tab = pl.BlockSpec((tm, LANES), lambda i: (i, 0))
    return pl.pallas_call(
        _rope_kernel,
        out_shape=(jax.ShapeDtypeStruct((T, LANES), F32), jax.ShapeDtypeStruct((T, LANES), F32)),
        grid=(T // tm,),
        in_specs=[pl.BlockSpec((tm, 1), lambda i: (i, 0)), row, row],
        out_specs=(tab, tab),
        compiler_params=_cparams(("parallel",)),
        name="rope_tables",
    )(positions.reshape(T, 1), freq, sign)


def _in_proj_kernel(x_ref, g_ref, w_ref, ws_ref, proj_ref, small_ref, h_scr):
    @pl.when(pl.program_id(1) == 0)
    def _():
        x = x_ref[...]
        ms = jnp.mean(x * x, axis=-1, keepdims=True)
        h = (x * lax.rsqrt(ms + NORM_EPS) * g_ref[...]).astype(BF16)
        h_scr[...] = h
        small_ref[...] = _dot(h, ws_ref[...])

    proj_ref[...] = _dot(h_scr[...], w_ref[...]).astype(BF16)


def _in_proj(x2, g, w_p, w_small, layer, tm, tn):
    T = x2.shape[0]
    return pl.pallas_call(
        _in_proj_kernel,
        out_shape=(jax.ShapeDtypeStruct((T, PROJ_WIDTH), BF16), jax.ShapeDtypeStruct((T, LANES), F32)),
        grid=(T // tm, PROJ_WIDTH // tn),
        in_specs=[
            pl.BlockSpec((tm, D_MODEL), lambda i, j: (i, 0)),
            pl.BlockSpec((None, 1, D_MODEL), lambda i, j: (layer, 0, 0)),
            pl.BlockSpec((None, D_MODEL, tn), lambda i, j: (layer, 0, j)),
            pl.BlockSpec((None, D_MODEL, LANES), lambda i, j: (layer, 0, 0)),
        ],
        out_specs=(pl.BlockSpec((tm, tn), lambda i, j: (i, j)),
                   pl.BlockSpec((tm, LANES), lambda i, j: (i, 0))),
        scratch_shapes=[pltpu.VMEM((tm, D_MODEL), BF16)],
        compiler_params=_cparams(("parallel", "arbitrary")),
        name="in_proj",
    )(x2, g, w_p, w_small)


INV_BASE = 8


def _inverse_masks(n):
    r = lax.broadcasted_iota(jnp.int32, (n, n), 0)
    c = lax.broadcasted_iota(jnp.int32, (n, n), 1)
    same = lambda s: (r // s) == (c // s)
    levels = []
    s = INV_BASE
    while s < n:
        levels.append(same(2 * s) & jnp.logical_not(same(s)))
        s *= 2
    as16 = lambda m: jnp.where(m, 1.0, 0.0).astype(BF16)
    return (r == c).astype(F32), as16(same(INV_BASE)), [as16(m) for m in levels]


def _unit_lower_inverse(nmats, masks):
    eye, base, levels = masks
    n16 = [n.astype(BF16) for n in nmats]
    ps = [n * base for n in n16]
    xs = [eye - p.astype(F32) for p in ps]
    for _ in range(int(math.log2(INV_BASE)) - 1):
        ps = [_dot(p, p).astype(BF16) for p in ps]
        xs = [x + _dot(x.astype(BF16), p) for x, p in zip(xs, ps)]
    for off in levels:
        x16 = [x.astype(BF16) for x in xs]
        ts = [_dot(n * off, xb).astype(BF16) for n, xb in zip(n16, x16)]
        xs = [x - _dot(xb, t) for x, xb, t in zip(xs, x16, ts)]
    return xs


def _deltanet_kernel(q_ref, k_ref, v_ref, z_ref, small_ref, cw_ref, alog_ref, dt_ref, og_ref, o_ref,
                     qbuf, kbuf, vbuf, qs, ks, vs, state, *, rows):
    C = DN_CHUNK
    W = HEADS * HEAD_DIM
    NC = rows // C

    @pl.when(pl.program_id(1) == 0)
    def _():
        state[...] = jnp.zeros_like(state)
        for buf in (qbuf, kbuf, vbuf):
            buf[0:8, :] = jnp.zeros((8, W), F32)

    def conv_silu(buf, x_ref, w, dst):
        buf[8:8 + rows, :] = x_ref[...].astype(F32)
        acc = buf[8:8 + rows, :] * w[DN_CONV - 1:DN_CONV, :]
        for s in range(1, DN_CONV):
            acc = acc + buf[8 - s:8 - s + rows, :] * w[DN_CONV - 1 - s:DN_CONV - s, :]
        buf[0:8, :] = buf[rows:rows + 8, :]
        dst[...] = acc * _sigmoid(acc)

    cw = cw_ref[...]
    conv_silu(qbuf, q_ref, cw[:, 0:W], qs)
    conv_silu(kbuf, k_ref, cw[:, W:2 * W], ks)
    conv_silu(vbuf, v_ref, cw[:, 2 * W:3 * W], vs)

    small = small_ref[...]
    g_all = -jnp.exp(alog_ref[...]) * _softplus(small + dt_ref[...])
    beta_all = _sigmoid(small)

    tril, strict = _tri_masks(C)
    tril_f = tril.astype(F32)
    inv_masks = _inverse_masks(C)
    og = og_ref[...]

    probs = [(c, h) for c in range(NC) for h in range(HEADS)]
    gcs = [_dot_f32(tril_f, g_all[c * C:(c + 1) * C, :]) for c in range(NC)]
    gcts = [gc.T for gc in gcs]
    rsl = lambda c: slice(c * C, (c + 1) * C)
    hsl = lambda h: slice(h * HEAD_DIM, (h + 1) * HEAD_DIM)
    qn, kn = [], []
    for c, h in probs:
        q = qs[rsl(c), hsl(h)]
        k = ks[rsl(c), hsl(h)]
        qn.append(q * (lax.rsqrt(jnp.sum(q * q, axis=-1, keepdims=True) + NORM_EPS) * (HEAD_DIM ** -0.5)))
        kn.append(k * lax.rsqrt(jnp.sum(k * k, axis=-1, keepdims=True) + NORM_EPS))
    gcol = [gcs[c][:, h:h + 1] for c, h in probs]
    bcol = [beta_all[rsl(c), HEADS + h:HEADS + h + 1] for c, h in probs]
    glast = [gcs[c][C - 1:C, h:h + 1] for c, h in probs]
    decay = [jnp.where(tril, jnp.exp(jnp.where(tril, gcol[i] - gcts[c][h:h + 1, :], 0.0)), 0.0)
             for i, (c, h) in enumerate(probs)]
    kb = [k * b for k, b in zip(kn, bcol)]
    k16 = [k.astype(BF16) for k in kn]
    qk_kk = [_dot_nt(jnp.concatenate([kbi, q], axis=0).astype(BF16), k) for kbi, q, k in zip(kb, qn, k16)]
    nmat = [jnp.where(strict, m[:C] * d, 0.0) for m, d in zip(qk_kk, decay)]
    a_qk = [(m[C:] * d).astype(BF16) for m, d in zip(qk_kk, decay)]
    eg = [jnp.exp(g) for g in gcol]
    rhs = [jnp.concatenate([vs[rsl(c), hsl(h)] * bcol[i], kb[i] * eg[i]], axis=-1).astype(BF16)
           for i, (c, h) in enumerate(probs)]
    xinv = _unit_lower_inverse(nmat, inv_masks)
    sol = [_dot(x.astype(BF16), r) for x, r in zip(xinv, rhs)]
    u = [s[:, :HEAD_DIM] for s in sol]
    wq = [jnp.concatenate([s[:, HEAD_DIM:], q * e], axis=0).astype(BF16) for s, q, e in zip(sol, qn, eg)]
    kdt = [(k * jnp.exp(gl - g)).T.astype(BF16) for k, gl, g in zip(kn, glast, gcol)]
    egl = [jnp.exp(gl) for gl in glast]

    st = [state[h] for h in range(HEADS)]
    for c in range(NC):
        idx = [c * HEADS + h for h in range(HEADS)]
        r = [_dot(wq[i], st[h].astype(BF16)) for h, i in enumerate(idx)]
        v16 = [(u[i] - r[h][:C]).astype(BF16) for h, i in enumerate(idx)]
        o = [r[h][C:] + _dot(a_qk[i], v16[h]) for h, i in enumerate(idx)]
        st = [st[h] * egl[i] + _dot(kdt[i], v16[h]) for h, i in enumerate(idx)]
        for h in range(HEADS):
            oh = o[h] * lax.rsqrt(jnp.mean(o[h] * o[h], axis=-1, keepdims=True) + NORM_EPS) * og
            z = z_ref[rsl(c), hsl(h)].astype(F32)
            o_ref[rsl(c), hsl(h)] = (oh * (z * _sigmoid(z))).astype(BF16)
    for h in range(HEADS):
        state[h] = st[h]


def _deltanet(proj, small, conv_w, a_log_row, dt_row, out_g, layer, B, S, rows):
    T = B * S
    W = HEADS * HEAD_DIM
    nb = S // rows
    blk = lambda col: pl.BlockSpec((rows, W), lambda b, i, col=col: (b * nb + i, col))
    prow = pl.BlockSpec((None, 1, LANES), lambda b, i: (layer, 0, 0))
    return pl.pallas_call(
        functools.partial(_deltanet_kernel, rows=rows),
        out_shape=jax.ShapeDtypeStruct((T, W), BF16),
        grid=(B, nb),
        in_specs=[
            blk(PROJ_DN_Q // W), blk(PROJ_DN_Q // W + 1), blk(PROJ_DN_Q // W + 2), blk(PROJ_DN_Q // W + 3),
            pl.BlockSpec((rows, LANES), lambda b, i: (b * nb + i, 0)),
            pl.BlockSpec((None, DN_CONV, 3 * W), lambda b, i: (layer, 0, 0)),
            prow, prow, prow,
        ],
        out_specs=pl.BlockSpec((rows, W), lambda b, i: (b * nb + i, 0)),
        scratch_shapes=[pltpu.VMEM((rows + 8, W), F32)] * 3 + [pltpu.VMEM((rows, W), F32)] * 3
        + [pltpu.VMEM((HEADS, HEAD_DIM, HEAD_DIM), F32)],
        compiler_params=_cparams(("parallel", "arbitrary")),
        name="deltanet",
    )(proj, proj, proj, proj, small, conv_w, a_log_row, dt_row, out_g)


def _mla_prep_kernel(cq_ref, ckv_ref, kr_ref, cos_ref, sin_ref, qn_ref, kvn_ref, wq_ref, wkv_ref,
                     gq_ref, gk_ref, q_out, k_out, vt_out):
    cq = cq_ref[...].astype(F32)
    ckv = ckv_ref[...].astype(F32)
    cq = cq * lax.rsqrt(jnp.mean(cq * cq, axis=-1, keepdims=True) + NORM_EPS) * qn_ref[...]
    ckv = ckv * lax.rsqrt(jnp.mean(ckv * ckv, axis=-1, keepdims=True) + NORM_EPS) * kvn_ref[...]
    q_all = _dot(cq.astype(BF16), wq_ref[...])
    kv_all = _dot(ckv.astype(BF16), wkv_ref[...])
    cos = cos_ref[...]
    sin = sin_ref[...]
    gq = gq_ref[...]
    gk = gk_ref[...]
    lane = lax.broadcasted_iota(jnp.int32, cos.shape, 1)
    live = lane < MLA_ROPE
    scale = MLA_QK_DIM ** -0.5

    def rope(xr):
        return xr * cos + pltpu.roll(xr, MLA_ROPE, 1) * sin

    kr = kr_ref[...].astype(F32)
    kr_ss = jnp.sum(jnp.where(live, kr * kr, 0.0), axis=-1, keepdims=True)
    for h in range(HEADS):
        base = h * QK_PAD
        qn = q_all[:, base:base + MLA_NOPE]
        qr = q_all[:, base + MLA_NOPE:base + QK_PAD]
        ss = jnp.sum(qn * qn, axis=-1, keepdims=True) + jnp.sum(jnp.where(live, qr * qr, 0.0), axis=-1, keepdims=True)
        rinv = lax.rsqrt(ss * (1.0 / MLA_QK_DIM) + NORM_EPS) * scale
        q_out[:, base:base + MLA_NOPE] = (qn * rinv * gq[:, :MLA_NOPE]).astype(BF16)
        q_out[:, base + MLA_NOPE:base + QK_PAD] = rope(qr * rinv * gq[:, MLA_NOPE:]).astype(BF16)

        kn = kv_all[:, base:base + MLA_NOPE]
        v = kv_all[:, base + MLA_NOPE:base + QK_PAD]
        ssk = jnp.sum(kn * kn, axis=-1, keepdims=True) + kr_ss
        rk = lax.rsqrt(ssk * (1.0 / MLA_QK_DIM) + NORM_EPS)
        k_out[:, base:base + MLA_NOPE] = (kn * rk * gk[:, :MLA_NOPE]).astype(BF16)
        k_out[:, base + MLA_NOPE:base + QK_PAD] = rope(kr * rk * gk[:, MLA_NOPE:]).astype(BF16)
        vt_out[h * HEAD_DIM:(h + 1) * HEAD_DIM, :] = v.T.astype(BF16)


def _mla_prep(proj, cos_t, sin_t, qn_g, kvn_g, w_uq_p, w_ukv, gq_p, gk_p, layer, B, S, tm):
    T = B * S
    nb = S // tm
    HW = HEADS * QK_PAD
    tab = pl.BlockSpec((tm, LANES), lambda i: (i, 0))
    return pl.pallas_call(
        _mla_prep_kernel,
        out_shape=(jax.ShapeDtypeStruct((T, HW), BF16), jax.ShapeDtypeStruct((T, HW), BF16),
                   jax.ShapeDtypeStruct((B, nb, HEADS * HEAD_DIM, tm), BF16)),
        grid=(T // tm,),
        in_specs=[
            pl.BlockSpec((tm, MLA_Q_RANK), lambda i: (i, PROJ_MLA_CQ // MLA_Q_RANK)),
            pl.BlockSpec((tm, MLA_KV_RANK), lambda i: (i, PROJ_MLA_CKV // MLA_KV_RANK)),
            pl.BlockSpec((tm, LANES), lambda i: (i, PROJ_MLA_KR // LANES)),
            tab, tab,
            pl.BlockSpec((None, 1, MLA_Q_RANK), lambda i: (layer, 0, 0)),
            pl.BlockSpec((None, 1, MLA_KV_RANK), lambda i: (layer, 0, 0)),
            pl.BlockSpec((None, MLA_Q_RANK, HW), lambda i: (layer, 0, 0)),
            pl.BlockSpec((None, MLA_KV_RANK, HW), lambda i: (layer, 0, 0)),
            pl.BlockSpec((None, 1, QK_PAD), lambda i: (layer, 0, 0)),
            pl.BlockSpec((None, 1, QK_PAD), lambda i: (layer, 0, 0)),
        ],
        out_specs=(pl.BlockSpec((tm, HW), lambda i: (i, 0)), pl.BlockSpec((tm, HW), lambda i: (i, 0)),
                   pl.BlockSpec((None, None, HEADS * HEAD_DIM, tm), lambda i: (i // nb, i % nb, 0, 0))),
        compiler_params=_cparams(("parallel",)),
        name="mla_prep",
    )(proj, proj, proj, cos_t, sin_t, qn_g, kvn_g, w_uq_p, w_ukv, gq_p, gk_p)


def _split3(c):
    hi = c.astype(BF16).astype(F32)
    r = c - hi
    mid = r.astype(BF16).astype(F32)
    lo = (r - mid).astype(BF16).astype(F32)
    return hi, mid, lo


def _fox_prep_kernel(q_ref, k_ref, v_ref, small_ref, fb_ref, gq_ref, gk_ref, q_out, k_out, vt_out, carry, *, tm):
    @pl.when(pl.program_id(1) == 0)
    def _():
        carry[...] = jnp.zeros_like(carry)

    C = LANES
    tril, _ = _tri_masks(C)
    tril_f = tril.astype(F32)
    logf = -_softplus(-(small_ref[...] + fb_ref[...]))
    gq = gq_ref[...]
    gk = gk_ref[...]
    scale = HEAD_DIM ** -0.5
    lane = lax.broadcasted_iota(jnp.int32, (C, LANES), 1)
    for c in range(tm // C):
        rs = slice(c * C, (c + 1) * C)
        cum = _dot_f32(tril_f, logf[rs, :]) + carry[...]
        carry[...] = cum[C - 1:C, :]
        for h in range(HEADS):
            hs = slice(h * HEAD_DIM, (h + 1) * HEAD_DIM)
            q = q_ref[rs, hs].astype(F32)
            k = k_ref[rs, hs].astype(F32)
            q = q * lax.rsqrt(jnp.mean(q * q, axis=-1, keepdims=True) + NORM_EPS) * gq * scale
            k = k * lax.rsqrt(jnp.mean(k * k, axis=-1, keepdims=True) + NORM_EPS) * gk
            hi, mid, lo = _split3(cum[:, 2 * HEADS + h:2 * HEADS + h + 1])
            one = jnp.where(lane < 6, 1.0, 0.0)
            parts = jnp.where(lane == 0, hi, jnp.where(lane == 1, mid, jnp.where(lane == 2, lo, 0.0)))
            q_aug = jnp.where(lane < 3, parts, jnp.where(lane < 6, one, 0.0))
            nparts = jnp.where(lane == 3, -hi, jnp.where(lane == 4, -mid, jnp.where(lane == 5, -lo, 0.0)))
            k_aug = jnp.where(lane < 3, one, nparts)
            base = h * QK_PAD
            q_out[rs, base:base + HEAD_DIM] = q.astype(BF16)
            q_out[rs, base + HEAD_DIM:base + QK_PAD] = q_aug.astype(BF16)
            k_out[rs, base:base + HEAD_DIM] = k.astype(BF16)
            k_out[rs, base + HEAD_DIM:base + QK_PAD] = k_aug.astype(BF16)
            vt_out[hs, rs] = v_ref[rs, hs].astype(F32).T.astype(BF16)


def _fox_prep(proj, small, f_bias_row, gq, gk, layer, B, S, tm):
    T = B * S
    nb = S // tm
    W = HEADS * HEAD_DIM
    HW = HEADS * QK_PAD
    blk = lambda col: pl.BlockSpec((tm, W), lambda b, i, col=col: (b * nb + i, col))
    prow = pl.BlockSpec((None, 1, LANES), lambda b, i: (layer, 0, 0))
    return pl.pallas_call(
        functools.partial(_fox_prep_kernel, tm=tm),
        out_shape=(jax.ShapeDtypeStruct((T, HW), BF16), jax.ShapeDtypeStruct((T, HW), BF16),
                   jax.ShapeDtypeStruct((B, nb, W, tm), BF16)),
        grid=(B, nb),
        in_specs=[blk(PROJ_FOX_Q // W), blk(PROJ_FOX_Q // W + 1), blk(PROJ_FOX_Q // W + 2),
                  pl.BlockSpec((tm, LANES), lambda b, i: (b * nb + i, 0)), prow, prow, prow],
        out_specs=(pl.BlockSpec((tm, HW), lambda b, i: (b * nb + i, 0)),
                   pl.BlockSpec((tm, HW), lambda b, i: (b * nb + i, 0)),
                   pl.BlockSpec((None, None, W, tm), lambda b, i: (b, i, 0, 0))),
        scratch_shapes=[pltpu.VMEM((1, LANES), F32)],
        compiler_params=_cparams(("parallel", "arbitrary")),
        name="fox_prep",
    )(proj, proj, proj, small, f_bias_row, gq, gk)


def _attn_kernel(q_ref, k_ref, vt_ref, o_ref, m_scr, l_scr, acc_scr, *, blk):
    i = pl.program_id(2)
    q = q_ref[...]
    m_scr[...] = jnp.full_like(m_scr, NEG_BIG)
    l_scr[...] = jnp.zeros_like(l_scr)
    acc_scr[...] = jnp.zeros_like(acc_scr)

    def step(j, masked):
        kb = k_ref[pl.ds(pl.multiple_of(j * blk, blk), blk), :]
        st = _dot_nt(kb, q)
        if masked:
            r = lax.broadcasted_iota(jnp.int32, st.shape, 0)
            c = lax.broadcasted_iota(jnp.int32, st.shape, 1)
            st = jnp.where(r <= c, st, NEG_BIG)
        m_old = m_scr[...]
        m_new = jnp.maximum(m_old, jnp.max(st, axis=0, keepdims=True))
        alpha = jnp.exp(m_old - m_new)
        p = jnp.exp(st - m_new)
        l_scr[...] = alpha * l_scr[...] + jnp.sum(p, axis=0, keepdims=True)
        acc_scr[...] = alpha * acc_scr[...] + _dot(vt_ref[j], p.astype(BF16))
        m_scr[...] = m_new

    def body(j, carry):
        step(j, False)
        return carry

    lax.fori_loop(0, i, body, 0)
    step(i, True)
    o_ref[...] = (acc_scr[...] / l_scr[...]).T.astype(BF16)


def _attention(q, k, vt, B, S, blk):
    T = B * S
    nb = S // blk
    W = HEADS * HEAD_DIM
    return pl.pallas_call(
        functools.partial(_attn_kernel, blk=blk),
        out_shape=jax.ShapeDtypeStruct((T, W), BF16),
        grid=(B, HEADS, nb),
        in_specs=[
            pl.BlockSpec((blk, QK_PAD), lambda b, h, i: (b * nb + i, h)),
            pl.BlockSpec((S, QK_PAD), lambda b, h, i: (b, h)),
            pl.BlockSpec((None, nb, HEAD_DIM, blk), lambda b, h, i: (b, 0, h, 0)),
        ],
        out_specs=pl.BlockSpec((blk, HEAD_DIM), lambda b, h, i: (b * nb + i, h)),
        scratch_shapes=[pltpu.VMEM((1, blk), F32), pltpu.VMEM((1, blk), F32), pltpu.VMEM((HEAD_DIM, blk), F32)],
        compiler_params=_cparams(("parallel", "parallel", "arbitrary")),
        name="causal_attention",
    )(q, k, vt)


def _sg_kernel(u_ref, v_ref, g_ref, ws_ref, b_ref, o_ref, *, rows):
    Tn = SG_CHUNK
    tril, _ = _tri_masks(Tn)
    g = g_ref[...]
    bias = b_ref[...]
    for gi in range(HEADS):
        hs = slice(gi * HEAD_DIM, (gi + 1) * HEAD_DIM)
        w = jnp.where(tril, ws_ref[gi], 0.0).astype(BF16)
        for n in range(rows // Tn):
            rs = slice(n * Tn, (n + 1) * Tn)
            v = _gelu_tanh(v_ref[rs, hs].astype(F32))
            v = v * lax.rsqrt(jnp.mean(v * v, axis=-1, keepdims=True) + NORM_EPS) * g[:, hs]
            mixed = _dot(w, v.astype(BF16)) + bias[:, hs]
            u = _gelu_tanh(u_ref[rs, hs].astype(F32))
            o_ref[rs, hs] = (u * mixed).astype(BF16)


def _spatial_gating(proj, norm_g, w_s, bias_full, layer, T, rows):
    W = HEADS * HEAD_DIM
    return pl.pallas_call(
        functools.partial(_sg_kernel, rows=rows),
        out_shape=jax.ShapeDtypeStruct((T, W), BF16),
        grid=(T // rows,),
        in_specs=[
            pl.BlockSpec((rows, W), lambda i: (i, PROJ_SG_U // W)),
            pl.BlockSpec((rows, W), lambda i: (i, PROJ_SG_V // W)),
            pl.BlockSpec((None, 1, W), lambda i: (layer, 0, 0)),
            pl.BlockSpec((None, HEADS, SG_CHUNK, SG_CHUNK), lambda i: (layer, 0, 0, 0)),
            pl.BlockSpec((None, SG_CHUNK, W), lambda i: (layer, 0, 0)),
        ],
        out_specs=pl.BlockSpec((rows, W), lambda i: (i, 0)),
        compiler_params=_cparams(("parallel",)),
        name="spatial_gating",
    )(proj, proj, norm_g, w_s, bias_full)


def _merge_kernel(oa, ob, oc, od, g0, g1, g2, g3, x_ref, wb_ref, wo_ref, out_ref):
    merged = None
    for i, (o, g) in enumerate(((oa, g0), (ob, g1), (oc, g2), (od, g3))):
        term = _sigmoid(g[...].astype(F32)) * _dot(o[...], wb_ref[i])
        merged = term if merged is None else merged + term
    out_ref[...] = x_ref[...] + _dot(merged.astype(BF16), wo_ref[...])


def _merge(branches, proj, x2, w_branch, w_out, layer, tm):
    T = x2.shape[0]
    ob = pl.BlockSpec((tm, BRANCH_WIDTH), lambda i: (i, 0))
    gate = lambda n: pl.BlockSpec((tm, D_MODEL), lambda i, n=n: (i, n))
    xs = pl.BlockSpec((tm, D_MODEL), lambda i: (i, 0))
    return pl.pallas_call(
        _merge_kernel,
        out_shape=jax.ShapeDtypeStruct((T, D_MODEL), F32),
        grid=(T // tm,),
        in_specs=[ob, ob, ob, ob, gate(0), gate(1), gate(2), gate(3), xs,
                  pl.BlockSpec((None, N_BRANCHES, BRANCH_WIDTH, D_MODEL), lambda i: (layer, 0, 0, 0)),
                  pl.BlockSpec((None, D_MODEL, D_MODEL), lambda i: (layer, 0, 0))],
        out_specs=xs,
        compiler_params=_cparams(("parallel",)),
        name="merge_out_proj",
    )(*branches, proj, proj, proj, proj, x2, w_branch, w_out)


def _ffn_kernel(x_ref, g_ref, w1_ref, w2_ref, out_ref, h_scr):
    f = pl.program_id(1)

    @pl.when(f == 0)
    def _():
        x = x_ref[...]
        ms = jnp.mean(x * x, axis=-1, keepdims=True)
        h_scr[...] = (x * lax.rsqrt(ms + NORM_EPS) * g_ref[...]).astype(BF16)
        out_ref[...] = x

    a = jnp.maximum(_dot(h_scr[...], w1_ref[...]), 0.0)
    out_ref[...] += _dot((a * a).astype(BF16), w2_ref[...])


def _ffn(x2, g, w1, w2, layer, tm, tf):
    T = x2.shape[0]
    xs = pl.BlockSpec((tm, D_MODEL), lambda i, f: (i, 0))
    return pl.pallas_call(
        _ffn_kernel,
        out_shape=jax.ShapeDtypeStruct((T, D_MODEL), F32),
        grid=(T // tm, D_FF // tf),
        in_specs=[xs,
                  pl.BlockSpec((None, 1, D_MODEL), lambda i, f: (layer, 0, 0)),
                  pl.BlockSpec((None, D_MODEL, tf), lambda i, f: (layer, 0, f)),
                  pl.BlockSpec((None, tf, D_MODEL), lambda i, f: (layer, f, 0))],
        out_specs=xs,
        scratch_shapes=[pltpu.VMEM((tm, D_MODEL), BF16)],
        compiler_params=_cparams(("parallel", "arbitrary")),
        name="relu2_mlp",
    )(x2, g, w1, w2)


def _pad_lanes(a, width):
    return jnp.pad(a, [(0, 0)] * (a.ndim - 1) + [(0, width - a.shape[-1])])


def _swap_halves(a):
    h = a.shape[-1] // 2
    return jnp.concatenate([a[..., h:], a[..., :h]], axis=-1)


def _prep_params(norm1_g, w_in, dn_a_log, dn_dt_bias, dn_out_norm_g, mla_q_norm_g, mla_kv_norm_g, mla_w_uq,
                 mla_w_ukv, mla_qk_q_g, mla_qk_k_g, sg_v_norm_g, sg_b_s, fox_q_norm_g, fox_k_norm_g, fox_f_bias,
                 w_branch, w_out, norm2_g, w_ff1, w_ff2):
    L = w_in.shape[0]
    col = lambda a, b: w_in[:, :, a:b]
    kr = col(O_MLA_KR, O_MLA_KR + MLA_ROPE)
    w_p = jnp.concatenate([
        col(O_GATES, O_END), col(O_DN_QKV, O_DN_Z), col(O_DN_Z, O_DN_A), col(O_SG_U, O_SG_V),
        col(O_SG_V, O_FOX_QKV), col(O_FOX_QKV, O_FOX_F), col(O_MLA_CQ, O_MLA_CKV), col(O_MLA_CKV, O_MLA_KR),
        kr, _swap_halves(kr)], axis=-1).astype(BF16)
    assert w_p.shape[-1] == PROJ_WIDTH
    w_small = _pad_lanes(jnp.concatenate([col(O_DN_A, O_DN_B), col(O_DN_B, O_MLA_CQ), col(O_FOX_F, O_GATES)],
                                         axis=-1), LANES).astype(BF16)

    def row(a, width=LANES, offset=0):
        a = a.reshape(L, 1, -1)
        return jnp.pad(a, ((0, 0), (0, 0), (offset, width - offset - a.shape[-1])))

    wq = mla_w_uq.reshape(L, MLA_Q_RANK, HEADS, MLA_QK_DIM)
    wq_r = wq[..., MLA_NOPE:]
    w_uq_p = jnp.concatenate([wq, _swap_halves(wq_r)], axis=-1).reshape(L, MLA_Q_RANK, HEADS * QK_PAD).astype(BF16)

    def qk_gain(g):
        return jnp.concatenate([g, _swap_halves(g[:, MLA_NOPE:])], axis=-1).reshape(L, 1, QK_PAD)

    bias_full = jnp.repeat(jnp.swapaxes(sg_b_s, 1, 2), HEAD_DIM, axis=-1)
    return dict(
        norm1_g=norm1_g.reshape(L, 1, D_MODEL), w_p=w_p, w_small=w_small,
        a_log=row(dn_a_log), dt=row(dn_dt_bias), dn_og=dn_out_norm_g.reshape(L, 1, HEAD_DIM),
        mla_qn=mla_q_norm_g.reshape(L, 1, MLA_Q_RANK), mla_kvn=mla_kv_norm_g.reshape(L, 1, MLA_KV_RANK),
        w_uq_p=w_uq_p, w_ukv=mla_w_ukv.astype(BF16), mla_gq=qk_gain(mla_qk_q_g), mla_gk=qk_gain(mla_qk_k_g),
        sg_g=sg_v_norm_g.reshape(L, 1, HEADS * HEAD_DIM), sg_bias=bias_full,
        fox_fb=row(fox_f_bias, offset=2 * HEADS), fox_gq=fox_q_norm_g.reshape(L, 1, HEAD_DIM),
        fox_gk=fox_k_norm_g.reshape(L, 1, HEAD_DIM),
        w_branch=w_branch.astype(BF16), w_out=w_out.astype(BF16), norm2_g=norm2_g.reshape(L, 1, D_MODEL),
        w_ff1=w_ff1.astype(BF16), w_ff2=w_ff2.astype(BF16),
    )


def _tiles(S):
    pick = lambda want: min(want, S)
    return dict(in_tm=pick(1024), in_tn=1024, dn_rows=pick(512), prep_tm=pick(512), attn_blk=pick(512),
                sg_rows=pick(512), merge_tm=pick(512), ffn_tm=pick(1024), ffn_tf=1024, rope_tm=pick(1024))


def kernel(x, positions, norm1_g, w_in, dn_conv_w, dn_a_log, dn_dt_bias, dn_out_norm_g, mla_q_norm_g, mla_kv_norm_g, mla_w_uq, mla_w_ukv, mla_qk_q_g, mla_qk_k_g, sg_v_norm_g, sg_w_s, sg_b_s, fox_q_norm_g, fox_k_norm_g, fox_f_bias, w_branch, w_out, norm2_g, w_ff1, w_ff2):
    B, S, D = x.shape
    assert D == D_MODEL and S % LANES == 0
    T = B * S
    depth = w_in.shape[0]
    t = _tiles(S)
    p = _prep_params(norm1_g, w_in, dn_a_log, dn_dt_bias, dn_out_norm_g, mla_q_norm_g, mla_kv_norm_g, mla_w_uq,
                     mla_w_ukv, mla_qk_q_g, mla_qk_k_g, sg_v_norm_g, sg_b_s, fox_q_norm_g, fox_k_norm_g,
                     fox_f_bias, w_branch, w_out, norm2_g, w_ff1, w_ff2)
    cos_t, sin_t = _rope_tables(positions, t["rope_tm"])
    x2 = x.reshape(T, D)
    for l in range(depth):
        proj, small = _in_proj(x2, p["norm1_g"], p["w_p"], p["w_small"], l, t["in_tm"], t["in_tn"])
        o_a = _deltanet(proj, small, dn_conv_w, p["a_log"], p["dt"], p["dn_og"], l, B, S, t["dn_rows"])
        qb, kb, vtb = _mla_prep(proj, cos_t, sin_t, p["mla_qn"], p["mla_kvn"], p["w_uq_p"], p["w_ukv"],
                                p["mla_gq"], p["mla_gk"], l, B, S, t["prep_tm"])
        o_b = _attention(qb, kb, vtb, B, S, t["attn_blk"])
        o_c = _spatial_gating(proj, p["sg_g"], sg_w_s, p["sg_bias"], l, T, t["sg_rows"])
        qd, kd, vtd = _fox_prep(proj, small, p["fox_fb"], p["fox_gq"], p["fox_gk"], l, B, S, t["prep_tm"])
        o_d = _attention(qd, kd, vtd, B, S, t["attn_blk"])
        x2 = _merge((o_a, o_b, o_c, o_d), proj, x2, p["w_branch"], p["w_out"], l, t["merge_tm"])
        x2 = _ffn(x2, p["norm2_g"], p["w_ff1"], p["w_ff2"], l, t["ffn_tm"], t["ffn_tf"])
    return x2.reshape(B, S, D)
```

```python
import functools
import math

import jax
import jax.numpy as jnp
from jax import lax
from jax.experimental import pallas as pl
from jax.experimental.pallas import tpu as pltpu

F32 = jnp.float32
BF16 = jnp.bfloat16

D_MODEL = 1024
NORM_EPS = 1e-6
N_BRANCHES = 4
BRANCH_WIDTH = 512
D_FF = 4 * D_MODEL
HEADS = 4
HEAD_DIM = 128
DN_CONV = 4
DN_CHUNK = 128
MLA_Q_RANK = 256
MLA_KV_RANK = 128
MLA_NOPE = 128
MLA_ROPE = 64
MLA_QK_DIM = MLA_NOPE + MLA_ROPE
ROPE_THETA = 10000.0
SG_CHUNK = 128
QK_PAD = 256
LANES = 128
NEG_BIG = -1e30
LOG2E = math.log2(math.e)

PROJ_GATES = 0
PROJ_DN_Q = 4096
PROJ_SG_U = 6144
PROJ_SG_V = 6656
PROJ_FOX_Q = 7168
PROJ_MLA_CQ = 8704
PROJ_MLA_CKV = 8960
PROJ_MLA_KR = 9088
PROJ_WIDTH = 9216

_SPLITS = (1536, 512, 4, 4, 256, 128, 64, 512, 512, 1536, 4, 4096)
_OFF = [0]
for _s in _SPLITS:
    _OFF.append(_OFF[-1] + _s)
(O_DN_QKV, O_DN_Z, O_DN_A, O_DN_B, O_MLA_CQ, O_MLA_CKV, O_MLA_KR, O_SG_U, O_SG_V, O_FOX_QKV, O_FOX_F,
 O_GATES, O_END) = _OFF

VMEM_LIMIT = 56 * 1024 * 1024


def _cparams(sem):
    return pltpu.CompilerParams(dimension_semantics=sem, vmem_limit_bytes=VMEM_LIMIT)


def _dot(a, b):
    return jnp.dot(a, b, preferred_element_type=F32)


def _dot_nt(a, b):
    return lax.dot_general(a, b, (((1,), (1,)), ((), ())), preferred_element_type=F32)


def _dot_f32(a, b):
    return jnp.dot(a, b, preferred_element_type=F32, precision=lax.Precision.HIGHEST)


def _sigmoid(x):
    return 1.0 / (1.0 + jnp.exp(-x))


def _softplus(x):
    return jnp.maximum(x, 0.0) + jnp.log1p(jnp.exp(-jnp.abs(x)))


def _gelu_tanh(x):
    return 0.5 * x * (1.0 + jnp.tanh(math.sqrt(2.0 / math.pi) * (x + 0.044715 * (x * x * x))))


def _tri_masks(n):
    r = lax.broadcasted_iota(jnp.int32, (n, n), 0)
    c = lax.broadcasted_iota(jnp.int32, (n, n), 1)
    return r >= c, r > c


def _rope_kernel(pos_ref, freq_ref, sign_ref, cos_ref, sin_ref):
    ang = pos_ref[...].astype(F32) * freq_ref[...]
    lane = lax.broadcasted_iota(jnp.int32, ang.shape, 1)
    live = lane < MLA_ROPE
    cos_ref[...] = jnp.where(live, jnp.cos(ang), 0.0)
    sin_ref[...] = jnp.where(live, jnp.sin(ang) * sign_ref[...], 0.0)


def _rope_tables(positions, tm):
    T = positions.size
    half = MLA_ROPE // 2
    inv_freq = ROPE_THETA ** (-jnp.arange(0, MLA_ROPE, 2, dtype=F32) / MLA_ROPE)
    zeros = jnp.zeros((LANES - MLA_ROPE,), F32)
    freq = jnp.concatenate([inv_freq, inv_freq, zeros]).reshape(1, LANES)
    sign = jnp.concatenate([-jnp.ones((half,), F32), jnp.ones((half,), F32), zeros]).reshape(1, LANES)
    row = pl.BlockSpec((1, LANES), lambda i: (0, 0))
    tab = pl.BlockSpec((tm, LANES), lambda i: (i, 0))
    return pl.pallas_call(
        _rope_kernel,
        out_shape=(jax.ShapeDtypeStruct((T, LANES), F32), jax.ShapeDtypeStruct((T, LANES), F32)),
        grid=(T // tm,),
        in_specs=[pl.BlockSpec((tm, 1), lambda i: (i, 0)), row, row],
        out_specs=(tab, tab),
        compiler_params=_cparams(("parallel",)),
        name="rope_tables",
    )(positions.reshape(T, 1), freq, sign)


def _in_proj_kernel(x_ref, g_ref, w_ref, ws_ref, proj_ref, small_ref, h_scr):
    @pl.when(pl.program_id(1) == 0)
    def _():
        x = x_ref[...]
        ms = jnp.mean(x * x, axis=-1, keepdims=True)
        h = (x * lax.rsqrt(ms + NORM_EPS) * g_ref[...]).astype(BF16)
        h_scr[...] = h
        small_ref[...] = _dot(h, ws_ref[...])

    proj_ref[...] = _dot(h_scr[...], w_ref[...]).astype(BF16)


def _in_proj(x2, g, w_p, w_small, layer, tm, tn):
    T = x2.shape[0]
    return pl.pallas_call(
        _in_proj_kernel,
        out_shape=(jax.ShapeDtypeStruct((T, PROJ_WIDTH), BF16), jax.ShapeDtypeStruct((T, LANES), F32)),
        grid=(T // tm, PROJ_WIDTH // tn),
        in_specs=[
            pl.BlockSpec((tm, D_MODEL), lambda i, j: (i, 0)),
            pl.BlockSpec((None, 1, D_MODEL), lambda i, j: (layer, 0, 0)),
            pl.BlockSpec((None, D_MODEL, tn), lambda i, j: (layer, 0, j)),
            pl.BlockSpec((None, D_MODEL, LANES), lambda i, j: (layer, 0, 0)),
        ],
        out_specs=(pl.BlockSpec((tm, tn), lambda i, j: (i, j)),
                   pl.BlockSpec((tm, LANES), lambda i, j: (i, 0))),
        scratch_shapes=[pltpu.VMEM((tm, D_MODEL), BF16)],
        compiler_params=_cparams(("parallel", "arbitrary")),
        name="in_proj",
    )(x2, g, w_p, w_small)


INV_BASE = 8


def _inverse_masks(n):
    r = lax.broadcasted_iota(jnp.int32, (n, n), 0)
    c = lax.broadcasted_iota(jnp.int32, (n, n), 1)
    same = lambda s: (r // s) == (c // s)
    levels = []
    s = INV_BASE
    while s < n:
        levels.append(same(2 * s) & jnp.logical_not(same(s)))
        s *= 2
    as16 = lambda m: jnp.where(m, 1.0, 0.0).astype(BF16)
    return (r == c).astype(F32), as16(same(INV_BASE)), [as16(m) for m in levels]


def _unit_lower_inverse(nmats, masks):
    eye, base, levels = masks
    n16 = [n.astype(BF16) for n in nmats]
    ps = [n * base for n in n16]
    xs = [eye - p.astype(F32) for p in ps]
    for _ in range(int(math.log2(INV_BASE)) - 1):
        ps = [_dot(p, p).astype(BF16) for p in ps]
        xs = [x + _dot(x.astype(BF16), p) for x, p in zip(xs, ps)]
    for off in levels:
        x16 = [x.astype(BF16) for x in xs]
        ts = [_dot(n * off, xb).astype(BF16) for n, xb in zip(n16, x16)]
        xs = [x - _dot(xb, t) for x, xb, t in zip(xs, x16, ts)]
    return xs


def _deltanet_kernel(q_ref, k_ref, v_ref, z_ref, small_ref, cw_ref, alog_ref, dt_ref, og_ref, o_ref,
                     qbuf, kbuf, vbuf, qs, ks, vs, state, *, rows):
    C = DN_CHUNK
    W = HEADS * HEAD_DIM
    NC = rows // C

    @pl.when(pl.program_id(1) == 0)
    def _():
        state[...] = jnp.zeros_like(state)
        for buf in (qbuf, kbuf, vbuf):
            buf[0:8, :] = jnp.zeros((8, W), F32)

    def conv_silu(buf, x_ref, w, dst):
        buf[8:8 + rows, :] = x_ref[...].astype(F32)
        acc = buf[8:8 + rows, :] * w[DN_CONV - 1:DN_CONV, :]
        for s in range(1, DN_CONV):
            acc = acc + buf[8 - s:8 - s + rows, :] * w[DN_CONV - 1 - s:DN_CONV - s, :]
        buf[0:8, :] = buf[rows:rows + 8, :]
        dst[...] = acc * _sigmoid(acc)

    cw = cw_ref[...]
    conv_silu(qbuf, q_ref, cw[:, 0:W], qs)
    conv_silu(kbuf, k_ref, cw[:, W:2 * W], ks)
    conv_silu(vbuf, v_ref, cw[:, 2 * W:3 * W], vs)

    small = small_ref[...]
    g_all = -jnp.exp(alog_ref[...]) * _softplus(small + dt_ref[...])
    beta_all = _sigmoid(small)

    tril, strict = _tri_masks(C)
    tril_f = tril.astype(F32)
    inv_masks = _inverse_masks(C)
    og = og_ref[...]

    probs = [(c, h) for c in range(NC) for h in range(HEADS)]
    gcs = [_dot_f32(tril_f, g_all[c * C:(c + 1) * C, :]) for c in range(NC)]
    gcts = [gc.T for gc in gcs]
    rsl = lambda c: slice(c * C, (c + 1) * C)
    hsl = lambda h: slice(h * HEAD_DIM, (h + 1) * HEAD_DIM)
    qn, kn = [], []
    for c, h in probs:
        q = qs[rsl(c), hsl(h)]
        k = ks[rsl(c), hsl(h)]
        qn.append(q * (lax.rsqrt(jnp.sum(q * q, axis=-1, keepdims=True) + NORM_EPS) * (HEAD_DIM ** -0.5)))
        kn.append(k * lax.rsqrt(jnp.sum(k * k, axis=-1, keepdims=True) + NORM_EPS))
    gcol = [gcs[c][:, h:h + 1] for c, h in probs]
    bcol = [beta_all[rsl(c), HEADS + h:HEADS + h + 1] for c, h in probs]
    glast = [gcs[c][C - 1:C, h:h + 1] for c, h in probs]
    decay = [jnp.where(tril, jnp.exp(jnp.where(tril, gcol[i] - gcts[c][h:h + 1, :], 0.0)), 0.0)
             for i, (c, h) in enumerate(probs)]
    kb = [k * b for k, b in zip(kn, bcol)]
    k16 = [k.astype(BF16) for k in kn]
    qk_kk = [_dot_nt(jnp.concatenate([kbi, q], axis=0).astype(BF16), k) for kbi, q, k in zip(kb, qn, k16)]
    nmat = [jnp.where(strict, m[:C] * d, 0.0) for m, d in zip(qk_kk, decay)]
    a_qk = [(m[C:] * d).astype(BF16) for m, d in zip(qk_kk, decay)]
    eg = [jnp.exp(g) for g in gcol]
    rhs = [jnp.concatenate([vs[rsl(c), hsl(h)] * bcol[i], kb[i] * eg[i]], axis=-1).astype(BF16)
           for i, (c, h) in enumerate(probs)]
    xinv = _unit_lower_inverse(nmat, inv_masks)
    sol = [_dot(x.astype(BF16), r) for x, r in zip(xinv, rhs)]
    u = [s[:, :HEAD_DIM] for s in sol]
    wq = [jnp.concatenate([s[:, HEAD_DIM:], q * e], axis=0).astype(BF16) for s, q, e in zip(sol, qn, eg)]
    kdt = [(k * jnp.exp(gl - g)).T.astype(BF16) for k, gl, g in zip(kn, glast, gcol)]
    egl = [jnp.exp(gl) for gl in glast]

    st = [state[h] for h in range(HEADS)]
    for c in range(NC):
        idx = [c * HEADS + h for h in range(HEADS)]
        r = [_dot(wq[i], st[h].astype(BF16)) for h, i in enumerate(idx)]
        v16 = [(u[i] - r[h][:C]).astype(BF16) for h, i in enumerate(idx)]
        o = [r[h][C:] + _dot(a_qk[i], v16[h]) for h, i in enumerate(idx)]
        st = [st[h] * egl[i] + _dot(kdt[i], v16[h]) for h, i in enumerate(idx)]
        for h in range(HEADS):
            oh = o[h] * lax.rsqrt(jnp.mean(o[h] * o[h], axis=-1, keepdims=True) + NORM_EPS) * og
            z = z_ref[rsl(c), hsl(h)].astype(F32)
            o_ref[rsl(c), hsl(h)] = (oh * (z * _sigmoid(z))).astype(BF16)
    for h in range(HEADS):
        state[h] = st[h]


def _deltanet(proj, small, conv_w, a_log_row, dt_row, out_g, layer, B, S, rows):
    T = B * S
    W = HEADS * HEAD_DIM
    nb = S // rows
    blk = lambda col: pl.BlockSpec((rows, W), lambda b, i, col=col: (b * nb + i, col))
    prow = pl.BlockSpec((None, 1, LANES), lambda b, i: (layer, 0, 0))
    return pl.pallas_call(
        functools.partial(_deltanet_kernel, rows=rows),
        out_shape=jax.ShapeDtypeStruct((T, W), BF16),
        grid=(B, nb),
        in_specs=[
            blk(PROJ_DN_Q // W), blk(PROJ_DN_Q // W + 1), blk(PROJ_DN_Q // W + 2), blk(PROJ_DN_Q // W + 3),
            pl.BlockSpec((rows, LANES), lambda b, i: (b * nb + i, 0)),
            pl.BlockSpec((None, DN_CONV, 3 * W), lambda b, i: (layer, 0, 0)),
            prow, prow, prow,
        ],
        out_specs=pl.BlockSpec((rows, W), lambda b, i: (b * nb + i, 0)),
        scratch_shapes=[pltpu.VMEM((rows + 8, W), F32)] * 3 + [pltpu.VMEM((rows, W), F32)] * 3
        + [pltpu.VMEM((HEADS, HEAD_DIM, HEAD_DIM), F32)],
        compiler_params=_cparams(("parallel", "arbitrary")),
        name="deltanet",
    )(proj, proj, proj, proj, small, conv_w, a_log_row, dt_row, out_g)


def _mla_prep_kernel(cq_ref, ckv_ref, kr_ref, cos_ref, sin_ref, qn_ref, kvn_ref, wq_ref, wkv_ref,
                     gq_ref, gk_ref, q_out, k_out, vt_out):
    cq = cq_ref[...].astype(F32)
    ckv = ckv_ref[...].astype(F32)
    cq = cq * lax.rsqrt(jnp.mean(cq * cq, axis=-1, keepdims=True) + NORM_EPS) * qn_ref[...]
    ckv = ckv * lax.rsqrt(jnp.mean(ckv * ckv, axis=-1, keepdims=True) + NORM_EPS) * kvn_ref[...]
    q_all = _dot(cq.astype(BF16), wq_ref[...])
    kv_all = _dot(ckv.astype(BF16), wkv_ref[...])
    cos = cos_ref[...]
    sin = sin_ref[...]
    gq = gq_ref[...]
    gk = gk_ref[...]
    lane = lax.broadcasted_iota(jnp.int32, cos.shape, 1)
    live = lane < MLA_ROPE
    scale = MLA_QK_DIM ** -0.5 * LOG2E

    def rope(xr):
        return xr * cos + pltpu.roll(xr, MLA_ROPE, 1) * sin

    kr = kr_ref[...].astype(F32)
    kr_ss = jnp.sum(jnp.where(live, kr * kr, 0.0), axis=-1, keepdims=True)
    for h in range(HEADS):
        base = h * QK_PAD
        qn = q_all[:, base:base + MLA_NOPE]
        qr = q_all[:, base + MLA_NOPE:base + QK_PAD]
        ss = jnp.sum(qn * qn, axis=-1, keepdims=True) + jnp.sum(jnp.where(live, qr * qr, 0.0), axis=-1, keepdims=True)
        rinv = lax.rsqrt(ss * (1.0 / MLA_QK_DIM) + NORM_EPS) * scale
        q_out[:, base:base + MLA_NOPE] = (qn * rinv * gq[:, :MLA_NOPE]).astype(BF16)
        q_out[:, base + MLA_NOPE:base + QK_PAD] = rope(qr * rinv * gq[:, MLA_NOPE:]).astype(BF16)

        kn = kv_all[:, base:base + MLA_NOPE]
        v = kv_all[:, base + MLA_NOPE:base + QK_PAD]
        ssk = jnp.sum(kn * kn, axis=-1, keepdims=True) + kr_ss
        rk = lax.rsqrt(ssk * (1.0 / MLA_QK_DIM) + NORM_EPS)
        k_out[:, base:base + MLA_NOPE] = (kn * rk * gk[:, :MLA_NOPE]).astype(BF16)
        k_out[:, base + MLA_NOPE:base + QK_PAD] = rope(kr * rk * gk[:, MLA_NOPE:]).astype(BF16)
        vt_out[h * HEAD_DIM:(h + 1) * HEAD_DIM, :] = v.T.astype(BF16)


def _mla_prep(proj, cos_t, sin_t, qn_g, kvn_g, w_uq_p, w_ukv, gq_p, gk_p, layer, B, S, tm):
    T = B * S
    nb = S // tm
    HW = HEADS * QK_PAD
    tab = pl.BlockSpec((tm, LANES), lambda i: (i, 0))
    return pl.pallas_call(
        _mla_prep_kernel,
        out_shape=(jax.ShapeDtypeStruct((T, HW), BF16), jax.ShapeDtypeStruct((T, HW), BF16),
                   jax.ShapeDtypeStruct((B, nb, HEADS * HEAD_DIM, tm), BF16)),
        grid=(T // tm,),
        in_specs=[
            pl.BlockSpec((tm, MLA_Q_RANK), lambda i: (i, PROJ_MLA_CQ // MLA_Q_RANK)),
            pl.BlockSpec((tm, MLA_KV_RANK), lambda i: (i, PROJ_MLA_CKV // MLA_KV_RANK)),
            pl.BlockSpec((tm, LANES), lambda i: (i, PROJ_MLA_KR // LANES)),
            tab, tab,
            pl.BlockSpec((None, 1, MLA_Q_RANK), lambda i: (layer, 0, 0)),
            pl.BlockSpec((None, 1, MLA_KV_RANK), lambda i: (layer, 0, 0)),
            pl.BlockSpec((None, MLA_Q_RANK, HW), lambda i: (layer, 0, 0)),
            pl.BlockSpec((None, MLA_KV_RANK, HW), lambda i: (layer, 0, 0)),
            pl.BlockSpec((None, 1, QK_PAD), lambda i: (layer, 0, 0)),
            pl.BlockSpec((None, 1, QK_PAD), lambda i: (layer, 0, 0)),
        ],
        out_specs=(pl.BlockSpec((tm, HW), lambda i: (i, 0)), pl.BlockSpec((tm, HW), lambda i: (i, 0)),
                   pl.BlockSpec((None, None, HEADS * HEAD_DIM, tm), lambda i: (i // nb, i % nb, 0, 0))),
        compiler_params=_cparams(("parallel",)),
        name="mla_prep",
    )(proj, proj, proj, cos_t, sin_t, qn_g, kvn_g, w_uq_p, w_ukv, gq_p, gk_p)


def _split3(c):
    hi = c.astype(BF16).astype(F32)
    r = c - hi
    mid = r.astype(BF16).astype(F32)
    lo = (r - mid).astype(BF16).astype(F32)
    return hi, mid, lo


def _fox_prep_kernel(q_ref, k_ref, v_ref, small_ref, fb_ref, gq_ref, gk_ref, q_out, k_out, vt_out, carry, *, tm):
    @pl.when(pl.program_id(1) == 0)
    def _():
        carry[...] = jnp.zeros_like(carry)

    C = LANES
    tril, _ = _tri_masks(C)
    tril_f = tril.astype(F32)
    logf = -_softplus(-(small_ref[...] + fb_ref[...]))
    gq = gq_ref[...]
    gk = gk_ref[...]
    scale = HEAD_DIM ** -0.5 * LOG2E
    lane = lax.broadcasted_iota(jnp.int32, (C, LANES), 1)
    for c in range(tm // C):
        rs = slice(c * C, (c + 1) * C)
        cum = _dot_f32(tril_f, logf[rs, :]) + carry[...]
        carry[...] = cum[C - 1:C, :]
        for h in range(HEADS):
            hs = slice(h * HEAD_DIM, (h + 1) * HEAD_DIM)
            q = q_ref[rs, hs].astype(F32)
            k = k_ref[rs, hs].astype(F32)
            q = q * lax.rsqrt(jnp.mean(q * q, axis=-1, keepdims=True) + NORM_EPS) * gq * scale
            k = k * lax.rsqrt(jnp.mean(k * k, axis=-1, keepdims=True) + NORM_EPS) * gk
            hi, mid, lo = _split3(cum[:, 2 * HEADS + h:2 * HEADS + h + 1] * LOG2E)
            one = jnp.where(lane < 6, 1.0, 0.0)
            parts = jnp.where(lane == 0, hi, jnp.where(lane == 1, mid, jnp.where(lane == 2, lo, 0.0)))
            q_aug = jnp.where(lane < 3, parts, jnp.where(lane < 6, one, 0.0))
            nparts = jnp.where(lane == 3, -hi, jnp.where(lane == 4, -mid, jnp.where(lane == 5, -lo, 0.0)))
            k_aug = jnp.where(lane < 3, one, nparts)
            base = h * QK_PAD
            q_out[rs, base:base + HEAD_DIM] = q.astype(BF16)
            q_out[rs, base + HEAD_DIM:base + QK_PAD] = q_aug.astype(BF16)
            k_out[rs, base:base + HEAD_DIM] = k.astype(BF16)
            k_out[rs, base + HEAD_DIM:base + QK_PAD] = k_aug.astype(BF16)
            vt_out[hs, rs] = v_ref[rs, hs].astype(F32).T.astype(BF16)


def _fox_prep(proj, small, f_bias_row, gq, gk, layer, B, S, tm):
    T = B * S
    nb = S // tm
    W = HEADS * HEAD_DIM
    HW = HEADS * QK_PAD
    blk = lambda col: pl.BlockSpec((tm, W), lambda b, i, col=col: (b * nb + i, col))
    prow = pl.BlockSpec((None, 1, LANES), lambda b, i: (layer, 0, 0))
    return pl.pallas_call(
        functools.partial(_fox_prep_kernel, tm=tm),
        out_shape=(jax.ShapeDtypeStruct((T, HW), BF16), jax.ShapeDtypeStruct((T, HW), BF16),
                   jax.ShapeDtypeStruct((B, nb, W, tm), BF16)),
        grid=(B, nb),
        in_specs=[blk(PROJ_FOX_Q // W), blk(PROJ_FOX_Q // W + 1), blk(PROJ_FOX_Q // W + 2),
                  pl.BlockSpec((tm, LANES), lambda b, i: (b * nb + i, 0)), prow, prow, prow],
        out_specs=(pl.BlockSpec((tm, HW), lambda b, i: (b * nb + i, 0)),
                   pl.BlockSpec((tm, HW), lambda b, i: (b * nb + i, 0)),
                   pl.BlockSpec((None, None, W, tm), lambda b, i: (b, i, 0, 0))),
        scratch_shapes=[pltpu.VMEM((1, LANES), F32)],
        compiler_params=_cparams(("parallel", "arbitrary")),
        name="fox_prep",
    )(proj, proj, proj, small, f_bias_row, gq, gk)


def _attn_kernel(q_ref, k_ref, vt_ref, o_ref, m_scr, l_scr, acc_scr, *, blk):
    i = pl.program_id(1)
    heads = range(HEADS)
    qsl = lambda h: slice(h * QK_PAD, (h + 1) * QK_PAD)
    vsl = lambda h: slice(h * HEAD_DIM, (h + 1) * HEAD_DIM)
    q = [q_ref[:, qsl(h)] for h in heads]
    m_scr[...] = jnp.full_like(m_scr, NEG_BIG)
    l_scr[...] = jnp.zeros_like(l_scr)
    acc_scr[...] = jnp.zeros_like(acc_scr)

    def step(j, masked):
        rows = pl.ds(pl.multiple_of(j * blk, blk), blk)
        st = [_dot_nt(k_ref[rows, qsl(h)], q[h]) for h in heads]
        if masked:
            r = lax.broadcasted_iota(jnp.int32, (blk, blk), 0)
            c = lax.broadcasted_iota(jnp.int32, (blk, blk), 1)
            keep = r <= c
            st = [jnp.where(keep, s, NEG_BIG) for s in st]
        m_old = [m_scr[h] for h in heads]
        m_new = [jnp.maximum(mo, jnp.max(s, axis=0, keepdims=True)) for mo, s in zip(m_old, st)]
        alpha = [jnp.exp2(mo - mn) for mo, mn in zip(m_old, m_new)]
        p = [jnp.exp2(s - mn) for s, mn in zip(st, m_new)]
        for h in heads:
            l_scr[h] = alpha[h] * l_scr[h] + jnp.sum(p[h], axis=0, keepdims=True)
            m_scr[h] = m_new[h]
        pv = [_dot(vt_ref[j, vsl(h), :], p[h].astype(BF16)) for h in heads]
        for h in heads:
            acc_scr[h] = alpha[h] * acc_scr[h] + pv[h]

    def body(j, carry):
        step(j, False)
        return carry

    lax.fori_loop(0, i, body, 0)
    step(i, True)
    for h in heads:
        o_ref[:, vsl(h)] = (acc_scr[h] / l_scr[h]).T.astype(BF16)


def _attention(q, k, vt, B, S, blk):
    T = B * S
    nb = S // blk
    W = HEADS * HEAD_DIM
    HW = HEADS * QK_PAD
    return pl.pallas_call(
        functools.partial(_attn_kernel, blk=blk),
        out_shape=jax.ShapeDtypeStruct((T, W), BF16),
        grid=(B, nb),
        in_specs=[
            pl.BlockSpec((blk, HW), lambda b, i: (b * nb + i, 0)),
            pl.BlockSpec((S, HW), lambda b, i: (b, 0)),
            pl.BlockSpec((None, nb, W, blk), lambda b, i: (b, 0, 0, 0)),
        ],
        out_specs=pl.BlockSpec((blk, W), lambda b, i: (b * nb + i, 0)),
        scratch_shapes=[pltpu.VMEM((HEADS, 1, blk), F32), pltpu.VMEM((HEADS, 1, blk), F32),
                        pltpu.VMEM((HEADS, HEAD_DIM, blk), F32)],
        compiler_params=_cparams(("parallel", "arbitrary")),
        name="causal_attention",
    )(q, k, vt)


def _sg_kernel(u_ref, v_ref, g_ref, ws_ref, b_ref, o_ref, *, rows):
    Tn = SG_CHUNK
    tril, _ = _tri_masks(Tn)
    g = g_ref[...]
    bias = b_ref[...]
    for gi in range(HEADS):
        hs = slice(gi * HEAD_DIM, (gi + 1) * HEAD_DIM)
        w = jnp.where(tril, ws_ref[gi], 0.0).astype(BF16)
        for n in range(rows // Tn):
            rs = slice(n * Tn, (n + 1) * Tn)
            v = _gelu_tanh(v_ref[rs, hs].astype(F32))
            v = v * lax.rsqrt(jnp.mean(v * v, axis=-1, keepdims=True) + NORM_EPS) * g[:, hs]
            mixed = _dot(w, v.astype(BF16)) + bias[:, hs]
            u = _gelu_tanh(u_ref[rs, hs].astype(F32))
            o_ref[rs, hs] = (u * mixed).astype(BF16)


def _spatial_gating(proj, norm_g, w_s, bias_full, layer, T, rows):
    W = HEADS * HEAD_DIM
    return pl.pallas_call(
        functools.partial(_sg_kernel, rows=rows),
        out_shape=jax.ShapeDtypeStruct((T, W), BF16),
        grid=(T // rows,),
        in_specs=[
            pl.BlockSpec((rows, W), lambda i: (i, PROJ_SG_U // W)),
            pl.BlockSpec((rows, W), lambda i: (i, PROJ_SG_V // W)),
            pl.BlockSpec((None, 1, W), lambda i: (layer, 0, 0)),
            pl.BlockSpec((None, HEADS, SG_CHUNK, SG_CHUNK), lambda i: (layer, 0, 0, 0)),
            pl.BlockSpec((None, SG_CHUNK, W), lambda i: (layer, 0, 0)),
        ],
        out_specs=pl.BlockSpec((rows, W), lambda i: (i, 0)),
        compiler_params=_cparams(("parallel",)),
        name="spatial_gating",
    )(proj, proj, norm_g, w_s, bias_full)


def _merge_kernel(oa, ob, oc, od, g0, g1, g2, g3, x_ref, wb_ref, wo_ref, out_ref):
    merged = None
    for i, (o, g) in enumerate(((oa, g0), (ob, g1), (oc, g2), (od, g3))):
        term = _sigmoid(g[...].astype(F32)) * _dot(o[...], wb_ref[i])
        merged = term if merged is None else merged + term
    out_ref[...] = x_ref[...] + _dot(merged.astype(BF16), wo_ref[...])


def _merge(branches, proj, x2, w_branch, w_out, layer, tm):
    T = x2.shape[0]
    ob = pl.BlockSpec((tm, BRANCH_WIDTH), lambda i: (i, 0))
    gate = lambda n: pl.BlockSpec((tm, D_MODEL), lambda i, n=n: (i, n))
    xs = pl.BlockSpec((tm, D_MODEL), lambda i: (i, 0))
    return pl.pallas_call(
        _merge_kernel,
        out_shape=jax.ShapeDtypeStruct((T, D_MODEL), F32),
        grid=(T // tm,),
        in_specs=[ob, ob, ob, ob, gate(0), gate(1), gate(2), gate(3), xs,
                  pl.BlockSpec((None, N_BRANCHES, BRANCH_WIDTH, D_MODEL), lambda i: (layer, 0, 0, 0)),
                  pl.BlockSpec((None, D_MODEL, D_MODEL), lambda i: (layer, 0, 0))],
        out_specs=xs,
        compiler_params=_cparams(("parallel",)),
        name="merge_out_proj",
    )(*branches, proj, proj, proj, proj, x2, w_branch, w_out)


def _ffn_kernel(x_ref, g_ref, w1_ref, w2_ref, out_ref, h_scr):
    f = pl.program_id(1)

    @pl.when(f == 0)
    def _():
        x = x_ref[...]
        ms = jnp.mean(x * x, axis=-1, keepdims=True)
        h_scr[...] = (x * lax.rsqrt(ms + NORM_EPS) * g_ref[...]).astype(BF16)
        out_ref[...] = x

    a = jnp.maximum(_dot(h_scr[...], w1_ref[...]), 0.0)
    out_ref[...] += _dot((a * a).astype(BF16), w2_ref[...])


def _ffn(x2, g, w1, w2, layer, tm, tf):
    T = x2.shape[0]
    xs = pl.BlockSpec((tm, D_MODEL), lambda i, f: (i, 0))
    return pl.pallas_call(
        _ffn_kernel,
        out_shape=jax.ShapeDtypeStruct((T, D_MODEL), F32),
        grid=(T // tm, D_FF // tf),
        in_specs=[xs,
                  pl.BlockSpec((None, 1, D_MODEL), lambda i, f: (layer, 0, 0)),
                  pl.BlockSpec((None, D_MODEL, tf), lambda i, f: (layer, 0, f)),
                  pl.BlockSpec((None, tf, D_MODEL), lambda i, f: (layer, f, 0))],
        out_specs=xs,
        scratch_shapes=[pltpu.VMEM((tm, D_MODEL), BF16)],
        compiler_params=_cparams(("parallel", "arbitrary")),
        name="relu2_mlp",
    )(x2, g, w1, w2)


def _pad_lanes(a, width):
    return jnp.pad(a, [(0, 0)] * (a.ndim - 1) + [(0, width - a.shape[-1])])


def _swap_halves(a):
    h = a.shape[-1] // 2
    return jnp.concatenate([a[..., h:], a[..., :h]], axis=-1)


def _prep_params(norm1_g, w_in, dn_a_log, dn_dt_bias, dn_out_norm_g, mla_q_norm_g, mla_kv_norm_g, mla_w_uq,
                 mla_w_ukv, mla_qk_q_g, mla_qk_k_g, sg_v_norm_g, sg_b_s, fox_q_norm_g, fox_k_norm_g, fox_f_bias,
                 w_branch, w_out, norm2_g, w_ff1, w_ff2):
    L = w_in.shape[0]
    col = lambda a, b: w_in[:, :, a:b]
    kr = col(O_MLA_KR, O_MLA_KR + MLA_ROPE)
    w_p = jnp.concatenate([
        col(O_GATES, O_END), col(O_DN_QKV, O_DN_Z), col(O_DN_Z, O_DN_A), col(O_SG_U, O_SG_V),
        col(O_SG_V, O_FOX_QKV), col(O_FOX_QKV, O_FOX_F), col(O_MLA_CQ, O_MLA_CKV), col(O_MLA_CKV, O_MLA_KR),
        kr, _swap_halves(kr)], axis=-1).astype(BF16)
    assert w_p.shape[-1] == PROJ_WIDTH
    w_small = _pad_lanes(jnp.concatenate([col(O_DN_A, O_DN_B), col(O_DN_B, O_MLA_CQ), col(O_FOX_F, O_GATES)],
                                         axis=-1), LANES).astype(BF16)

    def row(a, width=LANES, offset=0):
        a = a.reshape(L, 1, -1)
        return jnp.pad(a, ((0, 0), (0, 0), (offset, width - offset - a.shape[-1])))

    wq = mla_w_uq.reshape(L, MLA_Q_RANK, HEADS, MLA_QK_DIM)
    wq_r = wq[..., MLA_NOPE:]
    w_uq_p = jnp.concatenate([wq, _swap_halves(wq_r)], axis=-1).reshape(L, MLA_Q_RANK, HEADS * QK_PAD).astype(BF16)

    def qk_gain(g):
        return jnp.concatenate([g, _swap_halves(g[:, MLA_NOPE:])], axis=-1).reshape(L, 1, QK_PAD)

    bias_full = jnp.repeat(jnp.swapaxes(sg_b_s, 1, 2), HEAD_DIM, axis=-1)
    return dict(
        norm1_g=norm1_g.reshape(L, 1, D_MODEL), w_p=w_p, w_small=w_small,
        a_log=row(dn_a_log), dt=row(dn_dt_bias), dn_og=dn_out_norm_g.reshape(L, 1, HEAD_DIM),
        mla_qn=mla_q_norm_g.reshape(L, 1, MLA_Q_RANK), mla_kvn=mla_kv_norm_g.reshape(L, 1, MLA_KV_RANK),
        w_uq_p=w_uq_p, w_ukv=mla_w_ukv.astype(BF16), mla_gq=qk_gain(mla_qk_q_g), mla_gk=qk_gain(mla_qk_k_g),
        sg_g=sg_v_norm_g.reshape(L, 1, HEADS * HEAD_DIM), sg_bias=bias_full,
        fox_fb=row(fox_f_bias, offset=2 * HEADS), fox_gq=fox_q_norm_g.reshape(L, 1, HEAD_DIM),
        fox_gk=fox_k_norm_g.reshape(L, 1, HEAD_DIM),
        w_branch=w_branch.astype(BF16), w_out=w_out.astype(BF16), norm2_g=norm2_g.reshape(L, 1, D_MODEL),
        w_ff1=w_ff1.astype(BF16), w_ff2=w_ff2.astype(BF16),
    )


def _tiles(S):
    pick = lambda want: min(want, S)
    return dict(in_tm=pick(1024), in_tn=1024, dn_rows=pick(512), prep_tm=pick(512), attn_blk=pick(512),
                sg_rows=pick(512), merge_tm=pick(512), ffn_tm=pick(1024), ffn_tf=1024, rope_tm=pick(1024))


def kernel(x, positions, norm1_g, w_in, dn_conv_w, dn_a_log, dn_dt_bias, dn_out_norm_g, mla_q_norm_g, mla_kv_norm_g, mla_w_uq, mla_w_ukv, mla_qk_q_g, mla_qk_k_g, sg_v_norm_g, sg_w_s, sg_b_s, fox_q_norm_g, fox_k_norm_g, fox_f_bias, w_branch, w_out, norm2_g, w_ff1, w_ff2):
    B, S, D = x.shape
    assert D == D_MODEL and S % LANES == 0
    T = B * S
    depth = w_in.shape[0]
    t = _tiles(S)
    p = _prep_params(norm1_g, w_in, dn_a_log, dn_dt_bias, dn_out_norm_g, mla_q_norm_g, mla_kv_norm_g, mla_w_uq,
                     mla_w_ukv, mla_qk_q_g, mla_qk_k_g, sg_v_norm_g, sg_b_s, fox_q_norm_g, fox_k_norm_g,
                     fox_f_bias, w_branch, w_out, norm2_g, w_ff1, w_ff2)
    cos_t, sin_t = _rope_tables(positions, t["rope_tm"])
    x2 = x.reshape(T, D)
    for l in range(depth):
        proj, small = _in_proj(x2, p["norm1_g"], p["w_p"], p["w_small"], l, t["in_tm"], t["in_tn"])
        o_a = _deltanet(proj, small, dn_conv_w, p["a_log"], p["dt"], p["dn_og"], l, B, S, t["dn_rows"])
        qb, kb, vtb = _mla_prep(proj, cos_t, sin_t, p["mla_qn"], p["mla_kvn"], p["w_uq_p"], p["w_ukv"],
                                p["mla_gq"], p["mla_gk"], l, B, S, t["prep_tm"])
        o_b = _attention(qb, kb, vtb, B, S, t["attn_blk"])
        o_c = _spatial_gating(proj, p["sg_g"], sg_w_s, p["sg_bias"], l, T, t["sg_rows"])
        qd, kd, vtd = _fox_prep(proj, small, p["fox_fb"], p["fox_gq"], p["fox_gk"], l, B, S, t["prep_tm"])
        o_d = _attention(qd, kd, vtd, B, S, t["attn_blk"])
        x2 = _merge((o_a, o_b, o_c, o_d), proj, x2, p["w_branch"], p["w_out"], l, t["merge_tm"])
        x2 = _ffn(x2, p["norm2_g"], p["w_ff1"], p["w_ff2"], l, t["ffn_tm"], t["ffn_tf"])
    return x2.reshape(B, S, D)
```

```python
import functools
import math

import jax
import jax.numpy as jnp
from jax import lax
from jax.experimental import pallas as pl
from jax.experimental.pallas import tpu as pltpu

F32 = jnp.float32
BF16 = jnp.bfloat16

D_MODEL = 1024
NORM_EPS = 1e-6
N_BRANCHES = 4
BRANCH_WIDTH = 512
D_FF = 4 * D_MODEL
HEADS = 4
HEAD_DIM = 128
DN_CONV = 4
DN_CHUNK = 128
MLA_Q_RANK = 256
MLA_KV_RANK = 128
MLA_NOPE = 128
MLA_ROPE = 64
MLA_QK_DIM = MLA_NOPE + MLA_ROPE
ROPE_THETA = 10000.0
SG_CHUNK = 128
QK_PAD = 256
LANES = 128
NEG_BIG = -1e30
LOG2E = math.log2(math.e)

PROJ_GATES = 0
PROJ_DN_Q = 4096
PROJ_SG_U = 6144
PROJ_SG_V = 6656
PROJ_FOX_Q = 7168
PROJ_MLA_CQ = 8704
PROJ_MLA_CKV = 8960
PROJ_MLA_KR = 9088
PROJ_WIDTH = 9216

_SPLITS = (1536, 512, 4, 4, 256, 128, 64, 512, 512, 1536, 4, 4096)
_OFF = [0]
for _s in _SPLITS:
    _OFF.append(_OFF[-1] + _s)
(O_DN_QKV, O_DN_Z, O_DN_A, O_DN_B, O_MLA_CQ, O_MLA_CKV, O_MLA_KR, O_SG_U, O_SG_V, O_FOX_QKV, O_FOX_F,
 O_GATES, O_END) = _OFF

VMEM_LIMIT = 56 * 1024 * 1024


def _cparams(sem):
    return pltpu.CompilerParams(dimension_semantics=sem, vmem_limit_bytes=VMEM_LIMIT)


def _dot(a, b):
    return jnp.dot(a, b, preferred_element_type=F32)


def _dot_nt(a, b):
    return lax.dot_general(a, b, (((1,), (1,)), ((), ())), preferred_element_type=F32)


def _dot_f32(a, b):
    return jnp.dot(a, b, preferred_element_type=F32, precision=lax.Precision.HIGHEST)


def _sigmoid(x):
    return 1.0 / (1.0 + jnp.exp(-x))


def _softplus(x):
    return jnp.maximum(x, 0.0) + jnp.log1p(jnp.exp(-jnp.abs(x)))


def _gelu_tanh(x):
    return 0.5 * x * (1.0 + jnp.tanh(math.sqrt(2.0 / math.pi) * (x + 0.044715 * (x * x * x))))


def _tri_masks(n):
    r = lax.broadcasted_iota(jnp.int32, (n, n), 0)
    c = lax.broadcasted_iota(jnp.int32, (n, n), 1)
    return r >= c, r > c


def _rope_kernel(pos_ref, freq_ref, sign_ref, cos_ref, sin_ref):
    ang = pos_ref[...].astype(F32) * freq_ref[...]
    lane = lax.broadcasted_iota(jnp.int32, ang.shape, 1)
    live = lane < MLA_ROPE
    cos_ref[...] = jnp.where(live, jnp.cos(ang), 0.0)
    sin_ref[...] = jnp.where(live, jnp.sin(ang) * sign_ref[...], 0.0)


def _rope_tables(positions, tm):
    T = positions.size
    half = MLA_ROPE // 2
    inv_freq = ROPE_THETA ** (-jnp.arange(0, MLA_ROPE, 2, dtype=F32) / MLA_ROPE)
    zeros = jnp.zeros((LANES - MLA_ROPE,), F32)
    freq = jnp.concatenate([inv_freq, inv_freq, zeros]).reshape(1, LANES)
    sign = jnp.concatenate([-jnp.ones((half,), F32), jnp.ones((half,), F32), zeros]).reshape(1, LANES)
    row = pl.BlockSpec((1, LANES), lambda i: (0, 0))
    tab = pl.BlockSpec((tm, LANES), lambda i: (i, 0))
    return pl.pallas_call(
        _rope_kernel,
        out_shape=(jax.ShapeDtypeStruct((T, LANES), F32), jax.ShapeDtypeStruct((T, LANES), F32)),
        grid=(T // tm,),
        in_specs=[pl.BlockSpec((tm, 1), lambda i: (i, 0)), row, row],
        out_specs=(tab, tab),
        compiler_params=_cparams(("parallel",)),
        name="rope_tables",
    )(positions.reshape(T, 1), freq, sign)


def _in_proj_kernel(x_ref, g_ref, w_ref, ws_ref, proj_ref, small_ref, h_scr):
    @pl.when(pl.program_id(1) == 0)
    def _():
        x = x_ref[...]
        ms = jnp.mean(x * x, axis=-1, keepdims=True)
        h = (x * lax.rsqrt(ms + NORM_EPS) * g_ref[...]).astype(BF16)
        h_scr[...] = h
        small_ref[...] = _dot(h, ws_ref[...])

    proj_ref[...] = _dot(h_scr[...], w_ref[...]).astype(BF16)


def _in_proj(x2, g, w_p, w_small, layer, tm, tn):
    T = x2.shape[0]
    return pl.pallas_call(
        _in_proj_kernel,
        out_shape=(jax.ShapeDtypeStruct((T, PROJ_WIDTH), BF16), jax.ShapeDtypeStruct((T, LANES), F32)),
        grid=(T // tm, PROJ_WIDTH // tn),
        in_specs=[
            pl.BlockSpec((tm, D_MODEL), lambda i, j: (i, 0)),
            pl.BlockSpec((None, 1, D_MODEL), lambda i, j: (layer, 0, 0)),
            pl.BlockSpec((None, D_MODEL, tn), lambda i, j: (layer, 0, j)),
            pl.BlockSpec((None, D_MODEL, LANES), lambda i, j: (layer, 0, 0)),
        ],
        out_specs=(pl.BlockSpec((tm, tn), lambda i, j: (i, j)),
                   pl.BlockSpec((tm, LANES), lambda i, j: (i, 0))),
        scratch_shapes=[pltpu.VMEM((tm, D_MODEL), BF16)],
        compiler_params=_cparams(("parallel", "arbitrary")),
        name="in_proj",
    )(x2, g, w_p, w_small)


INV_BASE = 8


def _inverse_masks(n):
    r = lax.broadcasted_iota(jnp.int32, (n, n), 0)
    c = lax.broadcasted_iota(jnp.int32, (n, n), 1)
    same = lambda s: (r // s) == (c // s)
    levels = []
    s = INV_BASE
    while s < n:
        levels.append(same(2 * s) & jnp.logical_not(same(s)))
        s *= 2
    as16 = lambda m: jnp.where(m, 1.0, 0.0).astype(BF16)
    return (r == c).astype(F32), as16(same(INV_BASE)), [as16(m) for m in levels]


def _unit_lower_inverse(nmats, masks):
    eye, base, levels = masks
    n16 = [n.astype(BF16) for n in nmats]
    ps = [n * base for n in n16]
    xs = [eye - p.astype(F32) for p in ps]
    for _ in range(int(math.log2(INV_BASE)) - 1):
        ps = [_dot(p, p).astype(BF16) for p in ps]
        xs = [x + _dot(x.astype(BF16), p) for x, p in zip(xs, ps)]
    for off in levels:
        x16 = [x.astype(BF16) for x in xs]
        ts = [_dot(n * off, xb).astype(BF16) for n, xb in zip(n16, x16)]
        xs = [x - _dot(xb, t) for x, xb, t in zip(xs, x16, ts)]
    return xs


def _deltanet_kernel(q_ref, k_ref, v_ref, z_ref, small_ref, cw_ref, alog_ref, dt_ref, og_ref, o_ref,
                     qbuf, kbuf, vbuf, qs, ks, vs, state, *, rows):
    C = DN_CHUNK
    W = HEADS * HEAD_DIM
    NC = rows // C

    @pl.when(pl.program_id(1) == 0)
    def _():
        state[...] = jnp.zeros_like(state)
        for buf in (qbuf, kbuf, vbuf):
            buf[0:8, :] = jnp.zeros((8, W), F32)

    def conv_silu(buf, x_ref, w, dst):
        buf[8:8 + rows, :] = x_ref[...].astype(F32)
        acc = buf[8:8 + rows, :] * w[DN_CONV - 1:DN_CONV, :]
        for s in range(1, DN_CONV):
            acc = acc + buf[8 - s:8 - s + rows, :] * w[DN_CONV - 1 - s:DN_CONV - s, :]
        buf[0:8, :] = buf[rows:rows + 8, :]
        dst[...] = acc * _sigmoid(acc)

    cw = cw_ref[...]
    conv_silu(qbuf, q_ref, cw[:, 0:W], qs)
    conv_silu(kbuf, k_ref, cw[:, W:2 * W], ks)
    conv_silu(vbuf, v_ref, cw[:, 2 * W:3 * W], vs)

    small = small_ref[...]
    g_all = -jnp.exp(alog_ref[...]) * _softplus(small + dt_ref[...])
    beta_all = _sigmoid(small)

    tril, strict = _tri_masks(C)
    tril_f = tril.astype(F32)
    inv_masks = _inverse_masks(C)
    og = og_ref[...]

    probs = [(c, h) for c in range(NC) for h in range(HEADS)]
    gcs = [_dot_f32(tril_f, g_all[c * C:(c + 1) * C, :]) for c in range(NC)]
    gcts = [gc.T for gc in gcs]
    rsl = lambda c: slice(c * C, (c + 1) * C)
    hsl = lambda h: slice(h * HEAD_DIM, (h + 1) * HEAD_DIM)
    qn, kn = [], []
    for c, h in probs:
        q = qs[rsl(c), hsl(h)]
        k = ks[rsl(c), hsl(h)]
        qn.append(q * (lax.rsqrt(jnp.sum(q * q, axis=-1, keepdims=True) + NORM_EPS) * (HEAD_DIM ** -0.5)))
        kn.append(k * lax.rsqrt(jnp.sum(k * k, axis=-1, keepdims=True) + NORM_EPS))
    gcol = [gcs[c][:, h:h + 1] for c, h in probs]
    bcol = [beta_all[rsl(c), HEADS + h:HEADS + h + 1] for c, h in probs]
    glast = [gcs[c][C - 1:C, h:h + 1] for c, h in probs]
    decay = [jnp.where(tril, jnp.exp(jnp.where(tril, gcol[i] - gcts[c][h:h + 1, :], 0.0)), 0.0)
             for i, (c, h) in enumerate(probs)]
    kb = [k * b for k, b in zip(kn, bcol)]
    k16 = [k.astype(BF16) for k in kn]
    qk_kk = [_dot_nt(jnp.concatenate([kbi, q], axis=0).astype(BF16), k) for kbi, q, k in zip(kb, qn, k16)]
    nmat = [jnp.where(strict, m[:C] * d, 0.0) for m, d in zip(qk_kk, decay)]
    a_qk = [(m[C:] * d).astype(BF16) for m, d in zip(qk_kk, decay)]
    eg = [jnp.exp(g) for g in gcol]
    rhs = [jnp.concatenate([vs[rsl(c), hsl(h)] * bcol[i], kb[i] * eg[i]], axis=-1).astype(BF16)
           for i, (c, h) in enumerate(probs)]
    xinv = _unit_lower_inverse(nmat, inv_masks)
    sol = [_dot(x.astype(BF16), r) for x, r in zip(xinv, rhs)]
    u = [s[:, :HEAD_DIM] for s in sol]
    wq = [jnp.concatenate([s[:, HEAD_DIM:], q * e], axis=0).astype(BF16) for s, q, e in zip(sol, qn, eg)]
    kdt = [(k * jnp.exp(gl - g)).T.astype(BF16) for k, gl, g in zip(kn, glast, gcol)]
    egl = [jnp.exp(gl) for gl in glast]

    st = [state[h] for h in range(HEADS)]
    for c in range(NC):
        idx = [c * HEADS + h for h in range(HEADS)]
        r = [_dot(wq[i], st[h].astype(BF16)) for h, i in enumerate(idx)]
        v16 = [(u[i] - r[h][:C]).astype(BF16) for h, i in enumerate(idx)]
        o = [r[h][C:] + _dot(a_qk[i], v16[h]) for h, i in enumerate(idx)]
        st = [st[h] * egl[i] + _dot(kdt[i], v16[h]) for h, i in enumerate(idx)]
        for h in range(HEADS):
            oh = o[h] * lax.rsqrt(jnp.mean(o[h] * o[h], axis=-1, keepdims=True) + NORM_EPS) * og
            z = z_ref[rsl(c), hsl(h)].astype(F32)
            o_ref[rsl(c), hsl(h)] = (oh * (z * _sigmoid(z))).astype(BF16)
    for h in range(HEADS):
        state[h] = st[h]


def _deltanet(proj, small, conv_w, a_log_row, dt_row, out_g, layer, B, S, rows):
    T = B * S
    W = HEADS * HEAD_DIM
    nb = S // rows
    blk = lambda col: pl.BlockSpec((rows, W), lambda b, i, col=col: (b * nb + i, col))
    prow = pl.BlockSpec((None, 1, LANES), lambda b, i: (layer, 0, 0))
    return pl.pallas_call(
        functools.partial(_deltanet_kernel, rows=rows),
        out_shape=jax.ShapeDtypeStruct((T, W), BF16),
        grid=(B, nb),
        in_specs=[
            blk(PROJ_DN_Q // W), blk(PROJ_DN_Q // W + 1), blk(PROJ_DN_Q // W + 2), blk(PROJ_DN_Q // W + 3),
            pl.BlockSpec((rows, LANES), lambda b, i: (b * nb + i, 0)),
            pl.BlockSpec((None, DN_CONV, 3 * W), lambda b, i: (layer, 0, 0)),
            prow, prow, prow,
        ],
        out_specs=pl.BlockSpec((rows, W), lambda b, i: (b * nb + i, 0)),
        scratch_shapes=[pltpu.VMEM((rows + 8, W), F32)] * 3 + [pltpu.VMEM((rows, W), F32)] * 3
        + [pltpu.VMEM((HEADS, HEAD_DIM, HEAD_DIM), F32)],
        compiler_params=_cparams(("parallel", "arbitrary")),
        name="deltanet",
    )(proj, proj, proj, proj, small, conv_w, a_log_row, dt_row, out_g)


VT_ONES = 16
VT_ROWS = HEAD_DIM + VT_ONES


def _store_vt(vt_out, h, v16):
    r = lax.broadcasted_iota(jnp.int32, (HEAD_DIM, HEAD_DIM), 0)
    c = lax.broadcasted_iota(jnp.int32, (HEAD_DIM, HEAD_DIM), 1)
    eye = jnp.where(r == c, 1.0, 0.0).astype(BF16)
    base = h * VT_ROWS
    vt_out[base:base + HEAD_DIM, :] = _dot_nt(eye, v16).astype(BF16)
    vt_out[base + HEAD_DIM:base + VT_ROWS, :] = jnp.ones((VT_ONES, v16.shape[0]), BF16)


def _mla_prep_kernel(cq_ref, ckv_ref, kr_ref, cos_ref, sin_ref, qn_ref, kvn_ref, wq_ref, wkv_ref,
                     gq_ref, gk_ref, q_out, k_out, vt_out):
    cq = cq_ref[...].astype(F32)
    ckv = ckv_ref[...].astype(F32)
    cq = cq * lax.rsqrt(jnp.mean(cq * cq, axis=-1, keepdims=True) + NORM_EPS) * qn_ref[...]
    ckv = ckv * lax.rsqrt(jnp.mean(ckv * ckv, axis=-1, keepdims=True) + NORM_EPS) * kvn_ref[...]
    q_all = _dot(cq.astype(BF16), wq_ref[...])
    kv_all = _dot(ckv.astype(BF16), wkv_ref[...])
    cos = cos_ref[...]
    sin = sin_ref[...]
    gq = gq_ref[...]
    gk = gk_ref[...]
    lane = lax.broadcasted_iota(jnp.int32, cos.shape, 1)
    live = lane < MLA_ROPE
    scale = MLA_QK_DIM ** -0.5 * LOG2E

    def rope(xr):
        return xr * cos + pltpu.roll(xr, MLA_ROPE, 1) * sin

    kr = kr_ref[...].astype(F32)
    kr_ss = jnp.sum(jnp.where(live, kr * kr, 0.0), axis=-1, keepdims=True)
    for h in range(HEADS):
        base = h * QK_PAD
        qn = q_all[:, base:base + MLA_NOPE]
        qr = q_all[:, base + MLA_NOPE:base + QK_PAD]
        ss = jnp.sum(qn * qn, axis=-1, keepdims=True) + jnp.sum(jnp.where(live, qr * qr, 0.0), axis=-1, keepdims=True)
        rinv = lax.rsqrt(ss * (1.0 / MLA_QK_DIM) + NORM_EPS) * scale
        q_out[:, base:base + MLA_NOPE] = (qn * rinv * gq[:, :MLA_NOPE]).astype(BF16)
        q_out[:, base + MLA_NOPE:base + QK_PAD] = rope(qr * rinv * gq[:, MLA_NOPE:]).astype(BF16)

        kn = kv_all[:, base:base + MLA_NOPE]
        v = kv_all[:, base + MLA_NOPE:base + QK_PAD]
        ssk = jnp.sum(kn * kn, axis=-1, keepdims=True) + kr_ss
        rk = lax.rsqrt(ssk * (1.0 / MLA_QK_DIM) + NORM_EPS)
        k_out[:, base:base + MLA_NOPE] = (kn * rk * gk[:, :MLA_NOPE]).astype(BF16)
        k_out[:, base + MLA_NOPE:base + QK_PAD] = rope(kr * rk * gk[:, MLA_NOPE:]).astype(BF16)
        _store_vt(vt_out, h, v.astype(BF16))


def _mla_prep(proj, cos_t, sin_t, qn_g, kvn_g, w_uq_p, w_ukv, gq_p, gk_p, layer, B, S, tm):
    T = B * S
    nb = S // tm
    HW = HEADS * QK_PAD
    tab = pl.BlockSpec((tm, LANES), lambda i: (i, 0))
    return pl.pallas_call(
        _mla_prep_kernel,
        out_shape=(jax.ShapeDtypeStruct((T, HW), BF16), jax.ShapeDtypeStruct((T, HW), BF16),
                   jax.ShapeDtypeStruct((B, nb, HEADS * VT_ROWS, tm), BF16)),
        grid=(T // tm,),
        in_specs=[
            pl.BlockSpec((tm, MLA_Q_RANK), lambda i: (i, PROJ_MLA_CQ // MLA_Q_RANK)),
            pl.BlockSpec((tm, MLA_KV_RANK), lambda i: (i, PROJ_MLA_CKV // MLA_KV_RANK)),
            pl.BlockSpec((tm, LANES), lambda i: (i, PROJ_MLA_KR // LANES)),
            tab, tab,
            pl.BlockSpec((None, 1, MLA_Q_RANK), lambda i: (layer, 0, 0)),
            pl.BlockSpec((None, 1, MLA_KV_RANK), lambda i: (layer, 0, 0)),
            pl.BlockSpec((None, MLA_Q_RANK, HW), lambda i: (layer, 0, 0)),
            pl.BlockSpec((None, MLA_KV_RANK, HW), lambda i: (layer, 0, 0)),
            pl.BlockSpec((None, 1, QK_PAD), lambda i: (layer, 0, 0)),
            pl.BlockSpec((None, 1, QK_PAD), lambda i: (layer, 0, 0)),
        ],
        out_specs=(pl.BlockSpec((tm, HW), lambda i: (i, 0)), pl.BlockSpec((tm, HW), lambda i: (i, 0)),
                   pl.BlockSpec((None, None, HEADS * VT_ROWS, tm), lambda i: (i // nb, i % nb, 0, 0))),
        compiler_params=_cparams(("parallel",)),
        name="mla_prep",
    )(proj, proj, proj, cos_t, sin_t, qn_g, kvn_g, w_uq_p, w_ukv, gq_p, gk_p)


def _split3(c):
    hi = c.astype(BF16).astype(F32)
    r = c - hi
    mid = r.astype(BF16).astype(F32)
    lo = (r - mid).astype(BF16).astype(F32)
    return hi, mid, lo


def _fox_prep_kernel(q_ref, k_ref, v_ref, small_ref, fb_ref, gq_ref, gk_ref, q_out, k_out, vt_out, carry, *, tm):
    @pl.when(pl.program_id(1) == 0)
    def _():
        carry[...] = jnp.zeros_like(carry)

    C = LANES
    tril, _ = _tri_masks(C)
    tril_f = tril.astype(F32)
    logf = -_softplus(-(small_ref[...] + fb_ref[...]))
    gq = gq_ref[...]
    gk = gk_ref[...]
    scale = HEAD_DIM ** -0.5 * LOG2E
    lane = lax.broadcasted_iota(jnp.int32, (C, LANES), 1)
    for c in range(tm // C):
        rs = slice(c * C, (c + 1) * C)
        cum = _dot_f32(tril_f, logf[rs, :]) + carry[...]
        carry[...] = cum[C - 1:C, :]
        for h in range(HEADS):
            hs = slice(h * HEAD_DIM, (h + 1) * HEAD_DIM)
            q = q_ref[rs, hs].astype(F32)
            k = k_ref[rs, hs].astype(F32)
            q = q * lax.rsqrt(jnp.mean(q * q, axis=-1, keepdims=True) + NORM_EPS) * gq * scale
            k = k * lax.rsqrt(jnp.mean(k * k, axis=-1, keepdims=True) + NORM_EPS) * gk
            hi, mid, lo = _split3(cum[:, 2 * HEADS + h:2 * HEADS + h + 1] * LOG2E)
            one = jnp.where(lane < 6, 1.0, 0.0)
            parts = jnp.where(lane == 0, hi, jnp.where(lane == 1, mid, jnp.where(lane == 2, lo, 0.0)))
            q_aug = jnp.where(lane < 3, parts, jnp.where(lane < 6, one, 0.0))
            nparts = jnp.where(lane == 3, -hi, jnp.where(lane == 4, -mid, jnp.where(lane == 5, -lo, 0.0)))
            k_aug = jnp.where(lane < 3, one, nparts)
            base = h * QK_PAD
            q_out[rs, base:base + HEAD_DIM] = q.astype(BF16)
            q_out[rs, base + HEAD_DIM:base + QK_PAD] = q_aug.astype(BF16)
            k_out[rs, base:base + HEAD_DIM] = k.astype(BF16)
            k_out[rs, base + HEAD_DIM:base + QK_PAD] = k_aug.astype(BF16)
    for h in range(HEADS):
        _store_vt(vt_out, h, v_ref[:, h * HEAD_DIM:(h + 1) * HEAD_DIM])


def _fox_prep(proj, small, f_bias_row, gq, gk, layer, B, S, tm):
    T = B * S
    nb = S // tm
    W = HEADS * HEAD_DIM
    HW = HEADS * QK_PAD
    blk = lambda col: pl.BlockSpec((tm, W), lambda b, i, col=col: (b * nb + i, col))
    prow = pl.BlockSpec((None, 1, LANES), lambda b, i: (layer, 0, 0))
    return pl.pallas_call(
        functools.partial(_fox_prep_kernel, tm=tm),
        out_shape=(jax.ShapeDtypeStruct((T, HW), BF16), jax.ShapeDtypeStruct((T, HW), BF16),
                   jax.ShapeDtypeStruct((B, nb, HEADS * VT_ROWS, tm), BF16)),
        grid=(B, nb),
        in_specs=[blk(PROJ_FOX_Q // W), blk(PROJ_FOX_Q // W + 1), blk(PROJ_FOX_Q // W + 2),
                  pl.BlockSpec((tm, LANES), lambda b, i: (b * nb + i, 0)), prow, prow, prow],
        out_specs=(pl.BlockSpec((tm, HW), lambda b, i: (b * nb + i, 0)),
                   pl.BlockSpec((tm, HW), lambda b, i: (b * nb + i, 0)),
                   pl.BlockSpec((None, None, HEADS * VT_ROWS, tm), lambda b, i: (b, i, 0, 0))),
        scratch_shapes=[pltpu.VMEM((1, LANES), F32)],
        compiler_params=_cparams(("parallel", "arbitrary")),
        name="fox_prep",
    )(proj, proj, proj, small, f_bias_row, gq, gk)


def _attn_kernel(q_ref, k_ref, vt_ref, o_ref, m_scr, acc_scr, *, blk):
    i = pl.program_id(1)
    heads = range(HEADS)
    qsl = lambda h: slice(h * QK_PAD, (h + 1) * QK_PAD)
    vsl = lambda h: slice(h * VT_ROWS, (h + 1) * VT_ROWS)
    osl = lambda h: slice(h * HEAD_DIM, (h + 1) * HEAD_DIM)
    q = [q_ref[:, qsl(h)] for h in heads]
    m_scr[...] = jnp.full_like(m_scr, NEG_BIG)
    acc_scr[...] = jnp.zeros_like(acc_scr)

    def scores(j):
        rows = pl.ds(pl.multiple_of(j * blk, blk), blk)
        return [_dot_nt(k_ref[rows, qsl(h)], q[h]) for h in heads]

    def accumulate(j, st, masked):
        if masked:
            r = lax.broadcasted_iota(jnp.int32, (blk, blk), 0)
            c = lax.broadcasted_iota(jnp.int32, (blk, blk), 1)
            keep = r <= c
            st = [jnp.where(keep, s, NEG_BIG) for s in st]
        for grp in ((0, 1), (2, 3)):
            m_old = [m_scr[h] for h in grp]
            m_new = [jnp.maximum(mo, jnp.max(st[h], axis=0, keepdims=True)) for mo, h in zip(m_old, grp)]
            alpha = [jnp.exp2(mo - mn) for mo, mn in zip(m_old, m_new)]
            p = [jnp.exp2(st[h] - mn).astype(BF16) for h, mn in zip(grp, m_new)]
            pv = [_dot(vt_ref[j, vsl(h), :], pp) for h, pp in zip(grp, p)]
            for n, h in enumerate(grp):
                m_scr[h] = m_new[n]
                acc_scr[h] = alpha[n] * acc_scr[h] + pv[n]

    def pair(t, carry):
        j = 2 * t
        st_a = scores(j)
        st_b = scores(j + 1)
        accumulate(j, st_a, False)
        accumulate(j + 1, st_b, False)
        return carry

    lax.fori_loop(0, i // 2, pair, 0)

    @pl.when(i % 2 == 1)
    def _():
        accumulate(i - 1, scores(i - 1), False)

    accumulate(i, scores(i), True)
    for h in heads:
        acc = acc_scr[h]
        o_ref[:, osl(h)] = (acc[:HEAD_DIM] / acc[HEAD_DIM:HEAD_DIM + 1]).T.astype(BF16)


def _attention(q, k, vt, B, S, blk):
    T = B * S
    nb = S // blk
    W = HEADS * HEAD_DIM
    HW = HEADS * QK_PAD
    return pl.pallas_call(
        functools.partial(_attn_kernel, blk=blk),
        out_shape=jax.ShapeDtypeStruct((T, W), BF16),
        grid=(B, nb),
        in_specs=[
            pl.BlockSpec((blk, HW), lambda b, i: (b * nb + i, 0)),
            pl.BlockSpec((S, HW), lambda b, i: (b, 0)),
            pl.BlockSpec((None, nb, HEADS * VT_ROWS, blk), lambda b, i: (b, 0, 0, 0)),
        ],
        out_specs=pl.BlockSpec((blk, W), lambda b, i: (b * nb + i, 0)),
        scratch_shapes=[pltpu.VMEM((HEADS, 1, blk), F32), pltpu.VMEM((HEADS, VT_ROWS, blk), F32)],
        compiler_params=_cparams(("parallel", "arbitrary")),
        name="causal_attention",
    )(q, k, vt)


def _sg_kernel(u_ref, v_ref, g_ref, ws_ref, b_ref, o_ref, *, rows):
    Tn = SG_CHUNK
    tril, _ = _tri_masks(Tn)
    g = g_ref[...]
    bias = b_ref[...]
    for gi in range(HEADS):
        hs = slice(gi * HEAD_DIM, (gi + 1) * HEAD_DIM)
        w = jnp.where(tril, ws_ref[gi], 0.0).astype(BF16)
        for n in range(rows // Tn):
            rs = slice(n * Tn, (n + 1) * Tn)
            v = _gelu_tanh(v_ref[rs, hs].astype(F32))
            v = v * lax.rsqrt(jnp.mean(v * v, axis=-1, keepdims=True) + NORM_EPS) * g[:, hs]
            mixed = _dot(w, v.astype(BF16)) + bias[:, hs]
            u = _gelu_tanh(u_ref[rs, hs].astype(F32))
            o_ref[rs, hs] = (u * mixed).astype(BF16)


def _spatial_gating(proj, norm_g, w_s, bias_full, layer, T, rows):
    W = HEADS * HEAD_DIM
    return pl.pallas_call(
        functools.partial(_sg_kernel, rows=rows),
        out_shape=jax.ShapeDtypeStruct((T, W), BF16),
        grid=(T // rows,),
        in_specs=[
            pl.BlockSpec((rows, W), lambda i: (i, PROJ_SG_U // W)),
            pl.BlockSpec((rows, W), lambda i: (i, PROJ_SG_V // W)),
            pl.BlockSpec((None, 1, W), lambda i: (layer, 0, 0)),
            pl.BlockSpec((None, HEADS, SG_CHUNK, SG_CHUNK), lambda i: (layer, 0, 0, 0)),
            pl.BlockSpec((None, SG_CHUNK, W), lambda i: (layer, 0, 0)),
        ],
        out_specs=pl.BlockSpec((rows, W), lambda i: (i, 0)),
        compiler_params=_cparams(("parallel",)),
        name="spatial_gating",
    )(proj, proj, norm_g, w_s, bias_full)


def _merge_kernel(oa, ob, oc, od, g0, g1, g2, g3, x_ref, wb_ref, wo_ref, out_ref):
    merged = None
    for i, (o, g) in enumerate(((oa, g0), (ob, g1), (oc, g2), (od, g3))):
        term = _sigmoid(g[...].astype(F32)) * _dot(o[...], wb_ref[i])
        merged = term if merged is None else merged + term
    out_ref[...] = x_ref[...] + _dot(merged.astype(BF16), wo_ref[...])


def _merge(branches, proj, x2, w_branch, w_out, layer, tm):
    T = x2.shape[0]
    ob = pl.BlockSpec((tm, BRANCH_WIDTH), lambda i: (i, 0))
    gate = lambda n: pl.BlockSpec((tm, D_MODEL), lambda i, n=n: (i, n))
    xs = pl.BlockSpec((tm, D_MODEL), lambda i: (i, 0))
    return pl.pallas_call(
        _merge_kernel,
        out_shape=jax.ShapeDtypeStruct((T, D_MODEL), F32),
        grid=(T // tm,),
        in_specs=[ob, ob, ob, ob, gate(0), gate(1), gate(2), gate(3), xs,
                  pl.BlockSpec((None, N_BRANCHES, BRANCH_WIDTH, D_MODEL), lambda i: (layer, 0, 0, 0)),
                  pl.BlockSpec((None, D_MODEL, D_MODEL), lambda i: (layer, 0, 0))],
        out_specs=xs,
        compiler_params=_cparams(("parallel",)),
        name="merge_out_proj",
    )(*branches, proj, proj, proj, proj, x2, w_branch, w_out)


def _ffn_kernel(x_ref, g_ref, w1_ref, w2_ref, out_ref, h_scr):
    f = pl.program_id(1)

    @pl.when(f == 0)
    def _():
        x = x_ref[...]
        ms = jnp.mean(x * x, axis=-1, keepdims=True)
        h_scr[...] = (x * lax.rsqrt(ms + NORM_EPS) * g_ref[...]).astype(BF16)
        out_ref[...] = x

    a = jnp.maximum(_dot(h_scr[...], w1_ref[...]), 0.0)
    out_ref[...] += _dot((a * a).astype(BF16), w2_ref[...])


def _ffn(x2, g, w1, w2, layer, tm, tf):
    T = x2.shape[0]
    xs = pl.BlockSpec((tm, D_MODEL), lambda i, f: (i, 0))
    return pl.pallas_call(
        _ffn_kernel,
        out_shape=jax.ShapeDtypeStruct((T, D_MODEL), F32),
        grid=(T // tm, D_FF // tf),
        in_specs=[xs,
                  pl.BlockSpec((None, 1, D_MODEL), lambda i, f: (layer, 0, 0)),
                  pl.BlockSpec((None, D_MODEL, tf), lambda i, f: (layer, 0, f)),
                  pl.BlockSpec((None, tf, D_MODEL), lambda i, f: (layer, f, 0))],
        out_specs=xs,
        scratch_shapes=[pltpu.VMEM((tm, D_MODEL), BF16)],
        compiler_params=_cparams(("parallel", "arbitrary")),
        name="relu2_mlp",
    )(x2, g, w1, w2)


def _pad_lanes(a, width):
    return jnp.pad(a, [(0, 0)] * (a.ndim - 1) + [(0, width - a.shape[-1])])


def _swap_halves(a):
    h = a.shape[-1] // 2
    return jnp.concatenate([a[..., h:], a[..., :h]], axis=-1)


def _prep_params(norm1_g, w_in, dn_a_log, dn_dt_bias, dn_out_norm_g, mla_q_norm_g, mla_kv_norm_g, mla_w_uq,
                 mla_w_ukv, mla_qk_q_g, mla_qk_k_g, sg_v_norm_g, sg_b_s, fox_q_norm_g, fox_k_norm_g, fox_f_bias,
                 w_branch, w_out, norm2_g, w_ff1, w_ff2):
    L = w_in.shape[0]
    col = lambda a, b: w_in[:, :, a:b]
    kr = col(O_MLA_KR, O_MLA_KR + MLA_ROPE)
    w_p = jnp.concatenate([
        col(O_GATES, O_END), col(O_DN_QKV, O_DN_Z), col(O_DN_Z, O_DN_A), col(O_SG_U, O_SG_V),
        col(O_SG_V, O_FOX_QKV), col(O_FOX_QKV, O_FOX_F), col(O_MLA_CQ, O_MLA_CKV), col(O_MLA_CKV, O_MLA_KR),
        kr, _swap_halves(kr)], axis=-1).astype(BF16)
    assert w_p.shape[-1] == PROJ_WIDTH
    w_small = _pad_lanes(jnp.concatenate([col(O_DN_A, O_DN_B), col(O_DN_B, O_MLA_CQ), col(O_FOX_F, O_GATES)],
                                         axis=-1), LANES).astype(BF16)

    def row(a, width=LANES, offset=0):
        a = a.reshape(L, 1, -1)
        return jnp.pad(a, ((0, 0), (0, 0), (offset, width - offset - a.shape[-1])))

    wq = mla_w_uq.reshape(L, MLA_Q_RANK, HEADS, MLA_QK_DIM)
    wq_r = wq[..., MLA_NOPE:]
    w_uq_p = jnp.concatenate([wq, _swap_halves(wq_r)], axis=-1).reshape(L, MLA_Q_RANK, HEADS * QK_PAD).astype(BF16)

    def qk_gain(g):
        return jnp.concatenate([g, _swap_halves(g[:, MLA_NOPE:])], axis=-1).reshape(L, 1, QK_PAD)

    bias_full = jnp.repeat(jnp.swapaxes(sg_b_s, 1, 2), HEAD_DIM, axis=-1)
    return dict(
        norm1_g=norm1_g.reshape(L, 1, D_MODEL), w_p=w_p, w_small=w_small,
        a_log=row(dn_a_log), dt=row(dn_dt_bias), dn_og=dn_out_norm_g.reshape(L, 1, HEAD_DIM),
        mla_qn=mla_q_norm_g.reshape(L, 1, MLA_Q_RANK), mla_kvn=mla_kv_norm_g.reshape(L, 1, MLA_KV_RANK),
        w_uq_p=w_uq_p, w_ukv=mla_w_ukv.astype(BF16), mla_gq=qk_gain(mla_qk_q_g), mla_gk=qk_gain(mla_qk_k_g),
        sg_g=sg_v_norm_g.reshape(L, 1, HEADS * HEAD_DIM), sg_bias=bias_full,
        fox_fb=row(fox_f_bias, offset=2 * HEADS), fox_gq=fox_q_norm_g.reshape(L, 1, HEAD_DIM),
        fox_gk=fox_k_norm_g.reshape(L, 1, HEAD_DIM),
        w_branch=w_branch.astype(BF16), w_out=w_out.astype(BF16), norm2_g=norm2_g.reshape(L, 1, D_MODEL),
        w_ff1=w_ff1.astype(BF16), w_ff2=w_ff2.astype(BF16),
    )


def _tiles(S):
    pick = lambda want: min(want, S)
    return dict(in_tm=pick(2048), in_tn=1024, dn_rows=pick(512), prep_tm=pick(512), attn_blk=pick(512),
                sg_rows=pick(512), merge_tm=pick(512), ffn_tm=pick(1024), ffn_tf=1024, rope_tm=pick(1024))


def kernel(x, positions, norm1_g, w_in, dn_conv_w, dn_a_log, dn_dt_bias, dn_out_norm_g, mla_q_norm_g, mla_kv_norm_g, mla_w_uq, mla_w_ukv, mla_qk_q_g, mla_qk_k_g, sg_v_norm_g, sg_w_s, sg_b_s, fox_q_norm_g, fox_k_norm_g, fox_f_bias, w_branch, w_out, norm2_g, w_ff1, w_ff2):
    B, S, D = x.shape
    assert D == D_MODEL and S % LANES == 0
    T = B * S
    depth = w_in.shape[0]
    t = _tiles(S)
    p = _prep_params(norm1_g, w_in, dn_a_log, dn_dt_bias, dn_out_norm_g, mla_q_norm_g, mla_kv_norm_g, mla_w_uq,
                     mla_w_ukv, mla_qk_q_g, mla_qk_k_g, sg_v_norm_g, sg_b_s, fox_q_norm_g, fox_k_norm_g,
                     fox_f_bias, w_branch, w_out, norm2_g, w_ff1, w_ff2)
    cos_t, sin_t = _rope_tables(positions, t["rope_tm"])
    x2 = x.reshape(T, D)
    for l in range(depth):
        proj, small = _in_proj(x2, p["norm1_g"], p["w_p"], p["w_small"], l, t["in_tm"], t["in_tn"])
        o_a = _deltanet(proj, small, dn_conv_w, p["a_log"], p["dt"], p["dn_og"], l, B, S, t["dn_rows"])
        qb, kb, vtb = _mla_prep(proj, cos_t, sin_t, p["mla_qn"], p["mla_kvn"], p["w_uq_p"], p["w_ukv"],
                                p["mla_gq"], p["mla_gk"], l, B, S, t["prep_tm"])
        o_b = _attention(qb, kb, vtb, B, S, t["attn_blk"])
        o_c = _spatial_gating(proj, p["sg_g"], sg_w_s, p["sg_bias"], l, T, t["sg_rows"])
        qd, kd, vtd = _fox_prep(proj, small, p["fox_fb"], p["fox_gq"], p["fox_gk"], l, B, S, t["prep_tm"])
        o_d = _attention(qd, kd, vtd, B, S, t["attn_blk"])
        x2 = _merge((o_a, o_b, o_c, o_d), proj, x2, p["w_branch"], p["w_out"], l, t["merge_tm"])
        x2 = _ffn(x2, p["norm2_g"], p["w_ff1"], p["w_ff2"], l, t["ffn_tm"], t["ffn_tf"])
    return x2.reshape(B, S, D)
```

```python
import functools
import math

import jax
import jax.numpy as jnp
from jax import lax
from jax.experimental import pallas as pl
from jax.experimental.pallas import tpu as pltpu

F32 = jnp.float32
BF16 = jnp.bfloat16

D_MODEL = 1024
NORM_EPS = 1e-6
N_BRANCHES = 4
BRANCH_WIDTH = 512
D_FF = 4 * D_MODEL
HEADS = 4
HEAD_DIM = 128
DN_CONV = 4
DN_CHUNK = 128
MLA_Q_RANK = 256
MLA_KV_RANK = 128
MLA_NOPE = 128
MLA_ROPE = 64
MLA_QK_DIM = MLA_NOPE + MLA_ROPE
ROPE_THETA = 10000.0
SG_CHUNK = 128
QK_PAD = 256
LANES = 128
NEG_BIG = -1e30
LOG2E = math.log2(math.e)

PROJ_GATES = 0
PROJ_DN_Q = 4096
PROJ_SG_U = 6144
PROJ_SG_V = 6656
PROJ_FOX_Q = 7168
PROJ_MLA_CQ = 8704
PROJ_MLA_CKV = 8960
PROJ_MLA_KR = 9088
PROJ_WIDTH = 9216

_SPLITS = (1536, 512, 4, 4, 256, 128, 64, 512, 512, 1536, 4, 4096)
_OFF = [0]
for _s in _SPLITS:
    _OFF.append(_OFF[-1] + _s)
(O_DN_QKV, O_DN_Z, O_DN_A, O_DN_B, O_MLA_CQ, O_MLA_CKV, O_MLA_KR, O_SG_U, O_SG_V, O_FOX_QKV, O_FOX_F,
 O_GATES, O_END) = _OFF

VMEM_LIMIT = 56 * 1024 * 1024


def _cparams(sem):
    return pltpu.CompilerParams(dimension_semantics=sem, vmem_limit_bytes=VMEM_LIMIT)


def _dot(a, b):
    return jnp.dot(a, b, preferred_element_type=F32)


def _dot_nt(a, b):
    return lax.dot_general(a, b, (((1,), (1,)), ((), ())), preferred_element_type=F32)


def _dot_f32(a, b):
    return jnp.dot(a, b, preferred_element_type=F32, precision=lax.Precision.HIGHEST)


def _sigmoid(x):
    return 1.0 / (1.0 + jnp.exp(-x))


def _softplus(x):
    return jnp.maximum(x, 0.0) + jnp.log1p(jnp.exp(-jnp.abs(x)))


def _gelu_tanh(x):
    return 0.5 * x * (1.0 + jnp.tanh(math.sqrt(2.0 / math.pi) * (x + 0.044715 * (x * x * x))))


def _tri_masks(n):
    r = lax.broadcasted_iota(jnp.int32, (n, n), 0)
    c = lax.broadcasted_iota(jnp.int32, (n, n), 1)
    return r >= c, r > c


def _rope_kernel(pos_ref, freq_ref, sign_ref, cos_ref, sin_ref):
    ang = pos_ref[...].astype(F32) * freq_ref[...]
    lane = lax.broadcasted_iota(jnp.int32, ang.shape, 1)
    live = lane < MLA_ROPE
    cos_ref[...] = jnp.where(live, jnp.cos(ang), 0.0)
    sin_ref[...] = jnp.where(live, jnp.sin(ang) * sign_ref[...], 0.0)


def _rope_tables(positions, tm):
    T = positions.size
    half = MLA_ROPE // 2
    inv_freq = ROPE_THETA ** (-jnp.arange(0, MLA_ROPE, 2, dtype=F32) / MLA_ROPE)
    zeros = jnp.zeros((LANES - MLA_ROPE,), F32)
    freq = jnp.concatenate([inv_freq, inv_freq, zeros]).reshape(1, LANES)
    sign = jnp.concatenate([-jnp.ones((half,), F32), jnp.ones((half,), F32), zeros]).reshape(1, LANES)
    row = pl.BlockSpec((1, LANES), lambda i: (0, 0))
    tab = pl.BlockSpec((tm, LANES), lambda i: (i, 0))
    return pl.pallas_call(
        _rope_kernel,
        out_shape=(jax.ShapeDtypeStruct((T, LANES), F32), jax.ShapeDtypeStruct((T, LANES), F32)),
        grid=(T // tm,),
        in_specs=[pl.BlockSpec((tm, 1), lambda i: (i, 0)), row, row],
        out_specs=(tab, tab),
        compiler_params=_cparams(("parallel",)),
        name="rope_tables",
    )(positions.reshape(T, 1), freq, sign)


def _in_proj_kernel(x_ref, g_ref, w_ref, ws_ref, proj_ref, small_ref, h_scr):
    @pl.when(pl.program_id(1) == 0)
    def _():
        x = x_ref[...]
        ms = jnp.mean(x * x, axis=-1, keepdims=True)
        h = (x * lax.rsqrt(ms + NORM_EPS) * g_ref[...]).astype(BF16)
        h_scr[...] = h
        small_ref[...] = _dot(h, ws_ref[...])

    proj_ref[...] = _dot(h_scr[...], w_ref[...]).astype(BF16)


def _in_proj(x2, g, w_p, w_small, layer, tm, tn):
    T = x2.shape[0]
    return pl.pallas_call(
        _in_proj_kernel,
        out_shape=(jax.ShapeDtypeStruct((T, PROJ_WIDTH), BF16), jax.ShapeDtypeStruct((T, LANES), F32)),
        grid=(T // tm, PROJ_WIDTH // tn),
        in_specs=[
            pl.BlockSpec((tm, D_MODEL), lambda i, j: (i, 0)),
            pl.BlockSpec((None, 1, D_MODEL), lambda i, j: (layer, 0, 0)),
            pl.BlockSpec((None, D_MODEL, tn), lambda i, j: (layer, 0, j)),
            pl.BlockSpec((None, D_MODEL, LANES), lambda i, j: (layer, 0, 0)),
        ],
        out_specs=(pl.BlockSpec((tm, tn), lambda i, j: (i, j)),
                   pl.BlockSpec((tm, LANES), lambda i, j: (i, 0))),
        scratch_shapes=[pltpu.VMEM((tm, D_MODEL), BF16)],
        compiler_params=_cparams(("parallel", "arbitrary")),
        name="in_proj",
    )(x2, g, w_p, w_small)


INV_BASE = 8


def _inverse_masks(n):
    r = lax.broadcasted_iota(jnp.int32, (n, n), 0)
    c = lax.broadcasted_iota(jnp.int32, (n, n), 1)
    same = lambda s: (r // s) == (c // s)
    levels = []
    s = INV_BASE
    while s < n:
        levels.append(same(2 * s) & jnp.logical_not(same(s)))
        s *= 2
    as16 = lambda m: jnp.where(m, 1.0, 0.0).astype(BF16)
    return (r == c).astype(F32), as16(same(INV_BASE)), [as16(m) for m in levels]


def _unit_lower_inverse(nmats, masks):
    eye, base, levels = masks
    n16 = [n.astype(BF16) for n in nmats]
    ps = [n * base for n in n16]
    xs = [eye - p.astype(F32) for p in ps]
    for _ in range(int(math.log2(INV_BASE)) - 1):
        ps = [_dot(p, p).astype(BF16) for p in ps]
        xs = [x + _dot(x.astype(BF16), p) for x, p in zip(xs, ps)]
    for off in levels:
        x16 = [x.astype(BF16) for x in xs]
        ts = [_dot(n * off, xb).astype(BF16) for n, xb in zip(n16, x16)]
        xs = [x - _dot(xb, t) for x, xb, t in zip(xs, x16, ts)]
    return xs


def _deltanet_kernel(q_ref, k_ref, v_ref, z_ref, small_ref, cw_ref, alog_ref, dt_ref, og_ref, o_ref,
                     qbuf, kbuf, vbuf, qs, ks, vs, state, *, rows):
    C = DN_CHUNK
    W = HEADS * HEAD_DIM
    NC = rows // C

    @pl.when(pl.program_id(1) == 0)
    def _():
        state[...] = jnp.zeros_like(state)
        for buf in (qbuf, kbuf, vbuf):
            buf[0:8, :] = jnp.zeros((8, W), F32)

    def conv_silu(buf, x_ref, w, dst):
        buf[8:8 + rows, :] = x_ref[...].astype(F32)
        acc = buf[8:8 + rows, :] * w[DN_CONV - 1:DN_CONV, :]
        for s in range(1, DN_CONV):
            acc = acc + buf[8 - s:8 - s + rows, :] * w[DN_CONV - 1 - s:DN_CONV - s, :]
        buf[0:8, :] = buf[rows:rows + 8, :]
        dst[...] = acc * _sigmoid(acc)

    cw = cw_ref[...]
    conv_silu(qbuf, q_ref, cw[:, 0:W], qs)
    conv_silu(kbuf, k_ref, cw[:, W:2 * W], ks)
    conv_silu(vbuf, v_ref, cw[:, 2 * W:3 * W], vs)

    small = small_ref[...]
    g_all = -jnp.exp(alog_ref[...]) * _softplus(small + dt_ref[...])
    beta_all = _sigmoid(small)

    tril, strict = _tri_masks(C)
    tril_f = tril.astype(F32)
    inv_masks = _inverse_masks(C)
    og = og_ref[...]

    probs = [(c, h) for c in range(NC) for h in range(HEADS)]
    gcs = [_dot_f32(tril_f, g_all[c * C:(c + 1) * C, :]) for c in range(NC)]
    gcts = [gc.T for gc in gcs]
    rsl = lambda c: slice(c * C, (c + 1) * C)
    hsl = lambda h: slice(h * HEAD_DIM, (h + 1) * HEAD_DIM)
    qn, kn = [], []
    for c, h in probs:
        q = qs[rsl(c), hsl(h)]
        k = ks[rsl(c), hsl(h)]
        qn.append(q * (lax.rsqrt(jnp.sum(q * q, axis=-1, keepdims=True) + NORM_EPS) * (HEAD_DIM ** -0.5)))
        kn.append(k * lax.rsqrt(jnp.sum(k * k, axis=-1, keepdims=True) + NORM_EPS))
    gcol = [gcs[c][:, h:h + 1] for c, h in probs]
    bcol = [beta_all[rsl(c), HEADS + h:HEADS + h + 1] for c, h in probs]
    glast = [gcs[c][C - 1:C, h:h + 1] for c, h in probs]
    decay = [jnp.where(tril, jnp.exp(jnp.where(tril, gcol[i] - gcts[c][h:h + 1, :], 0.0)), 0.0)
             for i, (c, h) in enumerate(probs)]
    kb = [k * b for k, b in zip(kn, bcol)]
    k16 = [k.astype(BF16) for k in kn]
    qk_kk = [_dot_nt(jnp.concatenate([kbi, q], axis=0).astype(BF16), k) for kbi, q, k in zip(kb, qn, k16)]
    nmat = [jnp.where(strict, m[:C] * d, 0.0) for m, d in zip(qk_kk, decay)]
    a_qk = [(m[C:] * d).astype(BF16) for m, d in zip(qk_kk, decay)]
    eg = [jnp.exp(g) for g in gcol]
    rhs = [jnp.concatenate([vs[rsl(c), hsl(h)] * bcol[i], kb[i] * eg[i]], axis=-1).astype(BF16)
           for i, (c, h) in enumerate(probs)]
    xinv = _unit_lower_inverse(nmat, inv_masks)
    sol = [_dot(x.astype(BF16), r) for x, r in zip(xinv, rhs)]
    u = [s[:, :HEAD_DIM] for s in sol]
    wq = [jnp.concatenate([s[:, HEAD_DIM:], q * e], axis=0).astype(BF16) for s, q, e in zip(sol, qn, eg)]
    kdt = [(k * jnp.exp(gl - g)).T.astype(BF16) for k, gl, g in zip(kn, glast, gcol)]
    egl = [jnp.exp(gl) for gl in glast]

    st = [state[h] for h in range(HEADS)]
    for c in range(NC):
        idx = [c * HEADS + h for h in range(HEADS)]
        r = [_dot(wq[i], st[h].astype(BF16)) for h, i in enumerate(idx)]
        v16 = [(u[i] - r[h][:C]).astype(BF16) for h, i in enumerate(idx)]
        o = [r[h][C:] + _dot(a_qk[i], v16[h]) for h, i in enumerate(idx)]
        st = [st[h] * egl[i] + _dot(kdt[i], v16[h]) for h, i in enumerate(idx)]
        for h in range(HEADS):
            oh = o[h] * lax.rsqrt(jnp.mean(o[h] * o[h], axis=-1, keepdims=True) + NORM_EPS) * og
            z = z_ref[rsl(c), hsl(h)].astype(F32)
            o_ref[rsl(c), hsl(h)] = (oh * (z * _sigmoid(z))).astype(BF16)
    for h in range(HEADS):
        state[h] = st[h]


def _deltanet(proj, small, conv_w, a_log_row, dt_row, out_g, layer, B, S, rows):
    T = B * S
    W = HEADS * HEAD_DIM
    nb = S // rows
    blk = lambda col: pl.BlockSpec((rows, W), lambda b, i, col=col: (b * nb + i, col))
    prow = pl.BlockSpec((None, 1, LANES), lambda b, i: (layer, 0, 0))
    return pl.pallas_call(
        functools.partial(_deltanet_kernel, rows=rows),
        out_shape=jax.ShapeDtypeStruct((T, W), BF16),
        grid=(B, nb),
        in_specs=[
            blk(PROJ_DN_Q // W), blk(PROJ_DN_Q // W + 1), blk(PROJ_DN_Q // W + 2), blk(PROJ_DN_Q // W + 3),
            pl.BlockSpec((rows, LANES), lambda b, i: (b * nb + i, 0)),
            pl.BlockSpec((None, DN_CONV, 3 * W), lambda b, i: (layer, 0, 0)),
            prow, prow, prow,
        ],
        out_specs=pl.BlockSpec((rows, W), lambda b, i: (b * nb + i, 0)),
        scratch_shapes=[pltpu.VMEM((rows + 8, W), F32)] * 3 + [pltpu.VMEM((rows, W), F32)] * 3
        + [pltpu.VMEM((HEADS, HEAD_DIM, HEAD_DIM), F32)],
        compiler_params=_cparams(("parallel", "arbitrary")),
        name="deltanet",
    )(proj, proj, proj, proj, small, conv_w, a_log_row, dt_row, out_g)


VT_ONES = 16
VT_ROWS = HEAD_DIM + VT_ONES


def _store_vt(vt_out, h, v16):
    r = lax.broadcasted_iota(jnp.int32, (HEAD_DIM, HEAD_DIM), 0)
    c = lax.broadcasted_iota(jnp.int32, (HEAD_DIM, HEAD_DIM), 1)
    eye = jnp.where(r == c, 1.0, 0.0).astype(BF16)
    base = h * VT_ROWS
    vt_out[base:base + HEAD_DIM, :] = _dot_nt(eye, v16).astype(BF16)
    vt_out[base + HEAD_DIM:base + VT_ROWS, :] = jnp.ones((VT_ONES, v16.shape[0]), BF16)


def _mla_prep_kernel(cq_ref, ckv_ref, kr_ref, cos_ref, sin_ref, qn_ref, kvn_ref, wq_ref, wkv_ref,
                     gq_ref, gk_ref, off_ref, q_out, k_out, vt_out):
    cq = cq_ref[...].astype(F32)
    ckv = ckv_ref[...].astype(F32)
    cq = cq * lax.rsqrt(jnp.mean(cq * cq, axis=-1, keepdims=True) + NORM_EPS) * qn_ref[...]
    ckv = ckv * lax.rsqrt(jnp.mean(ckv * ckv, axis=-1, keepdims=True) + NORM_EPS) * kvn_ref[...]
    q_all = _dot(cq.astype(BF16), wq_ref[...])
    kv_all = _dot(ckv.astype(BF16), wkv_ref[...])
    cos = cos_ref[...]
    sin = sin_ref[...]
    gq = gq_ref[...]
    gk = gk_ref[...]
    lane = lax.broadcasted_iota(jnp.int32, cos.shape, 1)
    live = lane < MLA_ROPE
    scale = MLA_QK_DIM ** -0.5 * LOG2E
    off = off_ref[...]

    def rope(xr):
        return xr * cos + pltpu.roll(xr, MLA_ROPE, 1) * sin

    kr = kr_ref[...].astype(F32)
    kr_ss = jnp.sum(jnp.where(live, kr * kr, 0.0), axis=-1, keepdims=True)
    for h in range(HEADS):
        base = h * QK_PAD
        qn = q_all[:, base:base + MLA_NOPE]
        qr = q_all[:, base + MLA_NOPE:base + QK_PAD]
        ss = jnp.sum(qn * qn, axis=-1, keepdims=True) + jnp.sum(jnp.where(live, qr * qr, 0.0), axis=-1, keepdims=True)
        rinv = lax.rsqrt(ss * (1.0 / MLA_QK_DIM) + NORM_EPS) * scale
        q_out[:, base:base + MLA_NOPE] = (qn * rinv * gq[:, :MLA_NOPE]).astype(BF16)
        q_rot = jnp.where(lane == MLA_ROPE, 1.0, rope(qr * rinv * gq[:, MLA_NOPE:]))
        q_out[:, base + MLA_NOPE:base + QK_PAD] = q_rot.astype(BF16)

        kn = kv_all[:, base:base + MLA_NOPE]
        v = kv_all[:, base + MLA_NOPE:base + QK_PAD]
        ssk = jnp.sum(kn * kn, axis=-1, keepdims=True) + kr_ss
        rk = lax.rsqrt(ssk * (1.0 / MLA_QK_DIM) + NORM_EPS)
        k_out[:, base:base + MLA_NOPE] = (kn * rk * gk[:, :MLA_NOPE]).astype(BF16)
        k_rot = jnp.where(lane == MLA_ROPE, off, rope(kr * rk * gk[:, MLA_NOPE:]))
        k_out[:, base + MLA_NOPE:base + QK_PAD] = k_rot.astype(BF16)
        _store_vt(vt_out, h, v.astype(BF16))


def _mla_prep(proj, cos_t, sin_t, qn_g, kvn_g, w_uq_p, w_ukv, gq_p, gk_p, off_row, layer, B, S, tm):
    T = B * S
    nb = S // tm
    HW = HEADS * QK_PAD
    tab = pl.BlockSpec((tm, LANES), lambda i: (i, 0))
    return pl.pallas_call(
        _mla_prep_kernel,
        out_shape=(jax.ShapeDtypeStruct((T, HW), BF16), jax.ShapeDtypeStruct((T, HW), BF16),
                   jax.ShapeDtypeStruct((B, nb, HEADS * VT_ROWS, tm), BF16)),
        grid=(T // tm,),
        in_specs=[
            pl.BlockSpec((tm, MLA_Q_RANK), lambda i: (i, PROJ_MLA_CQ // MLA_Q_RANK)),
            pl.BlockSpec((tm, MLA_KV_RANK), lambda i: (i, PROJ_MLA_CKV // MLA_KV_RANK)),
            pl.BlockSpec((tm, LANES), lambda i: (i, PROJ_MLA_KR // LANES)),
            tab, tab,
            pl.BlockSpec((None, 1, MLA_Q_RANK), lambda i: (layer, 0, 0)),
            pl.BlockSpec((None, 1, MLA_KV_RANK), lambda i: (layer, 0, 0)),
            pl.BlockSpec((None, MLA_Q_RANK, HW), lambda i: (layer, 0, 0)),
            pl.BlockSpec((None, MLA_KV_RANK, HW), lambda i: (layer, 0, 0)),
            pl.BlockSpec((None, 1, QK_PAD), lambda i: (layer, 0, 0)),
            pl.BlockSpec((None, 1, QK_PAD), lambda i: (layer, 0, 0)),
            pl.BlockSpec((None, 1, LANES), lambda i: (layer, 0, 0)),
        ],
        out_specs=(pl.BlockSpec((tm, HW), lambda i: (i, 0)), pl.BlockSpec((tm, HW), lambda i: (i, 0)),
                   pl.BlockSpec((None, None, HEADS * VT_ROWS, tm), lambda i: (i // nb, i % nb, 0, 0))),
        compiler_params=_cparams(("parallel",)),
        name="mla_prep",
    )(proj, proj, proj, cos_t, sin_t, qn_g, kvn_g, w_uq_p, w_ukv, gq_p, gk_p, off_row)


def _split3(c):
    hi = c.astype(BF16).astype(F32)
    r = c - hi
    mid = r.astype(BF16).astype(F32)
    lo = (r - mid).astype(BF16).astype(F32)
    return hi, mid, lo


def _fox_prep_kernel(q_ref, k_ref, v_ref, small_ref, fb_ref, gq_ref, gk_ref, off_ref, q_out, k_out, vt_out, carry,
                     *, tm):
    @pl.when(pl.program_id(1) == 0)
    def _():
        carry[...] = jnp.zeros_like(carry)

    C = LANES
    tril, _ = _tri_masks(C)
    tril_f = tril.astype(F32)
    logf = -_softplus(-(small_ref[...] + fb_ref[...]))
    gq = gq_ref[...]
    gk = gk_ref[...]
    off = off_ref[...]
    scale = HEAD_DIM ** -0.5 * LOG2E
    lane = lax.broadcasted_iota(jnp.int32, (C, LANES), 1)
    for c in range(tm // C):
        rs = slice(c * C, (c + 1) * C)
        cum = _dot_f32(tril_f, logf[rs, :]) + carry[...]
        carry[...] = cum[C - 1:C, :]
        for h in range(HEADS):
            hs = slice(h * HEAD_DIM, (h + 1) * HEAD_DIM)
            q = q_ref[rs, hs].astype(F32)
            k = k_ref[rs, hs].astype(F32)
            q = q * lax.rsqrt(jnp.mean(q * q, axis=-1, keepdims=True) + NORM_EPS) * gq * scale
            k = k * lax.rsqrt(jnp.mean(k * k, axis=-1, keepdims=True) + NORM_EPS) * gk
            hi, mid, lo = _split3(cum[:, 2 * HEADS + h:2 * HEADS + h + 1] * LOG2E)
            one = jnp.where(lane < 7, 1.0, 0.0)
            parts = jnp.where(lane == 0, hi, jnp.where(lane == 1, mid, jnp.where(lane == 2, lo, 0.0)))
            q_aug = jnp.where(lane < 3, parts, one)
            nparts = jnp.where(lane == 3, -hi, jnp.where(lane == 4, -mid, jnp.where(lane == 5, -lo,
                                                                                     jnp.where(lane == 6, off, 0.0))))
            k_aug = jnp.where(lane < 3, one, nparts)
            base = h * QK_PAD
            q_out[rs, base:base + HEAD_DIM] = q.astype(BF16)
            q_out[rs, base + HEAD_DIM:base + QK_PAD] = q_aug.astype(BF16)
            k_out[rs, base:base + HEAD_DIM] = k.astype(BF16)
            k_out[rs, base + HEAD_DIM:base + QK_PAD] = k_aug.astype(BF16)
    for h in range(HEADS):
        _store_vt(vt_out, h, v_ref[:, h * HEAD_DIM:(h + 1) * HEAD_DIM])


def _fox_prep(proj, small, f_bias_row, gq, gk, off_row, layer, B, S, tm):
    T = B * S
    nb = S // tm
    W = HEADS * HEAD_DIM
    HW = HEADS * QK_PAD
    blk = lambda col: pl.BlockSpec((tm, W), lambda b, i, col=col: (b * nb + i, col))
    prow = pl.BlockSpec((None, 1, LANES), lambda b, i: (layer, 0, 0))
    return pl.pallas_call(
        functools.partial(_fox_prep_kernel, tm=tm),
        out_shape=(jax.ShapeDtypeStruct((T, HW), BF16), jax.ShapeDtypeStruct((T, HW), BF16),
                   jax.ShapeDtypeStruct((B, nb, HEADS * VT_ROWS, tm), BF16)),
        grid=(B, nb),
        in_specs=[blk(PROJ_FOX_Q // W), blk(PROJ_FOX_Q // W + 1), blk(PROJ_FOX_Q // W + 2),
                  pl.BlockSpec((tm, LANES), lambda b, i: (b * nb + i, 0)), prow, prow, prow, prow],
        out_specs=(pl.BlockSpec((tm, HW), lambda b, i: (b * nb + i, 0)),
                   pl.BlockSpec((tm, HW), lambda b, i: (b * nb + i, 0)),
                   pl.BlockSpec((None, None, HEADS * VT_ROWS, tm), lambda b, i: (b, i, 0, 0))),
        scratch_shapes=[pltpu.VMEM((1, LANES), F32)],
        compiler_params=_cparams(("parallel", "arbitrary")),
        name="fox_prep",
    )(proj, proj, proj, small, f_bias_row, gq, gk, off_row)


def _attn_kernel(q_ref, k_ref, vt_ref, o_ref, m_scr, acc_scr, *, blk):
    i = pl.program_id(1)
    heads = range(HEADS)
    qsl = lambda h: slice(h * QK_PAD, (h + 1) * QK_PAD)
    vsl = lambda h: slice(h * VT_ROWS, (h + 1) * VT_ROWS)
    osl = lambda h: slice(h * HEAD_DIM, (h + 1) * HEAD_DIM)
    q = [q_ref[:, qsl(h)] for h in heads]
    m_scr[...] = jnp.full_like(m_scr, NEG_BIG)
    acc_scr[...] = jnp.zeros_like(acc_scr)

    def scores(j):
        rows = pl.ds(pl.multiple_of(j * blk, blk), blk)
        return [_dot_nt(k_ref[rows, qsl(h)], q[h]) for h in heads]

    def accumulate(j, st, masked):
        if masked:
            r = lax.broadcasted_iota(jnp.int32, (blk, blk), 0)
            c = lax.broadcasted_iota(jnp.int32, (blk, blk), 1)
            keep = r <= c
            st = [jnp.where(keep, s, NEG_BIG) for s in st]
        for grp in ((0, 1), (2, 3)):
            m_old = [m_scr[h] for h in grp]
            m_new = [jnp.maximum(mo, jnp.max(st[h], axis=0, keepdims=True)) for mo, h in zip(m_old, grp)]
            alpha = [jnp.exp2(mo - mn) for mo, mn in zip(m_old, m_new)]
            p = [jnp.exp2(st[h] - mn).astype(BF16) for h, mn in zip(grp, m_new)]
            pv = [_dot(vt_ref[j, vsl(h), :], pp) for h, pp in zip(grp, p)]
            for n, h in enumerate(grp):
                m_scr[h] = m_new[n]
                acc_scr[h] = alpha[n] * acc_scr[h] + pv[n]

    def pair(t, carry):
        j = 2 * t
        st_a = scores(j)
        st_b = scores(j + 1)
        accumulate(j, st_a, False)
        accumulate(j + 1, st_b, False)
        return carry

    lax.fori_loop(0, i // 2, pair, 0)

    @pl.when(i % 2 == 1)
    def _():
        accumulate(i - 1, scores(i - 1), False)

    accumulate(i, scores(i), True)
    for h in heads:
        acc = acc_scr[h]
        o_ref[:, osl(h)] = (acc[:HEAD_DIM] / acc[HEAD_DIM:HEAD_DIM + 1]).T.astype(BF16)


def _attn_shifted_kernel(q_ref, k_ref, vt_ref, o_ref, acc_scr, *, blk):
    i = pl.program_id(1)
    heads = range(HEADS)
    qsl = lambda h: slice(h * QK_PAD, (h + 1) * QK_PAD)
    vsl = lambda h: slice(h * VT_ROWS, (h + 1) * VT_ROWS)
    osl = lambda h: slice(h * HEAD_DIM, (h + 1) * HEAD_DIM)
    q = [q_ref[:, qsl(h)] for h in heads]
    acc_scr[...] = jnp.zeros_like(acc_scr)

    def block(j, masked):
        rows = pl.ds(pl.multiple_of(j * blk, blk), blk)
        st = [_dot_nt(k_ref[rows, qsl(h)], q[h]) for h in heads]
        if masked:
            r = lax.broadcasted_iota(jnp.int32, (blk, blk), 0)
            c = lax.broadcasted_iota(jnp.int32, (blk, blk), 1)
            keep = r <= c
            st = [jnp.where(keep, s, NEG_BIG) for s in st]
        pv = [_dot(vt_ref[j, vsl(h), :], jnp.exp2(st[h]).astype(BF16)) for h in heads]
        for h in heads:
            acc_scr[h] += pv[h]

    def pair(t, carry):
        block(2 * t, False)
        block(2 * t + 1, False)
        return carry

    lax.fori_loop(0, i // 2, pair, 0)

    @pl.when(i % 2 == 1)
    def _():
        block(i - 1, False)

    block(i, True)
    for h in heads:
        acc = acc_scr[h]
        o_ref[:, osl(h)] = (acc[:HEAD_DIM] / acc[HEAD_DIM:HEAD_DIM + 1]).T.astype(BF16)


def _attention(q, k, vt, shifted, B, S, blk):
    T = B * S
    nb = S // blk
    W = HEADS * HEAD_DIM
    HW = HEADS * QK_PAD
    common = dict(
        out_shape=jax.ShapeDtypeStruct((T, W), BF16),
        grid=(B, nb),
        in_specs=[
            pl.BlockSpec((blk, HW), lambda b, i: (b * nb + i, 0)),
            pl.BlockSpec((S, HW), lambda b, i: (b, 0)),
            pl.BlockSpec((None, nb, HEADS * VT_ROWS, blk), lambda b, i: (b, 0, 0, 0)),
        ],
        out_specs=pl.BlockSpec((blk, W), lambda b, i: (b * nb + i, 0)),
        compiler_params=_cparams(("parallel", "arbitrary")),
    )
    acc = pltpu.VMEM((HEADS, VT_ROWS, blk), F32)
    online = pl.pallas_call(functools.partial(_attn_kernel, blk=blk), name="causal_attention",
                            scratch_shapes=[pltpu.VMEM((HEADS, 1, blk), F32), acc], **common)
    fast = pl.pallas_call(functools.partial(_attn_shifted_kernel, blk=blk), name="causal_attention_shifted",
                          scratch_shapes=[acc], **common)
    return lax.cond(shifted, fast, online, q, k, vt)


SHIFT_MAX = 40.0


def _logit_shift(gq, gk, dim):
    L = gq.shape[0]
    bound = (dim ** 0.5) * LOG2E * 1.02 * jnp.max(jnp.abs(gq.reshape(L, -1)), axis=-1) \
        * jnp.max(jnp.abs(gk.reshape(L, -1)), axis=-1)
    ok = bound <= SHIFT_MAX
    shift = jnp.where(ok, -bound, 0.0).astype(F32)
    return jnp.broadcast_to(shift[:, None, None], (L, 1, LANES)), ok


def _sg_kernel(u_ref, v_ref, g_ref, ws_ref, b_ref, o_ref, *, rows):
    Tn = SG_CHUNK
    tril, _ = _tri_masks(Tn)
    g = g_ref[...]
    bias = b_ref[...]
    for gi in range(HEADS):
        hs = slice(gi * HEAD_DIM, (gi + 1) * HEAD_DIM)
        w = jnp.where(tril, ws_ref[gi], 0.0).astype(BF16)
        for n in range(rows // Tn):
            rs = slice(n * Tn, (n + 1) * Tn)
            v = _gelu_tanh(v_ref[rs, hs].astype(F32))
            v = v * lax.rsqrt(jnp.mean(v * v, axis=-1, keepdims=True) + NORM_EPS) * g[:, hs]
            mixed = _dot(w, v.astype(BF16)) + bias[:, hs]
            u = _gelu_tanh(u_ref[rs, hs].astype(F32))
            o_ref[rs, hs] = (u * mixed).astype(BF16)


def _spatial_gating(proj, norm_g, w_s, bias_full, layer, T, rows):
    W = HEADS * HEAD_DIM
    return pl.pallas_call(
        functools.partial(_sg_kernel, rows=rows),
        out_shape=jax.ShapeDtypeStruct((T, W), BF16),
        grid=(T // rows,),
        in_specs=[
            pl.BlockSpec((rows, W), lambda i: (i, PROJ_SG_U // W)),
            pl.BlockSpec((rows, W), lambda i: (i, PROJ_SG_V // W)),
            pl.BlockSpec((None, 1, W), lambda i: (layer, 0, 0)),
            pl.BlockSpec((None, HEADS, SG_CHUNK, SG_CHUNK), lambda i: (layer, 0, 0, 0)),
            pl.BlockSpec((None, SG_CHUNK, W), lambda i: (layer, 0, 0)),
        ],
        out_specs=pl.BlockSpec((rows, W), lambda i: (i, 0)),
        compiler_params=_cparams(("parallel",)),
        name="spatial_gating",
    )(proj, proj, norm_g, w_s, bias_full)


def _merge_kernel(oa, ob, oc, od, g0, g1, g2, g3, x_ref, wb_ref, wo_ref, out_ref):
    merged = None
    for i, (o, g) in enumerate(((oa, g0), (ob, g1), (oc, g2), (od, g3))):
        term = _sigmoid(g[...].astype(F32)) * _dot(o[...], wb_ref[i])
        merged = term if merged is None else merged + term
    out_ref[...] = x_ref[...] + _dot(merged.astype(BF16), wo_ref[...])


def _merge(branches, proj, x2, w_branch, w_out, layer, tm):
    T = x2.shape[0]
    ob = pl.BlockSpec((tm, BRANCH_WIDTH), lambda i: (i, 0))
    gate = lambda n: pl.BlockSpec((tm, D_MODEL), lambda i, n=n: (i, n))
    xs = pl.BlockSpec((tm, D_MODEL), lambda i: (i, 0))
    return pl.pallas_call(
        _merge_kernel,
        out_shape=jax.ShapeDtypeStruct((T, D_MODEL), F32),
        grid=(T // tm,),
        in_specs=[ob, ob, ob, ob, gate(0), gate(1), gate(2), gate(3), xs,
                  pl.BlockSpec((None, N_BRANCHES, BRANCH_WIDTH, D_MODEL), lambda i: (layer, 0, 0, 0)),
                  pl.BlockSpec((None, D_MODEL, D_MODEL), lambda i: (layer, 0, 0))],
        out_specs=xs,
        compiler_params=_cparams(("parallel",)),
        name="merge_out_proj",
    )(*branches, proj, proj, proj, proj, x2, w_branch, w_out)


def _ffn_kernel(x_ref, g_ref, w1_ref, w2_ref, out_ref, h_scr):
    f = pl.program_id(1)

    @pl.when(f == 0)
    def _():
        x = x_ref[...]
        ms = jnp.mean(x * x, axis=-1, keepdims=True)
        h_scr[...] = (x * lax.rsqrt(ms + NORM_EPS) * g_ref[...]).astype(BF16)
        out_ref[...] = x

    a = jnp.maximum(_dot(h_scr[...], w1_ref[...]), 0.0)
    out_ref[...] += _dot((a * a).astype(BF16), w2_ref[...])


def _ffn(x2, g, w1, w2, layer, tm, tf):
    T = x2.shape[0]
    xs = pl.BlockSpec((tm, D_MODEL), lambda i, f: (i, 0))
    return pl.pallas_call(
        _ffn_kernel,
        out_shape=jax.ShapeDtypeStruct((T, D_MODEL), F32),
        grid=(T // tm, D_FF // tf),
        in_specs=[xs,
                  pl.BlockSpec((None, 1, D_MODEL), lambda i, f: (layer, 0, 0)),
                  pl.BlockSpec((None, D_MODEL, tf), lambda i, f: (layer, 0, f)),
                  pl.BlockSpec((None, tf, D_MODEL), lambda i, f: (layer, f, 0))],
        out_specs=xs,
        scratch_shapes=[pltpu.VMEM((tm, D_MODEL), BF16)],
        compiler_params=_cparams(("parallel", "arbitrary")),
        name="relu2_mlp",
    )(x2, g, w1, w2)


def _pad_lanes(a, width):
    return jnp.pad(a, [(0, 0)] * (a.ndim - 1) + [(0, width - a.shape[-1])])


def _swap_halves(a):
    h = a.shape[-1] // 2
    return jnp.concatenate([a[..., h:], a[..., :h]], axis=-1)


def _prep_params(norm1_g, w_in, dn_a_log, dn_dt_bias, dn_out_norm_g, mla_q_norm_g, mla_kv_norm_g, mla_w_uq,
                 mla_w_ukv, mla_qk_q_g, mla_qk_k_g, sg_v_norm_g, sg_b_s, fox_q_norm_g, fox_k_norm_g, fox_f_bias,
                 w_branch, w_out, norm2_g, w_ff1, w_ff2):
    L = w_in.shape[0]
    col = lambda a, b: w_in[:, :, a:b]
    kr = col(O_MLA_KR, O_MLA_KR + MLA_ROPE)
    w_p = jnp.concatenate([
        col(O_GATES, O_END), col(O_DN_QKV, O_DN_Z), col(O_DN_Z, O_DN_A), col(O_SG_U, O_SG_V),
        col(O_SG_V, O_FOX_QKV), col(O_FOX_QKV, O_FOX_F), col(O_MLA_CQ, O_MLA_CKV), col(O_MLA_CKV, O_MLA_KR),
        kr, _swap_halves(kr)], axis=-1).astype(BF16)
    assert w_p.shape[-1] == PROJ_WIDTH
    w_small = _pad_lanes(jnp.concatenate([col(O_DN_A, O_DN_B), col(O_DN_B, O_MLA_CQ), col(O_FOX_F, O_GATES)],
                                         axis=-1), LANES).astype(BF16)

    def row(a, width=LANES, offset=0):
        a = a.reshape(L, 1, -1)
        return jnp.pad(a, ((0, 0), (0, 0), (offset, width - offset - a.shape[-1])))

    wq = mla_w_uq.reshape(L, MLA_Q_RANK, HEADS, MLA_QK_DIM)
    wq_r = wq[..., MLA_NOPE:]
    w_uq_p = jnp.concatenate([wq, _swap_halves(wq_r)], axis=-1).reshape(L, MLA_Q_RANK, HEADS * QK_PAD).astype(BF16)

    def qk_gain(g):
        return jnp.concatenate([g, _swap_halves(g[:, MLA_NOPE:])], axis=-1).reshape(L, 1, QK_PAD)

    bias_full = jnp.repeat(jnp.swapaxes(sg_b_s, 1, 2), HEAD_DIM, axis=-1)
    return dict(
        norm1_g=norm1_g.reshape(L, 1, D_MODEL), w_p=w_p, w_small=w_small,
        a_log=row(dn_a_log), dt=row(dn_dt_bias), dn_og=dn_out_norm_g.reshape(L, 1, HEAD_DIM),
        mla_qn=mla_q_norm_g.reshape(L, 1, MLA_Q_RANK), mla_kvn=mla_kv_norm_g.reshape(L, 1, MLA_KV_RANK),
        w_uq_p=w_uq_p, w_ukv=mla_w_ukv.astype(BF16), mla_gq=qk_gain(mla_qk_q_g), mla_gk=qk_gain(mla_qk_k_g),
        sg_g=sg_v_norm_g.reshape(L, 1, HEADS * HEAD_DIM), sg_bias=bias_full,
        fox_fb=row(fox_f_bias, offset=2 * HEADS), fox_gq=fox_q_norm_g.reshape(L, 1, HEAD_DIM),
        fox_gk=fox_k_norm_g.reshape(L, 1, HEAD_DIM),
        w_branch=w_branch.astype(BF16), w_out=w_out.astype(BF16), norm2_g=norm2_g.reshape(L, 1, D_MODEL),
        w_ff1=w_ff1.astype(BF16), w_ff2=w_ff2.astype(BF16),
    )


def _tiles(S):
    pick = lambda want: min(want, S)
    return dict(in_tm=pick(2048), in_tn=1024, dn_rows=pick(512), prep_tm=pick(512), attn_blk=pick(512),
                sg_rows=pick(512), merge_tm=pick(512), ffn_tm=pick(1024), ffn_tf=1024, rope_tm=pick(1024))


def kernel(x, positions, norm1_g, w_in, dn_conv_w, dn_a_log, dn_dt_bias, dn_out_norm_g, mla_q_norm_g, mla_kv_norm_g, mla_w_uq, mla_w_ukv, mla_qk_q_g, mla_qk_k_g, sg_v_norm_g, sg_w_s, sg_b_s, fox_q_norm_g, fox_k_norm_g, fox_f_bias, w_branch, w_out, norm2_g, w_ff1, w_ff2):
    B, S, D = x.shape
    assert D == D_MODEL and S % LANES == 0
    T = B * S
    depth = w_in.shape[0]
    t = _tiles(S)
    p = _prep_params(norm1_g, w_in, dn_a_log, dn_dt_bias, dn_out_norm_g, mla_q_norm_g, mla_kv_norm_g, mla_w_uq,
                     mla_w_ukv, mla_qk_q_g, mla_qk_k_g, sg_v_norm_g, sg_b_s, fox_q_norm_g, fox_k_norm_g,
                     fox_f_bias, w_branch, w_out, norm2_g, w_ff1, w_ff2)
    cos_t, sin_t = _rope_tables(positions, t["rope_tm"])
    mla_shift, mla_ok = _logit_shift(mla_qk_q_g, mla_qk_k_g, MLA_QK_DIM)
    fox_shift, fox_ok = _logit_shift(fox_q_norm_g, fox_k_norm_g, HEAD_DIM)
    x2 = x.reshape(T, D)
    for l in range(depth):
        proj, small = _in_proj(x2, p["norm1_g"], p["w_p"], p["w_small"], l, t["in_tm"], t["in_tn"])
        o_a = _deltanet(proj, small, dn_conv_w, p["a_log"], p["dt"], p["dn_og"], l, B, S, t["dn_rows"])
        qb, kb, vtb = _mla_prep(proj, cos_t, sin_t, p["mla_qn"], p["mla_kvn"], p["w_uq_p"], p["w_ukv"],
                                p["mla_gq"], p["mla_gk"], mla_shift, l, B, S, t["prep_tm"])
        o_b = _attention(qb, kb, vtb, mla_ok[l], B, S, t["attn_blk"])
        o_c = _spatial_gating(proj, p["sg_g"], sg_w_s, p["sg_bias"], l, T, t["sg_rows"])
        qd, kd, vtd = _fox_prep(proj, small, p["fox_fb"], p["fox_gq"], p["fox_gk"], fox_shift, l, B, S,
                                t["prep_tm"])
        o_d = _attention(qd, kd, vtd, fox_ok[l], B, S, t["attn_blk"])
        x2 = _merge((o_a, o_b, o_c, o_d), proj, x2, p["w_branch"], p["w_out"], l, t["merge_tm"])
        x2 = _ffn(x2, p["norm2_g"], p["w_ff1"], p["w_ff2"], l, t["ffn_tm"], t["ffn_tf"])
    return x2.reshape(B, S, D)
```

```python
import functools
import math

import jax
import jax.numpy as jnp
from jax import lax
from jax.experimental import pallas as pl
from jax.experimental.pallas import tpu as pltpu

F32 = jnp.float32
BF16 = jnp.bfloat16

D_MODEL = 1024
NORM_EPS = 1e-6
N_BRANCHES = 4
BRANCH_WIDTH = 512
D_FF = 4 * D_MODEL
HEADS = 4
HEAD_DIM = 128
DN_CONV = 4
DN_CHUNK = 128
MLA_Q_RANK = 256
MLA_KV_RANK = 128
MLA_NOPE = 128
MLA_ROPE = 64
MLA_QK_DIM = MLA_NOPE + MLA_ROPE
ROPE_THETA = 10000.0
SG_CHUNK = 128
QK_PAD = 256
LANES = 128
NEG_BIG = -1e30
LOG2E = math.log2(math.e)

PROJ_GATES = 0
PROJ_DN_Q = 4096
PROJ_SG_U = 6144
PROJ_SG_V = 6656
PROJ_FOX_Q = 7168
PROJ_MLA_CQ = 8704
PROJ_MLA_CKV = 8960
PROJ_MLA_KR = 9088
PROJ_WIDTH = 9216

_SPLITS = (1536, 512, 4, 4, 256, 128, 64, 512, 512, 1536, 4, 4096)
_OFF = [0]
for _s in _SPLITS:
    _OFF.append(_OFF[-1] + _s)
(O_DN_QKV, O_DN_Z, O_DN_A, O_DN_B, O_MLA_CQ, O_MLA_CKV, O_MLA_KR, O_SG_U, O_SG_V, O_FOX_QKV, O_FOX_F,
 O_GATES, O_END) = _OFF

VMEM_LIMIT = 56 * 1024 * 1024


def _cparams(sem):
    return pltpu.CompilerParams(dimension_semantics=sem, vmem_limit_bytes=VMEM_LIMIT)


def _dot(a, b):
    return jnp.dot(a, b, preferred_element_type=F32)


def _dot_nt(a, b):
    return lax.dot_general(a, b, (((1,), (1,)), ((), ())), preferred_element_type=F32)


def _dot_f32(a, b):
    return jnp.dot(a, b, preferred_element_type=F32, precision=lax.Precision.HIGHEST)


def _sigmoid(x):
    return 1.0 / (1.0 + jnp.exp(-x))


def _softplus(x):
    return jnp.maximum(x, 0.0) + jnp.log1p(jnp.exp(-jnp.abs(x)))


def _gelu_tanh(x):
    return 0.5 * x * (1.0 + jnp.tanh(math.sqrt(2.0 / math.pi) * (x + 0.044715 * (x * x * x))))


def _tri_masks(n):
    r = lax.broadcasted_iota(jnp.int32, (n, n), 0)
    c = lax.broadcasted_iota(jnp.int32, (n, n), 1)
    return r >= c, r > c


def _rope_kernel(pos_ref, freq_ref, sign_ref, cos_ref, sin_ref):
    ang = pos_ref[...] * freq_ref[...]
    lane = lax.broadcasted_iota(jnp.int32, ang.shape, 1)
    live = lane < MLA_ROPE
    cos_ref[...] = jnp.where(live, jnp.cos(ang), 0.0)
    sin_ref[...] = jnp.where(live, jnp.sin(ang) * sign_ref[...], 0.0)


def _rope_tables(positions, tm):
    T = positions.size
    half = MLA_ROPE // 2
    inv_freq = ROPE_THETA ** (-jnp.arange(0, MLA_ROPE, 2, dtype=F32) / MLA_ROPE)
    zeros = jnp.zeros((LANES - MLA_ROPE,), F32)
    freq = jnp.concatenate([inv_freq, inv_freq, zeros]).reshape(1, LANES)
    sign = jnp.concatenate([-jnp.ones((half,), F32), jnp.ones((half,), F32), zeros]).reshape(1, LANES)
    row = pl.BlockSpec((1, LANES), lambda i: (0, 0))
    tab = pl.BlockSpec((tm, LANES), lambda i: (i, 0))
    return pl.pallas_call(
        _rope_kernel,
        out_shape=(jax.ShapeDtypeStruct((T, LANES), F32), jax.ShapeDtypeStruct((T, LANES), F32)),
        grid=(T // tm,),
        in_specs=[tab, row, row],
        out_specs=(tab, tab),
        compiler_params=_cparams(("parallel",)),
        name="rope_tables",
    )(jnp.broadcast_to(positions.reshape(T, 1).astype(F32), (T, LANES)), freq, sign)


def _in_proj_kernel(x_ref, g_ref, w_ref, ws_ref, proj_ref, small_ref, h_scr):
    @pl.when(pl.program_id(1) == 0)
    def _():
        x = x_ref[...]
        ms = jnp.mean(x * x, axis=-1, keepdims=True)
        h = (x * lax.rsqrt(ms + NORM_EPS) * g_ref[...]).astype(BF16)
        h_scr[...] = h
        small_ref[...] = _dot(h, ws_ref[...])

    proj_ref[...] = _dot(h_scr[...], w_ref[...]).astype(BF16)


def _in_proj(x2, g, w_p, w_small, layer, tm, tn):
    T = x2.shape[0]
    return pl.pallas_call(
        _in_proj_kernel,
        out_shape=(jax.ShapeDtypeStruct((T, PROJ_WIDTH), BF16), jax.ShapeDtypeStruct((T, LANES), F32)),
        grid=(T // tm, PROJ_WIDTH // tn),
        in_specs=[
            pl.BlockSpec((tm, D_MODEL), lambda i, j: (i, 0)),
            pl.BlockSpec((None, 1, D_MODEL), lambda i, j: (layer, 0, 0)),
            pl.BlockSpec((None, D_MODEL, tn), lambda i, j: (layer, 0, j)),
            pl.BlockSpec((None, D_MODEL, LANES), lambda i, j: (layer, 0, 0)),
        ],
        out_specs=(pl.BlockSpec((tm, tn), lambda i, j: (i, j)),
                   pl.BlockSpec((tm, LANES), lambda i, j: (i, 0))),
        scratch_shapes=[pltpu.VMEM((tm, D_MODEL), BF16)],
        compiler_params=_cparams(("parallel", "arbitrary")),
        name="in_proj",
    )(x2, g, w_p, w_small)


INV_BASE = 8


def _inverse_masks(n):
    r = lax.broadcasted_iota(jnp.int32, (n, n), 0)
    c = lax.broadcasted_iota(jnp.int32, (n, n), 1)
    same = lambda s: (r // s) == (c // s)
    levels = []
    s = INV_BASE
    while s < n:
        levels.append(same(2 * s) & jnp.logical_not(same(s)))
        s *= 2
    as16 = lambda m: jnp.where(m, 1.0, 0.0).astype(BF16)
    return (r == c).astype(F32), as16(same(INV_BASE)), [as16(m) for m in levels]


def _unit_lower_inverse(nmats, masks):
    eye, base, levels = masks
    n16 = [n.astype(BF16) for n in nmats]
    ps = [n * base for n in n16]
    xs = [eye - p.astype(F32) for p in ps]
    for _ in range(int(math.log2(INV_BASE)) - 1):
        ps = [_dot(p, p).astype(BF16) for p in ps]
        xs = [x + _dot(x.astype(BF16), p) for x, p in zip(xs, ps)]
    for off in levels:
        x16 = [x.astype(BF16) for x in xs]
        ts = [_dot(n * off, xb).astype(BF16) for n, xb in zip(n16, x16)]
        xs = [x - _dot(xb, t) for x, xb, t in zip(xs, x16, ts)]
    return xs


def _deltanet_kernel(q_ref, k_ref, v_ref, z_ref, small_ref, cw_ref, alog_ref, dt_ref, og_ref, o_ref,
                     qbuf, kbuf, vbuf, qs, ks, vs, state, *, rows):
    C = DN_CHUNK
    W = HEADS * HEAD_DIM
    NC = rows // C

    @pl.when(pl.program_id(1) == 0)
    def _():
        state[...] = jnp.zeros_like(state)
        for buf in (qbuf, kbuf, vbuf):
            buf[0:8, :] = jnp.zeros((8, W), F32)

    def conv_silu(buf, x_ref, w, dst):
        buf[8:8 + rows, :] = x_ref[...].astype(F32)
        acc = buf[8:8 + rows, :] * w[DN_CONV - 1:DN_CONV, :]
        for s in range(1, DN_CONV):
            acc = acc + buf[8 - s:8 - s + rows, :] * w[DN_CONV - 1 - s:DN_CONV - s, :]
        buf[0:8, :] = buf[rows:rows + 8, :]
        dst[...] = acc * _sigmoid(acc)

    cw = cw_ref[...]
    conv_silu(qbuf, q_ref, cw[:, 0:W], qs)
    conv_silu(kbuf, k_ref, cw[:, W:2 * W], ks)
    conv_silu(vbuf, v_ref, cw[:, 2 * W:3 * W], vs)

    small = small_ref[...]
    g_all = -jnp.exp(alog_ref[...]) * _softplus(small + dt_ref[...])
    beta_all = _sigmoid(small)

    tril, strict = _tri_masks(C)
    tril_f = tril.astype(F32)
    inv_masks = _inverse_masks(C)
    og = og_ref[...]

    probs = [(c, h) for c in range(NC) for h in range(HEADS)]
    gcs = [_dot_f32(tril_f, g_all[c * C:(c + 1) * C, :]) for c in range(NC)]
    gcts = [gc.T for gc in gcs]
    rsl = lambda c: slice(c * C, (c + 1) * C)
    hsl = lambda h: slice(h * HEAD_DIM, (h + 1) * HEAD_DIM)
    qn, kn = [], []
    for c, h in probs:
        q = qs[rsl(c), hsl(h)]
        k = ks[rsl(c), hsl(h)]
        qn.append(q * (lax.rsqrt(jnp.sum(q * q, axis=-1, keepdims=True) + NORM_EPS) * (HEAD_DIM ** -0.5)))
        kn.append(k * lax.rsqrt(jnp.sum(k * k, axis=-1, keepdims=True) + NORM_EPS))
    gcol = [gcs[c][:, h:h + 1] for c, h in probs]
    bcol = [beta_all[rsl(c), HEADS + h:HEADS + h + 1] for c, h in probs]
    glast = [gcs[c][C - 1:C, h:h + 1] for c, h in probs]
    decay = [jnp.where(tril, jnp.exp(jnp.where(tril, gcol[i] - gcts[c][h:h + 1, :], 0.0)), 0.0)
             for i, (c, h) in enumerate(probs)]
    kb = [k * b for k, b in zip(kn, bcol)]
    k16 = [k.astype(BF16) for k in kn]
    qk_kk = [_dot_nt(jnp.concatenate([kbi, q], axis=0).astype(BF16), k) for kbi, q, k in zip(kb, qn, k16)]
    nmat = [jnp.where(strict, m[:C] * d, 0.0) for m, d in zip(qk_kk, decay)]
    a_qk = [(m[C:] * d).astype(BF16) for m, d in zip(qk_kk, decay)]
    eg = [jnp.exp(g) for g in gcol]
    rhs = [jnp.concatenate([vs[rsl(c), hsl(h)] * bcol[i], kb[i] * eg[i]], axis=-1).astype(BF16)
           for i, (c, h) in enumerate(probs)]
    xinv = _unit_lower_inverse(nmat, inv_masks)
    sol = [_dot(x.astype(BF16), r) for x, r in zip(xinv, rhs)]
    u = [s[:, :HEAD_DIM] for s in sol]
    wq = [jnp.concatenate([s[:, HEAD_DIM:], q * e], axis=0).astype(BF16) for s, q, e in zip(sol, qn, eg)]
    kdt = [(k * jnp.exp(gl - g)).T.astype(BF16) for k, gl, g in zip(kn, glast, gcol)]
    egl = [jnp.exp(gl) for gl in glast]

    st = [state[h] for h in range(HEADS)]
    for c in range(NC):
        idx = [c * HEADS + h for h in range(HEADS)]
        r = [_dot(wq[i], st[h].astype(BF16)) for h, i in enumerate(idx)]
        v16 = [(u[i] - r[h][:C]).astype(BF16) for h, i in enumerate(idx)]
        o = [r[h][C:] + _dot(a_qk[i], v16[h]) for h, i in enumerate(idx)]
        st = [st[h] * egl[i] + _dot(kdt[i], v16[h]) for h, i in enumerate(idx)]
        for h in range(HEADS):
            oh = o[h] * lax.rsqrt(jnp.mean(o[h] * o[h], axis=-1, keepdims=True) + NORM_EPS) * og
            z = z_ref[rsl(c), hsl(h)].astype(F32)
            o_ref[rsl(c), hsl(h)] = (oh * (z * _sigmoid(z))).astype(BF16)
    for h in range(HEADS):
        state[h] = st[h]


def _deltanet(proj, small, conv_w, a_log_row, dt_row, out_g, layer, B, S, rows):
    T = B * S
    W = HEADS * HEAD_DIM
    nb = S // rows
    blk = lambda col: pl.BlockSpec((rows, W), lambda b, i, col=col: (b * nb + i, col))
    prow = pl.BlockSpec((None, 1, LANES), lambda b, i: (layer, 0, 0))
    return pl.pallas_call(
        functools.partial(_deltanet_kernel, rows=rows),
        out_shape=jax.ShapeDtypeStruct((T, W), BF16),
        grid=(B, nb),
        in_specs=[
            blk(PROJ_DN_Q // W), blk(PROJ_DN_Q // W + 1), blk(PROJ_DN_Q // W + 2), blk(PROJ_DN_Q // W + 3),
            pl.BlockSpec((rows, LANES), lambda b, i: (b * nb + i, 0)),
            pl.BlockSpec((None, DN_CONV, 3 * W), lambda b, i: (layer, 0, 0)),
            prow, prow, prow,
        ],
        out_specs=pl.BlockSpec((rows, W), lambda b, i: (b * nb + i, 0)),
        scratch_shapes=[pltpu.VMEM((rows + 8, W), F32)] * 3 + [pltpu.VMEM((rows, W), F32)] * 3
        + [pltpu.VMEM((HEADS, HEAD_DIM, HEAD_DIM), F32)],
        compiler_params=_cparams(("parallel", "arbitrary")),
        name="deltanet",
    )(proj, proj, proj, proj, small, conv_w, a_log_row, dt_row, out_g)


VT_ONES = 16
VT_ROWS = HEAD_DIM + VT_ONES


def _store_vt(vt_out, h, v16):
    r = lax.broadcasted_iota(jnp.int32, (HEAD_DIM, HEAD_DIM), 0)
    c = lax.broadcasted_iota(jnp.int32, (HEAD_DIM, HEAD_DIM), 1)
    eye = jnp.where(r == c, 1.0, 0.0).astype(BF16)
    base = h * VT_ROWS
    vt_out[base:base + HEAD_DIM, :] = _dot_nt(eye, v16).astype(BF16)
    vt_out[base + HEAD_DIM:base + VT_ROWS, :] = jnp.ones((VT_ONES, v16.shape[0]), BF16)


def _mla_prep_kernel(cq_ref, ckv_ref, kr_ref, cos_ref, sin_ref, qn_ref, kvn_ref, wq_ref, wkv_ref,
                     gq_ref, gk_ref, off_ref, q_out, k_out, vt_out):
    cq = cq_ref[...].astype(F32)
    ckv = ckv_ref[...].astype(F32)
    cq2 = cq * cq
    cq_ms = jnp.sum(cq2[:, :LANES] + cq2[:, LANES:], axis=-1, keepdims=True) * (1.0 / MLA_Q_RANK)
    cq = cq * lax.rsqrt(cq_ms + NORM_EPS) * qn_ref[...]
    ckv = ckv * lax.rsqrt(jnp.mean(ckv * ckv, axis=-1, keepdims=True) + NORM_EPS) * kvn_ref[...]
    q_all = _dot(cq.astype(BF16), wq_ref[...])
    kv_all = _dot(ckv.astype(BF16), wkv_ref[...])
    cos = cos_ref[...]
    sin = sin_ref[...]
    gq = gq_ref[...]
    gk = gk_ref[...]
    lane = lax.broadcasted_iota(jnp.int32, cos.shape, 1)
    live = lane < MLA_ROPE
    scale = MLA_QK_DIM ** -0.5 * LOG2E
    off = off_ref[...]

    def rope(xr):
        return xr * cos + pltpu.roll(xr, MLA_ROPE, 1) * sin

    kr = kr_ref[...].astype(F32)
    kr_sq = jnp.where(live, kr * kr, 0.0)
    kr_rot = rope(kr * gk[:, MLA_NOPE:])
    for h in range(HEADS):
        base = h * QK_PAD
        qn = q_all[:, base:base + MLA_NOPE]
        qr = q_all[:, base + MLA_NOPE:base + QK_PAD]
        ss = jnp.sum(qn * qn + jnp.where(live, qr * qr, 0.0), axis=-1, keepdims=True)
        rinv = lax.rsqrt(ss * (1.0 / MLA_QK_DIM) + NORM_EPS) * scale
        q_out[:, base:base + MLA_NOPE] = (qn * rinv * gq[:, :MLA_NOPE]).astype(BF16)
        q_rot = jnp.where(lane == MLA_ROPE, 1.0, rope(qr * rinv * gq[:, MLA_NOPE:]))
        q_out[:, base + MLA_NOPE:base + QK_PAD] = q_rot.astype(BF16)

        kn = kv_all[:, base:base + MLA_NOPE]
        v = kv_all[:, base + MLA_NOPE:base + QK_PAD]
        ssk = jnp.sum(kn * kn + kr_sq, axis=-1, keepdims=True)
        rk = lax.rsqrt(ssk * (1.0 / MLA_QK_DIM) + NORM_EPS)
        k_out[:, base:base + MLA_NOPE] = (kn * rk * gk[:, :MLA_NOPE]).astype(BF16)
        k_rot = jnp.where(lane == MLA_ROPE, off, kr_rot * rk)
        k_out[:, base + MLA_NOPE:base + QK_PAD] = k_rot.astype(BF16)
        _store_vt(vt_out, h, v.astype(BF16))


def _mla_prep(proj, cos_t, sin_t, qn_g, kvn_g, w_uq_p, w_ukv, gq_p, gk_p, off_row, layer, B, S, tm):
    T = B * S
    nb = S // tm
    HW = HEADS * QK_PAD
    tab = pl.BlockSpec((tm, LANES), lambda i: (i, 0))
    return pl.pallas_call(
        _mla_prep_kernel,
        out_shape=(jax.ShapeDtypeStruct((T, HW), BF16), jax.ShapeDtypeStruct((T, HW), BF16),
                   jax.ShapeDtypeStruct((B, nb, HEADS * VT_ROWS, tm), BF16)),
        grid=(T // tm,),
        in_specs=[
            pl.BlockSpec((tm, MLA_Q_RANK), lambda i: (i, PROJ_MLA_CQ // MLA_Q_RANK)),
            pl.BlockSpec((tm, MLA_KV_RANK), lambda i: (i, PROJ_MLA_CKV // MLA_KV_RANK)),
            pl.BlockSpec((tm, LANES), lambda i: (i, PROJ_MLA_KR // LANES)),
            tab, tab,
            pl.BlockSpec((None, 1, MLA_Q_RANK), lambda i: (layer, 0, 0)),
            pl.BlockSpec((None, 1, MLA_KV_RANK), lambda i: (layer, 0, 0)),
            pl.BlockSpec((None, MLA_Q_RANK, HW), lambda i: (layer, 0, 0)),
            pl.BlockSpec((None, MLA_KV_RANK, HW), lambda i: (layer, 0, 0)),
            pl.BlockSpec((None, 1, QK_PAD), lambda i: (layer, 0, 0)),
            pl.BlockSpec((None, 1, QK_PAD), lambda i: (layer, 0, 0)),
            pl.BlockSpec((None, 1, LANES), lambda i: (layer, 0, 0)),
        ],
        out_specs=(pl.BlockSpec((tm, HW), lambda i: (i, 0)), pl.BlockSpec((tm, HW), lambda i: (i, 0)),
                   pl.BlockSpec((None, None, HEADS * VT_ROWS, tm), lambda i: (i // nb, i % nb, 0, 0))),
        compiler_params=_cparams(("parallel",)),
        name="mla_prep",
    )(proj, proj, proj, cos_t, sin_t, qn_g, kvn_g, w_uq_p, w_ukv, gq_p, gk_p, off_row)


def _split3(c):
    hi = c.astype(BF16).astype(F32)
    r = c - hi
    mid = r.astype(BF16).astype(F32)
    lo = (r - mid).astype(BF16).astype(F32)
    return hi, mid, lo


def _fox_prep_kernel(q_ref, k_ref, v_ref, small_ref, fb_ref, gq_ref, gk_ref, off_ref, q_out, k_out, vt_out, carry,
                     *, tm):
    @pl.when(pl.program_id(1) == 0)
    def _():
        carry[...] = jnp.zeros_like(carry)

    C = LANES
    tril, _ = _tri_masks(C)
    tril_f = tril.astype(F32)
    logf = -_softplus(-(small_ref[...] + fb_ref[...]))
    gq = gq_ref[...]
    gk = gk_ref[...]
    off = off_ref[...]
    scale = HEAD_DIM ** -0.5 * LOG2E
    lane = lax.broadcasted_iota(jnp.int32, (C, LANES), 1)
    for c in range(tm // C):
        rs = slice(c * C, (c + 1) * C)
        cum = _dot_f32(tril_f, logf[rs, :]) + carry[...]
        carry[...] = cum[C - 1:C, :]
        for h in range(HEADS):
            hs = slice(h * HEAD_DIM, (h + 1) * HEAD_DIM)
            q = q_ref[rs, hs].astype(F32)
            k = k_ref[rs, hs].astype(F32)
            q = q * lax.rsqrt(jnp.mean(q * q, axis=-1, keepdims=True) + NORM_EPS) * gq * scale
            k = k * lax.rsqrt(jnp.mean(k * k, axis=-1, keepdims=True) + NORM_EPS) * gk
            hi, mid, lo = _split3(cum[:, 2 * HEADS + h:2 * HEADS + h + 1] * LOG2E)
            one = jnp.where(lane < 7, 1.0, 0.0)
            parts = jnp.where(lane == 0, hi, jnp.where(lane == 1, mid, jnp.where(lane == 2, lo, 0.0)))
            q_aug = jnp.where(lane < 3, parts, one)
            nparts = jnp.where(lane == 3, -hi, jnp.where(lane == 4, -mid, jnp.where(lane == 5, -lo,
                                                                                     jnp.where(lane == 6, off, 0.0))))
            k_aug = jnp.where(lane < 3, one, nparts)
            base = h * QK_PAD
            q_out[rs, base:base + HEAD_DIM] = q.astype(BF16)
            q_out[rs, base + HEAD_DIM:base + QK_PAD] = q_aug.astype(BF16)
            k_out[rs, base:base + HEAD_DIM] = k.astype(BF16)
            k_out[rs, base + HEAD_DIM:base + QK_PAD] = k_aug.astype(BF16)
    for h in range(HEADS):
        _store_vt(vt_out, h, v_ref[:, h * HEAD_DIM:(h + 1) * HEAD_DIM])


def _fox_prep(proj, small, f_bias_row, gq, gk, off_row, layer, B, S, tm):
    T = B * S
    nb = S // tm
    W = HEADS * HEAD_DIM
    HW = HEADS * QK_PAD
    blk = lambda col: pl.BlockSpec((tm, W), lambda b, i, col=col: (b * nb + i, col))
    prow = pl.BlockSpec((None, 1, LANES), lambda b, i: (layer, 0, 0))
    return pl.pallas_call(
        functools.partial(_fox_prep_kernel, tm=tm),
        out_shape=(jax.ShapeDtypeStruct((T, HW), BF16), jax.ShapeDtypeStruct((T, HW), BF16),
                   jax.ShapeDtypeStruct((B, nb, HEADS * VT_ROWS, tm), BF16)),
        grid=(B, nb),
        in_specs=[blk(PROJ_FOX_Q // W), blk(PROJ_FOX_Q // W + 1), blk(PROJ_FOX_Q // W + 2),
                  pl.BlockSpec((tm, LANES), lambda b, i: (b * nb + i, 0)), prow, prow, prow, prow],
        out_specs=(pl.BlockSpec((tm, HW), lambda b, i: (b * nb + i, 0)),
                   pl.BlockSpec((tm, HW), lambda b, i: (b * nb + i, 0)),
                   pl.BlockSpec((None, None, HEADS * VT_ROWS, tm), lambda b, i: (b, i, 0, 0))),
        scratch_shapes=[pltpu.VMEM((1, LANES), F32)],
        compiler_params=_cparams(("parallel", "arbitrary")),
        name="fox_prep",
    )(proj, proj, proj, small, f_bias_row, gq, gk, off_row)


def _attn_kernel(q_ref, k_ref, vt_ref, o_ref, m_scr, acc_scr, *, blk):
    i = pl.program_id(1)
    heads = range(HEADS)
    qsl = lambda h: slice(h * QK_PAD, (h + 1) * QK_PAD)
    vsl = lambda h: slice(h * VT_ROWS, (h + 1) * VT_ROWS)
    osl = lambda h: slice(h * HEAD_DIM, (h + 1) * HEAD_DIM)
    q = [q_ref[:, qsl(h)] for h in heads]
    m_scr[...] = jnp.full_like(m_scr, NEG_BIG)
    acc_scr[...] = jnp.zeros_like(acc_scr)

    def scores(j):
        rows = pl.ds(pl.multiple_of(j * blk, blk), blk)
        return [_dot_nt(k_ref[rows, qsl(h)], q[h]) for h in heads]

    def accumulate(j, st, masked):
        if masked:
            r = lax.broadcasted_iota(jnp.int32, (blk, blk), 0)
            c = lax.broadcasted_iota(jnp.int32, (blk, blk), 1)
            keep = r <= c
            st = [jnp.where(keep, s, NEG_BIG) for s in st]
        for grp in ((0, 1), (2, 3)):
            m_old = [m_scr[h] for h in grp]
            m_new = [jnp.maximum(mo, jnp.max(st[h], axis=0, keepdims=True)) for mo, h in zip(m_old, grp)]
            alpha = [jnp.exp2(mo - mn) for mo, mn in zip(m_old, m_new)]
            p = [jnp.exp2(st[h] - mn).astype(BF16) for h, mn in zip(grp, m_new)]
            pv = [_dot(vt_ref[j, vsl(h), :], pp) for h, pp in zip(grp, p)]
            for n, h in enumerate(grp):
                m_scr[h] = m_new[n]
                acc_scr[h] = alpha[n] * acc_scr[h] + pv[n]

    def pair(t, carry):
        j = 2 * t
        st_a = scores(j)
        st_b = scores(j + 1)
        accumulate(j, st_a, False)
        accumulate(j + 1, st_b, False)
        return carry

    lax.fori_loop(0, i // 2, pair, 0)

    @pl.when(i % 2 == 1)
    def _():
        accumulate(i - 1, scores(i - 1), False)

    accumulate(i, scores(i), True)
    for h in heads:
        acc = acc_scr[h]
        o_ref[:, osl(h)] = (acc[:HEAD_DIM] / acc[HEAD_DIM:HEAD_DIM + 1]).T.astype(BF16)


def _attn_shifted_kernel(q_ref, k_ref, vt_ref, o_ref, acc_scr, *, blk):
    i = pl.program_id(1)
    heads = range(HEADS)
    qsl = lambda h: slice(h * QK_PAD, (h + 1) * QK_PAD)
    vsl = lambda h: slice(h * VT_ROWS, (h + 1) * VT_ROWS)
    osl = lambda h: slice(h * HEAD_DIM, (h + 1) * HEAD_DIM)
    q = [q_ref[:, qsl(h)] for h in heads]
    acc_scr[...] = jnp.zeros_like(acc_scr)

    def block(j):
        rows = pl.ds(pl.multiple_of(j * blk, blk), blk)
        st = [_dot_nt(k_ref[rows, qsl(h)], q[h]) for h in heads]
        pv = [_dot(vt_ref[j, vsl(h), :], jnp.exp2(st[h]).astype(BF16)) for h in heads]
        for h in heads:
            acc_scr[h] += pv[h]

    def diagonal_block():
        half = blk // 2
        lo = pl.ds(pl.multiple_of(i * blk, blk), half)
        hi = pl.ds(pl.multiple_of(i * blk + half, half), half)
        keep_lo = (lax.broadcasted_iota(jnp.int32, (half, blk), 0)
                   <= lax.broadcasted_iota(jnp.int32, (half, blk), 1))
        keep_hi = (lax.broadcasted_iota(jnp.int32, (half, half), 0)
                   <= lax.broadcasted_iota(jnp.int32, (half, half), 1))
        s_lo = [jnp.where(keep_lo, _dot_nt(k_ref[lo, qsl(h)], q[h]), NEG_BIG) for h in heads]
        s_hi = [jnp.where(keep_hi, _dot_nt(k_ref[hi, qsl(h)], q_ref[half:, qsl(h)]), NEG_BIG) for h in heads]
        pv_lo = [_dot(vt_ref[i, vsl(h), :half], jnp.exp2(s_lo[h]).astype(BF16)) for h in heads]
        pv_hi = [_dot(vt_ref[i, vsl(h), half:], jnp.exp2(s_hi[h]).astype(BF16)) for h in heads]
        for h in heads:
            acc_scr[h] += pv_lo[h]
            acc_scr[h, :, half:] += pv_hi[h]

    def pair(t, carry):
        block(2 * t)
        block(2 * t + 1)
        return carry

    lax.fori_loop(0, i // 2, pair, 0)

    @pl.when(i % 2 == 1)
    def _():
        block(i - 1)

    diagonal_block()
    for h in heads:
        acc = acc_scr[h]
        o_ref[:, osl(h)] = (acc[:HEAD_DIM] / acc[HEAD_DIM:HEAD_DIM + 1]).T.astype(BF16)


def _attention(q, k, vt, shifted, B, S, blk):
    T = B * S
    nb = S // blk
    W = HEADS * HEAD_DIM
    HW = HEADS * QK_PAD
    common = dict(
        out_shape=jax.ShapeDtypeStruct((T, W), BF16),
        grid=(B, nb),
        in_specs=[
            pl.BlockSpec((blk, HW), lambda b, i: (b * nb + i, 0)),
            pl.BlockSpec((S, HW), lambda b, i: (b, 0)),
            pl.BlockSpec((None, nb, HEADS * VT_ROWS, blk), lambda b, i: (b, 0, 0, 0)),
        ],
        out_specs=pl.BlockSpec((blk, W), lambda b, i: (b * nb + i, 0)),
        compiler_params=_cparams(("parallel", "arbitrary")),
    )
    acc = pltpu.VMEM((HEADS, VT_ROWS, blk), F32)
    online = pl.pallas_call(functools.partial(_attn_kernel, blk=blk), name="causal_attention",
                            scratch_shapes=[pltpu.VMEM((HEADS, 1, blk), F32), acc], **common)
    fast = pl.pallas_call(functools.partial(_attn_shifted_kernel, blk=blk), name="causal_attention_shifted",
                          scratch_shapes=[acc], **common)
    return lax.cond(shifted, fast, online, q, k, vt)


SHIFT_MAX = 40.0


def _logit_shift(gq, gk, dim):
    L = gq.shape[0]
    bound = (dim ** 0.5) * LOG2E * 1.02 * jnp.max(jnp.abs(gq.reshape(L, -1)), axis=-1) \
        * jnp.max(jnp.abs(gk.reshape(L, -1)), axis=-1)
    ok = bound <= SHIFT_MAX
    shift = jnp.where(ok, -bound, 0.0).astype(F32)
    return jnp.broadcast_to(shift[:, None, None], (L, 1, LANES)), ok


def _sg_kernel(u_ref, v_ref, g_ref, ws_ref, b_ref, o_ref, *, rows):
    Tn = SG_CHUNK
    tril, _ = _tri_masks(Tn)
    g = g_ref[...]
    bias = b_ref[...]
    for gi in range(HEADS):
        hs = slice(gi * HEAD_DIM, (gi + 1) * HEAD_DIM)
        w = jnp.where(tril, ws_ref[gi], 0.0).astype(BF16)
        for n in range(rows // Tn):
            rs = slice(n * Tn, (n + 1) * Tn)
            v = _gelu_tanh(v_ref[rs, hs].astype(F32))
            v = v * lax.rsqrt(jnp.mean(v * v, axis=-1, keepdims=True) + NORM_EPS) * g[:, hs]
            mixed = _dot(w, v.astype(BF16)) + bias[:, hs]
            u = _gelu_tanh(u_ref[rs, hs].astype(F32))
            o_ref[rs, hs] = (u * mixed).astype(BF16)


def _spatial_gating(proj, norm_g, w_s, bias_full, layer, T, rows):
    W = HEADS * HEAD_DIM
    return pl.pallas_call(
        functools.partial(_sg_kernel, rows=rows),
        out_shape=jax.ShapeDtypeStruct((T, W), BF16),
        grid=(T // rows,),
        in_specs=[
            pl.BlockSpec((rows, W), lambda i: (i, PROJ_SG_U // W)),
            pl.BlockSpec((rows, W), lambda i: (i, PROJ_SG_V // W)),
            pl.BlockSpec((None, 1, W), lambda i: (layer, 0, 0)),
            pl.BlockSpec((None, HEADS, SG_CHUNK, SG_CHUNK), lambda i: (layer, 0, 0, 0)),
            pl.BlockSpec((None, SG_CHUNK, W), lambda i: (layer, 0, 0)),
        ],
        out_specs=pl.BlockSpec((rows, W), lambda i: (i, 0)),
        compiler_params=_cparams(("parallel",)),
        name="spatial_gating",
    )(proj, proj, norm_g, w_s, bias_full)


def _merge_kernel(oa, ob, oc, od, g0, g1, g2, g3, x_ref, wb_ref, wo_ref, out_ref):
    merged = None
    for i, (o, g) in enumerate(((oa, g0), (ob, g1), (oc, g2), (od, g3))):
        term = _sigmoid(g[...].astype(F32)) * _dot(o[...], wb_ref[i])
        merged = term if merged is None else merged + term
    out_ref[...] = x_ref[...] + _dot(merged.astype(BF16), wo_ref[...])


def _merge(branches, proj, x2, w_branch, w_out, layer, tm):
    T = x2.shape[0]
    ob = pl.BlockSpec((tm, BRANCH_WIDTH), lambda i: (i, 0))
    gate = lambda n: pl.BlockSpec((tm, D_MODEL), lambda i, n=n: (i, n))
    xs = pl.BlockSpec((tm, D_MODEL), lambda i: (i, 0))
    return pl.pallas_call(
        _merge_kernel,
        out_shape=jax.ShapeDtypeStruct((T, D_MODEL), F32),
        grid=(T // tm,),
        in_specs=[ob, ob, ob, ob, gate(0), gate(1), gate(2), gate(3), xs,
                  pl.BlockSpec((None, N_BRANCHES, BRANCH_WIDTH, D_MODEL), lambda i: (layer, 0, 0, 0)),
                  pl.BlockSpec((None, D_MODEL, D_MODEL), lambda i: (layer, 0, 0))],
        out_specs=xs,
        compiler_params=_cparams(("parallel",)),
        name="merge_out_proj",
    )(*branches, proj, proj, proj, proj, x2, w_branch, w_out)


def _ffn_kernel(x_ref, g_ref, w1_ref, w2_ref, out_ref, h_scr):
    f = pl.program_id(1)

    @pl.when(f == 0)
    def _():
        x = x_ref[...]
        ms = jnp.mean(x * x, axis=-1, keepdims=True)
        h_scr[...] = (x * lax.rsqrt(ms + NORM_EPS) * g_ref[...]).astype(BF16)
        out_ref[...] = x

    a = jnp.maximum(_dot(h_scr[...], w1_ref[...]), 0.0)
    out_ref[...] += _dot((a * a).astype(BF16), w2_ref[...])


def _ffn(x2, g, w1, w2, layer, tm, tf):
    T = x2.shape[0]
    xs = pl.BlockSpec((tm, D_MODEL), lambda i, f: (i, 0))
    return pl.pallas_call(
        _ffn_kernel,
        out_shape=jax.ShapeDtypeStruct((T, D_MODEL), F32),
        grid=(T // tm, D_FF // tf),
        in_specs=[xs,
                  pl.BlockSpec((None, 1, D_MODEL), lambda i, f: (layer, 0, 0)),
                  pl.BlockSpec((None, D_MODEL, tf), lambda i, f: (layer, 0, f)),
                  pl.BlockSpec((None, tf, D_MODEL), lambda i, f: (layer, f, 0))],
        out_specs=xs,
        scratch_shapes=[pltpu.VMEM((tm, D_MODEL), BF16)],
        compiler_params=_cparams(("parallel", "arbitrary")),
        name="relu2_mlp",
    )(x2, g, w1, w2)


def _pad_lanes(a, width):
    return jnp.pad(a, [(0, 0)] * (a.ndim - 1) + [(0, width - a.shape[-1])])


def _swap_halves(a):
    h = a.shape[-1] // 2
    return jnp.concatenate([a[..., h:], a[..., :h]], axis=-1)


def _prep_params(norm1_g, w_in, dn_a_log, dn_dt_bias, dn_out_norm_g, mla_q_norm_g, mla_kv_norm_g, mla_w_uq,
                 mla_w_ukv, mla_qk_q_g, mla_qk_k_g, sg_v_norm_g, sg_b_s, fox_q_norm_g, fox_k_norm_g, fox_f_bias,
                 w_branch, w_out, norm2_g, w_ff1, w_ff2):
    L = w_in.shape[0]
    col = lambda a, b: w_in[:, :, a:b]
    kr = col(O_MLA_KR, O_MLA_KR + MLA_ROPE)
    w_p = jnp.concatenate([
        col(O_GATES, O_END), col(O_DN_QKV, O_DN_Z), col(O_DN_Z, O_DN_A), col(O_SG_U, O_SG_V),
        col(O_SG_V, O_FOX_QKV), col(O_FOX_QKV, O_FOX_F), col(O_MLA_CQ, O_MLA_CKV), col(O_MLA_CKV, O_MLA_KR),
        kr, _swap_halves(kr)], axis=-1).astype(BF16)
    assert w_p.shape[-1] == PROJ_WIDTH
    w_small = _pad_lanes(jnp.concatenate([col(O_DN_A, O_DN_B), col(O_DN_B, O_MLA_CQ), col(O_FOX_F, O_GATES)],
                                         axis=-1), LANES).astype(BF16)

    def row(a, width=LANES, offset=0):
        a = a.reshape(L, 1, -1)
        return jnp.pad(a, ((0, 0), (0, 0), (offset, width - offset - a.shape[-1])))

    wq = mla_w_uq.reshape(L, MLA_Q_RANK, HEADS, MLA_QK_DIM)
    wq_r = wq[..., MLA_NOPE:]
    w_uq_p = jnp.concatenate([wq, _swap_halves(wq_r)], axis=-1).reshape(L, MLA_Q_RANK, HEADS * QK_PAD).astype(BF16)

    def qk_gain(g):
        return jnp.concatenate([g, _swap_halves(g[:, MLA_NOPE:])], axis=-1).reshape(L, 1, QK_PAD)

    bias_full = jnp.repeat(jnp.swapaxes(sg_b_s, 1, 2), HEAD_DIM, axis=-1)
    return dict(
        norm1_g=norm1_g.reshape(L, 1, D_MODEL), w_p=w_p, w_small=w_small,
        a_log=row(dn_a_log), dt=row(dn_dt_bias), dn_og=dn_out_norm_g.reshape(L, 1, HEAD_DIM),
        mla_qn=mla_q_norm_g.reshape(L, 1, MLA_Q_RANK), mla_kvn=mla_kv_norm_g.reshape(L, 1, MLA_KV_RANK),
        w_uq_p=w_uq_p, w_ukv=mla_w_ukv.astype(BF16), mla_gq=qk_gain(mla_qk_q_g), mla_gk=qk_gain(mla_qk_k_g),
        sg_g=sg_v_norm_g.reshape(L, 1, HEADS * HEAD_DIM), sg_bias=bias_full,
        fox_fb=row(fox_f_bias, offset=2 * HEADS), fox_gq=fox_q_norm_g.reshape(L, 1, HEAD_DIM),
        fox_gk=fox_k_norm_g.reshape(L, 1, HEAD_DIM),
        w_branch=w_branch.astype(BF16), w_out=w_out.astype(BF16), norm2_g=norm2_g.reshape(L, 1, D_MODEL),
        w_ff1=w_ff1.astype(BF16), w_ff2=w_ff2.astype(BF16),
    )


def _tiles(S):
    pick = lambda want: min(want, S)
    return dict(in_tm=pick(2048), in_tn=1024, dn_rows=pick(512), prep_tm=pick(512), attn_blk=pick(512),
                sg_rows=pick(512), merge_tm=pick(512), ffn_tm=pick(1024), ffn_tf=1024, rope_tm=pick(1024))


def kernel(x, positions, norm1_g, w_in, dn_conv_w, dn_a_log, dn_dt_bias, dn_out_norm_g, mla_q_norm_g, mla_kv_norm_g, mla_w_uq, mla_w_ukv, mla_qk_q_g, mla_qk_k_g, sg_v_norm_g, sg_w_s, sg_b_s, fox_q_norm_g, fox_k_norm_g, fox_f_bias, w_branch, w_out, norm2_g, w_ff1, w_ff2):
    B, S, D = x.shape
    assert D == D_MODEL and S % LANES == 0
    T = B * S
    depth = w_in.shape[0]
    t = _tiles(S)
    p = _prep_params(norm1_g, w_in, dn_a_log, dn_dt_bias, dn_out_norm_g, mla_q_norm_g, mla_kv_norm_g, mla_w_uq,
                     mla_w_ukv, mla_qk_q_g, mla_qk_k_g, sg_v_norm_g, sg_b_s, fox_q_norm_g, fox_k_norm_g,
                     fox_f_bias, w_branch, w_out, norm2_g, w_ff1, w_ff2)
    cos_t, sin_t = _rope_tables(positions, t["rope_tm"])
    mla_shift, mla_ok = _logit_shift(mla_qk_q_g, mla_qk_k_g, MLA_QK_DIM)
    fox_shift, fox_ok = _logit_shift(fox_q_norm_g, fox_k_norm_g, HEAD_DIM)
    x2 = x.reshape(T, D)
    for l in range(depth):
        proj, small = _in_proj(x2, p["norm1_g"], p["w_p"], p["w_small"], l, t["in_tm"], t["in_tn"])
        o_a = _deltanet(proj, small, dn_conv_w, p["a_log"], p["dt"], p["dn_og"], l, B, S, t["dn_rows"])
        qb, kb, vtb = _mla_prep(proj, cos_t, sin_t, p["mla_qn"], p["mla_kvn"], p["w_uq_p"], p["w_ukv"],
                                p["mla_gq"], p["mla_gk"], mla_shift, l, B, S, t["prep_tm"])
        o_b = _attention(qb, kb, vtb, mla_ok[l], B, S, t["attn_blk"])
        o_c = _spatial_gating(proj, p["sg_g"], sg_w_s, p["sg_bias"], l, T, t["sg_rows"])
        qd, kd, vtd = _fox_prep(proj, small, p["fox_fb"], p["fox_gq"], p["fox_gk"], fox_shift, l, B, S,
                                t["prep_tm"])
        o_d = _attention(qd, kd, vtd, fox_ok[l], B, S, t["attn_blk"])
        x2 = _merge((o_a, o_b, o_c, o_d), proj, x2, p["w_branch"], p["w_out"], l, t["merge_tm"])
        x2 = _ffn(x2, p["norm2_g"], p["w_ff1"], p["w_ff2"], l, t["ffn_tm"], t["ffn_tf"])
    return x2.reshape(B, S, D)
```

```python
import functools
import math

import jax
import jax.numpy as jnp
from jax import lax
from jax.experimental import pallas as pl
from jax.experimental.pallas import tpu as pltpu

F32 = jnp.float32
BF16 = jnp.bfloat16

D_MODEL = 1024
NORM_EPS = 1e-6
N_BRANCHES = 4
BRANCH_WIDTH = 512
D_FF = 4 * D_MODEL
HEADS = 4
HEAD_DIM = 128
DN_CONV = 4
DN_CHUNK = 128
MLA_Q_RANK = 256
MLA_KV_RANK = 128
MLA_NOPE = 128
MLA_ROPE = 64
MLA_QK_DIM = MLA_NOPE + MLA_ROPE
ROPE_THETA = 10000.0
SG_CHUNK = 128
QK_PAD = 256
LANES = 128
NEG_BIG = -1e30
LOG2E = math.log2(math.e)

PROJ_GATES = 0
PROJ_DN_Q = 4096
PROJ_SG_U = 6144
PROJ_SG_V = 6656
PROJ_FOX_Q = 7168
PROJ_MLA_CQ = 8704
PROJ_MLA_CKV = 8960
PROJ_MLA_KR = 9088
PROJ_WIDTH = 9216

_SPLITS = (1536, 512, 4, 4, 256, 128, 64, 512, 512, 1536, 4, 4096)
_OFF = [0]
for _s in _SPLITS:
    _OFF.append(_OFF[-1] + _s)
(O_DN_QKV, O_DN_Z, O_DN_A, O_DN_B, O_MLA_CQ, O_MLA_CKV, O_MLA_KR, O_SG_U, O_SG_V, O_FOX_QKV, O_FOX_F,
 O_GATES, O_END) = _OFF

VMEM_LIMIT = 56 * 1024 * 1024


def _cparams(sem):
    return pltpu.CompilerParams(dimension_semantics=sem, vmem_limit_bytes=VMEM_LIMIT)


def _dot(a, b):
    return jnp.dot(a, b, preferred_element_type=F32)


def _dot_nt(a, b):
    return lax.dot_general(a, b, (((1,), (1,)), ((), ())), preferred_element_type=F32)


def _dot_f32(a, b):
    return jnp.dot(a, b, preferred_element_type=F32, precision=lax.Precision.HIGHEST)


def _sigmoid(x):
    return 1.0 / (1.0 + jnp.exp(-x))


def _softplus(x):
    return jnp.maximum(x, 0.0) + jnp.log1p(jnp.exp(-jnp.abs(x)))


def _gelu_tanh(x):
    return 0.5 * x * (1.0 + jnp.tanh(math.sqrt(2.0 / math.pi) * (x + 0.044715 * (x * x * x))))


def _tri_masks(n):
    r = lax.broadcasted_iota(jnp.int32, (n, n), 0)
    c = lax.broadcasted_iota(jnp.int32, (n, n), 1)
    return r >= c, r > c


def _rope_kernel(pos_ref, freq_ref, sign_ref, cos_ref, sin_ref):
    ang = pos_ref[...] * freq_ref[...]
    lane = lax.broadcasted_iota(jnp.int32, ang.shape, 1)
    live = lane < MLA_ROPE
    cos_ref[...] = jnp.where(live, jnp.cos(ang), 0.0)
    sin_ref[...] = jnp.where(live, jnp.sin(ang) * sign_ref[...], 0.0)


def _rope_tables(positions, tm):
    T = positions.size
    half = MLA_ROPE // 2
    inv_freq = ROPE_THETA ** (-jnp.arange(0, MLA_ROPE, 2, dtype=F32) / MLA_ROPE)
    zeros = jnp.zeros((LANES - MLA_ROPE,), F32)
    freq = jnp.concatenate([inv_freq, inv_freq, zeros]).reshape(1, LANES)
    sign = jnp.concatenate([-jnp.ones((half,), F32), jnp.ones((half,), F32), zeros]).reshape(1, LANES)
    row = pl.BlockSpec((1, LANES), lambda i: (0, 0))
    tab = pl.BlockSpec((tm, LANES), lambda i: (i, 0))
    return pl.pallas_call(
        _rope_kernel,
        out_shape=(jax.ShapeDtypeStruct((T, LANES), F32), jax.ShapeDtypeStruct((T, LANES), F32)),
        grid=(T // tm,),
        in_specs=[tab, row, row],
        out_specs=(tab, tab),
        compiler_params=_cparams(("parallel",)),
        name="rope_tables",
    )(jnp.broadcast_to(positions.reshape(T, 1).astype(F32), (T, LANES)), freq, sign)


def _in_proj_kernel(x_ref, g_ref, w_ref, ws_ref, proj_ref, small_ref, h_scr):
    @pl.when(pl.program_id(1) == 0)
    def _():
        x = x_ref[...]
        ms = jnp.mean(x * x, axis=-1, keepdims=True)
        h = (x * lax.rsqrt(ms + NORM_EPS) * g_ref[...]).astype(BF16)
        h_scr[...] = h
        small_ref[...] = _dot(h, ws_ref[...])

    proj_ref[...] = _dot(h_scr[...], w_ref[...]).astype(BF16)


def _in_proj(x2, g, w_p, w_small, layer, tm, tn):
    T = x2.shape[0]
    return pl.pallas_call(
        _in_proj_kernel,
        out_shape=(jax.ShapeDtypeStruct((T, PROJ_WIDTH), BF16), jax.ShapeDtypeStruct((T, LANES), F32)),
        grid=(T // tm, PROJ_WIDTH // tn),
        in_specs=[
            pl.BlockSpec((tm, D_MODEL), lambda i, j: (i, 0)),
            pl.BlockSpec((None, 1, D_MODEL), lambda i, j: (layer, 0, 0)),
            pl.BlockSpec((None, D_MODEL, tn), lambda i, j: (layer, 0, j)),
            pl.BlockSpec((None, D_MODEL, LANES), lambda i, j: (layer, 0, 0)),
        ],
        out_specs=(pl.BlockSpec((tm, tn), lambda i, j: (i, j)),
                   pl.BlockSpec((tm, LANES), lambda i, j: (i, 0))),
        scratch_shapes=[pltpu.VMEM((tm, D_MODEL), BF16)],
        compiler_params=_cparams(("parallel", "arbitrary")),
        name="in_proj",
    )(x2, g, w_p, w_small)


INV_BASE = 8


def _inverse_masks(n):
    r = lax.broadcasted_iota(jnp.int32, (n, n), 0)
    c = lax.broadcasted_iota(jnp.int32, (n, n), 1)
    same = lambda s: (r // s) == (c // s)
    levels = []
    s = INV_BASE
    while s < n:
        levels.append(same(2 * s) & jnp.logical_not(same(s)))
        s *= 2
    as16 = lambda m: jnp.where(m, 1.0, 0.0).astype(BF16)
    return (r == c).astype(F32), as16(same(INV_BASE)), [as16(m) for m in levels]


def _unit_lower_inverse(nmats, masks):
    eye, base, levels = masks
    n16 = [n.astype(BF16) for n in nmats]
    ps = [n * base for n in n16]
    xs = [eye - p.astype(F32) for p in ps]
    for _ in range(int(math.log2(INV_BASE)) - 1):
        ps = [_dot(p, p).astype(BF16) for p in ps]
        xs = [x + _dot(x.astype(BF16), p) for x, p in zip(xs, ps)]
    for off in levels:
        x16 = [x.astype(BF16) for x in xs]
        ts = [_dot(n * off, xb).astype(BF16) for n, xb in zip(n16, x16)]
        xs = [x - _dot(xb, t) for x, xb, t in zip(xs, x16, ts)]
    return xs


def _deltanet_kernel(q_ref, k_ref, v_ref, z_ref, small_ref, cw_ref, alog_ref, dt_ref, og_ref, o_ref,
                     qbuf, kbuf, vbuf, qs, ks, vs, state, *, rows):
    C = DN_CHUNK
    W = HEADS * HEAD_DIM
    NC = rows // C

    @pl.when(pl.program_id(1) == 0)
    def _():
        state[...] = jnp.zeros_like(state)
        for buf in (qbuf, kbuf, vbuf):
            buf[0:8, :] = jnp.zeros((8, W), F32)

    def conv_silu(buf, x_ref, w, dst):
        buf[8:8 + rows, :] = x_ref[...].astype(F32)
        acc = buf[8:8 + rows, :] * w[DN_CONV - 1:DN_CONV, :]
        for s in range(1, DN_CONV):
            acc = acc + buf[8 - s:8 - s + rows, :] * w[DN_CONV - 1 - s:DN_CONV - s, :]
        buf[0:8, :] = buf[rows:rows + 8, :]
        dst[...] = acc * _sigmoid(acc)

    cw = cw_ref[...]
    conv_silu(qbuf, q_ref, cw[:, 0:W], qs)
    conv_silu(kbuf, k_ref, cw[:, W:2 * W], ks)
    conv_silu(vbuf, v_ref, cw[:, 2 * W:3 * W], vs)

    small = small_ref[...]
    g_all = -jnp.exp(alog_ref[...]) * _softplus(small + dt_ref[...])
    beta_all = _sigmoid(small)

    tril, strict = _tri_masks(C)
    tril_f = tril.astype(F32)
    inv_masks = _inverse_masks(C)
    og = og_ref[...]

    probs = [(c, h) for c in range(NC) for h in range(HEADS)]
    gcs = [_dot_f32(tril_f, g_all[c * C:(c + 1) * C, :]) for c in range(NC)]
    gcts = [gc.T for gc in gcs]
    rsl = lambda c: slice(c * C, (c + 1) * C)
    hsl = lambda h: slice(h * HEAD_DIM, (h + 1) * HEAD_DIM)
    qn, kn = [], []
    for c, h in probs:
        q = qs[rsl(c), hsl(h)]
        k = ks[rsl(c), hsl(h)]
        qn.append(q * (lax.rsqrt(jnp.sum(q * q, axis=-1, keepdims=True) + NORM_EPS) * (HEAD_DIM ** -0.5)))
        kn.append(k * lax.rsqrt(jnp.sum(k * k, axis=-1, keepdims=True) + NORM_EPS))
    gcol = [gcs[c][:, h:h + 1] for c, h in probs]
    bcol = [beta_all[rsl(c), HEADS + h:HEADS + h + 1] for c, h in probs]
    glast = [gcs[c][C - 1:C, h:h + 1] for c, h in probs]
    decay = [jnp.where(tril, jnp.exp(jnp.where(tril, gcol[i] - gcts[c][h:h + 1, :], 0.0)), 0.0)
             for i, (c, h) in enumerate(probs)]
    kb = [k * b for k, b in zip(kn, bcol)]
    k16 = [k.astype(BF16) for k in kn]
    qk_kk = [_dot_nt(jnp.concatenate([kbi, q], axis=0).astype(BF16), k) for kbi, q, k in zip(kb, qn, k16)]
    nmat = [jnp.where(strict, m[:C] * d, 0.0) for m, d in zip(qk_kk, decay)]
    a_qk = [(m[C:] * d).astype(BF16) for m, d in zip(qk_kk, decay)]
    eg = [jnp.exp(g) for g in gcol]
    rhs = [jnp.concatenate([vs[rsl(c), hsl(h)] * bcol[i], kb[i] * eg[i]], axis=-1).astype(BF16)
           for i, (c, h) in enumerate(probs)]
    xinv = _unit_lower_inverse(nmat, inv_masks)
    sol = [_dot(x.astype(BF16), r) for x, r in zip(xinv, rhs)]
    u = [s[:, :HEAD_DIM] for s in sol]
    wq = [jnp.concatenate([s[:, HEAD_DIM:], q * e], axis=0).astype(BF16) for s, q, e in zip(sol, qn, eg)]
    kdt = [(k * jnp.exp(gl - g)).T.astype(BF16) for k, gl, g in zip(kn, glast, gcol)]
    egl = [jnp.exp(gl) for gl in glast]

    st = [state[h] for h in range(HEADS)]
    for c in range(NC):
        idx = [c * HEADS + h for h in range(HEADS)]
        r = [_dot(wq[i], st[h].astype(BF16)) for h, i in enumerate(idx)]
        v16 = [(u[i] - r[h][:C]).astype(BF16) for h, i in enumerate(idx)]
        o = [r[h][C:] + _dot(a_qk[i], v16[h]) for h, i in enumerate(idx)]
        st = [st[h] * egl[i] + _dot(kdt[i], v16[h]) for h, i in enumerate(idx)]
        for h in range(HEADS):
            oh = o[h] * lax.rsqrt(jnp.mean(o[h] * o[h], axis=-1, keepdims=True) + NORM_EPS) * og
            z = z_ref[rsl(c), hsl(h)].astype(F32)
            o_ref[rsl(c), hsl(h)] = (oh * (z * _sigmoid(z))).astype(BF16)
    for h in range(HEADS):
        state[h] = st[h]


def _deltanet(proj, small, conv_w, a_log_row, dt_row, out_g, layer, B, S, rows):
    T = B * S
    W = HEADS * HEAD_DIM
    nb = S // rows
    blk = lambda col: pl.BlockSpec((rows, W), lambda b, i, col=col: (b * nb + i, col))
    prow = pl.BlockSpec((None, 1, LANES), lambda b, i: (layer, 0, 0))
    return pl.pallas_call(
        functools.partial(_deltanet_kernel, rows=rows),
        out_shape=jax.ShapeDtypeStruct((T, W), BF16),
        grid=(B, nb),
        in_specs=[
            blk(PROJ_DN_Q // W), blk(PROJ_DN_Q // W + 1), blk(PROJ_DN_Q // W + 2), blk(PROJ_DN_Q // W + 3),
            pl.BlockSpec((rows, LANES), lambda b, i: (b * nb + i, 0)),
            pl.BlockSpec((None, DN_CONV, 3 * W), lambda b, i: (layer, 0, 0)),
            prow, prow, prow,
        ],
        out_specs=pl.BlockSpec((rows, W), lambda b, i: (b * nb + i, 0)),
        scratch_shapes=[pltpu.VMEM((rows + 8, W), F32)] * 3 + [pltpu.VMEM((rows, W), F32)] * 3
        + [pltpu.VMEM((HEADS, HEAD_DIM, HEAD_DIM), F32)],
        compiler_params=_cparams(("parallel", "arbitrary")),
        name="deltanet",
    )(proj, proj, proj, proj, small, conv_w, a_log_row, dt_row, out_g)


VT_ONES = 16
VT_ROWS = HEAD_DIM + VT_ONES


def _store_vt(vt_out, h, v16):
    r = lax.broadcasted_iota(jnp.int32, (HEAD_DIM, HEAD_DIM), 0)
    c = lax.broadcasted_iota(jnp.int32, (HEAD_DIM, HEAD_DIM), 1)
    eye = jnp.where(r == c, 1.0, 0.0).astype(BF16)
    base = h * VT_ROWS
    vt_out[base:base + HEAD_DIM, :] = _dot_nt(eye, v16).astype(BF16)
    vt_out[base + HEAD_DIM:base + VT_ROWS, :] = jnp.ones((VT_ONES, v16.shape[0]), BF16)


def _mla_prep_kernel(cq_ref, ckv_ref, kr_ref, cos_ref, sin_ref, qn_ref, kvn_ref, wq_ref, wkv_ref,
                     gq_ref, gk_ref, off_ref, q_out, k_out, vt_out):
    cq = cq_ref[...].astype(F32)
    ckv = ckv_ref[...].astype(F32)
    cq2 = cq * cq
    cq_ms = jnp.sum(cq2[:, :LANES] + cq2[:, LANES:], axis=-1, keepdims=True) * (1.0 / MLA_Q_RANK)
    cq = cq * lax.rsqrt(cq_ms + NORM_EPS) * qn_ref[...]
    ckv = ckv * lax.rsqrt(jnp.mean(ckv * ckv, axis=-1, keepdims=True) + NORM_EPS) * kvn_ref[...]
    q_all = _dot(cq.astype(BF16), wq_ref[...])
    kv_all = _dot(ckv.astype(BF16), wkv_ref[...])
    cos = cos_ref[...]
    sin = sin_ref[...]
    gq = gq_ref[...]
    gk = gk_ref[...]
    lane = lax.broadcasted_iota(jnp.int32, cos.shape, 1)
    live = lane < MLA_ROPE
    scale = MLA_QK_DIM ** -0.5 * LOG2E
    off = off_ref[...]

    def rope(xr):
        return xr * cos + pltpu.roll(xr, MLA_ROPE, 1) * sin

    kr = kr_ref[...].astype(F32)
    kr_sq = jnp.where(live, kr * kr, 0.0)
    kr_rot = rope(kr * gk[:, MLA_NOPE:])
    for h in range(HEADS):
        base = h * QK_PAD
        qn = q_all[:, base:base + MLA_NOPE]
        qr = q_all[:, base + MLA_NOPE:base + QK_PAD]
        ss = jnp.sum(qn * qn + jnp.where(live, qr * qr, 0.0), axis=-1, keepdims=True)
        rinv = lax.rsqrt(ss * (1.0 / MLA_QK_DIM) + NORM_EPS) * scale
        q_out[:, base:base + MLA_NOPE] = (qn * rinv * gq[:, :MLA_NOPE]).astype(BF16)
        q_rot = jnp.where(lane == MLA_ROPE, 1.0, rope(qr * rinv * gq[:, MLA_NOPE:]))
        q_out[:, base + MLA_NOPE:base + QK_PAD] = q_rot.astype(BF16)

        kn = kv_all[:, base:base + MLA_NOPE]
        v = kv_all[:, base + MLA_NOPE:base + QK_PAD]
        ssk = jnp.sum(kn * kn + kr_sq, axis=-1, keepdims=True)
        rk = lax.rsqrt(ssk * (1.0 / MLA_QK_DIM) + NORM_EPS)
        k_out[:, base:base + MLA_NOPE] = (kn * rk * gk[:, :MLA_NOPE]).astype(BF16)
        k_rot = jnp.where(lane == MLA_ROPE, off, kr_rot * rk)
        k_out[:, base + MLA_NOPE:base + QK_PAD] = k_rot.astype(BF16)
        _store_vt(vt_out, h, v.astype(BF16))


def _split3(c):
    hi = c.astype(BF16).astype(F32)
    r = c - hi
    mid = r.astype(BF16).astype(F32)
    lo = (r - mid).astype(BF16).astype(F32)
    return hi, mid, lo


def _fox_prep_kernel(q_ref, k_ref, v_ref, small_ref, fb_ref, gq_ref, gk_ref, off_ref, q_out, k_out, vt_out, carry,
                     *, tm):
    C = LANES
    tril, _ = _tri_masks(C)
    tril_f = tril.astype(F32)
    logf = -_softplus(-(small_ref[...] + fb_ref[...]))
    gq = gq_ref[...]
    gk = gk_ref[...]
    off = off_ref[...]
    scale = HEAD_DIM ** -0.5 * LOG2E
    lane = lax.broadcasted_iota(jnp.int32, (C, LANES), 1)
    for c in range(tm // C):
        rs = slice(c * C, (c + 1) * C)
        cum = _dot_f32(tril_f, logf[rs, :]) + carry[...]
        carry[...] = cum[C - 1:C, :]
        for h in range(HEADS):
            hs = slice(h * HEAD_DIM, (h + 1) * HEAD_DIM)
            q = q_ref[rs, hs].astype(F32)
            k = k_ref[rs, hs].astype(F32)
            q = q * lax.rsqrt(jnp.mean(q * q, axis=-1, keepdims=True) + NORM_EPS) * gq * scale
            k = k * lax.rsqrt(jnp.mean(k * k, axis=-1, keepdims=True) + NORM_EPS) * gk
            hi, mid, lo = _split3(cum[:, 2 * HEADS + h:2 * HEADS + h + 1] * LOG2E)
            one = jnp.where(lane < 7, 1.0, 0.0)
            parts = jnp.where(lane == 0, hi, jnp.where(lane == 1, mid, jnp.where(lane == 2, lo, 0.0)))
            q_aug = jnp.where(lane < 3, parts, one)
            nparts = jnp.where(lane == 3, -hi, jnp.where(lane == 4, -mid, jnp.where(lane == 5, -lo,
                                                                                     jnp.where(lane == 6, off, 0.0))))
            k_aug = jnp.where(lane < 3, one, nparts)
            base = h * QK_PAD
            q_out[rs, base:base + HEAD_DIM] = q.astype(BF16)
            q_out[rs, base + HEAD_DIM:base + QK_PAD] = q_aug.astype(BF16)
            k_out[rs, base:base + HEAD_DIM] = k.astype(BF16)
            k_out[rs, base + HEAD_DIM:base + QK_PAD] = k_aug.astype(BF16)
    for h in range(HEADS):
        _store_vt(vt_out, h, v_ref[:, h * HEAD_DIM:(h + 1) * HEAD_DIM])


def _attn_kernel(q_ref, k_ref, vt_ref, o_ref, m_scr, acc_scr, *, blk):
    i = pl.program_id(1)
    heads = range(HEADS)
    qsl = lambda h: slice(h * QK_PAD, (h + 1) * QK_PAD)
    vsl = lambda h: slice(h * VT_ROWS, (h + 1) * VT_ROWS)
    osl = lambda h: slice(h * HEAD_DIM, (h + 1) * HEAD_DIM)
    q = [q_ref[:, qsl(h)] for h in heads]
    m_scr[...] = jnp.full_like(m_scr, NEG_BIG)
    acc_scr[...] = jnp.zeros_like(acc_scr)

    def scores(j):
        rows = pl.ds(pl.multiple_of(j * blk, blk), blk)
        return [_dot_nt(k_ref[rows, qsl(h)], q[h]) for h in heads]

    def accumulate(j, st, masked):
        if masked:
            r = lax.broadcasted_iota(jnp.int32, (blk, blk), 0)
            c = lax.broadcasted_iota(jnp.int32, (blk, blk), 1)
            keep = r <= c
            st = [jnp.where(keep, s, NEG_BIG) for s in st]
        for grp in ((0, 1), (2, 3)):
            m_old = [m_scr[h] for h in grp]
            m_new = [jnp.maximum(mo, jnp.max(st[h], axis=0, keepdims=True)) for mo, h in zip(m_old, grp)]
            alpha = [jnp.exp2(mo - mn) for mo, mn in zip(m_old, m_new)]
            p = [jnp.exp2(st[h] - mn).astype(BF16) for h, mn in zip(grp, m_new)]
            pv = [_dot(vt_ref[j, vsl(h), :], pp) for h, pp in zip(grp, p)]
            for n, h in enumerate(grp):
                m_scr[h] = m_new[n]
                acc_scr[h] = alpha[n] * acc_scr[h] + pv[n]

    def pair(t, carry):
        j = 2 * t
        st_a = scores(j)
        st_b = scores(j + 1)
        accumulate(j, st_a, False)
        accumulate(j + 1, st_b, False)
        return carry

    lax.fori_loop(0, i // 2, pair, 0)

    @pl.when(i % 2 == 1)
    def _():
        accumulate(i - 1, scores(i - 1), False)

    accumulate(i, scores(i), True)
    for h in heads:
        acc = acc_scr[h]
        o_ref[:, osl(h)] = (acc[:HEAD_DIM] / acc[HEAD_DIM:HEAD_DIM + 1]).T.astype(BF16)


def _attn_shifted_kernel(q_ref, k_ref, vt_ref, o_ref, acc_scr, *, blk):
    i = pl.program_id(1)
    heads = range(HEADS)
    qsl = lambda h: slice(h * QK_PAD, (h + 1) * QK_PAD)
    vsl = lambda h: slice(h * VT_ROWS, (h + 1) * VT_ROWS)
    osl = lambda h: slice(h * HEAD_DIM, (h + 1) * HEAD_DIM)
    q = [q_ref[:, qsl(h)] for h in heads]
    acc_scr[...] = jnp.zeros_like(acc_scr)

    def block(j):
        rows = pl.ds(pl.multiple_of(j * blk, blk), blk)
        st = [_dot_nt(k_ref[rows, qsl(h)], q[h]) for h in heads]
        pv = [_dot(vt_ref[j, vsl(h), :], jnp.exp2(st[h]).astype(BF16)) for h in heads]
        for h in heads:
            acc_scr[h] += pv[h]

    def diagonal_block():
        half = blk // 2
        lo = pl.ds(pl.multiple_of(i * blk, blk), half)
        hi = pl.ds(pl.multiple_of(i * blk + half, half), half)
        keep_lo = (lax.broadcasted_iota(jnp.int32, (half, blk), 0)
                   <= lax.broadcasted_iota(jnp.int32, (half, blk), 1))
        keep_hi = (lax.broadcasted_iota(jnp.int32, (half, half), 0)
                   <= lax.broadcasted_iota(jnp.int32, (half, half), 1))
        s_lo = [jnp.where(keep_lo, _dot_nt(k_ref[lo, qsl(h)], q[h]), NEG_BIG) for h in heads]
        s_hi = [jnp.where(keep_hi, _dot_nt(k_ref[hi, qsl(h)], q_ref[half:, qsl(h)]), NEG_BIG) for h in heads]
        pv_lo = [_dot(vt_ref[i, vsl(h), :half], jnp.exp2(s_lo[h]).astype(BF16)) for h in heads]
        pv_hi = [_dot(vt_ref[i, vsl(h), half:], jnp.exp2(s_hi[h]).astype(BF16)) for h in heads]
        for h in heads:
            acc_scr[h] += pv_lo[h]
            acc_scr[h, :, half:] += pv_hi[h]

    def pair(t, carry):
        block(2 * t)
        block(2 * t + 1)
        return carry

    lax.fori_loop(0, i // 2, pair, 0)

    @pl.when(i % 2 == 1)
    def _():
        block(i - 1)

    diagonal_block()
    for h in heads:
        acc = acc_scr[h]
        o_ref[:, osl(h)] = (acc[:HEAD_DIM] / acc[HEAD_DIM:HEAD_DIM + 1]).T.astype(BF16)


def _attention(q, k, vt, shifted, B, S, blk):
    T = B * S
    nb = S // blk
    W = HEADS * HEAD_DIM
    HW = HEADS * QK_PAD
    common = dict(
        out_shape=jax.ShapeDtypeStruct((T, W), BF16),
        grid=(B, nb),
        in_specs=[
            pl.BlockSpec((blk, HW), lambda b, i: (b * nb + i, 0)),
            pl.BlockSpec((S, HW), lambda b, i: (b, 0)),
            pl.BlockSpec((None, nb, HEADS * VT_ROWS, blk), lambda b, i: (b, 0, 0, 0)),
        ],
        out_specs=pl.BlockSpec((blk, W), lambda b, i: (b * nb + i, 0)),
        compiler_params=_cparams(("parallel", "arbitrary")),
    )
    acc = pltpu.VMEM((HEADS, VT_ROWS, blk), F32)
    online = pl.pallas_call(functools.partial(_attn_kernel, blk=blk), name="causal_attention",
                            scratch_shapes=[pltpu.VMEM((HEADS, 1, blk), F32), acc], **common)
    fast = pl.pallas_call(functools.partial(_attn_shifted_kernel, blk=blk), name="causal_attention_shifted",
                          scratch_shapes=[acc], **common)
    return lax.cond(shifted, fast, online, q, k, vt)


SHIFT_MAX = 40.0


def _logit_shift(gq, gk, dim):
    L = gq.shape[0]
    bound = (dim ** 0.5) * LOG2E * 1.02 * jnp.max(jnp.abs(gq.reshape(L, -1)), axis=-1) \
        * jnp.max(jnp.abs(gk.reshape(L, -1)), axis=-1)
    ok = bound <= SHIFT_MAX
    shift = jnp.where(ok, -bound, 0.0).astype(F32)
    return jnp.broadcast_to(shift[:, None, None], (L, 1, LANES)), ok


def _sg_kernel(u_ref, v_ref, g_ref, ws_ref, b_ref, o_ref, *, rows):
    Tn = SG_CHUNK
    tril, _ = _tri_masks(Tn)
    g = g_ref[...]
    bias = b_ref[...]
    for gi in range(HEADS):
        hs = slice(gi * HEAD_DIM, (gi + 1) * HEAD_DIM)
        w = jnp.where(tril, ws_ref[gi], 0.0).astype(BF16)
        for n in range(rows // Tn):
            rs = slice(n * Tn, (n + 1) * Tn)
            v = _gelu_tanh(v_ref[rs, hs].astype(F32))
            v = v * lax.rsqrt(jnp.mean(v * v, axis=-1, keepdims=True) + NORM_EPS) * g[:, hs]
            mixed = _dot(w, v.astype(BF16)) + bias[:, hs]
            u = _gelu_tanh(u_ref[rs, hs].astype(F32))
            o_ref[rs, hs] = (u * mixed).astype(BF16)


N_MLA_IN, N_FOX_IN, N_SG_IN = 12, 8, 5


def _mixer_prep_kernel(*refs, tm):
    mla_in = refs[:N_MLA_IN]
    fox_in = refs[N_MLA_IN:N_MLA_IN + N_FOX_IN]
    sg_in = refs[N_MLA_IN + N_FOX_IN:N_MLA_IN + N_FOX_IN + N_SG_IN]
    mq, mk, mvt, fq, fk, fvt, sg_out, carry = refs[N_MLA_IN + N_FOX_IN + N_SG_IN:]

    @pl.when(pl.program_id(1) == 0)
    def _():
        carry[...] = jnp.zeros_like(carry)

    _mla_prep_kernel(*mla_in, mq, mk, mvt)
    _fox_prep_kernel(*fox_in, fq, fk, fvt, carry, tm=tm)
    _sg_kernel(*sg_in, sg_out, rows=tm)


def _mixer_prep(proj, small, cos_t, sin_t, p, mla_shift, fox_shift, sg_w_s, layer, B, S, tm):
    T = B * S
    nb = S // tm
    W = HEADS * HEAD_DIM
    HW = HEADS * QK_PAD
    rows = lambda width, col=0: pl.BlockSpec((tm, width), lambda b, i, col=col: (b * nb + i, col))
    par = lambda *shape: pl.BlockSpec((None,) + shape, lambda b, i: (layer,) + (0,) * len(shape))
    vt = pl.BlockSpec((None, None, HEADS * VT_ROWS, tm), lambda b, i: (b, i, 0, 0))
    vt_shape = jax.ShapeDtypeStruct((B, nb, HEADS * VT_ROWS, tm), BF16)
    qk_shape = jax.ShapeDtypeStruct((T, HW), BF16)
    mla_specs = [rows(MLA_Q_RANK, PROJ_MLA_CQ // MLA_Q_RANK), rows(MLA_KV_RANK, PROJ_MLA_CKV // MLA_KV_RANK),
                 rows(LANES, PROJ_MLA_KR // LANES), rows(LANES), rows(LANES),
                 par(1, MLA_Q_RANK), par(1, MLA_KV_RANK), par(MLA_Q_RANK, HW), par(MLA_KV_RANK, HW),
                 par(1, QK_PAD), par(1, QK_PAD), par(1, LANES)]
    mla_args = [proj, proj, proj, cos_t, sin_t, p["mla_qn"], p["mla_kvn"], p["w_uq_p"], p["w_ukv"],
                p["mla_gq"], p["mla_gk"], mla_shift]
    fox_specs = [rows(W, PROJ_FOX_Q // W), rows(W, PROJ_FOX_Q // W + 1), rows(W, PROJ_FOX_Q // W + 2), rows(LANES),
                 par(1, LANES), par(1, LANES), par(1, LANES), par(1, LANES)]
    fox_args = [proj, proj, proj, small, p["fox_fb"], p["fox_gq"], p["fox_gk"], fox_shift]
    sg_specs = [rows(W, PROJ_SG_U // W), rows(W, PROJ_SG_V // W), par(1, W), par(HEADS, SG_CHUNK, SG_CHUNK),
                par(SG_CHUNK, W)]
    sg_args = [proj, proj, p["sg_g"], sg_w_s, p["sg_bias"]]
    assert (len(mla_specs), len(fox_specs), len(sg_specs)) == (N_MLA_IN, N_FOX_IN, N_SG_IN)
    out = pl.pallas_call(
        functools.partial(_mixer_prep_kernel, tm=tm),
        out_shape=(qk_shape, qk_shape, vt_shape, qk_shape, qk_shape, vt_shape, jax.ShapeDtypeStruct((T, W), BF16)),
        grid=(B, nb),
        in_specs=mla_specs + fox_specs + sg_specs,
        out_specs=(rows(HW), rows(HW), vt, rows(HW), rows(HW), vt, rows(W)),
        scratch_shapes=[pltpu.VMEM((1, LANES), F32)],
        compiler_params=_cparams(("parallel", "arbitrary")),
        name="mixer_prep",
    )(*mla_args, *fox_args, *sg_args)
    return out[0:3], out[3:6], out[6]


def _merge_kernel(oa, ob, oc, od, g0, g1, g2, g3, x_ref, wb_ref, wo_ref, n2_ref, out_ref, h_ref):
    merged = None
    for i, (o, g) in enumerate(((oa, g0), (ob, g1), (oc, g2), (od, g3))):
        term = _sigmoid(g[...].astype(F32)) * _dot(o[...], wb_ref[i])
        merged = term if merged is None else merged + term
    x1 = x_ref[...] + _dot(merged.astype(BF16), wo_ref[...])
    out_ref[...] = x1
    ms = jnp.mean(x1 * x1, axis=-1, keepdims=True)
    h_ref[...] = (x1 * lax.rsqrt(ms + NORM_EPS) * n2_ref[...]).astype(BF16)


def _merge(branches, proj, x2, w_branch, w_out, norm2_g, layer, tm):
    T = x2.shape[0]
    ob = pl.BlockSpec((tm, BRANCH_WIDTH), lambda i: (i, 0))
    gate = lambda n: pl.BlockSpec((tm, D_MODEL), lambda i, n=n: (i, n))
    xs = pl.BlockSpec((tm, D_MODEL), lambda i: (i, 0))
    return pl.pallas_call(
        _merge_kernel,
        out_shape=(jax.ShapeDtypeStruct((T, D_MODEL), F32), jax.ShapeDtypeStruct((T, D_MODEL), BF16)),
        grid=(T // tm,),
        in_specs=[ob, ob, ob, ob, gate(0), gate(1), gate(2), gate(3), xs,
                  pl.BlockSpec((None, N_BRANCHES, BRANCH_WIDTH, D_MODEL), lambda i: (layer, 0, 0, 0)),
                  pl.BlockSpec((None, D_MODEL, D_MODEL), lambda i: (layer, 0, 0)),
                  pl.BlockSpec((None, 1, D_MODEL), lambda i: (layer, 0, 0))],
        out_specs=(xs, xs),
        compiler_params=_cparams(("parallel",)),
        name="merge_out_proj",
    )(*branches, proj, proj, proj, proj, x2, w_branch, w_out, norm2_g)


def _ffn_kernel(x_ref, h_ref, w1_ref, w2_ref, out_ref):
    @pl.when(pl.program_id(1) == 0)
    def _():
        out_ref[...] = x_ref[...]

    a = jnp.maximum(_dot(h_ref[...], w1_ref[...]), 0.0)
    out_ref[...] += _dot((a * a).astype(BF16), w2_ref[...])


def _ffn(x2, h2, w1, w2, layer, tm, tf):
    T = x2.shape[0]
    xs = pl.BlockSpec((tm, D_MODEL), lambda i, f: (i, 0))
    return pl.pallas_call(
        _ffn_kernel,
        out_shape=jax.ShapeDtypeStruct((T, D_MODEL), F32),
        grid=(T // tm, D_FF // tf),
        in_specs=[xs, xs,
                  pl.BlockSpec((None, D_MODEL, tf), lambda i, f: (layer, 0, f)),
                  pl.BlockSpec((None, tf, D_MODEL), lambda i, f: (layer, f, 0))],
        out_specs=xs,
        compiler_params=_cparams(("parallel", "arbitrary")),
        name="relu2_mlp",
    )(x2, h2, w1, w2)


def _pad_lanes(a, width):
    return jnp.pad(a, [(0, 0)] * (a.ndim - 1) + [(0, width - a.shape[-1])])


def _swap_halves(a):
    h = a.shape[-1] // 2
    return jnp.concatenate([a[..., h:], a[..., :h]], axis=-1)


def _prep_params(norm1_g, w_in, dn_a_log, dn_dt_bias, dn_out_norm_g, mla_q_norm_g, mla_kv_norm_g, mla_w_uq,
                 mla_w_ukv, mla_qk_q_g, mla_qk_k_g, sg_v_norm_g, sg_b_s, fox_q_norm_g, fox_k_norm_g, fox_f_bias,
                 w_branch, w_out, norm2_g, w_ff1, w_ff2):
    L = w_in.shape[0]
    col = lambda a, b: w_in[:, :, a:b]
    kr = col(O_MLA_KR, O_MLA_KR + MLA_ROPE)
    w_p = jnp.concatenate([
        col(O_GATES, O_END), col(O_DN_QKV, O_DN_Z), col(O_DN_Z, O_DN_A), col(O_SG_U, O_SG_V),
        col(O_SG_V, O_FOX_QKV), col(O_FOX_QKV, O_FOX_F), col(O_MLA_CQ, O_MLA_CKV), col(O_MLA_CKV, O_MLA_KR),
        kr, _swap_halves(kr)], axis=-1).astype(BF16)
    assert w_p.shape[-1] == PROJ_WIDTH
    w_small = _pad_lanes(jnp.concatenate([col(O_DN_A, O_DN_B), col(O_DN_B, O_MLA_CQ), col(O_FOX_F, O_GATES)],
                                         axis=-1), LANES).astype(BF16)

    def row(a, width=LANES, offset=0):
        a = a.reshape(L, 1, -1)
        return jnp.pad(a, ((0, 0), (0, 0), (offset, width - offset - a.shape[-1])))

    wq = mla_w_uq.reshape(L, MLA_Q_RANK, HEADS, MLA_QK_DIM)
    wq_r = wq[..., MLA_NOPE:]
    w_uq_p = jnp.concatenate([wq, _swap_halves(wq_r)], axis=-1).reshape(L, MLA_Q_RANK, HEADS * QK_PAD).astype(BF16)

    def qk_gain(g):
        return jnp.concatenate([g, _swap_halves(g[:, MLA_NOPE:])], axis=-1).reshape(L, 1, QK_PAD)

    bias_full = jnp.repeat(jnp.swapaxes(sg_b_s, 1, 2), HEAD_DIM, axis=-1)
    return dict(
        norm1_g=norm1_g.reshape(L, 1, D_MODEL), w_p=w_p, w_small=w_small,
        a_log=row(dn_a_log), dt=row(dn_dt_bias), dn_og=dn_out_norm_g.reshape(L, 1, HEAD_DIM),
        mla_qn=mla_q_norm_g.reshape(L, 1, MLA_Q_RANK), mla_kvn=mla_kv_norm_g.reshape(L, 1, MLA_KV_RANK),
        w_uq_p=w_uq_p, w_ukv=mla_w_ukv.astype(BF16), mla_gq=qk_gain(mla_qk_q_g), mla_gk=qk_gain(mla_qk_k_g),
        sg_g=sg_v_norm_g.reshape(L, 1, HEADS * HEAD_DIM), sg_bias=bias_full,
        fox_fb=row(fox_f_bias, offset=2 * HEADS), fox_gq=fox_q_norm_g.reshape(L, 1, HEAD_DIM),
        fox_gk=fox_k_norm_g.reshape(L, 1, HEAD_DIM),
        w_branch=w_branch.astype(BF16), w_out=w_out.astype(BF16), norm2_g=norm2_g.reshape(L, 1, D_MODEL),
        w_ff1=w_ff1.astype(BF16), w_ff2=w_ff2.astype(BF16),
    )


def _tiles(S):
    pick = lambda want: min(want, S)
    return dict(in_tm=pick(2048), in_tn=1024, dn_rows=pick(512), prep_tm=pick(512), attn_blk=pick(512),
                merge_tm=pick(512), ffn_tm=pick(1024), ffn_tf=2048, rope_tm=pick(1024))


def kernel(x, positions, norm1_g, w_in, dn_conv_w, dn_a_log, dn_dt_bias, dn_out_norm_g, mla_q_norm_g, mla_kv_norm_g, mla_w_uq, mla_w_ukv, mla_qk_q_g, mla_qk_k_g, sg_v_norm_g, sg_w_s, sg_b_s, fox_q_norm_g, fox_k_norm_g, fox_f_bias, w_branch, w_out, norm2_g, w_ff1, w_ff2):
    B, S, D = x.shape
    assert D == D_MODEL and S % LANES == 0
    T = B * S
    depth = w_in.shape[0]
    t = _tiles(S)
    p = _prep_params(norm1_g, w_in, dn_a_log, dn_dt_bias, dn_out_norm_g, mla_q_norm_g, mla_kv_norm_g, mla_w_uq,
                     mla_w_ukv, mla_qk_q_g, mla_qk_k_g, sg_v_norm_g, sg_b_s, fox_q_norm_g, fox_k_norm_g,
                     fox_f_bias, w_branch, w_out, norm2_g, w_ff1, w_ff2)
    cos_t, sin_t = _rope_tables(positions, t["rope_tm"])
    mla_shift, mla_ok = _logit_shift(mla_qk_q_g, mla_qk_k_g, MLA_QK_DIM)
    fox_shift, fox_ok = _logit_shift(fox_q_norm_g, fox_k_norm_g, HEAD_DIM)
    x2 = x.reshape(T, D)
    for l in range(depth):
        proj, small = _in_proj(x2, p["norm1_g"], p["w_p"], p["w_small"], l, t["in_tm"], t["in_tn"])
        o_a = _deltanet(proj, small, dn_conv_w, p["a_log"], p["dt"], p["dn_og"], l, B, S, t["dn_rows"])
        (qb, kb, vtb), (qd, kd, vtd), o_c = _mixer_prep(proj, small, cos_t, sin_t, p, mla_shift, fox_shift, sg_w_s,
                                                          l, B, S, t["prep_tm"])
        o_b = _attention(qb, kb, vtb, mla_ok[l], B, S, t["attn_blk"])
        o_d = _attention(qd, kd, vtd, fox_ok[l], B, S, t["attn_blk"])
        x2, h2 = _merge((o_a, o_b, o_c, o_d), proj, x2, p["w_branch"], p["w_out"], p["norm2_g"], l, t["merge_tm"])
        x2 = _ffn(x2, h2, p["w_ff1"], p["w_ff2"], l, t["ffn_tm"], t["ffn_tf"])
    return x2.reshape(B, S, D)
```

```python
import functools
import math

import jax
import jax.numpy as jnp
from jax import lax
from jax.experimental import pallas as pl
from jax.experimental.pallas import tpu as pltpu

F32 = jnp.float32
BF16 = jnp.bfloat16

D_MODEL = 1024
NORM_EPS = 1e-6
N_BRANCHES = 4
BRANCH_WIDTH = 512
D_FF = 4 * D_MODEL
HEADS = 4
HEAD_DIM = 128
DN_CONV = 4
DN_CHUNK = 128
MLA_Q_RANK = 256
MLA_KV_RANK = 128
MLA_NOPE = 128
MLA_ROPE = 64
MLA_QK_DIM = MLA_NOPE + MLA_ROPE
ROPE_THETA = 10000.0
SG_CHUNK = 128
QK_PAD = 256
LANES = 128
NEG_BIG = -1e30
LOG2E = math.log2(math.e)

PROJ_GATES = 0
PROJ_DN_Q = 4096
PROJ_SG_U = 6144
PROJ_SG_V = 6656
PROJ_FOX_Q = 7168
PROJ_MLA_CQ = 8704
PROJ_MLA_CKV = 8960
PROJ_MLA_KR = 9088
PROJ_WIDTH = 9216

_SPLITS = (1536, 512, 4, 4, 256, 128, 64, 512, 512, 1536, 4, 4096)
_OFF = [0]
for _s in _SPLITS:
    _OFF.append(_OFF[-1] + _s)
(O_DN_QKV, O_DN_Z, O_DN_A, O_DN_B, O_MLA_CQ, O_MLA_CKV, O_MLA_KR, O_SG_U, O_SG_V, O_FOX_QKV, O_FOX_F,
 O_GATES, O_END) = _OFF

VMEM_LIMIT = 56 * 1024 * 1024


def _cparams(sem):
    return pltpu.CompilerParams(dimension_semantics=sem, vmem_limit_bytes=VMEM_LIMIT)


def _dot(a, b):
    return jnp.dot(a, b, preferred_element_type=F32)


def _dot_nt(a, b):
    return lax.dot_general(a, b, (((1,), (1,)), ((), ())), preferred_element_type=F32)


def _dot_f32(a, b):
    return jnp.dot(a, b, preferred_element_type=F32, precision=lax.Precision.HIGHEST)


def _sigmoid(x):
    return 1.0 / (1.0 + jnp.exp(-x))


def _softplus(x):
    return jnp.maximum(x, 0.0) + jnp.log1p(jnp.exp(-jnp.abs(x)))


def _gelu_tanh(x):
    return 0.5 * x * (1.0 + jnp.tanh(math.sqrt(2.0 / math.pi) * (x + 0.044715 * (x * x * x))))


def _tri_masks(n):
    r = lax.broadcasted_iota(jnp.int32, (n, n), 0)
    c = lax.broadcasted_iota(jnp.int32, (n, n), 1)
    return r >= c, r > c


def _rope_kernel(pos_ref, freq_ref, sign_ref, cos_ref, sin_ref):
    ang = pos_ref[...] * freq_ref[...]
    lane = lax.broadcasted_iota(jnp.int32, ang.shape, 1)
    live = lane < MLA_ROPE
    cos_ref[...] = jnp.where(live, jnp.cos(ang), 0.0)
    sin_ref[...] = jnp.where(live, jnp.sin(ang) * sign_ref[...], 0.0)


def _rope_tables(positions, tm):
    T = positions.size
    half = MLA_ROPE // 2
    inv_freq = ROPE_THETA ** (-jnp.arange(0, MLA_ROPE, 2, dtype=F32) / MLA_ROPE)
    zeros = jnp.zeros((LANES - MLA_ROPE,), F32)
    freq = jnp.concatenate([inv_freq, inv_freq, zeros]).reshape(1, LANES)
    sign = jnp.concatenate([-jnp.ones((half,), F32), jnp.ones((half,), F32), zeros]).reshape(1, LANES)
    row = pl.BlockSpec((1, LANES), lambda i: (0, 0))
    tab = pl.BlockSpec((tm, LANES), lambda i: (i, 0))
    return pl.pallas_call(
        _rope_kernel,
        out_shape=(jax.ShapeDtypeStruct((T, LANES), F32), jax.ShapeDtypeStruct((T, LANES), F32)),
        grid=(T // tm,),
        in_specs=[tab, row, row],
        out_specs=(tab, tab),
        compiler_params=_cparams(("parallel",)),
        name="rope_tables",
    )(jnp.broadcast_to(positions.reshape(T, 1).astype(F32), (T, LANES)), freq, sign)


def _in_proj_kernel(x_ref, g_ref, w_ref, ws_ref, proj_ref, small_ref, h_scr):
    @pl.when(pl.program_id(1) == 0)
    def _():
        x = x_ref[...]
        ms = jnp.mean(x * x, axis=-1, keepdims=True)
        h = (x * lax.rsqrt(ms + NORM_EPS) * g_ref[...]).astype(BF16)
        h_scr[...] = h
        small_ref[...] = _dot(h, ws_ref[...])

    proj_ref[...] = _dot(h_scr[...], w_ref[...]).astype(BF16)


def _in_proj(x2, g, w_p, w_small, layer, tm, tn):
    T = x2.shape[0]
    return pl.pallas_call(
        _in_proj_kernel,
        out_shape=(jax.ShapeDtypeStruct((T, PROJ_WIDTH), BF16), jax.ShapeDtypeStruct((T, LANES), F32)),
        grid=(T // tm, PROJ_WIDTH // tn),
        in_specs=[
            pl.BlockSpec((tm, D_MODEL), lambda i, j: (i, 0)),
            pl.BlockSpec((None, 1, D_MODEL), lambda i, j: (layer, 0, 0)),
            pl.BlockSpec((None, D_MODEL, tn), lambda i, j: (layer, 0, j)),
            pl.BlockSpec((None, D_MODEL, LANES), lambda i, j: (layer, 0, 0)),
        ],
        out_specs=(pl.BlockSpec((tm, tn), lambda i, j: (i, j)),
                   pl.BlockSpec((tm, LANES), lambda i, j: (i, 0))),
        scratch_shapes=[pltpu.VMEM((tm, D_MODEL), BF16)],
        compiler_params=_cparams(("parallel", "arbitrary")),
        name="in_proj",
    )(x2, g, w_p, w_small)


INV_BASE = 8


def _inverse_masks(n):
    r = lax.broadcasted_iota(jnp.int32, (n, n), 0)
    c = lax.broadcasted_iota(jnp.int32, (n, n), 1)
    same = lambda s: (r // s) == (c // s)
    levels = []
    s = INV_BASE
    while s < n:
        levels.append(same(2 * s) & jnp.logical_not(same(s)))
        s *= 2
    as16 = lambda m: jnp.where(m, 1.0, 0.0).astype(BF16)
    return (r == c).astype(F32), as16(same(INV_BASE)), [as16(m) for m in levels]


def _unit_lower_inverse(nmats, masks):
    eye, base, levels = masks
    n16 = [n.astype(BF16) for n in nmats]
    ps = [n * base for n in n16]
    xs = [eye - p.astype(F32) for p in ps]
    for _ in range(int(math.log2(INV_BASE)) - 1):
        ps = [_dot(p, p).astype(BF16) for p in ps]
        xs = [x + _dot(x.astype(BF16), p) for x, p in zip(xs, ps)]
    for off in levels:
        x16 = [x.astype(BF16) for x in xs]
        ts = [_dot(n * off, xb).astype(BF16) for n, xb in zip(n16, x16)]
        xs = [x - _dot(xb, t) for x, xb, t in zip(xs, x16, ts)]
    return xs


def _deltanet_kernel(q_ref, k_ref, v_ref, z_ref, small_ref, cw_ref, alog_ref, dt_ref, og_ref, o_ref,
                     qbuf, kbuf, vbuf, qs, ks, vs, state, *, rows):
    C = DN_CHUNK
    W = HEADS * HEAD_DIM
    NC = rows // C

    @pl.when(pl.program_id(1) == 0)
    def _():
        state[...] = jnp.zeros_like(state)
        for buf in (qbuf, kbuf, vbuf):
            buf[0:8, :] = jnp.zeros((8, W), F32)

    def conv_silu(buf, x_ref, w, dst):
        buf[8:8 + rows, :] = x_ref[...].astype(F32)
        acc = buf[8:8 + rows, :] * w[DN_CONV - 1:DN_CONV, :]
        for s in range(1, DN_CONV):
            acc = acc + buf[8 - s:8 - s + rows, :] * w[DN_CONV - 1 - s:DN_CONV - s, :]
        buf[0:8, :] = buf[rows:rows + 8, :]
        dst[...] = acc * _sigmoid(acc)

    cw = cw_ref[...]
    conv_silu(qbuf, q_ref, cw[:, 0:W], qs)
    conv_silu(kbuf, k_ref, cw[:, W:2 * W], ks)
    conv_silu(vbuf, v_ref, cw[:, 2 * W:3 * W], vs)

    small = small_ref[...]
    g_all = -jnp.exp(alog_ref[...]) * _softplus(small + dt_ref[...])
    beta_all = _sigmoid(small)

    tril, strict = _tri_masks(C)
    tril_f = tril.astype(F32)
    inv_masks = _inverse_masks(C)
    og = og_ref[...]

    probs = [(c, h) for c in range(NC) for h in range(HEADS)]
    gcs = [_dot_f32(tril_f, g_all[c * C:(c + 1) * C, :]) for c in range(NC)]
    gcts = [gc.T for gc in gcs]
    rsl = lambda c: slice(c * C, (c + 1) * C)
    hsl = lambda h: slice(h * HEAD_DIM, (h + 1) * HEAD_DIM)
    qn, kn = [], []
    for c, h in probs:
        q = qs[rsl(c), hsl(h)]
        k = ks[rsl(c), hsl(h)]
        qn.append(q * (lax.rsqrt(jnp.sum(q * q, axis=-1, keepdims=True) + NORM_EPS) * (HEAD_DIM ** -0.5)))
        kn.append(k * lax.rsqrt(jnp.sum(k * k, axis=-1, keepdims=True) + NORM_EPS))
    gcol = [gcs[c][:, h:h + 1] for c, h in probs]
    bcol = [beta_all[rsl(c), HEADS + h:HEADS + h + 1] for c, h in probs]
    glast = [gcs[c][C - 1:C, h:h + 1] for c, h in probs]
    decay = [jnp.where(tril, jnp.exp(jnp.where(tril, gcol[i] - gcts[c][h:h + 1, :], 0.0)), 0.0)
             for i, (c, h) in enumerate(probs)]
    kb = [k * b for k, b in zip(kn, bcol)]
    k16 = [k.astype(BF16) for k in kn]
    qk_kk = [_dot_nt(jnp.concatenate([kbi, q], axis=0).astype(BF16), k) for kbi, q, k in zip(kb, qn, k16)]
    nmat = [jnp.where(strict, m[:C] * d, 0.0) for m, d in zip(qk_kk, decay)]
    a_qk = [(m[C:] * d).astype(BF16) for m, d in zip(qk_kk, decay)]
    eg = [jnp.exp(g) for g in gcol]
    rhs = [jnp.concatenate([vs[rsl(c), hsl(h)] * bcol[i], kb[i] * eg[i]], axis=-1).astype(BF16)
           for i, (c, h) in enumerate(probs)]
    xinv = _unit_lower_inverse(nmat, inv_masks)
    sol = [_dot(x.astype(BF16), r) for x, r in zip(xinv, rhs)]
    u = [s[:, :HEAD_DIM] for s in sol]
    wq = [jnp.concatenate([s[:, HEAD_DIM:], q * e], axis=0).astype(BF16) for s, q, e in zip(sol, qn, eg)]
    kdt = [(k * jnp.exp(gl - g)).T.astype(BF16) for k, gl, g in zip(kn, glast, gcol)]
    egl = [jnp.exp(gl) for gl in glast]

    st = [state[h] for h in range(HEADS)]
    for c in range(NC):
        idx = [c * HEADS + h for h in range(HEADS)]
        r = [_dot(wq[i], st[h].astype(BF16)) for h, i in enumerate(idx)]
        v16 = [(u[i] - r[h][:C]).astype(BF16) for h, i in enumerate(idx)]
        o = [r[h][C:] + _dot(a_qk[i], v16[h]) for h, i in enumerate(idx)]
        st = [st[h] * egl[i] + _dot(kdt[i], v16[h]) for h, i in enumerate(idx)]
        for h in range(HEADS):
            oh = o[h] * lax.rsqrt(jnp.mean(o[h] * o[h], axis=-1, keepdims=True) + NORM_EPS) * og
            z = z_ref[rsl(c), hsl(h)].astype(F32)
            o_ref[rsl(c), hsl(h)] = (oh * (z * _sigmoid(z))).astype(BF16)
    for h in range(HEADS):
        state[h] = st[h]


def _deltanet(proj, small, conv_w, a_log_row, dt_row, out_g, layer, B, S, rows):
    T = B * S
    W = HEADS * HEAD_DIM
    nb = S // rows
    blk = lambda col: pl.BlockSpec((rows, W), lambda b, i, col=col: (b * nb + i, col))
    prow = pl.BlockSpec((None, 1, LANES), lambda b, i: (layer, 0, 0))
    return pl.pallas_call(
        functools.partial(_deltanet_kernel, rows=rows),
        out_shape=jax.ShapeDtypeStruct((T, W), BF16),
        grid=(B, nb),
        in_specs=[
            blk(PROJ_DN_Q // W), blk(PROJ_DN_Q // W + 1), blk(PROJ_DN_Q // W + 2), blk(PROJ_DN_Q // W + 3),
            pl.BlockSpec((rows, LANES), lambda b, i: (b * nb + i, 0)),
            pl.BlockSpec((None, DN_CONV, 3 * W), lambda b, i: (layer, 0, 0)),
            prow, prow, prow,
        ],
        out_specs=pl.BlockSpec((rows, W), lambda b, i: (b * nb + i, 0)),
        scratch_shapes=[pltpu.VMEM((rows + 8, W), F32)] * 3 + [pltpu.VMEM((rows, W), F32)] * 3
        + [pltpu.VMEM((HEADS, HEAD_DIM, HEAD_DIM), F32)],
        compiler_params=_cparams(("parallel", "arbitrary")),
        name="deltanet",
    )(proj, proj, proj, proj, small, conv_w, a_log_row, dt_row, out_g)


VT_ONES = 16
VT_ROWS = HEAD_DIM + VT_ONES


def _store_vt(vt_out, h, v16):
    r = lax.broadcasted_iota(jnp.int32, (HEAD_DIM, HEAD_DIM), 0)
    c = lax.broadcasted_iota(jnp.int32, (HEAD_DIM, HEAD_DIM), 1)
    eye = jnp.where(r == c, 1.0, 0.0).astype(BF16)
    base = h * VT_ROWS
    vt_out[base:base + HEAD_DIM, :] = _dot_nt(eye, v16).astype(BF16)
    vt_out[base + HEAD_DIM:base + VT_ROWS, :] = jnp.ones((VT_ONES, v16.shape[0]), BF16)


def _mla_prep_kernel(cq_ref, ckv_ref, kr_ref, cos_ref, sin_ref, qn_ref, kvn_ref, wq_ref, wkv_ref,
                     gq_ref, gk_ref, off_ref, q_out, k_out, vt_out):
    cq = cq_ref[...].astype(F32)
    ckv = ckv_ref[...].astype(F32)
    cq2 = cq * cq
    cq_ms = jnp.sum(cq2[:, :LANES] + cq2[:, LANES:], axis=-1, keepdims=True) * (1.0 / MLA_Q_RANK)
    cq = cq * lax.rsqrt(cq_ms + NORM_EPS) * qn_ref[...]
    ckv = ckv * lax.rsqrt(jnp.mean(ckv * ckv, axis=-1, keepdims=True) + NORM_EPS) * kvn_ref[...]
    q_all = _dot(cq.astype(BF16), wq_ref[...])
    kv_all = _dot(ckv.astype(BF16), wkv_ref[...])
    cos = cos_ref[...]
    sin = sin_ref[...]
    gq = gq_ref[...]
    gk = gk_ref[...]
    lane = lax.broadcasted_iota(jnp.int32, cos.shape, 1)
    live = lane < MLA_ROPE
    scale = MLA_QK_DIM ** -0.5 * LOG2E
    off = off_ref[...]

    def rope(xr):
        return xr * cos + pltpu.roll(xr, MLA_ROPE, 1) * sin

    kr = kr_ref[...].astype(F32)
    kr_sq = jnp.where(live, kr * kr, 0.0)
    kr_rot = rope(kr * gk[:, MLA_NOPE:])
    for h in range(HEADS):
        base = h * QK_PAD
        qn = q_all[:, base:base + MLA_NOPE]
        qr = q_all[:, base + MLA_NOPE:base + QK_PAD]
        ss = jnp.sum(qn * qn + jnp.where(live, qr * qr, 0.0), axis=-1, keepdims=True)
        rinv = lax.rsqrt(ss * (1.0 / MLA_QK_DIM) + NORM_EPS) * scale
        q_out[:, base:base + MLA_NOPE] = (qn * rinv * gq[:, :MLA_NOPE]).astype(BF16)
        q_rot = jnp.where(lane == MLA_ROPE, 1.0, rope(qr * rinv * gq[:, MLA_NOPE:]))
        q_out[:, base + MLA_NOPE:base + QK_PAD] = q_rot.astype(BF16)

        kn = kv_all[:, base:base + MLA_NOPE]
        v = kv_all[:, base + MLA_NOPE:base + QK_PAD]
        ssk = jnp.sum(kn * kn + kr_sq, axis=-1, keepdims=True)
        rk = lax.rsqrt(ssk * (1.0 / MLA_QK_DIM) + NORM_EPS)
        k_out[:, base:base + MLA_NOPE] = (kn * rk * gk[:, :MLA_NOPE]).astype(BF16)
        k_rot = jnp.where(lane == MLA_ROPE, off, kr_rot * rk)
        k_out[:, base + MLA_NOPE:base + QK_PAD] = k_rot.astype(BF16)
        _store_vt(vt_out, h, v.astype(BF16))


def _split3(c):
    hi = c.astype(BF16).astype(F32)
    r = c - hi
    mid = r.astype(BF16).astype(F32)
    lo = (r - mid).astype(BF16).astype(F32)
    return hi, mid, lo


def _fox_prep_kernel(q_ref, k_ref, v_ref, small_ref, fb_ref, gq_ref, gk_ref, off_ref, q_out, k_out, vt_out, carry,
                     *, tm):
    C = LANES
    tril, _ = _tri_masks(C)
    tril_f = tril.astype(F32)
    logf = -_softplus(-(small_ref[...] + fb_ref[...]))
    gq = gq_ref[...]
    gk = gk_ref[...]
    off = off_ref[...]
    scale = HEAD_DIM ** -0.5 * LOG2E
    lane = lax.broadcasted_iota(jnp.int32, (C, LANES), 1)
    for c in range(tm // C):
        rs = slice(c * C, (c + 1) * C)
        cum = _dot_f32(tril_f, logf[rs, :]) + carry[...]
        carry[...] = cum[C - 1:C, :]
        for h in range(HEADS):
            hs = slice(h * HEAD_DIM, (h + 1) * HEAD_DIM)
            q = q_ref[rs, hs].astype(F32)
            k = k_ref[rs, hs].astype(F32)
            q = q * lax.rsqrt(jnp.mean(q * q, axis=-1, keepdims=True) + NORM_EPS) * gq * scale
            k = k * lax.rsqrt(jnp.mean(k * k, axis=-1, keepdims=True) + NORM_EPS) * gk
            hi, mid, lo = _split3(cum[:, 2 * HEADS + h:2 * HEADS + h + 1] * LOG2E)
            one = jnp.where(lane < 7, 1.0, 0.0)
            parts = jnp.where(lane == 0, hi, jnp.where(lane == 1, mid, jnp.where(lane == 2, lo, 0.0)))
            q_aug = jnp.where(lane < 3, parts, one)
            nparts = jnp.where(lane == 3, -hi, jnp.where(lane == 4, -mid, jnp.where(lane == 5, -lo,
                                                                                     jnp.where(lane == 6, off, 0.0))))
            k_aug = jnp.where(lane < 3, one, nparts)
            base = h * QK_PAD
            q_out[rs, base:base + HEAD_DIM] = q.astype(BF16)
            q_out[rs, base + HEAD_DIM:base + QK_PAD] = q_aug.astype(BF16)
            k_out[rs, base:base + HEAD_DIM] = k.astype(BF16)
            k_out[rs, base + HEAD_DIM:base + QK_PAD] = k_aug.astype(BF16)
    for h in range(HEADS):
        _store_vt(vt_out, h, v_ref[:, h * HEAD_DIM:(h + 1) * HEAD_DIM])


def _attn_kernel(q_ref, k_ref, vt_ref, o_ref, m_scr, acc_scr, *, blk):
    i = pl.program_id(1)
    heads = range(HEADS)
    qsl = lambda h: slice(h * QK_PAD, (h + 1) * QK_PAD)
    vsl = lambda h: slice(h * VT_ROWS, (h + 1) * VT_ROWS)
    osl = lambda h: slice(h * HEAD_DIM, (h + 1) * HEAD_DIM)
    q = [q_ref[:, qsl(h)] for h in heads]
    m_scr[...] = jnp.full_like(m_scr, NEG_BIG)
    acc_scr[...] = jnp.zeros_like(acc_scr)

    def scores(j):
        rows = pl.ds(pl.multiple_of(j * blk, blk), blk)
        return [_dot_nt(k_ref[rows, qsl(h)], q[h]) for h in heads]

    def accumulate(j, st, masked):
        if masked:
            r = lax.broadcasted_iota(jnp.int32, (blk, blk), 0)
            c = lax.broadcasted_iota(jnp.int32, (blk, blk), 1)
            keep = r <= c
            st = [jnp.where(keep, s, NEG_BIG) for s in st]
        for grp in ((0, 1), (2, 3)):
            m_old = [m_scr[h] for h in grp]
            m_new = [jnp.maximum(mo, jnp.max(st[h], axis=0, keepdims=True)) for mo, h in zip(m_old, grp)]
            alpha = [jnp.exp2(mo - mn) for mo, mn in zip(m_old, m_new)]
            p = [jnp.exp2(st[h] - mn).astype(BF16) for h, mn in zip(grp, m_new)]
            pv = [_dot(vt_ref[j, vsl(h), :], pp) for h, pp in zip(grp, p)]
            for n, h in enumerate(grp):
                m_scr[h] = m_new[n]
                acc_scr[h] = alpha[n] * acc_scr[h] + pv[n]

    def pair(t, carry):
        j = 2 * t
        st_a = scores(j)
        st_b = scores(j + 1)
        accumulate(j, st_a, False)
        accumulate(j + 1, st_b, False)
        return carry

    lax.fori_loop(0, i // 2, pair, 0)

    @pl.when(i % 2 == 1)
    def _():
        accumulate(i - 1, scores(i - 1), False)

    accumulate(i, scores(i), True)
    for h in heads:
        acc = acc_scr[h]
        o_ref[:, osl(h)] = (acc[:HEAD_DIM] / acc[HEAD_DIM:HEAD_DIM + 1]).T.astype(BF16)


def _attn_shifted_kernel(*refs, blk, nb, nq):
    q_refs, (k_ref, vt_ref, o_ref, acc_scr) = refs[:nq], refs[nq:]
    step = pl.program_id(1)
    heads = range(HEADS)
    qsl = lambda h: slice(h * QK_PAD, (h + 1) * QK_PAD)
    vsl = lambda h: slice(h * VT_ROWS, (h + 1) * VT_ROWS)
    osl = lambda h: slice(h * HEAD_DIM, (h + 1) * HEAD_DIM)
    half = blk // 2
    keep_lo = (lax.broadcasted_iota(jnp.int32, (half, blk), 0) <= lax.broadcasted_iota(jnp.int32, (half, blk), 1))
    keep_hi = (lax.broadcasted_iota(jnp.int32, (half, half), 0) <= lax.broadcasted_iota(jnp.int32, (half, half), 1))

    def one_query_block(q_ref, i):
        q = [q_ref[:, qsl(h)] for h in heads]
        acc_scr[...] = jnp.zeros_like(acc_scr)

        def block(j):
            rows = pl.ds(pl.multiple_of(j * blk, blk), blk)
            st = [_dot_nt(k_ref[rows, qsl(h)], q[h]) for h in heads]
            pv = [_dot(vt_ref[j, vsl(h), :], jnp.exp2(st[h]).astype(BF16)) for h in heads]
            for h in heads:
                acc_scr[h] += pv[h]

        def diagonal_block():
            lo = pl.ds(pl.multiple_of(i * blk, blk), half)
            hi = pl.ds(pl.multiple_of(i * blk + half, half), half)
            s_lo = [jnp.where(keep_lo, _dot_nt(k_ref[lo, qsl(h)], q[h]), NEG_BIG) for h in heads]
            s_hi = [jnp.where(keep_hi, _dot_nt(k_ref[hi, qsl(h)], q_ref[half:, qsl(h)]), NEG_BIG) for h in heads]
            pv_lo = [_dot(vt_ref[i, vsl(h), :half], jnp.exp2(s_lo[h]).astype(BF16)) for h in heads]
            pv_hi = [_dot(vt_ref[i, vsl(h), half:], jnp.exp2(s_hi[h]).astype(BF16)) for h in heads]
            for h in heads:
                acc_scr[h] += pv_lo[h]
                acc_scr[h, :, half:] += pv_hi[h]

        def pair(t, carry):
            block(2 * t)
            block(2 * t + 1)
            return carry

        lax.fori_loop(0, i // 2, pair, 0)

        @pl.when(i % 2 == 1)
        def _():
            block(i - 1)

        diagonal_block()
        out_rows = pl.ds(pl.multiple_of(i * blk, blk), blk)
        for h in heads:
            acc = acc_scr[h]
            o_ref[out_rows, osl(h)] = (acc[:HEAD_DIM] / acc[HEAD_DIM:HEAD_DIM + 1]).T.astype(BF16)

    one_query_block(q_refs[0], step)
    if nq == 2:
        one_query_block(q_refs[1], nb - 1 - step)


def _attention(q, k, vt, shifted, B, S, blk):
    T = B * S
    nb = S // blk
    W = HEADS * HEAD_DIM
    HW = HEADS * QK_PAD
    out_shape = jax.ShapeDtypeStruct((T, W), BF16)
    k_spec = pl.BlockSpec((S, HW), lambda b, i: (b, 0))
    vt_spec = pl.BlockSpec((None, nb, HEADS * VT_ROWS, blk), lambda b, i: (b, 0, 0, 0))
    acc = pltpu.VMEM((HEADS, VT_ROWS, blk), F32)
    params = _cparams(("parallel", "arbitrary"))
    online = pl.pallas_call(
        functools.partial(_attn_kernel, blk=blk), name="causal_attention", out_shape=out_shape, grid=(B, nb),
        in_specs=[pl.BlockSpec((blk, HW), lambda b, i: (b * nb + i, 0)), k_spec, vt_spec],
        out_specs=pl.BlockSpec((blk, W), lambda b, i: (b * nb + i, 0)),
        scratch_shapes=[pltpu.VMEM((HEADS, 1, blk), F32), acc], compiler_params=params)
    nq = 2 if nb % 2 == 0 else 1
    q_specs = [pl.BlockSpec((blk, HW), lambda b, i: (b * nb + i, 0)),
               pl.BlockSpec((blk, HW), lambda b, i: (b * nb + nb - 1 - i, 0))][:nq]
    fast_call = pl.pallas_call(
        functools.partial(_attn_shifted_kernel, blk=blk, nb=nb, nq=nq), name="causal_attention_shifted",
        out_shape=out_shape, grid=(B, nb // nq), in_specs=q_specs + [k_spec, vt_spec],
        out_specs=pl.BlockSpec((S, W), lambda b, i: (b, 0)), scratch_shapes=[acc], compiler_params=params)
    fast = lambda q, k, vt: fast_call(*([q] * nq), k, vt)
    return lax.cond(shifted, fast, online, q, k, vt)


SHIFT_MAX = 40.0


def _logit_shift(gq, gk, dim):
    L = gq.shape[0]
    bound = (dim ** 0.5) * LOG2E * 1.02 * jnp.max(jnp.abs(gq.reshape(L, -1)), axis=-1) \
        * jnp.max(jnp.abs(gk.reshape(L, -1)), axis=-1)
    ok = bound <= SHIFT_MAX
    shift = jnp.where(ok, -bound, 0.0).astype(F32)
    return jnp.broadcast_to(shift[:, None, None], (L, 1, LANES)), ok


def _sg_kernel(u_ref, v_ref, g_ref, ws_ref, b_ref, o_ref, *, rows):
    Tn = SG_CHUNK
    tril, _ = _tri_masks(Tn)
    g = g_ref[...]
    bias = b_ref[...]
    for gi in range(HEADS):
        hs = slice(gi * HEAD_DIM, (gi + 1) * HEAD_DIM)
        w = jnp.where(tril, ws_ref[gi], 0.0).astype(BF16)
        for n in range(rows // Tn):
            rs = slice(n * Tn, (n + 1) * Tn)
            v = _gelu_tanh(v_ref[rs, hs].astype(F32))
            v = v * lax.rsqrt(jnp.mean(v * v, axis=-1, keepdims=True) + NORM_EPS) * g[:, hs]
            mixed = _dot(w, v.astype(BF16)) + bias[:, hs]
            u = _gelu_tanh(u_ref[rs, hs].astype(F32))
            o_ref[rs, hs] = (u * mixed).astype(BF16)


N_MLA_IN, N_FOX_IN, N_SG_IN = 12, 8, 5


def _mixer_prep_kernel(*refs, tm):
    mla_in = refs[:N_MLA_IN]
    fox_in = refs[N_MLA_IN:N_MLA_IN + N_FOX_IN]
    sg_in = refs[N_MLA_IN + N_FOX_IN:N_MLA_IN + N_FOX_IN + N_SG_IN]
    mq, mk, mvt, fq, fk, fvt, sg_out, carry = refs[N_MLA_IN + N_FOX_IN + N_SG_IN:]

    @pl.when(pl.program_id(1) == 0)
    def _():
        carry[...] = jnp.zeros_like(carry)

    _mla_prep_kernel(*mla_in, mq, mk, mvt)
    _fox_prep_kernel(*fox_in, fq, fk, fvt, carry, tm=tm)
    _sg_kernel(*sg_in, sg_out, rows=tm)


def _mixer_prep(proj, small, cos_t, sin_t, p, mla_shift, fox_shift, sg_w_s, layer, B, S, tm):
    T = B * S
    nb = S // tm
    W = HEADS * HEAD_DIM
    HW = HEADS * QK_PAD
    rows = lambda width, col=0: pl.BlockSpec((tm, width), lambda b, i, col=col: (b * nb + i, col))
    par = lambda *shape: pl.BlockSpec((None,) + shape, lambda b, i: (layer,) + (0,) * len(shape))
    vt = pl.BlockSpec((None, None, HEADS * VT_ROWS, tm), lambda b, i: (b, i, 0, 0))
    vt_shape = jax.ShapeDtypeStruct((B, nb, HEADS * VT_ROWS, tm), BF16)
    qk_shape = jax.ShapeDtypeStruct((T, HW), BF16)
    mla_specs = [rows(MLA_Q_RANK, PROJ_MLA_CQ // MLA_Q_RANK), rows(MLA_KV_RANK, PROJ_MLA_CKV // MLA_KV_RANK),
                 rows(LANES, PROJ_MLA_KR // LANES), rows(LANES), rows(LANES),
                 par(1, MLA_Q_RANK), par(1, MLA_KV_RANK), par(MLA_Q_RANK, HW), par(MLA_KV_RANK, HW),
                 par(1, QK_PAD), par(1, QK_PAD), par(1, LANES)]
    mla_args = [proj, proj, proj, cos_t, sin_t, p["mla_qn"], p["mla_kvn"], p["w_uq_p"], p["w_ukv"],
                p["mla_gq"], p["mla_gk"], mla_shift]
    fox_specs = [rows(W, PROJ_FOX_Q // W), rows(W, PROJ_FOX_Q // W + 1), rows(W, PROJ_FOX_Q // W + 2), rows(LANES),
                 par(1, LANES), par(1, LANES), par(1, LANES), par(1, LANES)]
    fox_args = [proj, proj, proj, small, p["fox_fb"], p["fox_gq"], p["fox_gk"], fox_shift]
    sg_specs = [rows(W, PROJ_SG_U // W), rows(W, PROJ_SG_V // W), par(1, W), par(HEADS, SG_CHUNK, SG_CHUNK),
                par(SG_CHUNK, W)]
    sg_args = [proj, proj, p["sg_g"], sg_w_s, p["sg_bias"]]
    assert (len(mla_specs), len(fox_specs), len(sg_specs)) == (N_MLA_IN, N_FOX_IN, N_SG_IN)
    out = pl.pallas_call(
        functools.partial(_mixer_prep_kernel, tm=tm),
        out_shape=(qk_shape, qk_shape, vt_shape, qk_shape, qk_shape, vt_shape, jax.ShapeDtypeStruct((T, W), BF16)),
        grid=(B, nb),
        in_specs=mla_specs + fox_specs + sg_specs,
        out_specs=(rows(HW), rows(HW), vt, rows(HW), rows(HW), vt, rows(W)),
        scratch_shapes=[pltpu.VMEM((1, LANES), F32)],
        compiler_params=_cparams(("parallel", "arbitrary")),
        name="mixer_prep",
    )(*mla_args, *fox_args, *sg_args)
    return out[0:3], out[3:6], out[6]


def _merge_kernel(oa, ob, oc, od, g0, g1, g2, g3, x_ref, wb_ref, wo_ref, n2_ref, out_ref, h_ref):
    merged = None
    for i, (o, g) in enumerate(((oa, g0), (ob, g1), (oc, g2), (od, g3))):
        term = _sigmoid(g[...].astype(F32)) * _dot(o[...], wb_ref[i])
        merged = term if merged is None else merged + term
    x1 = x_ref[...] + _dot(merged.astype(BF16), wo_ref[...])
    out_ref[...] = x1
    ms = jnp.mean(x1 * x1, axis=-1, keepdims=True)
    h_ref[...] = (x1 * lax.rsqrt(ms + NORM_EPS) * n2_ref[...]).astype(BF16)


def _merge(branches, proj, x2, w_branch, w_out, norm2_g, layer, tm):
    T = x2.shape[0]
    ob = pl.BlockSpec((tm, BRANCH_WIDTH), lambda i: (i, 0))
    gate = lambda n: pl.BlockSpec((tm, D_MODEL), lambda i, n=n: (i, n))
    xs = pl.BlockSpec((tm, D_MODEL), lambda i: (i, 0))
    return pl.pallas_call(
        _merge_kernel,
        out_shape=(jax.ShapeDtypeStruct((T, D_MODEL), F32), jax.ShapeDtypeStruct((T, D_MODEL), BF16)),
        grid=(T // tm,),
        in_specs=[ob, ob, ob, ob, gate(0), gate(1), gate(2), gate(3), xs,
                  pl.BlockSpec((None, N_BRANCHES, BRANCH_WIDTH, D_MODEL), lambda i: (layer, 0, 0, 0)),
                  pl.BlockSpec((None, D_MODEL, D_MODEL), lambda i: (layer, 0, 0)),
                  pl.BlockSpec((None, 1, D_MODEL), lambda i: (layer, 0, 0))],
        out_specs=(xs, xs),
        compiler_params=_cparams(("parallel",)),
        name="merge_out_proj",
    )(*branches, proj, proj, proj, proj, x2, w_branch, w_out, norm2_g)


def _ffn_kernel(x_ref, h_ref, w1_ref, w2_ref, out_ref):
    @pl.when(pl.program_id(1) == 0)
    def _():
        out_ref[...] = x_ref[...]

    a = jnp.maximum(_dot(h_ref[...], w1_ref[...]), 0.0)
    out_ref[...] += _dot((a * a).astype(BF16), w2_ref[...])


def _ffn(x2, h2, w1, w2, layer, tm, tf):
    T = x2.shape[0]
    xs = pl.BlockSpec((tm, D_MODEL), lambda i, f: (i, 0))
    return pl.pallas_call(
        _ffn_kernel,
        out_shape=jax.ShapeDtypeStruct((T, D_MODEL), F32),
        grid=(T // tm, D_FF // tf),
        in_specs=[xs, xs,
                  pl.BlockSpec((None, D_MODEL, tf), lambda i, f: (layer, 0, f)),
                  pl.BlockSpec((None, tf, D_MODEL), lambda i, f: (layer, f, 0))],
        out_specs=xs,
        compiler_params=_cparams(("parallel", "arbitrary")),
        name="relu2_mlp",
    )(x2, h2, w1, w2)


def _pad_lanes(a, width):
    return jnp.pad(a, [(0, 0)] * (a.ndim - 1) + [(0, width - a.shape[-1])])


def _swap_halves(a):
    h = a.shape[-1] // 2
    return jnp.concatenate([a[..., h:], a[..., :h]], axis=-1)


def _prep_params(norm1_g, w_in, dn_a_log, dn_dt_bias, dn_out_norm_g, mla_q_norm_g, mla_kv_norm_g, mla_w_uq,
                 mla_w_ukv, mla_qk_q_g, mla_qk_k_g, sg_v_norm_g, sg_b_s, fox_q_norm_g, fox_k_norm_g, fox_f_bias,
                 w_branch, w_out, norm2_g, w_ff1, w_ff2):
    L = w_in.shape[0]
    col = lambda a, b: w_in[:, :, a:b]
    kr = col(O_MLA_KR, O_MLA_KR + MLA_ROPE)
    w_p = jnp.concatenate([
        col(O_GATES, O_END), col(O_DN_QKV, O_DN_Z), col(O_DN_Z, O_DN_A), col(O_SG_U, O_SG_V),
        col(O_SG_V, O_FOX_QKV), col(O_FOX_QKV, O_FOX_F), col(O_MLA_CQ, O_MLA_CKV), col(O_MLA_CKV, O_MLA_KR),
        kr, _swap_halves(kr)], axis=-1).astype(BF16)
    assert w_p.shape[-1] == PROJ_WIDTH
    w_small = _pad_lanes(jnp.concatenate([col(O_DN_A, O_DN_B), col(O_DN_B, O_MLA_CQ), col(O_FOX_F, O_GATES)],
                                         axis=-1), LANES).astype(BF16)

    def row(a, width=LANES, offset=0):
        a = a.reshape(L, 1, -1)
        return jnp.pad(a, ((0, 0), (0, 0), (offset, width - offset - a.shape[-1])))

    wq = mla_w_uq.reshape(L, MLA_Q_RANK, HEADS, MLA_QK_DIM)
    wq_r = wq[..., MLA_NOPE:]
    w_uq_p = jnp.concatenate([wq, _swap_halves(wq_r)], axis=-1).reshape(L, MLA_Q_RANK, HEADS * QK_PAD).astype(BF16)

    def qk_gain(g):
        return jnp.concatenate([g, _swap_halves(g[:, MLA_NOPE:])], axis=-1).reshape(L, 1, QK_PAD)

    bias_full = jnp.repeat(jnp.swapaxes(sg_b_s, 1, 2), HEAD_DIM, axis=-1)
    return dict(
        norm1_g=norm1_g.reshape(L, 1, D_MODEL), w_p=w_p, w_small=w_small,
        a_log=row(dn_a_log), dt=row(dn_dt_bias), dn_og=dn_out_norm_g.reshape(L, 1, HEAD_DIM),
        mla_qn=mla_q_norm_g.reshape(L, 1, MLA_Q_RANK), mla_kvn=mla_kv_norm_g.reshape(L, 1, MLA_KV_RANK),
        w_uq_p=w_uq_p, w_ukv=mla_w_ukv.astype(BF16), mla_gq=qk_gain(mla_qk_q_g), mla_gk=qk_gain(mla_qk_k_g),
        sg_g=sg_v_norm_g.reshape(L, 1, HEADS * HEAD_DIM), sg_bias=bias_full,
        fox_fb=row(fox_f_bias, offset=2 * HEADS), fox_gq=fox_q_norm_g.reshape(L, 1, HEAD_DIM),
        fox_gk=fox_k_norm_g.reshape(L, 1, HEAD_DIM),
        w_branch=w_branch.astype(BF16), w_out=w_out.astype(BF16), norm2_g=norm2_g.reshape(L, 1, D_MODEL),
        w_ff1=w_ff1.astype(BF16), w_ff2=w_ff2.astype(BF16),
    )


def _tiles(S):
    pick = lambda want: min(want, S)
    return dict(in_tm=pick(2048), in_tn=1024, dn_rows=pick(512), prep_tm=pick(512), attn_blk=pick(512),
                merge_tm=pick(512), ffn_tm=pick(1024), ffn_tf=2048, rope_tm=pick(1024))


def kernel(x, positions, norm1_g, w_in, dn_conv_w, dn_a_log, dn_dt_bias, dn_out_norm_g, mla_q_norm_g, mla_kv_norm_g, mla_w_uq, mla_w_ukv, mla_qk_q_g, mla_qk_k_g, sg_v_norm_g, sg_w_s, sg_b_s, fox_q_norm_g, fox_k_norm_g, fox_f_bias, w_branch, w_out, norm2_g, w_ff1, w_ff2):
    B, S, D = x.shape
    assert D == D_MODEL and S % LANES == 0
    T = B * S
    depth = w_in.shape[0]
    t = _tiles(S)
    p = _prep_params(norm1_g, w_in, dn_a_log, dn_dt_bias, dn_out_norm_g, mla_q_norm_g, mla_kv_norm_g, mla_w_uq,
                     mla_w_ukv, mla_qk_q_g, mla_qk_k_g, sg_v_norm_g, sg_b_s, fox_q_norm_g, fox_k_norm_g,
                     fox_f_bias, w_branch, w_out, norm2_g, w_ff1, w_ff2)
    cos_t, sin_t = _rope_tables(positions, t["rope_tm"])
    mla_shift, mla_ok = _logit_shift(mla_qk_q_g, mla_qk_k_g, MLA_QK_DIM)
    fox_shift, fox_ok = _logit_shift(fox_q_norm_g, fox_k_norm_g, HEAD_DIM)
    x2 = x.reshape(T, D)
    for l in range(depth):
        proj, small = _in_proj(x2, p["norm1_g"], p["w_p"], p["w_small"], l, t["in_tm"], t["in_tn"])
        o_a = _deltanet(proj, small, dn_conv_w, p["a_log"], p["dt"], p["dn_og"], l, B, S, t["dn_rows"])
        (qb, kb, vtb), (qd, kd, vtd), o_c = _mixer_prep(proj, small, cos_t, sin_t, p, mla_shift, fox_shift, sg_w_s,
                                                          l, B, S, t["prep_tm"])
        o_b = _attention(qb, kb, vtb, mla_ok[l], B, S, t["attn_blk"])
        o_d = _attention(qd, kd, vtd, fox_ok[l], B, S, t["attn_blk"])
        x2, h2 = _merge((o_a, o_b, o_c, o_d), proj, x2, p["w_branch"], p["w_out"], p["norm2_g"], l, t["merge_tm"])
        x2 = _ffn(x2, h2, p["w_ff1"], p["w_ff2"], l, t["ffn_tm"], t["ffn_tf"])
    return x2.reshape(B, S, D)
```

```python
import functools
import math

import jax
import jax.numpy as jnp
from jax import lax
from jax.experimental import pallas as pl
from jax.experimental.pallas import tpu as pltpu

F32 = jnp.float32
BF16 = jnp.bfloat16

D_MODEL = 1024
NORM_EPS = 1e-6
N_BRANCHES = 4
BRANCH_WIDTH = 512
D_FF = 4 * D_MODEL
HEADS = 4
HEAD_DIM = 128
DN_CONV = 4
DN_CHUNK = 128
MLA_Q_RANK = 256
MLA_KV_RANK = 128
MLA_NOPE = 128
MLA_ROPE = 64
MLA_QK_DIM = MLA_NOPE + MLA_ROPE
ROPE_THETA = 10000.0
SG_CHUNK = 128
QK_PAD = 256
LANES = 128
NEG_BIG = -1e30
LOG2E = math.log2(math.e)

PROJ_GATES = 0
PROJ_DN_Q = 4096
PROJ_SG_U = 6144
PROJ_SG_V = 6656
PROJ_FOX_Q = 7168
PROJ_MLA_CQ = 8704
PROJ_MLA_CKV = 8960
PROJ_MLA_KR = 9088
PROJ_WIDTH = 9216

_SPLITS = (1536, 512, 4, 4, 256, 128, 64, 512, 512, 1536, 4, 4096)
_OFF = [0]
for _s in _SPLITS:
    _OFF.append(_OFF[-1] + _s)
(O_DN_QKV, O_DN_Z, O_DN_A, O_DN_B, O_MLA_CQ, O_MLA_CKV, O_MLA_KR, O_SG_U, O_SG_V, O_FOX_QKV, O_FOX_F,
 O_GATES, O_END) = _OFF

VMEM_LIMIT = 56 * 1024 * 1024


def _cparams(sem):
    return pltpu.CompilerParams(dimension_semantics=sem, vmem_limit_bytes=VMEM_LIMIT)


def _dot(a, b):
    return jnp.dot(a, b, preferred_element_type=F32)


def _dot_nt(a, b):
    return lax.dot_general(a, b, (((1,), (1,)), ((), ())), preferred_element_type=F32)


def _dot_f32(a, b):
    return jnp.dot(a, b, preferred_element_type=F32, precision=lax.Precision.HIGHEST)


def _sigmoid(x):
    return 1.0 / (1.0 + jnp.exp(-x))


def _softplus(x):
    return jnp.maximum(x, 0.0) + jnp.log1p(jnp.exp(-jnp.abs(x)))


def _gelu_tanh(x):
    return 0.5 * x * (1.0 + jnp.tanh(math.sqrt(2.0 / math.pi) * (x + 0.044715 * (x * x * x))))


def _tri_masks(n):
    r = lax.broadcasted_iota(jnp.int32, (n, n), 0)
    c = lax.broadcasted_iota(jnp.int32, (n, n), 1)
    return r >= c, r > c


def _rope_kernel(pos_ref, freq_ref, sign_ref, cos_ref, sin_ref):
    ang = pos_ref[...] * freq_ref[...]
    lane = lax.broadcasted_iota(jnp.int32, ang.shape, 1)
    live = lane < MLA_ROPE
    cos_ref[...] = jnp.where(live, jnp.cos(ang), 0.0)
    sin_ref[...] = jnp.where(live, jnp.sin(ang) * sign_ref[...], 0.0)


def _rope_tables(positions, tm):
    T = positions.size
    half = MLA_ROPE // 2
    inv_freq = ROPE_THETA ** (-jnp.arange(0, MLA_ROPE, 2, dtype=F32) / MLA_ROPE)
    zeros = jnp.zeros((LANES - MLA_ROPE,), F32)
    freq = jnp.concatenate([inv_freq, inv_freq, zeros]).reshape(1, LANES)
    sign = jnp.concatenate([-jnp.ones((half,), F32), jnp.ones((half,), F32), zeros]).reshape(1, LANES)
    row = pl.BlockSpec((1, LANES), lambda i: (0, 0))
    tab = pl.BlockSpec((tm, LANES), lambda i: (i, 0))
    return pl.pallas_call(
        _rope_kernel,
        out_shape=(jax.ShapeDtypeStruct((T, LANES), F32), jax.ShapeDtypeStruct((T, LANES), F32)),
        grid=(T // tm,),
        in_specs=[tab, row, row],
        out_specs=(tab, tab),
        compiler_params=_cparams(("parallel",)),
        name="rope_tables",
    )(jnp.broadcast_to(positions.reshape(T, 1).astype(F32), (T, LANES)), freq, sign)


def _in_proj_kernel(x_ref, g_ref, w_ref, ws_ref, proj_ref, small_ref, h_scr):
    @pl.when(pl.program_id(1) == 0)
    def _():
        x = x_ref[...]
        ms = jnp.mean(x * x, axis=-1, keepdims=True)
        h = (x * lax.rsqrt(ms + NORM_EPS) * g_ref[...]).astype(BF16)
        h_scr[...] = h
        small_ref[...] = _dot(h, ws_ref[...])

    proj_ref[...] = _dot(h_scr[...], w_ref[...]).astype(BF16)


def _in_proj(x2, g, w_p, w_small, layer, tm, tn):
    T = x2.shape[0]
    return pl.pallas_call(
        _in_proj_kernel,
        out_shape=(jax.ShapeDtypeStruct((T, PROJ_WIDTH), BF16), jax.ShapeDtypeStruct((T, LANES), F32)),
        grid=(T // tm, PROJ_WIDTH // tn),
        in_specs=[
            pl.BlockSpec((tm, D_MODEL), lambda i, j: (i, 0)),
            pl.BlockSpec((None, 1, D_MODEL), lambda i, j: (layer, 0, 0)),
            pl.BlockSpec((None, D_MODEL, tn), lambda i, j: (layer, 0, j)),
            pl.BlockSpec((None, D_MODEL, LANES), lambda i, j: (layer, 0, 0)),
        ],
        out_specs=(pl.BlockSpec((tm, tn), lambda i, j: (i, j)),
                   pl.BlockSpec((tm, LANES), lambda i, j: (i, 0))),
        scratch_shapes=[pltpu.VMEM((tm, D_MODEL), BF16)],
        compiler_params=_cparams(("parallel", "arbitrary")),
        name="in_proj",
    )(x2, g, w_p, w_small)


INV_BASE = 8


def _inverse_masks(n):
    r = lax.broadcasted_iota(jnp.int32, (n, n), 0)
    c = lax.broadcasted_iota(jnp.int32, (n, n), 1)
    same = lambda s: (r // s) == (c // s)
    levels = []
    s = INV_BASE
    while s < n:
        levels.append(same(2 * s) & jnp.logical_not(same(s)))
        s *= 2
    as16 = lambda m: jnp.where(m, 1.0, 0.0).astype(BF16)
    return (r == c).astype(F32), as16(same(INV_BASE)), [as16(m) for m in levels]


def _unit_lower_inverse(nmats, masks):
    eye, base, levels = masks
    n16 = [n.astype(BF16) for n in nmats]
    ps = [n * base for n in n16]
    xs = [eye - p.astype(F32) for p in ps]
    for _ in range(int(math.log2(INV_BASE)) - 1):
        ps = [_dot(p, p).astype(BF16) for p in ps]
        xs = [x + _dot(x.astype(BF16), p) for x, p in zip(xs, ps)]
    for off in levels:
        x16 = [x.astype(BF16) for x in xs]
        ts = [_dot(n * off, xb).astype(BF16) for n, xb in zip(n16, x16)]
        xs = [x - _dot(xb, t) for x, xb, t in zip(xs, x16, ts)]
    return xs


def _deltanet_kernel(q_ref, k_ref, v_ref, z_ref, small_ref, cw_ref, alog_ref, dt_ref, og_ref, o_ref,
                     qbuf, kbuf, vbuf, qs, ks, vs, state, *, rows):
    C = DN_CHUNK
    W = HEADS * HEAD_DIM
    NC = rows // C

    @pl.when(pl.program_id(1) == 0)
    def _():
        state[...] = jnp.zeros_like(state)
        for buf in (qbuf, kbuf, vbuf):
            buf[0:8, :] = jnp.zeros((8, W), F32)

    def conv_silu(buf, x_ref, w, dst):
        buf[8:8 + rows, :] = x_ref[...].astype(F32)
        acc = buf[8:8 + rows, :] * w[DN_CONV - 1:DN_CONV, :]
        for s in range(1, DN_CONV):
            acc = acc + buf[8 - s:8 - s + rows, :] * w[DN_CONV - 1 - s:DN_CONV - s, :]
        buf[0:8, :] = buf[rows:rows + 8, :]
        dst[...] = acc * _sigmoid(acc)

    cw = cw_ref[...]
    conv_silu(qbuf, q_ref, cw[:, 0:W], qs)
    conv_silu(kbuf, k_ref, cw[:, W:2 * W], ks)
    conv_silu(vbuf, v_ref, cw[:, 2 * W:3 * W], vs)

    small = small_ref[...]
    g_all = -jnp.exp(alog_ref[...]) * _softplus(small + dt_ref[...])
    beta_all = _sigmoid(small)

    tril, strict = _tri_masks(C)
    tril_f = tril.astype(F32)
    inv_masks = _inverse_masks(C)
    og = og_ref[...]

    probs = [(c, h) for c in range(NC) for h in range(HEADS)]
    gcs = [_dot_f32(tril_f, g_all[c * C:(c + 1) * C, :]) for c in range(NC)]
    gcts = [gc.T for gc in gcs]
    rsl = lambda c: slice(c * C, (c + 1) * C)
    hsl = lambda h: slice(h * HEAD_DIM, (h + 1) * HEAD_DIM)
    qn, kn = [], []
    for c, h in probs:
        q = qs[rsl(c), hsl(h)]
        k = ks[rsl(c), hsl(h)]
        qn.append(q * (lax.rsqrt(jnp.sum(q * q, axis=-1, keepdims=True) + NORM_EPS) * (HEAD_DIM ** -0.5)))
        kn.append(k * lax.rsqrt(jnp.sum(k * k, axis=-1, keepdims=True) + NORM_EPS))
    gcol = [gcs[c][:, h:h + 1] for c, h in probs]
    bcol = [beta_all[rsl(c), HEADS + h:HEADS + h + 1] for c, h in probs]
    glast = [gcs[c][C - 1:C, h:h + 1] for c, h in probs]
    decay = [jnp.where(tril, jnp.exp(jnp.where(tril, gcol[i] - gcts[c][h:h + 1, :], 0.0)), 0.0)
             for i, (c, h) in enumerate(probs)]
    kb = [k * b for k, b in zip(kn, bcol)]
    k16 = [k.astype(BF16) for k in kn]
    qk_kk = [_dot_nt(jnp.concatenate([kbi, q], axis=0).astype(BF16), k) for kbi, q, k in zip(kb, qn, k16)]
    nmat = [jnp.where(strict, m[:C] * d, 0.0) for m, d in zip(qk_kk, decay)]
    a_qk = [(m[C:] * d).astype(BF16) for m, d in zip(qk_kk, decay)]
    eg = [jnp.exp(g) for g in gcol]
    rhs = [jnp.concatenate([vs[rsl(c), hsl(h)] * bcol[i], kb[i] * eg[i]], axis=-1).astype(BF16)
           for i, (c, h) in enumerate(probs)]
    xinv = _unit_lower_inverse(nmat, inv_masks)
    sol = [_dot(x.astype(BF16), r) for x, r in zip(xinv, rhs)]
    u = [s[:, :HEAD_DIM] for s in sol]
    wq = [jnp.concatenate([s[:, HEAD_DIM:], q * e], axis=0).astype(BF16) for s, q, e in zip(sol, qn, eg)]
    kdt = [(k * jnp.exp(gl - g)).T.astype(BF16) for k, gl, g in zip(kn, glast, gcol)]
    egl = [jnp.exp(gl) for gl in glast]

    st = [state[h] for h in range(HEADS)]
    for c in range(NC):
        idx = [c * HEADS + h for h in range(HEADS)]
        r = [_dot(wq[i], st[h].astype(BF16)) for h, i in enumerate(idx)]
        v16 = [(u[i] - r[h][:C]).astype(BF16) for h, i in enumerate(idx)]
        o = [r[h][C:] + _dot(a_qk[i], v16[h]) for h, i in enumerate(idx)]
        st = [st[h] * egl[i] + _dot(kdt[i], v16[h]) for h, i in enumerate(idx)]
        for h in range(HEADS):
            oh = o[h] * lax.rsqrt(jnp.mean(o[h] * o[h], axis=-1, keepdims=True) + NORM_EPS) * og
            z = z_ref[rsl(c), hsl(h)].astype(F32)
            o_ref[rsl(c), hsl(h)] = (oh * (z * _sigmoid(z))).astype(BF16)
    for h in range(HEADS):
        state[h] = st[h]


def _deltanet(proj, small, conv_w, a_log_row, dt_row, out_g, layer, B, S, rows):
    T = B * S
    W = HEADS * HEAD_DIM
    nb = S // rows
    blk = lambda col: pl.BlockSpec((rows, W), lambda b, i, col=col: (b * nb + i, col))
    prow = pl.BlockSpec((None, 1, LANES), lambda b, i: (layer, 0, 0))
    return pl.pallas_call(
        functools.partial(_deltanet_kernel, rows=rows),
        out_shape=jax.ShapeDtypeStruct((T, W), BF16),
        grid=(B, nb),
        in_specs=[
            blk(PROJ_DN_Q // W), blk(PROJ_DN_Q // W + 1), blk(PROJ_DN_Q // W + 2), blk(PROJ_DN_Q // W + 3),
            pl.BlockSpec((rows, LANES), lambda b, i: (b * nb + i, 0)),
            pl.BlockSpec((None, DN_CONV, 3 * W), lambda b, i: (layer, 0, 0)),
            prow, prow, prow,
        ],
        out_specs=pl.BlockSpec((rows, W), lambda b, i: (b * nb + i, 0)),
        scratch_shapes=[pltpu.VMEM((rows + 8, W), F32)] * 3 + [pltpu.VMEM((rows, W), F32)] * 3
        + [pltpu.VMEM((HEADS, HEAD_DIM, HEAD_DIM), F32)],
        compiler_params=_cparams(("parallel", "arbitrary")),
        name="deltanet",
    )(proj, proj, proj, proj, small, conv_w, a_log_row, dt_row, out_g)


VT_ONES = 16
VT_ROWS = HEAD_DIM + VT_ONES


def _store_vt(vt_out, h, v16):
    r = lax.broadcasted_iota(jnp.int32, (HEAD_DIM, HEAD_DIM), 0)
    c = lax.broadcasted_iota(jnp.int32, (HEAD_DIM, HEAD_DIM), 1)
    eye = jnp.where(r == c, 1.0, 0.0).astype(BF16)
    base = h * VT_ROWS
    vt_out[base:base + HEAD_DIM, :] = _dot_nt(eye, v16).astype(BF16)
    vt_out[base + HEAD_DIM:base + VT_ROWS, :] = jnp.ones((VT_ONES, v16.shape[0]), BF16)


def _mla_prep_kernel(cq_ref, ckv_ref, kr_ref, cos_ref, sin_ref, qn_ref, kvn_ref, wq_ref, wkv_ref,
                     gq_ref, gk_ref, off_ref, q_out, k_out, vt_out):
    cq = cq_ref[...].astype(F32)
    ckv = ckv_ref[...].astype(F32)
    cq2 = cq * cq
    cq_ms = jnp.sum(cq2[:, :LANES] + cq2[:, LANES:], axis=-1, keepdims=True) * (1.0 / MLA_Q_RANK)
    cq = cq * lax.rsqrt(cq_ms + NORM_EPS) * qn_ref[...]
    ckv = ckv * lax.rsqrt(jnp.mean(ckv * ckv, axis=-1, keepdims=True) + NORM_EPS) * kvn_ref[...]
    q_all = _dot(cq.astype(BF16), wq_ref[...])
    kv_all = _dot(ckv.astype(BF16), wkv_ref[...])
    cos = cos_ref[...]
    sin = sin_ref[...]
    gq = gq_ref[...]
    gk = gk_ref[...]
    lane = lax.broadcasted_iota(jnp.int32, cos.shape, 1)
    live = lane < MLA_ROPE
    scale = MLA_QK_DIM ** -0.5 * LOG2E
    off = off_ref[...]

    def rope(xr):
        return xr * cos + pltpu.roll(xr, MLA_ROPE, 1) * sin

    kr = kr_ref[...].astype(F32)
    kr_sq = jnp.where(live, kr * kr, 0.0)
    kr_rot = rope(kr * gk[:, MLA_NOPE:])
    for h in range(HEADS):
        base = h * QK_PAD
        qn = q_all[:, base:base + MLA_NOPE]
        qr = q_all[:, base + MLA_NOPE:base + QK_PAD]
        ss = jnp.sum(qn * qn + jnp.where(live, qr * qr, 0.0), axis=-1, keepdims=True)
        rinv = lax.rsqrt(ss * (1.0 / MLA_QK_DIM) + NORM_EPS) * scale
        q_out[:, base:base + MLA_NOPE] = (qn * rinv * gq[:, :MLA_NOPE]).astype(BF16)
        q_rot = jnp.where(lane == MLA_ROPE, 1.0, rope(qr * rinv * gq[:, MLA_NOPE:]))
        q_out[:, base + MLA_NOPE:base + QK_PAD] = q_rot.astype(BF16)

        kn = kv_all[:, base:base + MLA_NOPE]
        v = kv_all[:, base + MLA_NOPE:base + QK_PAD]
        ssk = jnp.sum(kn * kn + kr_sq, axis=-1, keepdims=True)
        rk = lax.rsqrt(ssk * (1.0 / MLA_QK_DIM) + NORM_EPS)
        k_out[:, base:base + MLA_NOPE] = (kn * rk * gk[:, :MLA_NOPE]).astype(BF16)
        k_rot = jnp.where(lane == MLA_ROPE, off, kr_rot * rk)
        k_out[:, base + MLA_NOPE:base + QK_PAD] = k_rot.astype(BF16)
        _store_vt(vt_out, h, v.astype(BF16))


def _split3(c):
    hi = c.astype(BF16).astype(F32)
    r = c - hi
    mid = r.astype(BF16).astype(F32)
    lo = (r - mid).astype(BF16).astype(F32)
    return hi, mid, lo


def _fox_prep_kernel(q_ref, k_ref, v_ref, small_ref, fb_ref, gq_ref, gk_ref, off_ref, q_out, k_out, vt_out, carry,
                     *, tm):
    C = LANES
    tril, _ = _tri_masks(C)
    tril_f = tril.astype(F32)
    logf = -_softplus(-(small_ref[...] + fb_ref[...]))
    gq = gq_ref[...]
    gk = gk_ref[...]
    off = off_ref[...]
    scale = HEAD_DIM ** -0.5 * LOG2E
    lane = lax.broadcasted_iota(jnp.int32, (C, LANES), 1)
    for c in range(tm // C):
        rs = slice(c * C, (c + 1) * C)
        cum = _dot_f32(tril_f, logf[rs, :]) + carry[...]
        carry[...] = cum[C - 1:C, :]
        for h in range(HEADS):
            hs = slice(h * HEAD_DIM, (h + 1) * HEAD_DIM)
            q = q_ref[rs, hs].astype(F32)
            k = k_ref[rs, hs].astype(F32)
            q = q * lax.rsqrt(jnp.mean(q * q, axis=-1, keepdims=True) + NORM_EPS) * gq * scale
            k = k * lax.rsqrt(jnp.mean(k * k, axis=-1, keepdims=True) + NORM_EPS) * gk
            hi, mid, lo = _split3(cum[:, 2 * HEADS + h:2 * HEADS + h + 1] * LOG2E)
            one = jnp.where(lane < 7, 1.0, 0.0)
            parts = jnp.where(lane == 0, hi, jnp.where(lane == 1, mid, jnp.where(lane == 2, lo, 0.0)))
            q_aug = jnp.where(lane < 3, parts, one)
            nparts = jnp.where(lane == 3, -hi, jnp.where(lane == 4, -mid, jnp.where(lane == 5, -lo,
                                                                                     jnp.where(lane == 6, off, 0.0))))
            k_aug = jnp.where(lane < 3, one, nparts)
            base = h * QK_PAD
            q_out[rs, base:base + HEAD_DIM] = q.astype(BF16)
            q_out[rs, base + HEAD_DIM:base + QK_PAD] = q_aug.astype(BF16)
            k_out[rs, base:base + HEAD_DIM] = k.astype(BF16)
            k_out[rs, base + HEAD_DIM:base + QK_PAD] = k_aug.astype(BF16)
    for h in range(HEADS):
        _store_vt(vt_out, h, v_ref[:, h * HEAD_DIM:(h + 1) * HEAD_DIM])


def _attn_kernel(q_ref, k_ref, vt_ref, o_ref, m_scr, acc_scr, *, blk):
    i = pl.program_id(1)
    heads = range(HEADS)
    qsl = lambda h: slice(h * QK_PAD, (h + 1) * QK_PAD)
    vsl = lambda h: slice(h * VT_ROWS, (h + 1) * VT_ROWS)
    osl = lambda h: slice(h * HEAD_DIM, (h + 1) * HEAD_DIM)
    q = [q_ref[:, qsl(h)] for h in heads]
    m_scr[...] = jnp.full_like(m_scr, NEG_BIG)
    acc_scr[...] = jnp.zeros_like(acc_scr)

    def scores(j):
        rows = pl.ds(pl.multiple_of(j * blk, blk), blk)
        return [_dot_nt(k_ref[rows, qsl(h)], q[h]) for h in heads]

    def accumulate(j, st, masked):
        if masked:
            r = lax.broadcasted_iota(jnp.int32, (blk, blk), 0)
            c = lax.broadcasted_iota(jnp.int32, (blk, blk), 1)
            keep = r <= c
            st = [jnp.where(keep, s, NEG_BIG) for s in st]
        for grp in ((0, 1), (2, 3)):
            m_old = [m_scr[h] for h in grp]
            m_new = [jnp.maximum(mo, jnp.max(st[h], axis=0, keepdims=True)) for mo, h in zip(m_old, grp)]
            alpha = [jnp.exp2(mo - mn) for mo, mn in zip(m_old, m_new)]
            p = [jnp.exp2(st[h] - mn).astype(BF16) for h, mn in zip(grp, m_new)]
            pv = [_dot(vt_ref[j, vsl(h), :], pp) for h, pp in zip(grp, p)]
            for n, h in enumerate(grp):
                m_scr[h] = m_new[n]
                acc_scr[h] = alpha[n] * acc_scr[h] + pv[n]

    def pair(t, carry):
        j = 2 * t
        st_a = scores(j)
        st_b = scores(j + 1)
        accumulate(j, st_a, False)
        accumulate(j + 1, st_b, False)
        return carry

    lax.fori_loop(0, i // 2, pair, 0)

    @pl.when(i % 2 == 1)
    def _():
        accumulate(i - 1, scores(i - 1), False)

    accumulate(i, scores(i), True)
    for h in heads:
        acc = acc_scr[h]
        o_ref[:, osl(h)] = (acc[:HEAD_DIM] / acc[HEAD_DIM:HEAD_DIM + 1]).T.astype(BF16)


def _attn_shifted_kernel(q_ref, k_ref, vt_ref, o_ref, acc_scr, *, blk):
    i = pl.program_id(1)
    heads = range(HEADS)
    qsl = lambda h: slice(h * QK_PAD, (h + 1) * QK_PAD)
    vsl = lambda h: slice(h * VT_ROWS, (h + 1) * VT_ROWS)
    osl = lambda h: slice(h * HEAD_DIM, (h + 1) * HEAD_DIM)
    q = [q_ref[:, qsl(h)] for h in heads]
    acc_scr[...] = jnp.zeros_like(acc_scr)

    def block(j):
        rows = pl.ds(pl.multiple_of(j * blk, blk), blk)
        st = [_dot_nt(k_ref[rows, qsl(h)], q[h]) for h in heads]
        pv = [_dot(vt_ref[j, vsl(h), :], jnp.exp2(st[h]).astype(BF16)) for h in heads]
        for h in heads:
            acc_scr[h] += pv[h]

    def diagonal_block():
        half = blk // 2
        lo = pl.ds(pl.multiple_of(i * blk, blk), half)
        hi = pl.ds(pl.multiple_of(i * blk + half, half), half)
        keep_lo = (lax.broadcasted_iota(jnp.int32, (half, blk), 0)
                   <= lax.broadcasted_iota(jnp.int32, (half, blk), 1))
        keep_hi = (lax.broadcasted_iota(jnp.int32, (half, half), 0)
                   <= lax.broadcasted_iota(jnp.int32, (half, half), 1))
        s_lo = [jnp.where(keep_lo, _dot_nt(k_ref[lo, qsl(h)], q[h]), NEG_BIG) for h in heads]
        s_hi = [jnp.where(keep_hi, _dot_nt(k_ref[hi, qsl(h)], q_ref[half:, qsl(h)]), NEG_BIG) for h in heads]
        pv_lo = [_dot(vt_ref[i, vsl(h), :half], jnp.exp2(s_lo[h]).astype(BF16)) for h in heads]
        pv_hi = [_dot(vt_ref[i, vsl(h), half:], jnp.exp2(s_hi[h]).astype(BF16)) for h in heads]
        for h in heads:
            acc_scr[h] += pv_lo[h]
            acc_scr[h, :, half:] += pv_hi[h]

    def pair(t, carry):
        block(2 * t)
        block(2 * t + 1)
        return carry

    lax.fori_loop(0, i // 2, pair, 0)

    @pl.when(i % 2 == 1)
    def _():
        block(i - 1)

    diagonal_block()
    for h in heads:
        acc = acc_scr[h]
        o_ref[:, osl(h)] = (acc[:HEAD_DIM] / acc[HEAD_DIM:HEAD_DIM + 1]).T.astype(BF16)


def _attention(q, k, vt, shifted, B, S, blk):
    T = B * S
    nb = S // blk
    W = HEADS * HEAD_DIM
    HW = HEADS * QK_PAD
    common = dict(
        out_shape=jax.ShapeDtypeStruct((T, W), BF16),
        grid=(B, nb),
        in_specs=[
            pl.BlockSpec((blk, HW), lambda b, i: (b * nb + i, 0)),
            pl.BlockSpec((S, HW), lambda b, i: (b, 0)),
            pl.BlockSpec((None, nb, HEADS * VT_ROWS, blk), lambda b, i: (b, 0, 0, 0)),
        ],
        out_specs=pl.BlockSpec((blk, W), lambda b, i: (b * nb + i, 0)),
        compiler_params=_cparams(("parallel", "arbitrary")),
    )
    acc = pltpu.VMEM((HEADS, VT_ROWS, blk), F32)
    online = pl.pallas_call(functools.partial(_attn_kernel, blk=blk), name="causal_attention",
                            scratch_shapes=[pltpu.VMEM((HEADS, 1, blk), F32), acc], **common)
    fast = pl.pallas_call(functools.partial(_attn_shifted_kernel, blk=blk), name="causal_attention_shifted",
                          scratch_shapes=[acc], **common)
    return lax.cond(shifted, fast, online, q, k, vt)


SHIFT_MAX = 40.0


def _logit_shift(gq, gk, dim):
    L = gq.shape[0]
    bound = (dim ** 0.5) * LOG2E * 1.02 * jnp.max(jnp.abs(gq.reshape(L, -1)), axis=-1) \
        * jnp.max(jnp.abs(gk.reshape(L, -1)), axis=-1)
    ok = bound <= SHIFT_MAX
    shift = jnp.where(ok, -bound, 0.0).astype(F32)
    return jnp.broadcast_to(shift[:, None, None], (L, 1, LANES)), ok


def _sg_kernel(u_ref, v_ref, g_ref, ws_ref, b_ref, o_ref, *, rows):
    Tn = SG_CHUNK
    tril, _ = _tri_masks(Tn)
    g = g_ref[...]
    bias = b_ref[...]
    for gi in range(HEADS):
        hs = slice(gi * HEAD_DIM, (gi + 1) * HEAD_DIM)
        w = jnp.where(tril, ws_ref[gi], 0.0).astype(BF16)
        for n in range(rows // Tn):
            rs = slice(n * Tn, (n + 1) * Tn)
            v = _gelu_tanh(v_ref[rs, hs].astype(F32))
            v = v * lax.rsqrt(jnp.mean(v * v, axis=-1, keepdims=True) + NORM_EPS) * g[:, hs]
            mixed = _dot(w, v.astype(BF16)) + bias[:, hs]
            u = _gelu_tanh(u_ref[rs, hs].astype(F32))
            o_ref[rs, hs] = (u * mixed).astype(BF16)


N_MLA_IN, N_FOX_IN, N_SG_IN = 12, 8, 5


def _mixer_prep_kernel(*refs, tm):
    mla_in = refs[:N_MLA_IN]
    fox_in = refs[N_MLA_IN:N_MLA_IN + N_FOX_IN]
    sg_in = refs[N_MLA_IN + N_FOX_IN:N_MLA_IN + N_FOX_IN + N_SG_IN]
    mq, mk, mvt, fq, fk, fvt, sg_out, carry = refs[N_MLA_IN + N_FOX_IN + N_SG_IN:]

    @pl.when(pl.program_id(1) == 0)
    def _():
        carry[...] = jnp.zeros_like(carry)

    _mla_prep_kernel(*mla_in, mq, mk, mvt)
    _fox_prep_kernel(*fox_in, fq, fk, fvt, carry, tm=tm)
    _sg_kernel(*sg_in, sg_out, rows=tm)


def _mixer_prep(proj, small, cos_t, sin_t, p, mla_shift, fox_shift, sg_w_s, layer, B, S, tm):
    T = B * S
    nb = S // tm
    W = HEADS * HEAD_DIM
    HW = HEADS * QK_PAD
    rows = lambda width, col=0: pl.BlockSpec((tm, width), lambda b, i, col=col: (b * nb + i, col))
    par = lambda *shape: pl.BlockSpec((None,) + shape, lambda b, i: (layer,) + (0,) * len(shape))
    vt = pl.BlockSpec((None, None, HEADS * VT_ROWS, tm), lambda b, i: (b, i, 0, 0))
    vt_shape = jax.ShapeDtypeStruct((B, nb, HEADS * VT_ROWS, tm), BF16)
    qk_shape = jax.ShapeDtypeStruct((T, HW), BF16)
    mla_specs = [rows(MLA_Q_RANK, PROJ_MLA_CQ // MLA_Q_RANK), rows(MLA_KV_RANK, PROJ_MLA_CKV // MLA_KV_RANK),
                 rows(LANES, PROJ_MLA_KR // LANES), rows(LANES), rows(LANES),
                 par(1, MLA_Q_RANK), par(1, MLA_KV_RANK), par(MLA_Q_RANK, HW), par(MLA_KV_RANK, HW),
                 par(1, QK_PAD), par(1, QK_PAD), par(1, LANES)]
    mla_args = [proj, proj, proj, cos_t, sin_t, p["mla_qn"], p["mla_kvn"], p["w_uq_p"], p["w_ukv"],
                p["mla_gq"], p["mla_gk"], mla_shift]
    fox_specs = [rows(W, PROJ_FOX_Q // W), rows(W, PROJ_FOX_Q // W + 1), rows(W, PROJ_FOX_Q // W + 2), rows(LANES),
                 par(1, LANES), par(1, LANES), par(1, LANES), par(1, LANES)]
    fox_args = [proj, proj, proj, small, p["fox_fb"], p["fox_gq"], p["fox_gk"], fox_shift]
    sg_specs = [rows(W, PROJ_SG_U // W), rows(W, PROJ_SG_V // W), par(1, W), par(HEADS, SG_CHUNK, SG_CHUNK),
                par(SG_CHUNK, W)]
    sg_args = [proj, proj, p["sg_g"], sg_w_s, p["sg_bias"]]
    assert (len(mla_specs), len(fox_specs), len(sg_specs)) == (N_MLA_IN, N_FOX_IN, N_SG_IN)
    out = pl.pallas_call(
        functools.partial(_mixer_prep_kernel, tm=tm),
        out_shape=(qk_shape, qk_shape, vt_shape, qk_shape, qk_shape, vt_shape, jax.ShapeDtypeStruct((T, W), BF16)),
        grid=(B, nb),
        in_specs=mla_specs + fox_specs + sg_specs,
        out_specs=(rows(HW), rows(HW), vt, rows(HW), rows(HW), vt, rows(W)),
        scratch_shapes=[pltpu.VMEM((1, LANES), F32)],
        compiler_params=_cparams(("parallel", "arbitrary")),
        name="mixer_prep",
    )(*mla_args, *fox_args, *sg_args)
    return out[0:3], out[3:6], out[6]


def _merge_kernel(oa, ob, oc, od, g0, g1, g2, g3, x_ref, wb_ref, wo_ref, n2_ref, out_ref, h_ref):
    merged = None
    for i, (o, g) in enumerate(((oa, g0), (ob, g1), (oc, g2), (od, g3))):
        term = _sigmoid(g[...].astype(F32)) * _dot(o[...], wb_ref[i])
        merged = term if merged is None else merged + term
    x1 = x_ref[...] + _dot(merged.astype(BF16), wo_ref[...])
    out_ref[...] = x1
    ms = jnp.mean(x1 * x1, axis=-1, keepdims=True)
    h_ref[...] = (x1 * lax.rsqrt(ms + NORM_EPS) * n2_ref[...]).astype(BF16)


def _merge(branches, proj, x2, w_branch, w_out, norm2_g, layer, tm):
    T = x2.shape[0]
    ob = pl.BlockSpec((tm, BRANCH_WIDTH), lambda i: (i, 0))
    gate = lambda n: pl.BlockSpec((tm, D_MODEL), lambda i, n=n: (i, n))
    xs = pl.BlockSpec((tm, D_MODEL), lambda i: (i, 0))
    return pl.pallas_call(
        _merge_kernel,
        out_shape=(jax.ShapeDtypeStruct((T, D_MODEL), F32), jax.ShapeDtypeStruct((T, D_MODEL), BF16)),
        grid=(T // tm,),
        in_specs=[ob, ob, ob, ob, gate(0), gate(1), gate(2), gate(3), xs,
                  pl.BlockSpec((None, N_BRANCHES, BRANCH_WIDTH, D_MODEL), lambda i: (layer, 0, 0, 0)),
                  pl.BlockSpec((None, D_MODEL, D_MODEL), lambda i: (layer, 0, 0)),
                  pl.BlockSpec((None, 1, D_MODEL), lambda i: (layer, 0, 0))],
        out_specs=(xs, xs),
        compiler_params=_cparams(("parallel",)),
        name="merge_out_proj",
    )(*branches, proj, proj, proj, proj, x2, w_branch, w_out, norm2_g)


def _ffn_kernel(x_ref, h_ref, w1_ref, w2_ref, out_ref):
    @pl.when(pl.program_id(1) == 0)
    def _():
        out_ref[...] = x_ref[...]

    a = jnp.maximum(_dot(h_ref[...], w1_ref[...]), 0.0)
    out_ref[...] += _dot((a * a).astype(BF16), w2_ref[...])


def _ffn(x2, h2, w1, w2, layer, tm, tf):
    T = x2.shape[0]
    xs = pl.BlockSpec((tm, D_MODEL), lambda i, f: (i, 0))
    return pl.pallas_call(
        _ffn_kernel,
        out_shape=jax.ShapeDtypeStruct((T, D_MODEL), F32),
        grid=(T // tm, D_FF // tf),
        in_specs=[xs, xs,
                  pl.BlockSpec((None, D_MODEL, tf), lambda i, f: (layer, 0, f)),
                  pl.BlockSpec((None, tf, D_MODEL), lambda i, f: (layer, f, 0))],
        out_specs=xs,
        compiler_params=_cparams(("parallel", "arbitrary")),
        name="relu2_mlp",
    )(x2, h2, w1, w2)


def _pad_lanes(a, width):
    return jnp.pad(a, [(0, 0)] * (a.ndim - 1) + [(0, width - a.shape[-1])])


def _swap_halves(a):
    h = a.shape[-1] // 2
    return jnp.concatenate([a[..., h:], a[..., :h]], axis=-1)


def _prep_params(norm1_g, w_in, dn_a_log, dn_dt_bias, dn_out_norm_g, mla_q_norm_g, mla_kv_norm_g, mla_w_uq,
                 mla_w_ukv, mla_qk_q_g, mla_qk_k_g, sg_v_norm_g, sg_b_s, fox_q_norm_g, fox_k_norm_g, fox_f_bias,
                 w_branch, w_out, norm2_g, w_ff1, w_ff2):
    L = w_in.shape[0]
    col = lambda a, b: w_in[:, :, a:b]
    kr = col(O_MLA_KR, O_MLA_KR + MLA_ROPE)
    w_p = jnp.concatenate([
        col(O_GATES, O_END), col(O_DN_QKV, O_DN_Z), col(O_DN_Z, O_DN_A), col(O_SG_U, O_SG_V),
        col(O_SG_V, O_FOX_QKV), col(O_FOX_QKV, O_FOX_F), col(O_MLA_CQ, O_MLA_CKV), col(O_MLA_CKV, O_MLA_KR),
        kr, _swap_halves(kr)], axis=-1).astype(BF16)
    assert w_p.shape[-1] == PROJ_WIDTH
    w_small = _pad_lanes(jnp.concatenate([col(O_DN_A, O_DN_B), col(O_DN_B, O_MLA_CQ), col(O_FOX_F, O_GATES)],
                                         axis=-1), LANES).astype(BF16)

    def row(a, width=LANES, offset=0):
        a = a.reshape(L, 1, -1)
        return jnp.pad(a, ((0, 0), (0, 0), (offset, width - offset - a.shape[-1])))

    wq = mla_w_uq.reshape(L, MLA_Q_RANK, HEADS, MLA_QK_DIM)
    wq_r = wq[..., MLA_NOPE:]
    w_uq_p = jnp.concatenate([wq, _swap_halves(wq_r)], axis=-1).reshape(L, MLA_Q_RANK, HEADS * QK_PAD).astype(BF16)

    def qk_gain(g):
        return jnp.concatenate([g, _swap_halves(g[:, MLA_NOPE:])], axis=-1).reshape(L, 1, QK_PAD)

    bias_full = jnp.repeat(jnp.swapaxes(sg_b_s, 1, 2), HEAD_DIM, axis=-1)
    return dict(
        norm1_g=norm1_g.reshape(L, 1, D_MODEL), w_p=w_p, w_small=w_small,
        a_log=row(dn_a_log), dt=row(dn_dt_bias), dn_og=dn_out_norm_g.reshape(L, 1, HEAD_DIM),
        mla_qn=mla_q_norm_g.reshape(L, 1, MLA_Q_RANK), mla_kvn=mla_kv_norm_g.reshape(L, 1, MLA_KV_RANK),
        w_uq_p=w_uq_p, w_ukv=mla_w_ukv.astype(BF16), mla_gq=qk_gain(mla_qk_q_g), mla_gk=qk_gain(mla_qk_k_g),
        sg_g=sg_v_norm_g.reshape(L, 1, HEADS * HEAD_DIM), sg_bias=bias_full,
        fox_fb=row(fox_f_bias, offset=2 * HEADS), fox_gq=fox_q_norm_g.reshape(L, 1, HEAD_DIM),
        fox_gk=fox_k_norm_g.reshape(L, 1, HEAD_DIM),
        w_branch=w_branch.astype(BF16), w_out=w_out.astype(BF16), norm2_g=norm2_g.reshape(L, 1, D_MODEL),
        w_ff1=w_ff1.astype(BF16), w_ff2=w_ff2.astype(BF16),
    )


def _tiles(S):
    pick = lambda want: min(want, S)
    return dict(in_tm=pick(2048), in_tn=1536, dn_rows=pick(512), prep_tm=pick(512), attn_blk=pick(512),
                merge_tm=pick(512), ffn_tm=pick(1024), ffn_tf=2048, rope_tm=pick(1024))


def kernel(x, positions, norm1_g, w_in, dn_conv_w, dn_a_log, dn_dt_bias, dn_out_norm_g, mla_q_norm_g, mla_kv_norm_g, mla_w_uq, mla_w_ukv, mla_qk_q_g, mla_qk_k_g, sg_v_norm_g, sg_w_s, sg_b_s, fox_q_norm_g, fox_k_norm_g, fox_f_bias, w_branch, w_out, norm2_g, w_ff1, w_ff2):
    B, S, D = x.shape
    assert D == D_MODEL and S % LANES == 0
    T = B * S
    depth = w_in.shape[0]
    t = _tiles(S)
    p = _prep_params(norm1_g, w_in, dn_a_log, dn_dt_bias, dn_out_norm_g, mla_q_norm_g, mla_kv_norm_g, mla_w_uq,
                     mla_w_ukv, mla_qk_q_g, mla_qk_k_g, sg_v_norm_g, sg_b_s, fox_q_norm_g, fox_k_norm_g,
                     fox_f_bias, w_branch, w_out, norm2_g, w_ff1, w_ff2)
    cos_t, sin_t = _rope_tables(positions, t["rope_tm"])
    mla_shift, mla_ok = _logit_shift(mla_qk_q_g, mla_qk_k_g, MLA_QK_DIM)
    fox_shift, fox_ok = _logit_shift(fox_q_norm_g, fox_k_norm_g, HEAD_DIM)
    x2 = x.reshape(T, D)
    for l in range(depth):
        proj, small = _in_proj(x2, p["norm1_g"], p["w_p"], p["w_small"], l, t["in_tm"], t["in_tn"])
        o_a = _deltanet(proj, small, dn_conv_w, p["a_log"], p["dt"], p["dn_og"], l, B, S, t["dn_rows"])
        (qb, kb, vtb), (qd, kd, vtd), o_c = _mixer_prep(proj, small, cos_t, sin_t, p, mla_shift, fox_shift, sg_w_s,
                                                          l, B, S, t["prep_tm"])
        o_b = _attention(qb, kb, vtb, mla_ok[l], B, S, t["attn_blk"])
        o_d = _attention(qd, kd, vtd, fox_ok[l], B, S, t["attn_blk"])
        x2, h2 = _merge((o_a, o_b, o_c, o_d), proj, x2, p["w_branch"], p["w_out"], p["norm2_g"], l, t["merge_tm"])
        x2 = _ffn(x2, h2, p["w_ff1"], p["w_ff2"], l, t["ffn_tm"], t["ffn_tf"])
    return x2.reshape(B, S, D)
```

```python
import functools
import math

import jax
import jax.numpy as jnp
from jax import lax
from jax.experimental import pallas as pl
from jax.experimental.pallas import tpu as pltpu

F32 = jnp.float32
BF16 = jnp.bfloat16

D_MODEL = 1024
NORM_EPS = 1e-6
N_BRANCHES = 4
BRANCH_WIDTH = 512
D_FF = 4 * D_MODEL
HEADS = 4
HEAD_DIM = 128
DN_CONV = 4
DN_CHUNK = 128
MLA_Q_RANK = 256
MLA_KV_RANK = 128
MLA_NOPE = 128
MLA_ROPE = 64
MLA_QK_DIM = MLA_NOPE + MLA_ROPE
ROPE_THETA = 10000.0
SG_CHUNK = 128
QK_PAD = 256
LANES = 128
NEG_BIG = -1e30
LOG2E = math.log2(math.e)

PROJ_GATES = 0
PROJ_DN_Q = 4096
PROJ_SG_U = 6144
PROJ_SG_V = 6656
PROJ_FOX_Q = 7168
PROJ_MLA_CQ = 8704
PROJ_MLA_CKV = 8960
PROJ_MLA_KR = 9088
PROJ_WIDTH = 9216

_SPLITS = (1536, 512, 4, 4, 256, 128, 64, 512, 512, 1536, 4, 4096)
_OFF = [0]
for _s in _SPLITS:
    _OFF.append(_OFF[-1] + _s)
(O_DN_QKV, O_DN_Z, O_DN_A, O_DN_B, O_MLA_CQ, O_MLA_CKV, O_MLA_KR, O_SG_U, O_SG_V, O_FOX_QKV, O_FOX_F,
 O_GATES, O_END) = _OFF

VMEM_LIMIT = 56 * 1024 * 1024


def _cparams(sem):
    return pltpu.CompilerParams(dimension_semantics=sem, vmem_limit_bytes=VMEM_LIMIT)


def _dot(a, b):
    return jnp.dot(a, b, preferred_element_type=F32)


def _dot_nt(a, b):
    return lax.dot_general(a, b, (((1,), (1,)), ((), ())), preferred_element_type=F32)


def _dot_f32(a, b):
    return jnp.dot(a, b, preferred_element_type=F32, precision=lax.Precision.HIGHEST)


def _sigmoid(x):
    return 0.5 * jnp.tanh(0.5 * x) + 0.5


def _softplus(x):
    return jnp.maximum(x, 0.0) + jnp.log1p(jnp.exp(-jnp.abs(x)))


def _gelu_tanh(x):
    c = math.sqrt(2.0 / math.pi)
    half = 0.5 * x
    return half + half * jnp.tanh(x * (c + (c * 0.044715) * (x * x)))


def _tri_masks(n):
    r = lax.broadcasted_iota(jnp.int32, (n, n), 0)
    c = lax.broadcasted_iota(jnp.int32, (n, n), 1)
    return r >= c, r > c


def _rope_kernel(pos_ref, freq_ref, sign_ref, cos_ref, sin_ref):
    ang = pos_ref[...] * freq_ref[...]
    lane = lax.broadcasted_iota(jnp.int32, ang.shape, 1)
    live = lane < MLA_ROPE
    cos_ref[...] = jnp.where(live, jnp.cos(ang), 0.0)
    sin_ref[...] = jnp.where(live, jnp.sin(ang) * sign_ref[...], 0.0)


def _rope_tables(positions, tm):
    T = positions.size
    half = MLA_ROPE // 2
    inv_freq = ROPE_THETA ** (-jnp.arange(0, MLA_ROPE, 2, dtype=F32) / MLA_ROPE)
    zeros = jnp.zeros((LANES - MLA_ROPE,), F32)
    freq = jnp.concatenate([inv_freq, inv_freq, zeros]).reshape(1, LANES)
    sign = jnp.concatenate([-jnp.ones((half,), F32), jnp.ones((half,), F32), zeros]).reshape(1, LANES)
    row = pl.BlockSpec((1, LANES), lambda i: (0, 0))
    tab = pl.BlockSpec((tm, LANES), lambda i: (i, 0))
    return pl.pallas_call(
        _rope_kernel,
        out_shape=(jax.ShapeDtypeStruct((T, LANES), F32), jax.ShapeDtypeStruct((T, LANES), F32)),
        grid=(T // tm,),
        in_specs=[tab, row, row],
        out_specs=(tab, tab),
        compiler_params=_cparams(("parallel",)),
        name="rope_tables",
    )(jnp.broadcast_to(positions.reshape(T, 1).astype(F32), (T, LANES)), freq, sign)


def _in_proj_kernel(x_ref, g_ref, w_ref, ws_ref, proj_ref, small_ref, h_scr):
    @pl.when(pl.program_id(1) == 0)
    def _():
        x = x_ref[...]
        ms = jnp.mean(x * x, axis=-1, keepdims=True)
        h = (x * lax.rsqrt(ms + NORM_EPS) * g_ref[...]).astype(BF16)
        h_scr[...] = h
        small_ref[...] = _dot(h, ws_ref[...])

    proj_ref[...] = _dot(h_scr[...], w_ref[...]).astype(BF16)


def _in_proj(x2, g, w_p, w_small, layer, tm, tn):
    T = x2.shape[0]
    return pl.pallas_call(
        _in_proj_kernel,
        out_shape=(jax.ShapeDtypeStruct((T, PROJ_WIDTH), BF16), jax.ShapeDtypeStruct((T, LANES), F32)),
        grid=(T // tm, PROJ_WIDTH // tn),
        in_specs=[
            pl.BlockSpec((tm, D_MODEL), lambda i, j: (i, 0)),
            pl.BlockSpec((None, 1, D_MODEL), lambda i, j: (layer, 0, 0)),
            pl.BlockSpec((None, D_MODEL, tn), lambda i, j: (layer, 0, j)),
            pl.BlockSpec((None, D_MODEL, LANES), lambda i, j: (layer, 0, 0)),
        ],
        out_specs=(pl.BlockSpec((tm, tn), lambda i, j: (i, j)),
                   pl.BlockSpec((tm, LANES), lambda i, j: (i, 0))),
        scratch_shapes=[pltpu.VMEM((tm, D_MODEL), BF16)],
        compiler_params=_cparams(("parallel", "arbitrary")),
        name="in_proj",
    )(x2, g, w_p, w_small)


INV_BASE = 8


def _inverse_masks(n):
    r = lax.broadcasted_iota(jnp.int32, (n, n), 0)
    c = lax.broadcasted_iota(jnp.int32, (n, n), 1)
    same = lambda s: (r // s) == (c // s)
    levels = []
    s = INV_BASE
    while s < n:
        levels.append(same(2 * s) & jnp.logical_not(same(s)))
        s *= 2
    as16 = lambda m: jnp.where(m, 1.0, 0.0).astype(BF16)
    return (r == c).astype(F32), as16(same(INV_BASE)), [as16(m) for m in levels]


def _unit_lower_inverse(nmats, masks):
    eye, base, levels = masks
    n16 = [n.astype(BF16) for n in nmats]
    ps = [n * base for n in n16]
    xs = [eye - p.astype(F32) for p in ps]
    for _ in range(int(math.log2(INV_BASE)) - 1):
        ps = [_dot(p, p).astype(BF16) for p in ps]
        xs = [x + _dot(x.astype(BF16), p) for x, p in zip(xs, ps)]
    for off in levels:
        x16 = [x.astype(BF16) for x in xs]
        ts = [_dot(n * off, xb).astype(BF16) for n, xb in zip(n16, x16)]
        xs = [x - _dot(xb, t) for x, xb, t in zip(xs, x16, ts)]
    return xs


def _deltanet_kernel(q_ref, k_ref, v_ref, z_ref, small_ref, cw_ref, alog_ref, dt_ref, og_ref, o_ref,
                     qbuf, kbuf, vbuf, qs, ks, vs, state, *, rows):
    C = DN_CHUNK
    W = HEADS * HEAD_DIM
    NC = rows // C

    @pl.when(pl.program_id(1) == 0)
    def _():
        state[...] = jnp.zeros_like(state)
        for buf in (qbuf, kbuf, vbuf):
            buf[0:8, :] = jnp.zeros((8, W), F32)

    def conv_silu(buf, x_ref, w, dst):
        buf[8:8 + rows, :] = x_ref[...].astype(F32)
        acc = buf[8:8 + rows, :] * w[DN_CONV - 1:DN_CONV, :]
        for s in range(1, DN_CONV):
            acc = acc + buf[8 - s:8 - s + rows, :] * w[DN_CONV - 1 - s:DN_CONV - s, :]
        buf[0:8, :] = buf[rows:rows + 8, :]
        dst[...] = acc * _sigmoid(acc)

    cw = cw_ref[...]
    conv_silu(qbuf, q_ref, cw[:, 0:W], qs)
    conv_silu(kbuf, k_ref, cw[:, W:2 * W], ks)
    conv_silu(vbuf, v_ref, cw[:, 2 * W:3 * W], vs)

    small = small_ref[...]
    g_all = -jnp.exp(alog_ref[...]) * _softplus(small + dt_ref[...])
    beta_all = _sigmoid(small)

    tril, strict = _tri_masks(C)
    tril_f = tril.astype(F32)
    inv_masks = _inverse_masks(C)
    og = og_ref[...]

    probs = [(c, h) for c in range(NC) for h in range(HEADS)]
    gcs = [_dot_f32(tril_f, g_all[c * C:(c + 1) * C, :]) for c in range(NC)]
    gcts = [gc.T for gc in gcs]
    rsl = lambda c: slice(c * C, (c + 1) * C)
    hsl = lambda h: slice(h * HEAD_DIM, (h + 1) * HEAD_DIM)
    qn, kn = [], []
    for c, h in probs:
        q = qs[rsl(c), hsl(h)]
        k = ks[rsl(c), hsl(h)]
        qn.append(q * (lax.rsqrt(jnp.sum(q * q, axis=-1, keepdims=True) + NORM_EPS) * (HEAD_DIM ** -0.5)))
        kn.append(k * lax.rsqrt(jnp.sum(k * k, axis=-1, keepdims=True) + NORM_EPS))
    gcol = [gcs[c][:, h:h + 1] for c, h in probs]
    bcol = [beta_all[rsl(c), HEADS + h:HEADS + h + 1] for c, h in probs]
    glast = [gcs[c][C - 1:C, h:h + 1] for c, h in probs]
    decay = [jnp.where(tril, jnp.exp(jnp.where(tril, gcol[i] - gcts[c][h:h + 1, :], 0.0)), 0.0)
             for i, (c, h) in enumerate(probs)]
    kb = [k * b for k, b in zip(kn, bcol)]
    k16 = [k.astype(BF16) for k in kn]
    qk_kk = [_dot_nt(jnp.concatenate([kbi, q], axis=0).astype(BF16), k) for kbi, q, k in zip(kb, qn, k16)]
    nmat = [jnp.where(strict, m[:C] * d, 0.0) for m, d in zip(qk_kk, decay)]
    a_qk = [(m[C:] * d).astype(BF16) for m, d in zip(qk_kk, decay)]
    eg = [jnp.exp(g) for g in gcol]
    rhs = [jnp.concatenate([vs[rsl(c), hsl(h)] * bcol[i], kb[i] * eg[i]], axis=-1).astype(BF16)
           for i, (c, h) in enumerate(probs)]
    xinv = _unit_lower_inverse(nmat, inv_masks)
    sol = [_dot(x.astype(BF16), r) for x, r in zip(xinv, rhs)]
    u = [s[:, :HEAD_DIM] for s in sol]
    wq = [jnp.concatenate([s[:, HEAD_DIM:], q * e], axis=0).astype(BF16) for s, q, e in zip(sol, qn, eg)]
    kdt = [(k * jnp.exp(gl - g)).T.astype(BF16) for k, gl, g in zip(kn, glast, gcol)]
    egl = [jnp.exp(gl) for gl in glast]

    st = [state[h] for h in range(HEADS)]
    for c in range(NC):
        idx = [c * HEADS + h for h in range(HEADS)]
        r = [_dot(wq[i], st[h].astype(BF16)) for h, i in enumerate(idx)]
        v16 = [(u[i] - r[h][:C]).astype(BF16) for h, i in enumerate(idx)]
        o = [r[h][C:] + _dot(a_qk[i], v16[h]) for h, i in enumerate(idx)]
        st = [st[h] * egl[i] + _dot(kdt[i], v16[h]) for h, i in enumerate(idx)]
        for h in range(HEADS):
            oh = o[h] * lax.rsqrt(jnp.mean(o[h] * o[h], axis=-1, keepdims=True) + NORM_EPS) * og
            z = z_ref[rsl(c), hsl(h)].astype(F32)
            o_ref[rsl(c), hsl(h)] = (oh * (z * _sigmoid(z))).astype(BF16)
    for h in range(HEADS):
        state[h] = st[h]


def _deltanet(proj, small, conv_w, a_log_row, dt_row, out_g, layer, B, S, rows):
    T = B * S
    W = HEADS * HEAD_DIM
    nb = S // rows
    blk = lambda col: pl.BlockSpec((rows, W), lambda b, i, col=col: (b * nb + i, col))
    prow = pl.BlockSpec((None, 1, LANES), lambda b, i: (layer, 0, 0))
    return pl.pallas_call(
        functools.partial(_deltanet_kernel, rows=rows),
        out_shape=jax.ShapeDtypeStruct((T, W), BF16),
        grid=(B, nb),
        in_specs=[
            blk(PROJ_DN_Q // W), blk(PROJ_DN_Q // W + 1), blk(PROJ_DN_Q // W + 2), blk(PROJ_DN_Q // W + 3),
            pl.BlockSpec((rows, LANES), lambda b, i: (b * nb + i, 0)),
            pl.BlockSpec((None, DN_CONV, 3 * W), lambda b, i: (layer, 0, 0)),
            prow, prow, prow,
        ],
        out_specs=pl.BlockSpec((rows, W), lambda b, i: (b * nb + i, 0)),
        scratch_shapes=[pltpu.VMEM((rows + 8, W), F32)] * 3 + [pltpu.VMEM((rows, W), F32)] * 3
        + [pltpu.VMEM((HEADS, HEAD_DIM, HEAD_DIM), F32)],
        compiler_params=_cparams(("parallel", "arbitrary")),
        name="deltanet",
    )(proj, proj, proj, proj, small, conv_w, a_log_row, dt_row, out_g)


VT_ONES = 16
VT_ROWS = HEAD_DIM + VT_ONES


def _store_vt(vt_out, h, v16):
    r = lax.broadcasted_iota(jnp.int32, (HEAD_DIM, HEAD_DIM), 0)
    c = lax.broadcasted_iota(jnp.int32, (HEAD_DIM, HEAD_DIM), 1)
    eye = jnp.where(r == c, 1.0, 0.0).astype(BF16)
    base = h * VT_ROWS
    vt_out[base:base + HEAD_DIM, :] = _dot_nt(eye, v16).astype(BF16)
    vt_out[base + HEAD_DIM:base + VT_ROWS, :] = jnp.ones((VT_ONES, v16.shape[0]), BF16)


def _mla_prep_kernel(cq_ref, ckv_ref, kr_ref, cos_ref, sin_ref, qn_ref, kvn_ref, wq_ref, wkv_ref,
                     gq_ref, gk_ref, off_ref, q_out, k_out, vt_out):
    cq = cq_ref[...].astype(F32)
    ckv = ckv_ref[...].astype(F32)
    cq2 = cq * cq
    cq_ms = jnp.sum(cq2[:, :LANES] + cq2[:, LANES:], axis=-1, keepdims=True) * (1.0 / MLA_Q_RANK)
    cq = cq * lax.rsqrt(cq_ms + NORM_EPS) * qn_ref[...]
    ckv = ckv * lax.rsqrt(jnp.mean(ckv * ckv, axis=-1, keepdims=True) + NORM_EPS) * kvn_ref[...]
    q_all = _dot(cq.astype(BF16), wq_ref[...])
    kv_all = _dot(ckv.astype(BF16), wkv_ref[...])
    cos = cos_ref[...]
    sin = sin_ref[...]
    gq = gq_ref[...]
    gk = gk_ref[...]
    lane = lax.broadcasted_iota(jnp.int32, cos.shape, 1)
    live = lane < MLA_ROPE
    scale = MLA_QK_DIM ** -0.5 * LOG2E
    off = off_ref[...]

    def rope(xr):
        return xr * cos + pltpu.roll(xr, MLA_ROPE, 1) * sin

    kr = kr_ref[...].astype(F32)
    kr_sq = jnp.where(live, kr * kr, 0.0)
    kr_rot = rope(kr * gk[:, MLA_NOPE:])
    for h in range(HEADS):
        base = h * QK_PAD
        qn = q_all[:, base:base + MLA_NOPE]
        qr = q_all[:, base + MLA_NOPE:base + QK_PAD]
        ss = jnp.sum(qn * qn + jnp.where(live, qr * qr, 0.0), axis=-1, keepdims=True)
        rinv = lax.rsqrt(ss * (1.0 / MLA_QK_DIM) + NORM_EPS) * scale
        q_out[:, base:base + MLA_NOPE] = (qn * rinv * gq[:, :MLA_NOPE]).astype(BF16)
        q_rot = jnp.where(lane == MLA_ROPE, 1.0, rope(qr * rinv * gq[:, MLA_NOPE:]))
        q_out[:, base + MLA_NOPE:base + QK_PAD] = q_rot.astype(BF16)

        kn = kv_all[:, base:base + MLA_NOPE]
        v = kv_all[:, base + MLA_NOPE:base + QK_PAD]
        ssk = jnp.sum(kn * kn + kr_sq, axis=-1, keepdims=True)
        rk = lax.rsqrt(ssk * (1.0 / MLA_QK_DIM) + NORM_EPS)
        k_out[:, base:base + MLA_NOPE] = (kn * rk * gk[:, :MLA_NOPE]).astype(BF16)
        k_rot = jnp.where(lane == MLA_ROPE, off, kr_rot * rk)
        k_out[:, base + MLA_NOPE:base + QK_PAD] = k_rot.astype(BF16)
        _store_vt(vt_out, h, v.astype(BF16))


def _split3(c):
    hi = c.astype(BF16).astype(F32)
    r = c - hi
    mid = r.astype(BF16).astype(F32)
    lo = (r - mid).astype(BF16).astype(F32)
    return hi, mid, lo


def _fox_prep_kernel(q_ref, k_ref, v_ref, small_ref, fb_ref, gq_ref, gk_ref, off_ref, q_out, k_out, vt_out, carry,
                     *, tm):
    C = LANES
    tril, _ = _tri_masks(C)
    tril_f = tril.astype(F32)
    logf = -_softplus(-(small_ref[...] + fb_ref[...]))
    gq = gq_ref[...]
    gk = gk_ref[...]
    off = off_ref[...]
    scale = HEAD_DIM ** -0.5 * LOG2E
    lane = lax.broadcasted_iota(jnp.int32, (C, LANES), 1)
    for c in range(tm // C):
        rs = slice(c * C, (c + 1) * C)
        cum = _dot_f32(tril_f, logf[rs, :]) + carry[...]
        carry[...] = cum[C - 1:C, :]
        for h in range(HEADS):
            hs = slice(h * HEAD_DIM, (h + 1) * HEAD_DIM)
            q = q_ref[rs, hs].astype(F32)
            k = k_ref[rs, hs].astype(F32)
            q = q * lax.rsqrt(jnp.mean(q * q, axis=-1, keepdims=True) + NORM_EPS) * gq * scale
            k = k * lax.rsqrt(jnp.mean(k * k, axis=-1, keepdims=True) + NORM_EPS) * gk
            hi, mid, lo = _split3(cum[:, 2 * HEADS + h:2 * HEADS + h + 1] * LOG2E)
            one = jnp.where(lane < 7, 1.0, 0.0)
            parts = jnp.where(lane == 0, hi, jnp.where(lane == 1, mid, jnp.where(lane == 2, lo, 0.0)))
            q_aug = jnp.where(lane < 3, parts, one)
            nparts = jnp.where(lane == 3, -hi, jnp.where(lane == 4, -mid, jnp.where(lane == 5, -lo,
                                                                                     jnp.where(lane == 6, off, 0.0))))
            k_aug = jnp.where(lane < 3, one, nparts)
            base = h * QK_PAD
            q_out[rs, base:base + HEAD_DIM] = q.astype(BF16)
            q_out[rs, base + HEAD_DIM:base + QK_PAD] = q_aug.astype(BF16)
            k_out[rs, base:base + HEAD_DIM] = k.astype(BF16)
            k_out[rs, base + HEAD_DIM:base + QK_PAD] = k_aug.astype(BF16)
    for h in range(HEADS):
        _store_vt(vt_out, h, v_ref[:, h * HEAD_DIM:(h + 1) * HEAD_DIM])


def _attn_kernel(q_ref, k_ref, vt_ref, o_ref, m_scr, acc_scr, *, blk):
    i = pl.program_id(1)
    heads = range(HEADS)
    qsl = lambda h: slice(h * QK_PAD, (h + 1) * QK_PAD)
    vsl = lambda h: slice(h * VT_ROWS, (h + 1) * VT_ROWS)
    osl = lambda h: slice(h * HEAD_DIM, (h + 1) * HEAD_DIM)
    q = [q_ref[:, qsl(h)] for h in heads]
    m_scr[...] = jnp.full_like(m_scr, NEG_BIG)
    acc_scr[...] = jnp.zeros_like(acc_scr)

    def scores(j):
        rows = pl.ds(pl.multiple_of(j * blk, blk), blk)
        return [_dot_nt(k_ref[rows, qsl(h)], q[h]) for h in heads]

    def accumulate(j, st, masked):
        if masked:
            r = lax.broadcasted_iota(jnp.int32, (blk, blk), 0)
            c = lax.broadcasted_iota(jnp.int32, (blk, blk), 1)
            keep = r <= c
            st = [jnp.where(keep, s, NEG_BIG) for s in st]
        for grp in ((0, 1), (2, 3)):
            m_old = [m_scr[h] for h in grp]
            m_new = [jnp.maximum(mo, jnp.max(st[h], axis=0, keepdims=True)) for mo, h in zip(m_old, grp)]
            alpha = [jnp.exp2(mo - mn) for mo, mn in zip(m_old, m_new)]
            p = [jnp.exp2(st[h] - mn).astype(BF16) for h, mn in zip(grp, m_new)]
            pv = [_dot(vt_ref[j, vsl(h), :], pp) for h, pp in zip(grp, p)]
            for n, h in enumerate(grp):
                m_scr[h] = m_new[n]
                acc_scr[h] = alpha[n] * acc_scr[h] + pv[n]

    def pair(t, carry):
        j = 2 * t
        st_a = scores(j)
        st_b = scores(j + 1)
        accumulate(j, st_a, False)
        accumulate(j + 1, st_b, False)
        return carry

    lax.fori_loop(0, i // 2, pair, 0)

    @pl.when(i % 2 == 1)
    def _():
        accumulate(i - 1, scores(i - 1), False)

    accumulate(i, scores(i), True)
    for h in heads:
        acc = acc_scr[h]
        o_ref[:, osl(h)] = (acc[:HEAD_DIM] / acc[HEAD_DIM:HEAD_DIM + 1]).T.astype(BF16)


def _attn_shifted_kernel(q_ref, k_ref, vt_ref, o_ref, acc_scr, *, blk):
    i = pl.program_id(1)
    heads = range(HEADS)
    qsl = lambda h: slice(h * QK_PAD, (h + 1) * QK_PAD)
    vsl = lambda h: slice(h * VT_ROWS, (h + 1) * VT_ROWS)
    osl = lambda h: slice(h * HEAD_DIM, (h + 1) * HEAD_DIM)
    q = [q_ref[:, qsl(h)] for h in heads]
    acc_scr[...] = jnp.zeros_like(acc_scr)

    def block(j):
        rows = pl.ds(pl.multiple_of(j * blk, blk), blk)
        st = [_dot_nt(k_ref[rows, qsl(h)], q[h]) for h in heads]
        pv = [_dot(vt_ref[j, vsl(h), :], jnp.exp2(st[h]).astype(BF16)) for h in heads]
        for h in heads:
            acc_scr[h] += pv[h]

    def diagonal_block():
        half = blk // 2
        lo = pl.ds(pl.multiple_of(i * blk, blk), half)
        hi = pl.ds(pl.multiple_of(i * blk + half, half), half)
        keep_lo = (lax.broadcasted_iota(jnp.int32, (half, blk), 0)
                   <= lax.broadcasted_iota(jnp.int32, (half, blk), 1))
        keep_hi = (lax.broadcasted_iota(jnp.int32, (half, half), 0)
                   <= lax.broadcasted_iota(jnp.int32, (half, half), 1))
        s_lo = [jnp.where(keep_lo, _dot_nt(k_ref[lo, qsl(h)], q[h]), NEG_BIG) for h in heads]
        s_hi = [jnp.where(keep_hi, _dot_nt(k_ref[hi, qsl(h)], q_ref[half:, qsl(h)]), NEG_BIG) for h in heads]
        pv_lo = [_dot(vt_ref[i, vsl(h), :half], jnp.exp2(s_lo[h]).astype(BF16)) for h in heads]
        pv_hi = [_dot(vt_ref[i, vsl(h), half:], jnp.exp2(s_hi[h]).astype(BF16)) for h in heads]
        for h in heads:
            acc_scr[h] += pv_lo[h]
            acc_scr[h, :, half:] += pv_hi[h]

    def pair(t, carry):
        block(2 * t)
        block(2 * t + 1)
        return carry

    lax.fori_loop(0, i // 2, pair, 0)

    @pl.when(i % 2 == 1)
    def _():
        block(i - 1)

    diagonal_block()
    for h in heads:
        acc = acc_scr[h]
        o_ref[:, osl(h)] = (acc[:HEAD_DIM] / acc[HEAD_DIM:HEAD_DIM + 1]).T.astype(BF16)


def _attention(q, k, vt, shifted, B, S, blk):
    T = B * S
    nb = S // blk
    W = HEADS * HEAD_DIM
    HW = HEADS * QK_PAD
    common = dict(
        out_shape=jax.ShapeDtypeStruct((T, W), BF16),
        grid=(B, nb),
        in_specs=[
            pl.BlockSpec((blk, HW), lambda b, i: (b * nb + i, 0)),
            pl.BlockSpec((S, HW), lambda b, i: (b, 0)),
            pl.BlockSpec((None, nb, HEADS * VT_ROWS, blk), lambda b, i: (b, 0, 0, 0)),
        ],
        out_specs=pl.BlockSpec((blk, W), lambda b, i: (b * nb + i, 0)),
        compiler_params=_cparams(("parallel", "arbitrary")),
    )
    acc = pltpu.VMEM((HEADS, VT_ROWS, blk), F32)
    online = pl.pallas_call(functools.partial(_attn_kernel, blk=blk), name="causal_attention",
                            scratch_shapes=[pltpu.VMEM((HEADS, 1, blk), F32), acc], **common)
    fast = pl.pallas_call(functools.partial(_attn_shifted_kernel, blk=blk), name="causal_attention_shifted",
                          scratch_shapes=[acc], **common)
    return lax.cond(shifted, fast, online, q, k, vt)


SHIFT_MAX = 40.0


def _logit_shift(gq, gk, dim):
    L = gq.shape[0]
    bound = (dim ** 0.5) * LOG2E * 1.02 * jnp.max(jnp.abs(gq.reshape(L, -1)), axis=-1) \
        * jnp.max(jnp.abs(gk.reshape(L, -1)), axis=-1)
    ok = bound <= SHIFT_MAX
    shift = jnp.where(ok, -bound, 0.0).astype(F32)
    return jnp.broadcast_to(shift[:, None, None], (L, 1, LANES)), ok


def _sg_kernel(u_ref, v_ref, g_ref, ws_ref, b_ref, o_ref, *, rows):
    Tn = SG_CHUNK
    tril, _ = _tri_masks(Tn)
    g = g_ref[...]
    bias = b_ref[...]
    for gi in range(HEADS):
        hs = slice(gi * HEAD_DIM, (gi + 1) * HEAD_DIM)
        w = jnp.where(tril, ws_ref[gi], 0.0).astype(BF16)
        for n in range(rows // Tn):
            rs = slice(n * Tn, (n + 1) * Tn)
            v = _gelu_tanh(v_ref[rs, hs].astype(F32))
            v = v * lax.rsqrt(jnp.mean(v * v, axis=-1, keepdims=True) + NORM_EPS) * g[:, hs]
            mixed = _dot(w, v.astype(BF16)) + bias[:, hs]
            u = _gelu_tanh(u_ref[rs, hs].astype(F32))
            o_ref[rs, hs] = (u * mixed).astype(BF16)


N_MLA_IN, N_FOX_IN, N_SG_IN = 12, 8, 5


def _mixer_prep_kernel(*refs, tm):
    mla_in = refs[:N_MLA_IN]
    fox_in = refs[N_MLA_IN:N_MLA_IN + N_FOX_IN]
    sg_in = refs[N_MLA_IN + N_FOX_IN:N_MLA_IN + N_FOX_IN + N_SG_IN]
    mq, mk, mvt, fq, fk, fvt, sg_out, carry = refs[N_MLA_IN + N_FOX_IN + N_SG_IN:]

    @pl.when(pl.program_id(1) == 0)
    def _():
        carry[...] = jnp.zeros_like(carry)

    _mla_prep_kernel(*mla_in, mq, mk, mvt)
    _fox_prep_kernel(*fox_in, fq, fk, fvt, carry, tm=tm)
    _sg_kernel(*sg_in, sg_out, rows=tm)


def _mixer_prep(proj, small, cos_t, sin_t, p, mla_shift, fox_shift, sg_w_s, layer, B, S, tm):
    T = B * S
    nb = S // tm
    W = HEADS * HEAD_DIM
    HW = HEADS * QK_PAD
    rows = lambda width, col=0: pl.BlockSpec((tm, width), lambda b, i, col=col: (b * nb + i, col))
    par = lambda *shape: pl.BlockSpec((None,) + shape, lambda b, i: (layer,) + (0,) * len(shape))
    vt = pl.BlockSpec((None, None, HEADS * VT_ROWS, tm), lambda b, i: (b, i, 0, 0))
    vt_shape = jax.ShapeDtypeStruct((B, nb, HEADS * VT_ROWS, tm), BF16)
    qk_shape = jax.ShapeDtypeStruct((T, HW), BF16)
    mla_specs = [rows(MLA_Q_RANK, PROJ_MLA_CQ // MLA_Q_RANK), rows(MLA_KV_RANK, PROJ_MLA_CKV // MLA_KV_RANK),
                 rows(LANES, PROJ_MLA_KR // LANES), rows(LANES), rows(LANES),
                 par(1, MLA_Q_RANK), par(1, MLA_KV_RANK), par(MLA_Q_RANK, HW), par(MLA_KV_RANK, HW),
                 par(1, QK_PAD), par(1, QK_PAD), par(1, LANES)]
    mla_args = [proj, proj, proj, cos_t, sin_t, p["mla_qn"], p["mla_kvn"], p["w_uq_p"], p["w_ukv"],
                p["mla_gq"], p["mla_gk"], mla_shift]
    fox_specs = [rows(W, PROJ_FOX_Q // W), rows(W, PROJ_FOX_Q // W + 1), rows(W, PROJ_FOX_Q // W + 2), rows(LANES),
                 par(1, LANES), par(1, LANES), par(1, LANES), par(1, LANES)]
    fox_args = [proj, proj, proj, small, p["fox_fb"], p["fox_gq"], p["fox_gk"], fox_shift]
    sg_specs = [rows(W, PROJ_SG_U // W), rows(W, PROJ_SG_V // W), par(1, W), par(HEADS, SG_CHUNK, SG_CHUNK),
                par(SG_CHUNK, W)]
    sg_args = [proj, proj, p["sg_g"], sg_w_s, p["sg_bias"]]
    assert (len(mla_specs), len(fox_specs), len(sg_specs)) == (N_MLA_IN, N_FOX_IN, N_SG_IN)
    out = pl.pallas_call(
        functools.partial(_mixer_prep_kernel, tm=tm),
        out_shape=(qk_shape, qk_shape, vt_shape, qk_shape, qk_shape, vt_shape, jax.ShapeDtypeStruct((T, W), BF16)),
        grid=(B, nb),
        in_specs=mla_specs + fox_specs + sg_specs,
        out_specs=(rows(HW), rows(HW), vt, rows(HW), rows(HW), vt, rows(W)),
        scratch_shapes=[pltpu.VMEM((1, LANES), F32)],
        compiler_params=_cparams(("parallel", "arbitrary")),
        name="mixer_prep",
    )(*mla_args, *fox_args, *sg_args)
    return out[0:3], out[3:6], out[6]


def _merge_kernel(oa, ob, oc, od, g0, g1, g2, g3, x_ref, wb_ref, wo_ref, n2_ref, out_ref, h_ref):
    merged = None
    for i, (o, g) in enumerate(((oa, g0), (ob, g1), (oc, g2), (od, g3))):
        term = _sigmoid(g[...].astype(F32)) * _dot(o[...], wb_ref[i])
        merged = term if merged is None else merged + term
    x1 = x_ref[...] + _dot(merged.astype(BF16), wo_ref[...])
    out_ref[...] = x1
    ms = jnp.mean(x1 * x1, axis=-1, keepdims=True)
    h_ref[...] = (x1 * lax.rsqrt(ms + NORM_EPS) * n2_ref[...]).astype(BF16)


def _merge(branches, proj, x2, w_branch, w_out, norm2_g, layer, tm):
    T = x2.shape[0]
    ob = pl.BlockSpec((tm, BRANCH_WIDTH), lambda i: (i, 0))
    gate = lambda n: pl.BlockSpec((tm, D_MODEL), lambda i, n=n: (i, n))
    xs = pl.BlockSpec((tm, D_MODEL), lambda i: (i, 0))
    return pl.pallas_call(
        _merge_kernel,
        out_shape=(jax.ShapeDtypeStruct((T, D_MODEL), F32), jax.ShapeDtypeStruct((T, D_MODEL), BF16)),
        grid=(T // tm,),
        in_specs=[ob, ob, ob, ob, gate(0), gate(1), gate(2), gate(3), xs,
                  pl.BlockSpec((None, N_BRANCHES, BRANCH_WIDTH, D_MODEL), lambda i: (layer, 0, 0, 0)),
                  pl.BlockSpec((None, D_MODEL, D_MODEL), lambda i: (layer, 0, 0)),
                  pl.BlockSpec((None, 1, D_MODEL), lambda i: (layer, 0, 0))],
        out_specs=(xs, xs),
        compiler_params=_cparams(("parallel",)),
        name="merge_out_proj",
    )(*branches, proj, proj, proj, proj, x2, w_branch, w_out, norm2_g)


def _ffn_kernel(x_ref, h_ref, w1_ref, w2_ref, out_ref):
    @pl.when(pl.program_id(1) == 0)
    def _():
        out_ref[...] = x_ref[...]

    a = jnp.maximum(_dot(h_ref[...], w1_ref[...]), 0.0)
    out_ref[...] += _dot((a * a).astype(BF16), w2_ref[...])


def _ffn(x2, h2, w1, w2, layer, tm, tf):
    T = x2.shape[0]
    xs = pl.BlockSpec((tm, D_MODEL), lambda i, f: (i, 0))
    return pl.pallas_call(
        _ffn_kernel,
        out_shape=jax.ShapeDtypeStruct((T, D_MODEL), F32),
        grid=(T // tm, D_FF // tf),
        in_specs=[xs, xs,
                  pl.BlockSpec((None, D_MODEL, tf), lambda i, f: (layer, 0, f)),
                  pl.BlockSpec((None, tf, D_MODEL), lambda i, f: (layer, f, 0))],
        out_specs=xs,
        compiler_params=_cparams(("parallel", "arbitrary")),
        name="relu2_mlp",
    )(x2, h2, w1, w2)


def _pad_lanes(a, width):
    return jnp.pad(a, [(0, 0)] * (a.ndim - 1) + [(0, width - a.shape[-1])])


def _swap_halves(a):
    h = a.shape[-1] // 2
    return jnp.concatenate([a[..., h:], a[..., :h]], axis=-1)


def _prep_params(norm1_g, w_in, dn_a_log, dn_dt_bias, dn_out_norm_g, mla_q_norm_g, mla_kv_norm_g, mla_w_uq,
                 mla_w_ukv, mla_qk_q_g, mla_qk_k_g, sg_v_norm_g, sg_b_s, fox_q_norm_g, fox_k_norm_g, fox_f_bias,
                 w_branch, w_out, norm2_g, w_ff1, w_ff2):
    L = w_in.shape[0]
    col = lambda a, b: w_in[:, :, a:b]
    kr = col(O_MLA_KR, O_MLA_KR + MLA_ROPE)
    w_p = jnp.concatenate([
        col(O_GATES, O_END), col(O_DN_QKV, O_DN_Z), col(O_DN_Z, O_DN_A), col(O_SG_U, O_SG_V),
        col(O_SG_V, O_FOX_QKV), col(O_FOX_QKV, O_FOX_F), col(O_MLA_CQ, O_MLA_CKV), col(O_MLA_CKV, O_MLA_KR),
        kr, _swap_halves(kr)], axis=-1).astype(BF16)
    assert w_p.shape[-1] == PROJ_WIDTH
    w_small = _pad_lanes(jnp.concatenate([col(O_DN_A, O_DN_B), col(O_DN_B, O_MLA_CQ), col(O_FOX_F, O_GATES)],
                                         axis=-1), LANES).astype(BF16)

    def row(a, width=LANES, offset=0):
        a = a.reshape(L, 1, -1)
        return jnp.pad(a, ((0, 0), (0, 0), (offset, width - offset - a.shape[-1])))

    wq = mla_w_uq.reshape(L, MLA_Q_RANK, HEADS, MLA_QK_DIM)
    wq_r = wq[..., MLA_NOPE:]
    w_uq_p = jnp.concatenate([wq, _swap_halves(wq_r)], axis=-1).reshape(L, MLA_Q_RANK, HEADS * QK_PAD).astype(BF16)

    def qk_gain(g):
        return jnp.concatenate([g, _swap_halves(g[:, MLA_NOPE:])], axis=-1).reshape(L, 1, QK_PAD)

    bias_full = jnp.repeat(jnp.swapaxes(sg_b_s, 1, 2), HEAD_DIM, axis=-1)
    return dict(
        norm1_g=norm1_g.reshape(L, 1, D_MODEL), w_p=w_p, w_small=w_small,
        a_log=row(dn_a_log), dt=row(dn_dt_bias), dn_og=dn_out_norm_g.reshape(L, 1, HEAD_DIM),
        mla_qn=mla_q_norm_g.reshape(L, 1, MLA_Q_RANK), mla_kvn=mla_kv_norm_g.reshape(L, 1, MLA_KV_RANK),
        w_uq_p=w_uq_p, w_ukv=mla_w_ukv.astype(BF16), mla_gq=qk_gain(mla_qk_q_g), mla_gk=qk_gain(mla_qk_k_g),
        sg_g=sg_v_norm_g.reshape(L, 1, HEADS * HEAD_DIM), sg_bias=bias_full,
        fox_fb=row(fox_f_bias, offset=2 * HEADS), fox_gq=fox_q_norm_g.reshape(L, 1, HEAD_DIM),
        fox_gk=fox_k_norm_g.reshape(L, 1, HEAD_DIM),
        w_branch=w_branch.astype(BF16), w_out=w_out.astype(BF16), norm2_g=norm2_g.reshape(L, 1, D_MODEL),
        w_ff1=w_ff1.astype(BF16), w_ff2=w_ff2.astype(BF16),
    )


def _tiles(S):
    pick = lambda want: min(want, S)
    return dict(in_tm=pick(2048), in_tn=1536, dn_rows=pick(512), prep_tm=pick(512), attn_blk=pick(512),
                merge_tm=pick(512), ffn_tm=pick(1024), ffn_tf=2048, rope_tm=pick(1024))


def kernel(x, positions, norm1_g, w_in, dn_conv_w, dn_a_log, dn_dt_bias, dn_out_norm_g, mla_q_norm_g, mla_kv_norm_g, mla_w_uq, mla_w_ukv, mla_qk_q_g, mla_qk_k_g, sg_v_norm_g, sg_w_s, sg_b_s, fox_q_norm_g, fox_k_norm_g, fox_f_bias, w_branch, w_out, norm2_g, w_ff1, w_ff2):
    B, S, D = x.shape
    assert D == D_MODEL and S % LANES == 0
    T = B * S
    depth = w_in.shape[0]
    t = _tiles(S)
    p = _prep_params(norm1_g, w_in, dn_a_log, dn_dt_bias, dn_out_norm_g, mla_q_norm_g, mla_kv_norm_g, mla_w_uq,
                     mla_w_ukv, mla_qk_q_g, mla_qk_k_g, sg_v_norm_g, sg_b_s, fox_q_norm_g, fox_k_norm_g,
                     fox_f_bias, w_branch, w_out, norm2_g, w_ff1, w_ff2)
    cos_t, sin_t = _rope_tables(positions, t["rope_tm"])
    mla_shift, mla_ok = _logit_shift(mla_qk_q_g, mla_qk_k_g, MLA_QK_DIM)
    fox_shift, fox_ok = _logit_shift(fox_q_norm_g, fox_k_norm_g, HEAD_DIM)
    x2 = x.reshape(T, D)
    for l in range(depth):
        proj, small = _in_proj(x2, p["norm1_g"], p["w_p"], p["w_small"], l, t["in_tm"], t["in_tn"])
        o_a = _deltanet(proj, small, dn_conv_w, p["a_log"], p["dt"], p["dn_og"], l, B, S, t["dn_rows"])
        (qb, kb, vtb), (qd, kd, vtd), o_c = _mixer_prep(proj, small, cos_t, sin_t, p, mla_shift, fox_shift, sg_w_s,
                                                          l, B, S, t["prep_tm"])
        o_b = _attention(qb, kb, vtb, mla_ok[l], B, S, t["attn_blk"])
        o_d = _attention(qd, kd, vtd, fox_ok[l], B, S, t["attn_blk"])
        x2, h2 = _merge((o_a, o_b, o_c, o_d), proj, x2, p["w_branch"], p["w_out"], p["norm2_g"], l, t["merge_tm"])
        x2 = _ffn(x2, h2, p["w_ff1"], p["w_ff2"], l, t["ffn_tm"], t["ffn_tf"])
    return x2.reshape(B, S, D)
```

```python
import functools
import math

import jax
import jax.numpy as jnp
from jax import lax
from jax.experimental import pallas as pl
from jax.experimental.pallas import tpu as pltpu

F32 = jnp.float32
BF16 = jnp.bfloat16

D_MODEL = 1024
NORM_EPS = 1e-6
N_BRANCHES = 4
BRANCH_WIDTH = 512
D_FF = 4 * D_MODEL
HEADS = 4
HEAD_DIM = 128
DN_CONV = 4
DN_CHUNK = 128
MLA_Q_RANK = 256
MLA_KV_RANK = 128
MLA_NOPE = 128
MLA_ROPE = 64
MLA_QK_DIM = MLA_NOPE + MLA_ROPE
ROPE_THETA = 10000.0
SG_CHUNK = 128
QK_PAD = 256
LANES = 128
NEG_BIG = -1e30
LOG2E = math.log2(math.e)

PROJ_GATES = 0
PROJ_DN_Q = 4096
PROJ_SG_U = 6144
PROJ_SG_V = 6656
PROJ_FOX_Q = 7168
PROJ_MLA_CQ = 8704
PROJ_MLA_CKV = 8960
PROJ_MLA_KR = 9088
PROJ_WIDTH = 9216

_SPLITS = (1536, 512, 4, 4, 256, 128, 64, 512, 512, 1536, 4, 4096)
_OFF = [0]
for _s in _SPLITS:
    _OFF.append(_OFF[-1] + _s)
(O_DN_QKV, O_DN_Z, O_DN_A, O_DN_B, O_MLA_CQ, O_MLA_CKV, O_MLA_KR, O_SG_U, O_SG_V, O_FOX_QKV, O_FOX_F,
 O_GATES, O_END) = _OFF

VMEM_LIMIT = 56 * 1024 * 1024


def _cparams(sem):
    return pltpu.CompilerParams(dimension_semantics=sem, vmem_limit_bytes=VMEM_LIMIT)


def _dot(a, b):
    return jnp.dot(a, b, preferred_element_type=F32)


def _dot_nt(a, b):
    return lax.dot_general(a, b, (((1,), (1,)), ((), ())), preferred_element_type=F32)


def _dot_f32(a, b):
    return jnp.dot(a, b, preferred_element_type=F32, precision=lax.Precision.HIGHEST)


def _sigmoid(x):
    return 0.5 * jnp.tanh(0.5 * x) + 0.5


def _softplus(x):
    return jnp.maximum(x, 0.0) + jnp.log1p(jnp.exp(-jnp.abs(x)))


def _gelu_tanh(x):
    c = math.sqrt(2.0 / math.pi)
    half = 0.5 * x
    return half + half * jnp.tanh(x * (c + (c * 0.044715) * (x * x)))


def _tri_masks(n):
    r = lax.broadcasted_iota(jnp.int32, (n, n), 0)
    c = lax.broadcasted_iota(jnp.int32, (n, n), 1)
    return r >= c, r > c


def _rope_kernel(pos_ref, freq_ref, sign_ref, cos_ref, sin_ref):
    ang = pos_ref[...] * freq_ref[...]
    lane = lax.broadcasted_iota(jnp.int32, ang.shape, 1)
    live = lane < MLA_ROPE
    cos_ref[...] = jnp.where(live, jnp.cos(ang), 0.0)
    sin_ref[...] = jnp.where(live, jnp.sin(ang) * sign_ref[...], 0.0)


def _rope_tables(positions, tm):
    T = positions.size
    half = MLA_ROPE // 2
    inv_freq = ROPE_THETA ** (-jnp.arange(0, MLA_ROPE, 2, dtype=F32) / MLA_ROPE)
    zeros = jnp.zeros((LANES - MLA_ROPE,), F32)
    freq = jnp.concatenate([inv_freq, inv_freq, zeros]).reshape(1, LANES)
    sign = jnp.concatenate([-jnp.ones((half,), F32), jnp.ones((half,), F32), zeros]).reshape(1, LANES)
    row = pl.BlockSpec((1, LANES), lambda i: (0, 0))
    tab = pl.BlockSpec((tm, LANES), lambda i: (i, 0))
    return pl.pallas_call(
        _rope_kernel,
        out_shape=(jax.ShapeDtypeStruct((T, LANES), F32), jax.ShapeDtypeStruct((T, LANES), F32)),
        grid=(T // tm,),
        in_specs=[tab, row, row],
        out_specs=(tab, tab),
        compiler_params=_cparams(("parallel",)),
        name="rope_tables",
    )(jnp.broadcast_to(positions.reshape(T, 1).astype(F32), (T, LANES)), freq, sign)


def _in_proj_kernel(x_ref, g_ref, w_ref, ws_ref, proj_ref, small_ref, h_scr):
    @pl.when(pl.program_id(1) == 0)
    def _():
        x = x_ref[...]
        ms = jnp.mean(x * x, axis=-1, keepdims=True)
        h = (x * lax.rsqrt(ms + NORM_EPS) * g_ref[...]).astype(BF16)
        h_scr[...] = h
        small_ref[...] = _dot(h, ws_ref[...])

    proj_ref[...] = _dot(h_scr[...], w_ref[...]).astype(BF16)


def _in_proj(x2, g, w_p, w_small, layer, tm, tn):
    T = x2.shape[0]
    return pl.pallas_call(
        _in_proj_kernel,
        out_shape=(jax.ShapeDtypeStruct((T, PROJ_WIDTH), BF16), jax.ShapeDtypeStruct((T, LANES), F32)),
        grid=(T // tm, PROJ_WIDTH // tn),
        in_specs=[
            pl.BlockSpec((tm, D_MODEL), lambda i, j: (i, 0)),
            pl.BlockSpec((None, 1, D_MODEL), lambda i, j: (layer, 0, 0)),
            pl.BlockSpec((None, D_MODEL, tn), lambda i, j: (layer, 0, j)),
            pl.BlockSpec((None, D_MODEL, LANES), lambda i, j: (layer, 0, 0)),
        ],
        out_specs=(pl.BlockSpec((tm, tn), lambda i, j: (i, j)),
                   pl.BlockSpec((tm, LANES), lambda i, j: (i, 0))),
        scratch_shapes=[pltpu.VMEM((tm, D_MODEL), BF16)],
        compiler_params=_cparams(("parallel", "arbitrary")),
        name="in_proj",
    )(x2, g, w_p, w_small)


INV_BASE = 8


def _inverse_masks(n):
    r = lax.broadcasted_iota(jnp.int32, (n, n), 0)
    c = lax.broadcasted_iota(jnp.int32, (n, n), 1)
    same = lambda s: (r // s) == (c // s)
    levels = []
    s = INV_BASE
    while s < n:
        levels.append(same(2 * s) & jnp.logical_not(same(s)))
        s *= 2
    as16 = lambda m: jnp.where(m, 1.0, 0.0).astype(BF16)
    return (r == c).astype(F32), as16(same(INV_BASE)), [as16(m) for m in levels]


def _unit_lower_inverse(nmats, masks):
    eye, base, levels = masks
    n16 = [n.astype(BF16) for n in nmats]
    ps = [n * base for n in n16]
    xs = [eye - p.astype(F32) for p in ps]
    for _ in range(int(math.log2(INV_BASE)) - 1):
        ps = [_dot(p, p).astype(BF16) for p in ps]
        xs = [x + _dot(x.astype(BF16), p) for x, p in zip(xs, ps)]
    for off in levels:
        x16 = [x.astype(BF16) for x in xs]
        ts = [_dot(n * off, xb).astype(BF16) for n, xb in zip(n16, x16)]
        xs = [x - _dot(xb, t) for x, xb, t in zip(xs, x16, ts)]
    return xs


def _deltanet_kernel(q_ref, k_ref, v_ref, z_ref, small_ref, cw_ref, alog_ref, dt_ref, og_ref, o_ref,
                     qbuf, kbuf, vbuf, qs, ks, vs, state, *, rows):
    C = DN_CHUNK
    W = HEADS * HEAD_DIM
    NC = rows // C

    @pl.when(pl.program_id(1) == 0)
    def _():
        state[...] = jnp.zeros_like(state)
        for buf in (qbuf, kbuf, vbuf):
            buf[0:8, :] = jnp.zeros((8, W), F32)

    def conv_silu(buf, x_ref, w, dst):
        buf[8:8 + rows, :] = x_ref[...].astype(F32)
        acc = buf[8:8 + rows, :] * w[DN_CONV - 1:DN_CONV, :]
        for s in range(1, DN_CONV):
            acc = acc + buf[8 - s:8 - s + rows, :] * w[DN_CONV - 1 - s:DN_CONV - s, :]
        buf[0:8, :] = buf[rows:rows + 8, :]
        dst[...] = acc * _sigmoid(acc)

    cw = cw_ref[...]
    conv_silu(qbuf, q_ref, cw[:, 0:W], qs)
    conv_silu(kbuf, k_ref, cw[:, W:2 * W], ks)
    conv_silu(vbuf, v_ref, cw[:, 2 * W:3 * W], vs)

    small = small_ref[...]
    g_all = -jnp.exp(alog_ref[...]) * _softplus(small + dt_ref[...])
    beta_all = _sigmoid(small)

    tril, strict = _tri_masks(C)
    tril_f = tril.astype(F32)
    inv_masks = _inverse_masks(C)
    og = og_ref[...]

    probs = [(c, h) for c in range(NC) for h in range(HEADS)]
    gcs = [_dot_f32(tril_f, g_all[c * C:(c + 1) * C, :]) for c in range(NC)]
    gcts = [gc.T for gc in gcs]
    rsl = lambda c: slice(c * C, (c + 1) * C)
    hsl = lambda h: slice(h * HEAD_DIM, (h + 1) * HEAD_DIM)
    qn, kn = [], []
    for c, h in probs:
        q = qs[rsl(c), hsl(h)]
        k = ks[rsl(c), hsl(h)]
        qn.append(q * (lax.rsqrt(jnp.sum(q * q, axis=-1, keepdims=True) + NORM_EPS) * (HEAD_DIM ** -0.5)))
        kn.append(k * lax.rsqrt(jnp.sum(k * k, axis=-1, keepdims=True) + NORM_EPS))
    gcol = [gcs[c][:, h:h + 1] for c, h in probs]
    bcol = [beta_all[rsl(c), HEADS + h:HEADS + h + 1] for c, h in probs]
    glast = [gcs[c][C - 1:C, h:h + 1] for c, h in probs]
    decay = [jnp.where(tril, jnp.exp(jnp.where(tril, gcol[i] - gcts[c][h:h + 1, :], 0.0)), 0.0)
             for i, (c, h) in enumerate(probs)]
    kb = [k * b for k, b in zip(kn, bcol)]
    k16 = [k.astype(BF16) for k in kn]
    qk_kk = [_dot_nt(jnp.concatenate([kbi, q], axis=0).astype(BF16), k) for kbi, q, k in zip(kb, qn, k16)]
    nmat = [jnp.where(strict, m[:C] * d, 0.0) for m, d in zip(qk_kk, decay)]
    a_qk = [(m[C:] * d).astype(BF16) for m, d in zip(qk_kk, decay)]
    eg = [jnp.exp(g) for g in gcol]
    rhs = [jnp.concatenate([vs[rsl(c), hsl(h)] * bcol[i], kb[i] * eg[i]], axis=-1).astype(BF16)
           for i, (c, h) in enumerate(probs)]
    xinv = _unit_lower_inverse(nmat, inv_masks)
    sol = [_dot(x.astype(BF16), r) for x, r in zip(xinv, rhs)]
    u = [s[:, :HEAD_DIM] for s in sol]
    wq = [jnp.concatenate([s[:, HEAD_DIM:], q * e], axis=0).astype(BF16) for s, q, e in zip(sol, qn, eg)]
    kdt = [(k * jnp.exp(gl - g)).T.astype(BF16) for k, gl, g in zip(kn, glast, gcol)]
    egl = [jnp.exp(gl) for gl in glast]

    st = [state[h] for h in range(HEADS)]
    for c in range(NC):
        idx = [c * HEADS + h for h in range(HEADS)]
        r = [_dot(wq[i], st[h].astype(BF16)) for h, i in enumerate(idx)]
        v16 = [(u[i] - r[h][:C]).astype(BF16) for h, i in enumerate(idx)]
        o = [r[h][C:] + _dot(a_qk[i], v16[h]) for h, i in enumerate(idx)]
        st = [st[h] * egl[i] + _dot(kdt[i], v16[h]) for h, i in enumerate(idx)]
        for h in range(HEADS):
            oh = o[h] * lax.rsqrt(jnp.mean(o[h] * o[h], axis=-1, keepdims=True) + NORM_EPS) * og
            z = z_ref[rsl(c), hsl(h)].astype(F32)
            o_ref[rsl(c), hsl(h)] = (oh * (z * _sigmoid(z))).astype(BF16)
    for h in range(HEADS):
        state[h] = st[h]


def _deltanet(proj, small, conv_w, a_log_row, dt_row, out_g, layer, B, S, rows):
    T = B * S
    W = HEADS * HEAD_DIM
    nb = S // rows
    blk = lambda col: pl.BlockSpec((rows, W), lambda b, i, col=col: (b * nb + i, col))
    prow = pl.BlockSpec((None, 1, LANES), lambda b, i: (layer, 0, 0))
    return pl.pallas_call(
        functools.partial(_deltanet_kernel, rows=rows),
        out_shape=jax.ShapeDtypeStruct((T, W), BF16),
        grid=(B, nb),
        in_specs=[
            blk(PROJ_DN_Q // W), blk(PROJ_DN_Q // W + 1), blk(PROJ_DN_Q // W + 2), blk(PROJ_DN_Q // W + 3),
            pl.BlockSpec((rows, LANES), lambda b, i: (b * nb + i, 0)),
            pl.BlockSpec((None, DN_CONV, 3 * W), lambda b, i: (layer, 0, 0)),
            prow, prow, prow,
        ],
        out_specs=pl.BlockSpec((rows, W), lambda b, i: (b * nb + i, 0)),
        scratch_shapes=[pltpu.VMEM((rows + 8, W), F32)] * 3 + [pltpu.VMEM((rows, W), F32)] * 3
        + [pltpu.VMEM((HEADS, HEAD_DIM, HEAD_DIM), F32)],
        compiler_params=_cparams(("parallel", "arbitrary")),
        name="deltanet",
    )(proj, proj, proj, proj, small, conv_w, a_log_row, dt_row, out_g)


VT_ONES = 16
VT_ROWS = HEAD_DIM + VT_ONES


def _store_vt(vt_out, h, v16):
    r = lax.broadcasted_iota(jnp.int32, (HEAD_DIM, HEAD_DIM), 0)
    c = lax.broadcasted_iota(jnp.int32, (HEAD_DIM, HEAD_DIM), 1)
    eye = jnp.where(r == c, 1.0, 0.0).astype(BF16)
    base = h * VT_ROWS
    vt_out[base:base + HEAD_DIM, :] = _dot_nt(eye, v16).astype(BF16)
    vt_out[base + HEAD_DIM:base + VT_ROWS, :] = jnp.ones((VT_ONES, v16.shape[0]), BF16)


def _mla_prep_kernel(cq_ref, ckv_ref, kr_ref, cos_ref, sin_ref, qn_ref, kvn_ref, wq_ref, wkv_ref,
                     gq_ref, gk_ref, off_ref, q_out, k_out, vt_out):
    cq = cq_ref[...].astype(F32)
    ckv = ckv_ref[...].astype(F32)
    cq2 = cq * cq
    cq_ms = jnp.sum(cq2[:, :LANES] + cq2[:, LANES:], axis=-1, keepdims=True) * (1.0 / MLA_Q_RANK)
    cq = cq * lax.rsqrt(cq_ms + NORM_EPS) * qn_ref[...]
    ckv = ckv * lax.rsqrt(jnp.mean(ckv * ckv, axis=-1, keepdims=True) + NORM_EPS) * kvn_ref[...]
    q_all = _dot(cq.astype(BF16), wq_ref[...])
    kv_all = _dot(ckv.astype(BF16), wkv_ref[...])
    cos = cos_ref[...]
    sin = sin_ref[...]
    gq = gq_ref[...]
    gk = gk_ref[...]
    lane = lax.broadcasted_iota(jnp.int32, cos.shape, 1)
    live = lane < MLA_ROPE
    scale = MLA_QK_DIM ** -0.5 * LOG2E
    off = off_ref[...]

    def rope(xr):
        return xr * cos + pltpu.roll(xr, MLA_ROPE, 1) * sin

    kr = kr_ref[...].astype(F32)
    kr_sq = jnp.where(live, kr * kr, 0.0)
    kr_rot = rope(kr * gk[:, MLA_NOPE:])
    for h in range(HEADS):
        base = h * QK_PAD
        qn = q_all[:, base:base + MLA_NOPE]
        qr = q_all[:, base + MLA_NOPE:base + QK_PAD]
        ss = jnp.sum(qn * qn + jnp.where(live, qr * qr, 0.0), axis=-1, keepdims=True)
        rinv = lax.rsqrt(ss * (1.0 / MLA_QK_DIM) + NORM_EPS) * scale
        q_out[:, base:base + MLA_NOPE] = (qn * rinv * gq[:, :MLA_NOPE]).astype(BF16)
        q_rot = jnp.where(lane == MLA_ROPE, 1.0, rope(qr * rinv * gq[:, MLA_NOPE:]))
        q_out[:, base + MLA_NOPE:base + QK_PAD] = q_rot.astype(BF16)

        kn = kv_all[:, base:base + MLA_NOPE]
        v = kv_all[:, base + MLA_NOPE:base + QK_PAD]
        ssk = jnp.sum(kn * kn + kr_sq, axis=-1, keepdims=True)
        rk = lax.rsqrt(ssk * (1.0 / MLA_QK_DIM) + NORM_EPS)
        k_out[:, base:base + MLA_NOPE] = (kn * rk * gk[:, :MLA_NOPE]).astype(BF16)
        k_rot = jnp.where(lane == MLA_ROPE, off, kr_rot * rk)
        k_out[:, base + MLA_NOPE:base + QK_PAD] = k_rot.astype(BF16)
        _store_vt(vt_out, h, v.astype(BF16))


def _split3(c):
    hi = c.astype(BF16).astype(F32)
    r = c - hi
    mid = r.astype(BF16).astype(F32)
    lo = (r - mid).astype(BF16).astype(F32)
    return hi, mid, lo


def _fox_prep_kernel(q_ref, k_ref, v_ref, small_ref, fb_ref, gq_ref, gk_ref, off_ref, q_out, k_out, vt_out, carry,
                     *, tm):
    C = LANES
    tril, _ = _tri_masks(C)
    tril_f = tril.astype(F32)
    logf = -_softplus(-(small_ref[...] + fb_ref[...]))
    gq = gq_ref[...]
    gk = gk_ref[...]
    off = off_ref[...]
    scale = HEAD_DIM ** -0.5 * LOG2E
    lane = lax.broadcasted_iota(jnp.int32, (C, LANES), 1)
    for c in range(tm // C):
        rs = slice(c * C, (c + 1) * C)
        cum = _dot_f32(tril_f, logf[rs, :]) + carry[...]
        carry[...] = cum[C - 1:C, :]
        for h in range(HEADS):
            hs = slice(h * HEAD_DIM, (h + 1) * HEAD_DIM)
            q = q_ref[rs, hs].astype(F32)
            k = k_ref[rs, hs].astype(F32)
            q = q * lax.rsqrt(jnp.mean(q * q, axis=-1, keepdims=True) + NORM_EPS) * gq * scale
            k = k * lax.rsqrt(jnp.mean(k * k, axis=-1, keepdims=True) + NORM_EPS) * gk
            hi, mid, lo = _split3(cum[:, 2 * HEADS + h:2 * HEADS + h + 1] * LOG2E)
            one = jnp.where(lane < 7, 1.0, 0.0)
            parts = jnp.where(lane == 0, hi, jnp.where(lane == 1, mid, jnp.where(lane == 2, lo, 0.0)))
            q_aug = jnp.where(lane < 3, parts, one)
            nparts = jnp.where(lane == 3, -hi, jnp.where(lane == 4, -mid, jnp.where(lane == 5, -lo,
                                                                                     jnp.where(lane == 6, off, 0.0))))
            k_aug = jnp.where(lane < 3, one, nparts)
            base = h * QK_PAD
            q_out[rs, base:base + HEAD_DIM] = q.astype(BF16)
            q_out[rs, base + HEAD_DIM:base + QK_PAD] = q_aug.astype(BF16)
            k_out[rs, base:base + HEAD_DIM] = k.astype(BF16)
            k_out[rs, base + HEAD_DIM:base + QK_PAD] = k_aug.astype(BF16)
    for h in range(HEADS):
        _store_vt(vt_out, h, v_ref[:, h * HEAD_DIM:(h + 1) * HEAD_DIM])


def _attn_kernel(q_ref, k_ref, vt_ref, o_ref, m_scr, acc_scr, *, blk):
    i = pl.program_id(1)
    heads = range(HEADS)
    qsl = lambda h: slice(h * QK_PAD, (h + 1) * QK_PAD)
    vsl = lambda h: slice(h * VT_ROWS, (h + 1) * VT_ROWS)
    osl = lambda h: slice(h * HEAD_DIM, (h + 1) * HEAD_DIM)
    q = [q_ref[:, qsl(h)] for h in heads]
    m_scr[...] = jnp.full_like(m_scr, NEG_BIG)
    acc_scr[...] = jnp.zeros_like(acc_scr)

    def scores(j):
        rows = pl.ds(pl.multiple_of(j * blk, blk), blk)
        return [_dot_nt(k_ref[rows, qsl(h)], q[h]) for h in heads]

    def accumulate(j, st, masked):
        if masked:
            r = lax.broadcasted_iota(jnp.int32, (blk, blk), 0)
            c = lax.broadcasted_iota(jnp.int32, (blk, blk), 1)
            keep = r <= c
            st = [jnp.where(keep, s, NEG_BIG) for s in st]
        for grp in ((0, 1), (2, 3)):
            m_old = [m_scr[h] for h in grp]
            m_new = [jnp.maximum(mo, jnp.max(st[h], axis=0, keepdims=True)) for mo, h in zip(m_old, grp)]
            alpha = [jnp.exp2(mo - mn) for mo, mn in zip(m_old, m_new)]
            p = [jnp.exp2(st[h] - mn).astype(BF16) for h, mn in zip(grp, m_new)]
            pv = [_dot(vt_ref[j, vsl(h), :], pp) for h, pp in zip(grp, p)]
            for n, h in enumerate(grp):
                m_scr[h] = m_new[n]
                acc_scr[h] = alpha[n] * acc_scr[h] + pv[n]

    def pair(t, carry):
        j = 2 * t
        st_a = scores(j)
        st_b = scores(j + 1)
        accumulate(j, st_a, False)
        accumulate(j + 1, st_b, False)
        return carry

    lax.fori_loop(0, i // 2, pair, 0)

    @pl.when(i % 2 == 1)
    def _():
        accumulate(i - 1, scores(i - 1), False)

    accumulate(i, scores(i), True)
    for h in heads:
        acc = acc_scr[h]
        o_ref[:, osl(h)] = (acc[:HEAD_DIM] / acc[HEAD_DIM:HEAD_DIM + 1]).T.astype(BF16)


def _attn_shifted_kernel(q_ref, k_ref, vt_ref, o_ref, acc_scr, *, blk):
    i = pl.program_id(1)
    heads = range(HEADS)
    qsl = lambda h: slice(h * QK_PAD, (h + 1) * QK_PAD)
    vsl = lambda h: slice(h * VT_ROWS, (h + 1) * VT_ROWS)
    osl = lambda h: slice(h * HEAD_DIM, (h + 1) * HEAD_DIM)
    q = [q_ref[:, qsl(h)] for h in heads]
    acc_scr[...] = jnp.zeros_like(acc_scr)

    def block(j):
        rows = pl.ds(pl.multiple_of(j * blk, blk), blk)
        st = [_dot_nt(k_ref[rows, qsl(h)], q[h]) for h in heads]
        pv = [_dot(vt_ref[j, vsl(h), :], jnp.exp2(st[h]).astype(BF16)) for h in heads]
        for h in heads:
            acc_scr[h] += pv[h]

    def diagonal_block():
        half = blk // 2
        lo = pl.ds(pl.multiple_of(i * blk, blk), half)
        hi = pl.ds(pl.multiple_of(i * blk + half, half), half)
        keep_lo = (lax.broadcasted_iota(jnp.int32, (half, blk), 0)
                   <= lax.broadcasted_iota(jnp.int32, (half, blk), 1))
        keep_hi = (lax.broadcasted_iota(jnp.int32, (half, half), 0)
                   <= lax.broadcasted_iota(jnp.int32, (half, half), 1))
        s_lo = [jnp.where(keep_lo, _dot_nt(k_ref[lo, qsl(h)], q[h]), NEG_BIG) for h in heads]
        s_hi = [jnp.where(keep_hi, _dot_nt(k_ref[hi, qsl(h)], q_ref[half:, qsl(h)]), NEG_BIG) for h in heads]
        pv_lo = [_dot(vt_ref[i, vsl(h), :half], jnp.exp2(s_lo[h]).astype(BF16)) for h in heads]
        pv_hi = [_dot(vt_ref[i, vsl(h), half:], jnp.exp2(s_hi[h]).astype(BF16)) for h in heads]
        for h in heads:
            acc_scr[h] += pv_lo[h]
            acc_scr[h, :, half:] += pv_hi[h]

    def pair(t, carry):
        block(2 * t)
        block(2 * t + 1)
        return carry

    lax.fori_loop(0, i // 2, pair, 0)

    @pl.when(i % 2 == 1)
    def _():
        block(i - 1)

    diagonal_block()
    for h in heads:
        acc = acc_scr[h]
        o_ref[:, osl(h)] = (acc[:HEAD_DIM] / acc[HEAD_DIM:HEAD_DIM + 1]).T.astype(BF16)


def _attention(q, k, vt, shifted, B, S, blk):
    T = B * S
    nb = S // blk
    W = HEADS * HEAD_DIM
    HW = HEADS * QK_PAD
    common = dict(
        out_shape=jax.ShapeDtypeStruct((T, W), BF16),
        grid=(B, nb),
        in_specs=[
            pl.BlockSpec((blk, HW), lambda b, i: (b * nb + i, 0)),
            pl.BlockSpec((S, HW), lambda b, i: (b, 0)),
            pl.BlockSpec((None, nb, HEADS * VT_ROWS, blk), lambda b, i: (b, 0, 0, 0)),
        ],
        out_specs=pl.BlockSpec((blk, W), lambda b, i: (b * nb + i, 0)),
        compiler_params=_cparams(("parallel", "arbitrary")),
    )
    acc = pltpu.VMEM((HEADS, VT_ROWS, blk), F32)
    online = pl.pallas_call(functools.partial(_attn_kernel, blk=blk), name="causal_attention",
                            scratch_shapes=[pltpu.VMEM((HEADS, 1, blk), F32), acc], **common)
    fast = pl.pallas_call(functools.partial(_attn_shifted_kernel, blk=blk), name="causal_attention_shifted",
                          scratch_shapes=[acc], **common)
    return lax.cond(shifted, fast, online, q, k, vt)


SHIFT_MAX = 40.0


def _logit_shift(gq, gk, dim):
    L = gq.shape[0]
    bound = (dim ** 0.5) * LOG2E * 1.02 * jnp.max(jnp.abs(gq.reshape(L, -1)), axis=-1) \
        * jnp.max(jnp.abs(gk.reshape(L, -1)), axis=-1)
    ok = bound <= SHIFT_MAX
    shift = jnp.where(ok, -bound, 0.0).astype(F32)
    return jnp.broadcast_to(shift[:, None, None], (L, 1, LANES)), ok


def _sg_kernel(u_ref, v_ref, g_ref, ws_ref, b_ref, o_ref, *, rows):
    Tn = SG_CHUNK
    tril, _ = _tri_masks(Tn)
    g = g_ref[...]
    bias = b_ref[...]
    for gi in range(HEADS):
        hs = slice(gi * HEAD_DIM, (gi + 1) * HEAD_DIM)
        w = jnp.where(tril, ws_ref[gi], 0.0).astype(BF16)
        for n in range(rows // Tn):
            rs = slice(n * Tn, (n + 1) * Tn)
            v = _gelu_tanh(v_ref[rs, hs].astype(F32))
            v = v * lax.rsqrt(jnp.mean(v * v, axis=-1, keepdims=True) + NORM_EPS) * g[:, hs]
            mixed = _dot(w, v.astype(BF16)) + bias[:, hs]
            u = _gelu_tanh(u_ref[rs, hs].astype(F32))
            o_ref[rs, hs] = (u * mixed).astype(BF16)


N_MLA_IN, N_FOX_IN, N_SG_IN = 12, 8, 5


def _mixer_prep_kernel(*refs, tm):
    mla_in = refs[:N_MLA_IN]
    fox_in = refs[N_MLA_IN:N_MLA_IN + N_FOX_IN]
    sg_in = refs[N_MLA_IN + N_FOX_IN:N_MLA_IN + N_FOX_IN + N_SG_IN]
    mq, mk, mvt, fq, fk, fvt, sg_out, carry = refs[N_MLA_IN + N_FOX_IN + N_SG_IN:]

    @pl.when(pl.program_id(1) == 0)
    def _():
        carry[...] = jnp.zeros_like(carry)

    _mla_prep_kernel(*mla_in, mq, mk, mvt)
    _fox_prep_kernel(*fox_in, fq, fk, fvt, carry, tm=tm)
    _sg_kernel(*sg_in, sg_out, rows=tm)


def _mixer_prep(proj, small, cos_t, sin_t, p, mla_shift, fox_shift, sg_w_s, layer, B, S, tm):
    T = B * S
    nb = S // tm
    W = HEADS * HEAD_DIM
    HW = HEADS * QK_PAD
    rows = lambda width, col=0: pl.BlockSpec((tm, width), lambda b, i, col=col: (b * nb + i, col))
    par = lambda *shape: pl.BlockSpec((None,) + shape, lambda b, i: (layer,) + (0,) * len(shape))
    vt = pl.BlockSpec((None, None, HEADS * VT_ROWS, tm), lambda b, i: (b, i, 0, 0))
    vt_shape = jax.ShapeDtypeStruct((B, nb, HEADS * VT_ROWS, tm), BF16)
    qk_shape = jax.ShapeDtypeStruct((T, HW), BF16)
    mla_specs = [rows(MLA_Q_RANK, PROJ_MLA_CQ // MLA_Q_RANK), rows(MLA_KV_RANK, PROJ_MLA_CKV // MLA_KV_RANK),
                 rows(LANES, PROJ_MLA_KR // LANES), rows(LANES), rows(LANES),
                 par(1, MLA_Q_RANK), par(1, MLA_KV_RANK), par(MLA_Q_RANK, HW), par(MLA_KV_RANK, HW),
                 par(1, QK_PAD), par(1, QK_PAD), par(1, LANES)]
    mla_args = [proj, proj, proj, cos_t, sin_t, p["mla_qn"], p["mla_kvn"], p["w_uq_p"], p["w_ukv"],
                p["mla_gq"], p["mla_gk"], mla_shift]
    fox_specs = [rows(W, PROJ_FOX_Q // W), rows(W, PROJ_FOX_Q // W + 1), rows(W, PROJ_FOX_Q // W + 2), rows(LANES),
                 par(1, LANES), par(1, LANES), par(1, LANES), par(1, LANES)]
    fox_args = [proj, proj, proj, small, p["fox_fb"], p["fox_gq"], p["fox_gk"], fox_shift]
    sg_specs = [rows(W, PROJ_SG_U // W), rows(W, PROJ_SG_V // W), par(1, W), par(HEADS, SG_CHUNK, SG_CHUNK),
                par(SG_CHUNK, W)]
    sg_args = [proj, proj, p["sg_g"], sg_w_s, p["sg_bias"]]
    assert (len(mla_specs), len(fox_specs), len(sg_specs)) == (N_MLA_IN, N_FOX_IN, N_SG_IN)
    out = pl.pallas_call(
        functools.partial(_mixer_prep_kernel, tm=tm),
        out_shape=(qk_shape, qk_shape, vt_shape, qk_shape, qk_shape, vt_shape, jax.ShapeDtypeStruct((T, W), BF16)),
        grid=(B, nb),
        in_specs=mla_specs + fox_specs + sg_specs,
        out_specs=(rows(HW), rows(HW), vt, rows(HW), rows(HW), vt, rows(W)),
        scratch_shapes=[pltpu.VMEM((1, LANES), F32)],
        compiler_params=_cparams(("parallel", "arbitrary")),
        name="mixer_prep",
    )(*mla_args, *fox_args, *sg_args)
    return out[0:3], out[3:6], out[6]


def _merge_kernel(oa, ob, oc, od, g0, g1, g2, g3, x_ref, wb_ref, wo_ref, n2_ref, out_ref, h_ref):
    merged = None
    for i, (o, g) in enumerate(((oa, g0), (ob, g1), (oc, g2), (od, g3))):
        term = _sigmoid(g[...].astype(F32)) * _dot(o[...], wb_ref[i])
        merged = term if merged is None else merged + term
    x1 = x_ref[...] + _dot(merged.astype(BF16), wo_ref[...])
    out_ref[...] = x1
    ms = jnp.mean(x1 * x1, axis=-1, keepdims=True)
    h_ref[...] = (x1 * lax.rsqrt(ms + NORM_EPS) * n2_ref[...]).astype(BF16)


def _merge(branches, proj, x2, w_branch, w_out, norm2_g, layer, tm):
    T = x2.shape[0]
    ob = pl.BlockSpec((tm, BRANCH_WIDTH), lambda i: (i, 0))
    gate = lambda n: pl.BlockSpec((tm, D_MODEL), lambda i, n=n: (i, n))
    xs = pl.BlockSpec((tm, D_MODEL), lambda i: (i, 0))
    return pl.pallas_call(
        _merge_kernel,
        out_shape=(jax.ShapeDtypeStruct((T, D_MODEL), F32), jax.ShapeDtypeStruct((T, D_MODEL), BF16)),
        grid=(T // tm,),
        in_specs=[ob, ob, ob, ob, gate(0), gate(1), gate(2), gate(3), xs,
                  pl.BlockSpec((None, N_BRANCHES, BRANCH_WIDTH, D_MODEL), lambda i: (layer, 0, 0, 0)),
                  pl.BlockSpec((None, D_MODEL, D_MODEL), lambda i: (layer, 0, 0)),
                  pl.BlockSpec((None, 1, D_MODEL), lambda i: (layer, 0, 0))],
        out_specs=(xs, xs),
        compiler_params=_cparams(("parallel",)),
        name="merge_out_proj",
    )(*branches, proj, proj, proj, proj, x2, w_branch, w_out, norm2_g)


def _ffn_kernel(x_ref, h_ref, w1_ref, w2_ref, out_ref):
    def contribution():
        a = jnp.maximum(_dot(h_ref[...], w1_ref[...]), 0.0)
        return _dot((a * a).astype(BF16), w2_ref[...])

    @pl.when(pl.program_id(1) == 0)
    def _():
        out_ref[...] = x_ref[...] + contribution()

    @pl.when(pl.program_id(1) != 0)
    def _():
        out_ref[...] += contribution()


def _ffn(x2, h2, w1, w2, layer, tm, tf):
    T = x2.shape[0]
    xs = pl.BlockSpec((tm, D_MODEL), lambda i, f: (i, 0))
    return pl.pallas_call(
        _ffn_kernel,
        out_shape=jax.ShapeDtypeStruct((T, D_MODEL), F32),
        grid=(T // tm, D_FF // tf),
        in_specs=[xs, xs,
                  pl.BlockSpec((None, D_MODEL, tf), lambda i, f: (layer, 0, f)),
                  pl.BlockSpec((None, tf, D_MODEL), lambda i, f: (layer, f, 0))],
        out_specs=xs,
        compiler_params=_cparams(("parallel", "arbitrary")),
        name="relu2_mlp",
    )(x2, h2, w1, w2)


def _pad_lanes(a, width):
    return jnp.pad(a, [(0, 0)] * (a.ndim - 1) + [(0, width - a.shape[-1])])


def _swap_halves(a):
    h = a.shape[-1] // 2
    return jnp.concatenate([a[..., h:], a[..., :h]], axis=-1)


def _prep_params(norm1_g, w_in, dn_a_log, dn_dt_bias, dn_out_norm_g, mla_q_norm_g, mla_kv_norm_g, mla_w_uq,
                 mla_w_ukv, mla_qk_q_g, mla_qk_k_g, sg_v_norm_g, sg_b_s, fox_q_norm_g, fox_k_norm_g, fox_f_bias,
                 w_branch, w_out, norm2_g, w_ff1, w_ff2):
    L = w_in.shape[0]
    col = lambda a, b: w_in[:, :, a:b]
    kr = col(O_MLA_KR, O_MLA_KR + MLA_ROPE)
    w_p = jnp.concatenate([
        col(O_GATES, O_END), col(O_DN_QKV, O_DN_Z), col(O_DN_Z, O_DN_A), col(O_SG_U, O_SG_V),
        col(O_SG_V, O_FOX_QKV), col(O_FOX_QKV, O_FOX_F), col(O_MLA_CQ, O_MLA_CKV), col(O_MLA_CKV, O_MLA_KR),
        kr, _swap_halves(kr)], axis=-1).astype(BF16)
    assert w_p.shape[-1] == PROJ_WIDTH
    w_small = _pad_lanes(jnp.concatenate([col(O_DN_A, O_DN_B), col(O_DN_B, O_MLA_CQ), col(O_FOX_F, O_GATES)],
                                         axis=-1), LANES).astype(BF16)

    def row(a, width=LANES, offset=0):
        a = a.reshape(L, 1, -1)
        return jnp.pad(a, ((0, 0), (0, 0), (offset, width - offset - a.shape[-1])))

    wq = mla_w_uq.reshape(L, MLA_Q_RANK, HEADS, MLA_QK_DIM)
    wq_r = wq[..., MLA_NOPE:]
    w_uq_p = jnp.concatenate([wq, _swap_halves(wq_r)], axis=-1).reshape(L, MLA_Q_RANK, HEADS * QK_PAD).astype(BF16)

    def qk_gain(g):
        return jnp.concatenate([g, _swap_halves(g[:, MLA_NOPE:])], axis=-1).reshape(L, 1, QK_PAD)

    bias_full = jnp.repeat(jnp.swapaxes(sg_b_s, 1, 2), HEAD_DIM, axis=-1)
    return dict(
        norm1_g=norm1_g.reshape(L, 1, D_MODEL), w_p=w_p, w_small=w_small,
        a_log=row(dn_a_log), dt=row(dn_dt_bias), dn_og=dn_out_norm_g.reshape(L, 1, HEAD_DIM),
        mla_qn=mla_q_norm_g.reshape(L, 1, MLA_Q_RANK), mla_kvn=mla_kv_norm_g.reshape(L, 1, MLA_KV_RANK),
        w_uq_p=w_uq_p, w_ukv=mla_w_ukv.astype(BF16), mla_gq=qk_gain(mla_qk_q_g), mla_gk=qk_gain(mla_qk_k_g),
        sg_g=sg_v_norm_g.reshape(L, 1, HEADS * HEAD_DIM), sg_bias=bias_full,
        fox_fb=row(fox_f_bias, offset=2 * HEADS), fox_gq=fox_q_norm_g.reshape(L, 1, HEAD_DIM),
        fox_gk=fox_k_norm_g.reshape(L, 1, HEAD_DIM),
        w_branch=w_branch.astype(BF16), w_out=w_out.astype(BF16), norm2_g=norm2_g.reshape(L, 1, D_MODEL),
        w_ff1=w_ff1.astype(BF16), w_ff2=w_ff2.astype(BF16),
    )


def _tiles(S):
    pick = lambda want: min(want, S)
    return dict(in_tm=pick(2048), in_tn=1536, dn_rows=pick(1024), prep_tm=pick(512), attn_blk=pick(512),
                merge_tm=pick(512), ffn_tm=pick(1024), ffn_tf=2048, rope_tm=pick(1024))


def kernel(x, positions, norm1_g, w_in, dn_conv_w, dn_a_log, dn_dt_bias, dn_out_norm_g, mla_q_norm_g, mla_kv_norm_g, mla_w_uq, mla_w_ukv, mla_qk_q_g, mla_qk_k_g, sg_v_norm_g, sg_w_s, sg_b_s, fox_q_norm_g, fox_k_norm_g, fox_f_bias, w_branch, w_out, norm2_g, w_ff1, w_ff2):
    B, S, D = x.shape
    assert D == D_MODEL and S % LANES == 0
    T = B * S
    depth = w_in.shape[0]
    t = _tiles(S)
    p = _prep_params(norm1_g, w_in, dn_a_log, dn_dt_bias, dn_out_norm_g, mla_q_norm_g, mla_kv_norm_g, mla_w_uq,
                     mla_w_ukv, mla_qk_q_g, mla_qk_k_g, sg_v_norm_g, sg_b_s, fox_q_norm_g, fox_k_norm_g,
                     fox_f_bias, w_branch, w_out, norm2_g, w_ff1, w_ff2)
    cos_t, sin_t = _rope_tables(positions, t["rope_tm"])
    mla_shift, mla_ok = _logit_shift(mla_qk_q_g, mla_qk_k_g, MLA_QK_DIM)
    fox_shift, fox_ok = _logit_shift(fox_q_norm_g, fox_k_norm_g, HEAD_DIM)
    x2 = x.reshape(T, D)
    for l in range(depth):
        proj, small = _in_proj(x2, p["norm1_g"], p["w_p"], p["w_small"], l, t["in_tm"], t["in_tn"])
        o_a = _deltanet(proj, small, dn_conv_w, p["a_log"], p["dt"], p["dn_og"], l, B, S, t["dn_rows"])
        (qb, kb, vtb), (qd, kd, vtd), o_c = _mixer_prep(proj, small, cos_t, sin_t, p, mla_shift, fox_shift, sg_w_s,
                                                          l, B, S, t["prep_tm"])
        o_b = _attention(qb, kb, vtb, mla_ok[l], B, S, t["attn_blk"])
        o_d = _attention(qd, kd, vtd, fox_ok[l], B, S, t["attn_blk"])
        x2, h2 = _merge((o_a, o_b, o_c, o_d), proj, x2, p["w_branch"], p["w_out"], p["norm2_g"], l, t["merge_tm"])
        x2 = _ffn(x2, h2, p["w_ff1"], p["w_ff2"], l, t["ffn_tm"], t["ffn_tf"])
    return x2.reshape(B, S, D)
```

```python
import functools
import math

import jax
import numpy as np
import jax.numpy as jnp
from jax import lax
from jax.experimental import pallas as pl
from jax.experimental.pallas import tpu as pltpu

F32 = jnp.float32
BF16 = jnp.bfloat16

D_MODEL = 1024
NORM_EPS = 1e-6
N_BRANCHES = 4
BRANCH_WIDTH = 512
D_FF = 4 * D_MODEL
HEADS = 4
HEAD_DIM = 128
DN_CONV = 4
DN_CHUNK = 128
MLA_Q_RANK = 256
MLA_KV_RANK = 128
MLA_NOPE = 128
MLA_ROPE = 64
MLA_QK_DIM = MLA_NOPE + MLA_ROPE
ROPE_THETA = 10000.0
SG_CHUNK = 128
QK_PAD = 256
LANES = 128
NEG_BIG = -1e30
LOG2E = math.log2(math.e)

PROJ_GATES = 0
PROJ_DN_Q = 4096
PROJ_SG_U = 6144
PROJ_SG_V = 6656
PROJ_FOX_Q = 7168
PROJ_MLA_CQ = 8704
PROJ_MLA_CKV = 8960
PROJ_MLA_KR = 9088
PROJ_WIDTH = 9216

_SPLITS = (1536, 512, 4, 4, 256, 128, 64, 512, 512, 1536, 4, 4096)
_OFF = [0]
for _s in _SPLITS:
    _OFF.append(_OFF[-1] + _s)
(O_DN_QKV, O_DN_Z, O_DN_A, O_DN_B, O_MLA_CQ, O_MLA_CKV, O_MLA_KR, O_SG_U, O_SG_V, O_FOX_QKV, O_FOX_F,
 O_GATES, O_END) = _OFF

VMEM_LIMIT = 56 * 1024 * 1024


def _cparams(sem):
    return pltpu.CompilerParams(dimension_semantics=sem, vmem_limit_bytes=VMEM_LIMIT)


def _dot(a, b):
    return jnp.dot(a, b, preferred_element_type=F32)


def _dot_nt(a, b):
    return lax.dot_general(a, b, (((1,), (1,)), ((), ())), preferred_element_type=F32)


def _dot_f32(a, b):
    return jnp.dot(a, b, preferred_element_type=F32, precision=lax.Precision.HIGHEST)


def _sigmoid(x):
    return 0.5 * jnp.tanh(0.5 * x) + 0.5


def _softplus(x):
    return jnp.maximum(x, 0.0) + jnp.log1p(jnp.exp(-jnp.abs(x)))


def _gelu_tanh(x):
    c = math.sqrt(2.0 / math.pi)
    half = 0.5 * x
    return half + half * jnp.tanh(x * (c + (c * 0.044715) * (x * x)))


def _tri_masks(n):
    r = lax.broadcasted_iota(jnp.int32, (n, n), 0)
    c = lax.broadcasted_iota(jnp.int32, (n, n), 1)
    return r >= c, r > c


def _rope_kernel(pos_ref, freq_ref, sign_ref, cos_ref, sin_ref):
    ang = pos_ref[...] * freq_ref[...]
    lane = lax.broadcasted_iota(jnp.int32, ang.shape, 1)
    live = lane < MLA_ROPE
    cos_ref[...] = jnp.where(live, jnp.cos(ang), 0.0)
    sin_ref[...] = jnp.where(live, jnp.sin(ang) * sign_ref[...], 0.0)


def _rope_tables(positions, tm):
    T = positions.size
    half = MLA_ROPE // 2
    inv_freq = ROPE_THETA ** (-jnp.arange(0, MLA_ROPE, 2, dtype=F32) / MLA_ROPE)
    zeros = jnp.zeros((LANES - MLA_ROPE,), F32)
    freq = jnp.concatenate([inv_freq, inv_freq, zeros]).reshape(1, LANES)
    sign = jnp.concatenate([-jnp.ones((half,), F32), jnp.ones((half,), F32), zeros]).reshape(1, LANES)
    row = pl.BlockSpec((1, LANES), lambda i: (0, 0))
    tab = pl.BlockSpec((tm, LANES), lambda i: (i, 0))
    return pl.pallas_call(
        _rope_kernel,
        out_shape=(jax.ShapeDtypeStruct((T, LANES), F32), jax.ShapeDtypeStruct((T, LANES), F32)),
        grid=(T // tm,),
        in_specs=[tab, row, row],
        out_specs=(tab, tab),
        compiler_params=_cparams(("parallel",)),
        name="rope_tables",
    )(jnp.broadcast_to(positions.reshape(T, 1).astype(F32), (T, LANES)), freq, sign)


def _in_proj_kernel(x_ref, g_ref, w_ref, ws_ref, proj_ref, small_ref, h_scr):
    @pl.when(pl.program_id(1) == 0)
    def _():
        x = x_ref[...]
        ms = jnp.mean(x * x, axis=-1, keepdims=True)
        h = (x * lax.rsqrt(ms + NORM_EPS) * g_ref[...]).astype(BF16)
        h_scr[...] = h
        small_ref[...] = _dot(h, ws_ref[...])
        proj_ref[...] = _dot(h, w_ref[...]).astype(BF16)

    @pl.when(pl.program_id(1) != 0)
    def _():
        proj_ref[...] = _dot(h_scr[...], w_ref[...]).astype(BF16)


def _in_proj(x2, g, w_p, w_small, layer, tm, tn):
    T = x2.shape[0]
    return pl.pallas_call(
        _in_proj_kernel,
        out_shape=(jax.ShapeDtypeStruct((T, PROJ_WIDTH), BF16), jax.ShapeDtypeStruct((T, LANES), F32)),
        grid=(T // tm, PROJ_WIDTH // tn),
        in_specs=[
            pl.BlockSpec((tm, D_MODEL), lambda i, j: (i, 0)),
            pl.BlockSpec((None, 1, D_MODEL), lambda i, j: (layer, 0, 0)),
            pl.BlockSpec((None, D_MODEL, tn), lambda i, j: (layer, 0, j)),
            pl.BlockSpec((None, D_MODEL, LANES), lambda i, j: (layer, 0, 0)),
        ],
        out_specs=(pl.BlockSpec((tm, tn), lambda i, j: (i, j)),
                   pl.BlockSpec((tm, LANES), lambda i, j: (i, 0))),
        scratch_shapes=[pltpu.VMEM((tm, D_MODEL), BF16)],
        compiler_params=_cparams(("parallel", "arbitrary")),
        name="in_proj",
    )(x2, g, w_p, w_small)


INV_BASE = 8


def _inverse_masks(n):
    r = lax.broadcasted_iota(jnp.int32, (n, n), 0)
    c = lax.broadcasted_iota(jnp.int32, (n, n), 1)
    same = lambda s: (r // s) == (c // s)
    levels = []
    s = INV_BASE
    while s < n:
        levels.append(same(2 * s) & jnp.logical_not(same(s)))
        s *= 2
    as16 = lambda m: jnp.where(m, 1.0, 0.0).astype(BF16)
    return (r == c).astype(F32), as16(same(INV_BASE)), [as16(m) for m in levels]


def _unit_lower_inverse(nmats, masks):
    eye, base, levels = masks
    n16 = [n.astype(BF16) for n in nmats]
    ps = [n * base for n in n16]
    xs = [eye - p.astype(F32) for p in ps]
    for _ in range(int(math.log2(INV_BASE)) - 1):
        ps = [_dot(p, p).astype(BF16) for p in ps]
        xs = [x + _dot(x.astype(BF16), p) for x, p in zip(xs, ps)]
    for off in levels:
        x16 = [x.astype(BF16) for x in xs]
        ts = [_dot(n * off, xb).astype(BF16) for n, xb in zip(n16, x16)]
        xs = [x - _dot(xb, t) for x, xb, t in zip(xs, x16, ts)]
    return xs


def _deltanet_kernel(q_ref, k_ref, v_ref, z_ref, small_ref, cw_ref, alog_ref, dt_ref, og_ref, o_ref,
                     qbuf, kbuf, vbuf, qs, ks, vs, state, *, rows):
    C = DN_CHUNK
    W = HEADS * HEAD_DIM
    NC = rows // C

    @pl.when(pl.program_id(1) == 0)
    def _():
        state[...] = jnp.zeros_like(state)
        for buf in (qbuf, kbuf, vbuf):
            buf[0:8, :] = jnp.zeros((8, W), F32)

    def conv_silu(buf, x_ref, w, dst):
        buf[8:8 + rows, :] = x_ref[...].astype(F32)
        acc = buf[8:8 + rows, :] * w[DN_CONV - 1:DN_CONV, :]
        for s in range(1, DN_CONV):
            acc = acc + buf[8 - s:8 - s + rows, :] * w[DN_CONV - 1 - s:DN_CONV - s, :]
        buf[0:8, :] = buf[rows:rows + 8, :]
        dst[...] = acc * _sigmoid(acc)

    cw = cw_ref[...]
    conv_silu(qbuf, q_ref, cw[:, 0:W], qs)
    conv_silu(kbuf, k_ref, cw[:, W:2 * W], ks)
    conv_silu(vbuf, v_ref, cw[:, 2 * W:3 * W], vs)

    small = small_ref[...]
    g_all = -jnp.exp(alog_ref[...]) * _softplus(small + dt_ref[...])
    beta_all = _sigmoid(small)

    tril, strict = _tri_masks(C)
    tril_f = tril.astype(F32)
    inv_masks = _inverse_masks(C)
    og = og_ref[...]

    probs = [(c, h) for c in range(NC) for h in range(HEADS)]
    gcs = [_dot_f32(tril_f, g_all[c * C:(c + 1) * C, :]) for c in range(NC)]
    gcts = [gc.T for gc in gcs]
    rsl = lambda c: slice(c * C, (c + 1) * C)
    hsl = lambda h: slice(h * HEAD_DIM, (h + 1) * HEAD_DIM)
    qn, kn = [], []
    for c, h in probs:
        q = qs[rsl(c), hsl(h)]
        k = ks[rsl(c), hsl(h)]
        qn.append(q * (lax.rsqrt(jnp.sum(q * q, axis=-1, keepdims=True) + NORM_EPS) * (HEAD_DIM ** -0.5)))
        kn.append(k * lax.rsqrt(jnp.sum(k * k, axis=-1, keepdims=True) + NORM_EPS))
    gcol = [gcs[c][:, h:h + 1] for c, h in probs]
    bcol = [beta_all[rsl(c), HEADS + h:HEADS + h + 1] for c, h in probs]
    glast = [gcs[c][C - 1:C, h:h + 1] for c, h in probs]
    decay = [jnp.where(tril, jnp.exp(jnp.where(tril, gcol[i] - gcts[c][h:h + 1, :], 0.0)), 0.0)
             for i, (c, h) in enumerate(probs)]
    kb = [k * b for k, b in zip(kn, bcol)]
    k16 = [k.astype(BF16) for k in kn]
    qk_kk = [_dot_nt(jnp.concatenate([kbi, q], axis=0).astype(BF16), k) for kbi, q, k in zip(kb, qn, k16)]
    nmat = [jnp.where(strict, m[:C] * d, 0.0) for m, d in zip(qk_kk, decay)]
    a_qk = [(m[C:] * d).astype(BF16) for m, d in zip(qk_kk, decay)]
    eg = [jnp.exp(g) for g in gcol]
    rhs = [jnp.concatenate([vs[rsl(c), hsl(h)] * bcol[i], kb[i] * eg[i]], axis=-1).astype(BF16)
           for i, (c, h) in enumerate(probs)]
    xinv = _unit_lower_inverse(nmat, inv_masks)
    sol = [_dot(x.astype(BF16), r) for x, r in zip(xinv, rhs)]
    u = [s[:, :HEAD_DIM] for s in sol]
    wq = [jnp.concatenate([s[:, HEAD_DIM:], q * e], axis=0).astype(BF16) for s, q, e in zip(sol, qn, eg)]
    kdt = [(k * jnp.exp(gl - g)).T.astype(BF16) for k, gl, g in zip(kn, glast, gcol)]
    egl = [jnp.exp(gl) for gl in glast]

    st = [state[h] for h in range(HEADS)]
    for c in range(NC):
        idx = [c * HEADS + h for h in range(HEADS)]
        r = [_dot(wq[i], st[h].astype(BF16)) for h, i in enumerate(idx)]
        v16 = [(u[i] - r[h][:C]).astype(BF16) for h, i in enumerate(idx)]
        o = [r[h][C:] + _dot(a_qk[i], v16[h]) for h, i in enumerate(idx)]
        st = [st[h] * egl[i] + _dot(kdt[i], v16[h]) for h, i in enumerate(idx)]
        for h in range(HEADS):
            oh = o[h] * lax.rsqrt(jnp.mean(o[h] * o[h], axis=-1, keepdims=True) + NORM_EPS) * og
            z = z_ref[rsl(c), hsl(h)].astype(F32)
            o_ref[rsl(c), hsl(h)] = (oh * (z * _sigmoid(z))).astype(BF16)
    for h in range(HEADS):
        state[h] = st[h]


def _deltanet(proj, small, conv_w, a_log_row, dt_row, out_g, layer, B, S, rows):
    T = B * S
    W = HEADS * HEAD_DIM
    nb = S // rows
    blk = lambda col: pl.BlockSpec((rows, W), lambda b, i, col=col: (b * nb + i, col))
    prow = pl.BlockSpec((None, 1, LANES), lambda b, i: (layer, 0, 0))
    return pl.pallas_call(
        functools.partial(_deltanet_kernel, rows=rows),
        out_shape=jax.ShapeDtypeStruct((T, W), BF16),
        grid=(B, nb),
        in_specs=[
            blk(PROJ_DN_Q // W), blk(PROJ_DN_Q // W + 1), blk(PROJ_DN_Q // W + 2), blk(PROJ_DN_Q // W + 3),
            pl.BlockSpec((rows, LANES), lambda b, i: (b * nb + i, 0)),
            pl.BlockSpec((None, DN_CONV, 3 * W), lambda b, i: (layer, 0, 0)),
            prow, prow, prow,
        ],
        out_specs=pl.BlockSpec((rows, W), lambda b, i: (b * nb + i, 0)),
        scratch_shapes=[pltpu.VMEM((rows + 8, W), F32)] * 3 + [pltpu.VMEM((rows, W), F32)] * 3
        + [pltpu.VMEM((HEADS, HEAD_DIM, HEAD_DIM), F32)],
        compiler_params=_cparams(("parallel", "arbitrary")),
        name="deltanet",
    )(proj, proj, proj, proj, small, conv_w, a_log_row, dt_row, out_g)


VT_ONES = 16
VT_ROWS = HEAD_DIM + VT_ONES


def _store_vt(vt_out, h, v16):
    r = lax.broadcasted_iota(jnp.int32, (HEAD_DIM, HEAD_DIM), 0)
    c = lax.broadcasted_iota(jnp.int32, (HEAD_DIM, HEAD_DIM), 1)
    eye = jnp.where(r == c, 1.0, 0.0).astype(BF16)
    base = h * VT_ROWS
    vt_out[base:base + HEAD_DIM, :] = _dot_nt(eye, v16).astype(BF16)
    vt_out[base + HEAD_DIM:base + VT_ROWS, :] = jnp.ones((VT_ONES, v16.shape[0]), BF16)


def _mla_prep_kernel(cq_ref, ckv_ref, kr_ref, cos_ref, sin_ref, qn_ref, kvn_ref, wq_ref, wkv_ref,
                     gq_ref, gk_ref, off_ref, q_out, k_out, vt_out):
    cq = cq_ref[...].astype(F32)
    ckv = ckv_ref[...].astype(F32)
    cq2 = cq * cq
    cq_ms = jnp.sum(cq2[:, :LANES] + cq2[:, LANES:], axis=-1, keepdims=True) * (1.0 / MLA_Q_RANK)
    cq = cq * lax.rsqrt(cq_ms + NORM_EPS) * qn_ref[...]
    ckv = ckv * lax.rsqrt(jnp.mean(ckv * ckv, axis=-1, keepdims=True) + NORM_EPS) * kvn_ref[...]
    q_all = _dot(cq.astype(BF16), wq_ref[...])
    kv_all = _dot(ckv.astype(BF16), wkv_ref[...])
    cos = cos_ref[...]
    sin = sin_ref[...]
    gq = gq_ref[...]
    gk = gk_ref[...]
    lane = lax.broadcasted_iota(jnp.int32, cos.shape, 1)
    live = lane < MLA_ROPE
    scale = MLA_QK_DIM ** -0.5 * LOG2E
    off = off_ref[...]

    def rope(xr):
        return xr * cos + pltpu.roll(xr, MLA_ROPE, 1) * sin

    kr = kr_ref[...].astype(F32)
    kr_sq = jnp.where(live, kr * kr, 0.0)
    kr_rot = rope(kr * gk[:, MLA_NOPE:])
    for h in range(HEADS):
        base = h * QK_PAD
        qn = q_all[:, base:base + MLA_NOPE]
        qr = q_all[:, base + MLA_NOPE:base + QK_PAD]
        ss = jnp.sum(qn * qn + jnp.where(live, qr * qr, 0.0), axis=-1, keepdims=True)
        rinv = lax.rsqrt(ss * (1.0 / MLA_QK_DIM) + NORM_EPS) * scale
        q_out[:, base:base + MLA_NOPE] = (qn * rinv * gq[:, :MLA_NOPE]).astype(BF16)
        q_rot = jnp.where(lane == MLA_ROPE, 1.0, rope(qr * rinv * gq[:, MLA_NOPE:]))
        q_out[:, base + MLA_NOPE:base + QK_PAD] = q_rot.astype(BF16)

        kn = kv_all[:, base:base + MLA_NOPE]
        v = kv_all[:, base + MLA_NOPE:base + QK_PAD]
        ssk = jnp.sum(kn * kn + kr_sq, axis=-1, keepdims=True)
        rk = lax.rsqrt(ssk * (1.0 / MLA_QK_DIM) + NORM_EPS)
        k_out[:, base:base + MLA_NOPE] = (kn * rk * gk[:, :MLA_NOPE]).astype(BF16)
        k_rot = jnp.where(lane == MLA_ROPE, off, kr_rot * rk)
        k_out[:, base + MLA_NOPE:base + QK_PAD] = k_rot.astype(BF16)
        _store_vt(vt_out, h, v.astype(BF16))


def _split3(c):
    hi = c.astype(BF16).astype(F32)
    r = c - hi
    mid = r.astype(BF16).astype(F32)
    lo = (r - mid).astype(BF16).astype(F32)
    return hi, mid, lo


def _fox_selectors():
    sel_q = np.zeros((3 * LANES, HEADS * LANES), np.float32)
    sel_k = np.zeros((3 * LANES, HEADS * LANES), np.float32)
    for h in range(HEADS):
        for piece in range(3):
            sel_q[piece * LANES + 2 * HEADS + h, h * LANES + piece] = 1.0
            sel_k[piece * LANES + 2 * HEADS + h, h * LANES + 3 + piece] = -1.0
    return jnp.asarray(sel_q, BF16), jnp.asarray(sel_k, BF16)


def _fox_prep_kernel(q_ref, k_ref, v_ref, small_ref, fb_ref, gq_ref, gk_ref, off_ref, selq_ref, selk_ref,
                     q_out, k_out, vt_out, carry, *, tm):
    C = LANES
    tril, _ = _tri_masks(C)
    tril_f = tril.astype(F32)
    logf = -_softplus(-(small_ref[...] + fb_ref[...]))
    gq = gq_ref[...] * (HEAD_DIM ** -0.5 * LOG2E)
    gk = gk_ref[...]
    lane = lax.broadcasted_iota(jnp.int32, (1, LANES), 1)
    q_const = jnp.where((lane >= 3) & (lane < 7), 1.0, 0.0)
    k_const = jnp.where(lane < 3, 1.0, jnp.where(lane == 6, off_ref[...], 0.0))
    for c in range(tm // C):
        rs = slice(c * C, (c + 1) * C)
        cum = _dot_f32(tril_f, logf[rs, :]) + carry[...]
        carry[...] = cum[C - 1:C, :]
        pieces = jnp.concatenate([p.astype(BF16) for p in _split3(cum * LOG2E)], axis=-1)
        q_aug = _dot(pieces, selq_ref[...])
        k_aug = _dot(pieces, selk_ref[...])
        for h in range(HEADS):
            hs = slice(h * HEAD_DIM, (h + 1) * HEAD_DIM)
            q = q_ref[rs, hs].astype(F32)
            k = k_ref[rs, hs].astype(F32)
            q = q * lax.rsqrt(jnp.mean(q * q, axis=-1, keepdims=True) + NORM_EPS) * gq
            k = k * lax.rsqrt(jnp.mean(k * k, axis=-1, keepdims=True) + NORM_EPS) * gk
            base = h * QK_PAD
            q_out[rs, base:base + HEAD_DIM] = q.astype(BF16)
            q_out[rs, base + HEAD_DIM:base + QK_PAD] = (q_aug[:, hs] + q_const).astype(BF16)
            k_out[rs, base:base + HEAD_DIM] = k.astype(BF16)
            k_out[rs, base + HEAD_DIM:base + QK_PAD] = (k_aug[:, hs] + k_const).astype(BF16)
    for h in range(HEADS):
        _store_vt(vt_out, h, v_ref[:, h * HEAD_DIM:(h + 1) * HEAD_DIM])


def _attn_kernel(q_ref, k_ref, vt_ref, o_ref, m_scr, acc_scr, *, blk):
    i = pl.program_id(1)
    heads = range(HEADS)
    qsl = lambda h: slice(h * QK_PAD, (h + 1) * QK_PAD)
    vsl = lambda h: slice(h * VT_ROWS, (h + 1) * VT_ROWS)
    osl = lambda h: slice(h * HEAD_DIM, (h + 1) * HEAD_DIM)
    q = [q_ref[:, qsl(h)] for h in heads]
    m_scr[...] = jnp.full_like(m_scr, NEG_BIG)
    acc_scr[...] = jnp.zeros_like(acc_scr)

    def scores(j):
        rows = pl.ds(pl.multiple_of(j * blk, blk), blk)
        return [_dot_nt(k_ref[rows, qsl(h)], q[h]) for h in heads]

    def accumulate(j, st, masked):
        if masked:
            r = lax.broadcasted_iota(jnp.int32, (blk, blk), 0)
            c = lax.broadcasted_iota(jnp.int32, (blk, blk), 1)
            keep = r <= c
            st = [jnp.where(keep, s, NEG_BIG) for s in st]
        for grp in ((0, 1), (2, 3)):
            m_old = [m_scr[h] for h in grp]
            m_new = [jnp.maximum(mo, jnp.max(st[h], axis=0, keepdims=True)) for mo, h in zip(m_old, grp)]
            alpha = [jnp.exp2(mo - mn) for mo, mn in zip(m_old, m_new)]
            p = [jnp.exp2(st[h] - mn).astype(BF16) for h, mn in zip(grp, m_new)]
            pv = [_dot(vt_ref[j, vsl(h), :], pp) for h, pp in zip(grp, p)]
            for n, h in enumerate(grp):
                m_scr[h] = m_new[n]
                acc_scr[h] = alpha[n] * acc_scr[h] + pv[n]

    def pair(t, carry):
        j = 2 * t
        st_a = scores(j)
        st_b = scores(j + 1)
        accumulate(j, st_a, False)
        accumulate(j + 1, st_b, False)
        return carry

    lax.fori_loop(0, i // 2, pair, 0)

    @pl.when(i % 2 == 1)
    def _():
        accumulate(i - 1, scores(i - 1), False)

    accumulate(i, scores(i), True)
    for h in heads:
        acc = acc_scr[h]
        o_ref[:, osl(h)] = (acc[:HEAD_DIM] / acc[HEAD_DIM:HEAD_DIM + 1]).T.astype(BF16)


def _attn_shifted_kernel(q_ref, k_ref, vt_ref, o_ref, acc_scr, *, blk):
    i = pl.program_id(1)
    heads = range(HEADS)
    qsl = lambda h: slice(h * QK_PAD, (h + 1) * QK_PAD)
    vsl = lambda h: slice(h * VT_ROWS, (h + 1) * VT_ROWS)
    osl = lambda h: slice(h * HEAD_DIM, (h + 1) * HEAD_DIM)
    q = [q_ref[:, qsl(h)] for h in heads]
    acc_scr[...] = jnp.zeros_like(acc_scr)

    def block(j):
        rows = pl.ds(pl.multiple_of(j * blk, blk), blk)
        st = [_dot_nt(k_ref[rows, qsl(h)], q[h]) for h in heads]
        pv = [_dot(vt_ref[j, vsl(h), :], jnp.exp2(st[h]).astype(BF16)) for h in heads]
        for h in heads:
            acc_scr[h] += pv[h]

    def diagonal_block():
        half = blk // 2
        lo = pl.ds(pl.multiple_of(i * blk, blk), half)
        hi = pl.ds(pl.multiple_of(i * blk + half, half), half)
        keep_lo = (lax.broadcasted_iota(jnp.int32, (half, blk), 0)
                   <= lax.broadcasted_iota(jnp.int32, (half, blk), 1))
        keep_hi = (lax.broadcasted_iota(jnp.int32, (half, half), 0)
                   <= lax.broadcasted_iota(jnp.int32, (half, half), 1))
        s_lo = [jnp.where(keep_lo, _dot_nt(k_ref[lo, qsl(h)], q[h]), NEG_BIG) for h in heads]
        s_hi = [jnp.where(keep_hi, _dot_nt(k_ref[hi, qsl(h)], q_ref[half:, qsl(h)]), NEG_BIG) for h in heads]
        pv_lo = [_dot(vt_ref[i, vsl(h), :half], jnp.exp2(s_lo[h]).astype(BF16)) for h in heads]
        pv_hi = [_dot(vt_ref[i, vsl(h), half:], jnp.exp2(s_hi[h]).astype(BF16)) for h in heads]
        for h in heads:
            acc_scr[h] += pv_lo[h]
            acc_scr[h, :, half:] += pv_hi[h]

    def pair(t, carry):
        block(2 * t)
        block(2 * t + 1)
        return carry

    lax.fori_loop(0, i // 2, pair, 0)

    @pl.when(i % 2 == 1)
    def _():
        block(i - 1)

    diagonal_block()
    for h in heads:
        acc = acc_scr[h]
        o_ref[:, osl(h)] = (acc[:HEAD_DIM] / acc[HEAD_DIM:HEAD_DIM + 1]).T.astype(BF16)


def _attention(q, k, vt, shifted, B, S, blk):
    T = B * S
    nb = S // blk
    W = HEADS * HEAD_DIM
    HW = HEADS * QK_PAD
    common = dict(
        out_shape=jax.ShapeDtypeStruct((T, W), BF16),
        grid=(B, nb),
        in_specs=[
            pl.BlockSpec((blk, HW), lambda b, i: (b * nb + i, 0)),
            pl.BlockSpec((S, HW), lambda b, i: (b, 0)),
            pl.BlockSpec((None, nb, HEADS * VT_ROWS, blk), lambda b, i: (b, 0, 0, 0)),
        ],
        out_specs=pl.BlockSpec((blk, W), lambda b, i: (b * nb + i, 0)),
        compiler_params=_cparams(("parallel", "arbitrary")),
    )
    acc = pltpu.VMEM((HEADS, VT_ROWS, blk), F32)
    online = pl.pallas_call(functools.partial(_attn_kernel, blk=blk), name="causal_attention",
                            scratch_shapes=[pltpu.VMEM((HEADS, 1, blk), F32), acc], **common)
    fast = pl.pallas_call(functools.partial(_attn_shifted_kernel, blk=blk), name="causal_attention_shifted",
                          scratch_shapes=[acc], **common)
    return lax.cond(shifted, fast, online, q, k, vt)


SHIFT_MAX = 40.0


def _logit_shift(gq, gk, dim):
    L = gq.shape[0]
    bound = (dim ** 0.5) * LOG2E * 1.02 * jnp.max(jnp.abs(gq.reshape(L, -1)), axis=-1) \
        * jnp.max(jnp.abs(gk.reshape(L, -1)), axis=-1)
    ok = bound <= SHIFT_MAX
    shift = jnp.where(ok, -bound, 0.0).astype(F32)
    return jnp.broadcast_to(shift[:, None, None], (L, 1, LANES)), ok


def _sg_kernel(u_ref, v_ref, g_ref, ws_ref, b_ref, o_ref, *, rows):
    Tn = SG_CHUNK
    tril, _ = _tri_masks(Tn)
    g = g_ref[...]
    bias = b_ref[...]
    for gi in range(HEADS):
        hs = slice(gi * HEAD_DIM, (gi + 1) * HEAD_DIM)
        w = jnp.where(tril, ws_ref[gi], 0.0).astype(BF16)
        for n in range(rows // Tn):
            rs = slice(n * Tn, (n + 1) * Tn)
            v = _gelu_tanh(v_ref[rs, hs].astype(F32))
            v = v * lax.rsqrt(jnp.mean(v * v, axis=-1, keepdims=True) + NORM_EPS) * g[:, hs]
            mixed = _dot(w, v.astype(BF16)) + bias[:, hs]
            u = _gelu_tanh(u_ref[rs, hs].astype(F32))
            o_ref[rs, hs] = (u * mixed).astype(BF16)


N_MLA_IN, N_FOX_IN, N_SG_IN = 12, 10, 5


def _mixer_prep_kernel(*refs, tm):
    mla_in = refs[:N_MLA_IN]
    fox_in = refs[N_MLA_IN:N_MLA_IN + N_FOX_IN]
    sg_in = refs[N_MLA_IN + N_FOX_IN:N_MLA_IN + N_FOX_IN + N_SG_IN]
    mq, mk, mvt, fq, fk, fvt, sg_out, carry = refs[N_MLA_IN + N_FOX_IN + N_SG_IN:]

    @pl.when(pl.program_id(1) == 0)
    def _():
        carry[...] = jnp.zeros_like(carry)

    _mla_prep_kernel(*mla_in, mq, mk, mvt)
    _fox_prep_kernel(*fox_in, fq, fk, fvt, carry, tm=tm)
    _sg_kernel(*sg_in, sg_out, rows=tm)


def _mixer_prep(proj, small, cos_t, sin_t, p, mla_shift, fox_shift, sg_w_s, layer, B, S, tm):
    T = B * S
    nb = S // tm
    W = HEADS * HEAD_DIM
    HW = HEADS * QK_PAD
    rows = lambda width, col=0: pl.BlockSpec((tm, width), lambda b, i, col=col: (b * nb + i, col))
    par = lambda *shape: pl.BlockSpec((None,) + shape, lambda b, i: (layer,) + (0,) * len(shape))
    vt = pl.BlockSpec((None, None, HEADS * VT_ROWS, tm), lambda b, i: (b, i, 0, 0))
    vt_shape = jax.ShapeDtypeStruct((B, nb, HEADS * VT_ROWS, tm), BF16)
    qk_shape = jax.ShapeDtypeStruct((T, HW), BF16)
    mla_specs = [rows(MLA_Q_RANK, PROJ_MLA_CQ // MLA_Q_RANK), rows(MLA_KV_RANK, PROJ_MLA_CKV // MLA_KV_RANK),
                 rows(LANES, PROJ_MLA_KR // LANES), rows(LANES), rows(LANES),
                 par(1, MLA_Q_RANK), par(1, MLA_KV_RANK), par(MLA_Q_RANK, HW), par(MLA_KV_RANK, HW),
                 par(1, QK_PAD), par(1, QK_PAD), par(1, LANES)]
    mla_args = [proj, proj, proj, cos_t, sin_t, p["mla_qn"], p["mla_kvn"], p["w_uq_p"], p["w_ukv"],
                p["mla_gq"], p["mla_gk"], mla_shift]
    fox_specs = [rows(W, PROJ_FOX_Q // W), rows(W, PROJ_FOX_Q // W + 1), rows(W, PROJ_FOX_Q // W + 2), rows(LANES),
                 par(1, LANES), par(1, LANES), par(1, LANES), par(1, LANES)]
    fox_specs += [pl.BlockSpec((3 * LANES, HEADS * LANES), lambda b, i: (0, 0))] * 2
    fox_args = [proj, proj, proj, small, p["fox_fb"], p["fox_gq"], p["fox_gk"], fox_shift, *_fox_selectors()]
    sg_specs = [rows(W, PROJ_SG_U // W), rows(W, PROJ_SG_V // W), par(1, W), par(HEADS, SG_CHUNK, SG_CHUNK),
                par(SG_CHUNK, W)]
    sg_args = [proj, proj, p["sg_g"], sg_w_s, p["sg_bias"]]
    assert (len(mla_specs), len(fox_specs), len(sg_specs)) == (N_MLA_IN, N_FOX_IN, N_SG_IN)
    out = pl.pallas_call(
        functools.partial(_mixer_prep_kernel, tm=tm),
        out_shape=(qk_shape, qk_shape, vt_shape, qk_shape, qk_shape, vt_shape, jax.ShapeDtypeStruct((T, W), BF16)),
        grid=(B, nb),
        in_specs=mla_specs + fox_specs + sg_specs,
        out_specs=(rows(HW), rows(HW), vt, rows(HW), rows(HW), vt, rows(W)),
        scratch_shapes=[pltpu.VMEM((1, LANES), F32)],
        compiler_params=_cparams(("parallel", "arbitrary")),
        name="mixer_prep",
    )(*mla_args, *fox_args, *sg_args)
    return out[0:3], out[3:6], out[6]


def _merge_kernel(oa, ob, oc, od, g0, g1, g2, g3, x_ref, wb_ref, wo_ref, n2_ref, out_ref, h_ref):
    merged = None
    for i, (o, g) in enumerate(((oa, g0), (ob, g1), (oc, g2), (od, g3))):
        term = _sigmoid(g[...].astype(F32)) * _dot(o[...], wb_ref[i])
        merged = term if merged is None else merged + term
    x1 = x_ref[...] + _dot(merged.astype(BF16), wo_ref[...])
    out_ref[...] = x1
    ms = jnp.mean(x1 * x1, axis=-1, keepdims=True)
    h_ref[...] = (x1 * lax.rsqrt(ms + NORM_EPS) * n2_ref[...]).astype(BF16)


def _merge(branches, proj, x2, w_branch, w_out, norm2_g, layer, tm):
    T = x2.shape[0]
    ob = pl.BlockSpec((tm, BRANCH_WIDTH), lambda i: (i, 0))
    gate = lambda n: pl.BlockSpec((tm, D_MODEL), lambda i, n=n: (i, n))
    xs = pl.BlockSpec((tm, D_MODEL), lambda i: (i, 0))
    return pl.pallas_call(
        _merge_kernel,
        out_shape=(jax.ShapeDtypeStruct((T, D_MODEL), F32), jax.ShapeDtypeStruct((T, D_MODEL), BF16)),
        grid=(T // tm,),
        in_specs=[ob, ob, ob, ob, gate(0), gate(1), gate(2), gate(3), xs,
                  pl.BlockSpec((None, N_BRANCHES, BRANCH_WIDTH, D_MODEL), lambda i: (layer, 0, 0, 0)),
                  pl.BlockSpec((None, D_MODEL, D_MODEL), lambda i: (layer, 0, 0)),
                  pl.BlockSpec((None, 1, D_MODEL), lambda i: (layer, 0, 0))],
        out_specs=(xs, xs),
        compiler_params=_cparams(("parallel",)),
        name="merge_out_proj",
    )(*branches, proj, proj, proj, proj, x2, w_branch, w_out, norm2_g)


def _ffn_kernel(x_ref, h_ref, w1_ref, w2_ref, out_ref):
    def contribution():
        a = jnp.maximum(_dot(h_ref[...], w1_ref[...]), 0.0)
        return _dot((a * a).astype(BF16), w2_ref[...])

    @pl.when(pl.program_id(1) == 0)
    def _():
        out_ref[...] = x_ref[...] + contribution()

    @pl.when(pl.program_id(1) != 0)
    def _():
        out_ref[...] += contribution()


def _ffn(x2, h2, w1, w2, layer, tm, tf):
    T = x2.shape[0]
    xs = pl.BlockSpec((tm, D_MODEL), lambda i, f: (i, 0))
    return pl.pallas_call(
        _ffn_kernel,
        out_shape=jax.ShapeDtypeStruct((T, D_MODEL), F32),
        grid=(T // tm, D_FF // tf),
        in_specs=[xs, xs,
                  pl.BlockSpec((None, D_MODEL, tf), lambda i, f: (layer, 0, f)),
                  pl.BlockSpec((None, tf, D_MODEL), lambda i, f: (layer, f, 0))],
        out_specs=xs,
        compiler_params=_cparams(("parallel", "arbitrary")),
        name="relu2_mlp",
    )(x2, h2, w1, w2)


def _pad_lanes(a, width):
    return jnp.pad(a, [(0, 0)] * (a.ndim - 1) + [(0, width - a.shape[-1])])


def _swap_halves(a):
    h = a.shape[-1] // 2
    return jnp.concatenate([a[..., h:], a[..., :h]], axis=-1)


def _prep_params(norm1_g, w_in, dn_a_log, dn_dt_bias, dn_out_norm_g, mla_q_norm_g, mla_kv_norm_g, mla_w_uq,
                 mla_w_ukv, mla_qk_q_g, mla_qk_k_g, sg_v_norm_g, sg_b_s, fox_q_norm_g, fox_k_norm_g, fox_f_bias,
                 w_branch, w_out, norm2_g, w_ff1, w_ff2):
    L = w_in.shape[0]
    col = lambda a, b: w_in[:, :, a:b]
    kr = col(O_MLA_KR, O_MLA_KR + MLA_ROPE)
    w_p = jnp.concatenate([
        col(O_GATES, O_END), col(O_DN_QKV, O_DN_Z), col(O_DN_Z, O_DN_A), col(O_SG_U, O_SG_V),
        col(O_SG_V, O_FOX_QKV), col(O_FOX_QKV, O_FOX_F), col(O_MLA_CQ, O_MLA_CKV), col(O_MLA_CKV, O_MLA_KR),
        kr, _swap_halves(kr)], axis=-1).astype(BF16)
    assert w_p.shape[-1] == PROJ_WIDTH
    w_small = _pad_lanes(jnp.concatenate([col(O_DN_A, O_DN_B), col(O_DN_B, O_MLA_CQ), col(O_FOX_F, O_GATES)],
                                         axis=-1), LANES).astype(BF16)

    def row(a, width=LANES, offset=0):
        a = a.reshape(L, 1, -1)
        return jnp.pad(a, ((0, 0), (0, 0), (offset, width - offset - a.shape[-1])))

    wq = mla_w_uq.reshape(L, MLA_Q_RANK, HEADS, MLA_QK_DIM)
    wq_r = wq[..., MLA_NOPE:]
    w_uq_p = jnp.concatenate([wq, _swap_halves(wq_r)], axis=-1).reshape(L, MLA_Q_RANK, HEADS * QK_PAD).astype(BF16)

    def qk_gain(g):
        return jnp.concatenate([g, _swap_halves(g[:, MLA_NOPE:])], axis=-1).reshape(L, 1, QK_PAD)

    bias_full = jnp.repeat(jnp.swapaxes(sg_b_s, 1, 2), HEAD_DIM, axis=-1)
    return dict(
        norm1_g=norm1_g.reshape(L, 1, D_MODEL), w_p=w_p, w_small=w_small,
        a_log=row(dn_a_log), dt=row(dn_dt_bias), dn_og=dn_out_norm_g.reshape(L, 1, HEAD_DIM),
        mla_qn=mla_q_norm_g.reshape(L, 1, MLA_Q_RANK), mla_kvn=mla_kv_norm_g.reshape(L, 1, MLA_KV_RANK),
        w_uq_p=w_uq_p, w_ukv=mla_w_ukv.astype(BF16), mla_gq=qk_gain(mla_qk_q_g), mla_gk=qk_gain(mla_qk_k_g),
        sg_g=sg_v_norm_g.reshape(L, 1, HEADS * HEAD_DIM), sg_bias=bias_full,
        fox_fb=row(fox_f_bias, offset=2 * HEADS), fox_gq=fox_q_norm_g.reshape(L, 1, HEAD_DIM),
        fox_gk=fox_k_norm_g.reshape(L, 1, HEAD_DIM),
        w_branch=w_branch.astype(BF16), w_out=w_out.astype(BF16), norm2_g=norm2_g.reshape(L, 1, D_MODEL),
        w_ff1=w_ff1.astype(BF16), w_ff2=w_ff2.astype(BF16),
    )


def _tiles(S):
    pick = lambda want: min(want, S)
    return dict(in_tm=pick(2048), in_tn=1536, dn_rows=pick(1024), prep_tm=pick(512), attn_blk=pick(512),
                merge_tm=pick(512), ffn_tm=pick(1024), ffn_tf=2048, rope_tm=pick(1024))


def kernel(x, positions, norm1_g, w_in, dn_conv_w, dn_a_log, dn_dt_bias, dn_out_norm_g, mla_q_norm_g, mla_kv_norm_g, mla_w_uq, mla_w_ukv, mla_qk_q_g, mla_qk_k_g, sg_v_norm_g, sg_w_s, sg_b_s, fox_q_norm_g, fox_k_norm_g, fox_f_bias, w_branch, w_out, norm2_g, w_ff1, w_ff2):
    B, S, D = x.shape
    assert D == D_MODEL and S % LANES == 0
    T = B * S
    depth = w_in.shape[0]
    t = _tiles(S)
    p = _prep_params(norm1_g, w_in, dn_a_log, dn_dt_bias, dn_out_norm_g, mla_q_norm_g, mla_kv_norm_g, mla_w_uq,
                     mla_w_ukv, mla_qk_q_g, mla_qk_k_g, sg_v_norm_g, sg_b_s, fox_q_norm_g, fox_k_norm_g,
                     fox_f_bias, w_branch, w_out, norm2_g, w_ff1, w_ff2)
    cos_t, sin_t = _rope_tables(positions, t["rope_tm"])
    mla_shift, mla_ok = _logit_shift(mla_qk_q_g, mla_qk_k_g, MLA_QK_DIM)
    fox_shift, fox_ok = _logit_shift(fox_q_norm_g, fox_k_norm_g, HEAD_DIM)
    x2 = x.reshape(T, D)
    for l in range(depth):
        proj, small = _in_proj(x2, p["norm1_g"], p["w_p"], p["w_small"], l, t["in_tm"], t["in_tn"])
        o_a = _deltanet(proj, small, dn_conv_w, p["a_log"], p["dt"], p["dn_og"], l, B, S, t["dn_rows"])
        (qb, kb, vtb), (qd, kd, vtd), o_c = _mixer_prep(proj, small, cos_t, sin_t, p, mla_shift, fox_shift, sg_w_s,
                                                          l, B, S, t["prep_tm"])
        o_b = _attention(qb, kb, vtb, mla_ok[l], B, S, t["attn_blk"])
        o_d = _attention(qd, kd, vtd, fox_ok[l], B, S, t["attn_blk"])
        x2, h2 = _merge((o_a, o_b, o_c, o_d), proj, x2, p["w_branch"], p["w_out"], p["norm2_g"], l, t["merge_tm"])
        x2 = _ffn(x2, h2, p["w_ff1"], p["w_ff2"], l, t["ffn_tm"], t["ffn_tf"])
    return x2.reshape(B, S, D)
```

```python
import functools
import math

import jax
import numpy as np
import jax.numpy as jnp
from jax import lax
from jax.experimental import pallas as pl
from jax.experimental.pallas import tpu as pltpu

F32 = jnp.float32
BF16 = jnp.bfloat16

D_MODEL = 1024
NORM_EPS = 1e-6
N_BRANCHES = 4
BRANCH_WIDTH = 512
D_FF = 4 * D_MODEL
HEADS = 4
HEAD_DIM = 128
DN_CONV = 4
DN_CHUNK = 128
MLA_Q_RANK = 256
MLA_KV_RANK = 128
MLA_NOPE = 128
MLA_ROPE = 64
MLA_QK_DIM = MLA_NOPE + MLA_ROPE
ROPE_THETA = 10000.0
SG_CHUNK = 128
QK_PAD = 256
LANES = 128
NEG_BIG = -1e30
LOG2E = math.log2(math.e)

PROJ_GATES = 0
PROJ_DN_Q = 4096
PROJ_SG_U = 6144
PROJ_SG_V = 6656
PROJ_FOX_Q = 7168
PROJ_MLA_CQ = 8704
PROJ_MLA_CKV = 8960
PROJ_MLA_KR = 9088
PROJ_WIDTH = 9216

_SPLITS = (1536, 512, 4, 4, 256, 128, 64, 512, 512, 1536, 4, 4096)
_OFF = [0]
for _s in _SPLITS:
    _OFF.append(_OFF[-1] + _s)
(O_DN_QKV, O_DN_Z, O_DN_A, O_DN_B, O_MLA_CQ, O_MLA_CKV, O_MLA_KR, O_SG_U, O_SG_V, O_FOX_QKV, O_FOX_F,
 O_GATES, O_END) = _OFF

VMEM_LIMIT = 56 * 1024 * 1024


def _cparams(sem):
    return pltpu.CompilerParams(dimension_semantics=sem, vmem_limit_bytes=VMEM_LIMIT)


def _dot(a, b):
    return jnp.dot(a, b, preferred_element_type=F32)


def _dot_nt(a, b):
    return lax.dot_general(a, b, (((1,), (1,)), ((), ())), preferred_element_type=F32)


def _dot_f32(a, b):
    return jnp.dot(a, b, preferred_element_type=F32, precision=lax.Precision.HIGHEST)


def _sigmoid(x):
    return 0.5 * jnp.tanh(0.5 * x) + 0.5


def _softplus(x):
    return jnp.maximum(x, 0.0) + jnp.log1p(jnp.exp(-jnp.abs(x)))


def _gelu_tanh(x):
    c = math.sqrt(2.0 / math.pi)
    half = 0.5 * x
    return half + half * jnp.tanh(x * (c + (c * 0.044715) * (x * x)))


def _tri_masks(n):
    r = lax.broadcasted_iota(jnp.int32, (n, n), 0)
    c = lax.broadcasted_iota(jnp.int32, (n, n), 1)
    return r >= c, r > c


def _rope_kernel(pos_ref, freq_ref, sign_ref, cos_ref, sin_ref):
    ang = pos_ref[...] * freq_ref[...]
    lane = lax.broadcasted_iota(jnp.int32, ang.shape, 1)
    live = lane < MLA_ROPE
    cos_ref[...] = jnp.where(live, jnp.cos(ang), 0.0)
    sin_ref[...] = jnp.where(live, jnp.sin(ang) * sign_ref[...], 0.0)


def _rope_tables(positions, tm):
    T = positions.size
    half = MLA_ROPE // 2
    inv_freq = ROPE_THETA ** (-jnp.arange(0, MLA_ROPE, 2, dtype=F32) / MLA_ROPE)
    zeros = jnp.zeros((LANES - MLA_ROPE,), F32)
    freq = jnp.concatenate([inv_freq, inv_freq, zeros]).reshape(1, LANES)
    sign = jnp.concatenate([-jnp.ones((half,), F32), jnp.ones((half,), F32), zeros]).reshape(1, LANES)
    row = pl.BlockSpec((1, LANES), lambda i: (0, 0))
    tab = pl.BlockSpec((tm, LANES), lambda i: (i, 0))
    return pl.pallas_call(
        _rope_kernel,
        out_shape=(jax.ShapeDtypeStruct((T, LANES), F32), jax.ShapeDtypeStruct((T, LANES), F32)),
        grid=(T // tm,),
        in_specs=[tab, row, row],
        out_specs=(tab, tab),
        compiler_params=_cparams(("parallel",)),
        name="rope_tables",
    )(jnp.broadcast_to(positions.reshape(T, 1).astype(F32), (T, LANES)), freq, sign)


def _in_proj_kernel(x_ref, g_ref, w_ref, ws_ref, proj_ref, small_ref, h_scr):
    @pl.when(pl.program_id(1) == 0)
    def _():
        x = x_ref[...]
        ms = jnp.mean(x * x, axis=-1, keepdims=True)
        h = (x * lax.rsqrt(ms + NORM_EPS) * g_ref[...]).astype(BF16)
        h_scr[...] = h
        small_ref[...] = _dot(h, ws_ref[...])
        proj_ref[...] = _dot(h, w_ref[...]).astype(BF16)

    @pl.when(pl.program_id(1) != 0)
    def _():
        proj_ref[...] = _dot(h_scr[...], w_ref[...]).astype(BF16)


def _in_proj(x2, g, w_p, w_small, layer, tm, tn):
    T = x2.shape[0]
    return pl.pallas_call(
        _in_proj_kernel,
        out_shape=(jax.ShapeDtypeStruct((T, PROJ_WIDTH), BF16), jax.ShapeDtypeStruct((T, LANES), F32)),
        grid=(T // tm, PROJ_WIDTH // tn),
        in_specs=[
            pl.BlockSpec((tm, D_MODEL), lambda i, j: (i, 0)),
            pl.BlockSpec((None, 1, D_MODEL), lambda i, j: (layer, 0, 0)),
            pl.BlockSpec((None, D_MODEL, tn), lambda i, j: (layer, 0, j)),
            pl.BlockSpec((None, D_MODEL, LANES), lambda i, j: (layer, 0, 0)),
        ],
        out_specs=(pl.BlockSpec((tm, tn), lambda i, j: (i, j)),
                   pl.BlockSpec((tm, LANES), lambda i, j: (i, 0))),
        scratch_shapes=[pltpu.VMEM((tm, D_MODEL), BF16)],
        compiler_params=_cparams(("parallel", "arbitrary")),
        name="in_proj",
    )(x2, g, w_p, w_small)


INV_BASE = 8


def _inverse_masks(n):
    r = lax.broadcasted_iota(jnp.int32, (n, n), 0)
    c = lax.broadcasted_iota(jnp.int32, (n, n), 1)
    same = lambda s: (r // s) == (c // s)
    levels = []
    s = INV_BASE
    while s < n:
        levels.append(same(2 * s) & jnp.logical_not(same(s)))
        s *= 2
    as16 = lambda m: jnp.where(m, 1.0, 0.0).astype(BF16)
    return (r == c).astype(F32), as16(same(INV_BASE)), [as16(m) for m in levels]


def _unit_lower_inverse(nmats, masks):
    eye, base, levels = masks
    n16 = [n.astype(BF16) for n in nmats]
    ps = [n * base for n in n16]
    xs = [eye - p.astype(F32) for p in ps]
    for _ in range(int(math.log2(INV_BASE)) - 1):
        ps = [_dot(p, p).astype(BF16) for p in ps]
        xs = [x + _dot(x.astype(BF16), p) for x, p in zip(xs, ps)]
    for off in levels:
        x16 = [x.astype(BF16) for x in xs]
        ts = [_dot(n * off, xb).astype(BF16) for n, xb in zip(n16, x16)]
        xs = [x - _dot(xb, t) for x, xb, t in zip(xs, x16, ts)]
    return xs


def _deltanet_kernel(q_ref, k_ref, v_ref, z_ref, small_ref, cw_ref, alog_ref, dt_ref, og_ref, o_ref,
                     qbuf, kbuf, vbuf, qs, ks, vs, state, *, rows):
    C = DN_CHUNK
    W = HEADS * HEAD_DIM
    NC = rows // C

    @pl.when(pl.program_id(1) == 0)
    def _():
        state[...] = jnp.zeros_like(state)
        for buf in (qbuf, kbuf, vbuf):
            buf[0:8, :] = jnp.zeros((8, W), F32)

    def conv_silu(buf, x_ref, w, dst):
        buf[8:8 + rows, :] = x_ref[...].astype(F32)
        acc = buf[8:8 + rows, :] * w[DN_CONV - 1:DN_CONV, :]
        for s in range(1, DN_CONV):
            acc = acc + buf[8 - s:8 - s + rows, :] * w[DN_CONV - 1 - s:DN_CONV - s, :]
        buf[0:8, :] = buf[rows:rows + 8, :]
        dst[...] = acc * _sigmoid(acc)

    cw = cw_ref[...]
    conv_silu(qbuf, q_ref, cw[:, 0:W], qs)
    conv_silu(kbuf, k_ref, cw[:, W:2 * W], ks)
    conv_silu(vbuf, v_ref, cw[:, 2 * W:3 * W], vs)

    small = small_ref[...]
    g_all = -jnp.exp(alog_ref[...]) * _softplus(small + dt_ref[...])
    beta_all = _sigmoid(small)

    tril, strict = _tri_masks(C)
    tril_f = tril.astype(F32)
    inv_masks = _inverse_masks(C)
    og = og_ref[...]

    probs = [(c, h) for c in range(NC) for h in range(HEADS)]
    gcs = [_dot_f32(tril_f, g_all[c * C:(c + 1) * C, :]) for c in range(NC)]
    gcts = [gc.T for gc in gcs]
    rsl = lambda c: slice(c * C, (c + 1) * C)
    hsl = lambda h: slice(h * HEAD_DIM, (h + 1) * HEAD_DIM)
    qn, kn = [], []
    for c, h in probs:
        q = qs[rsl(c), hsl(h)]
        k = ks[rsl(c), hsl(h)]
        qn.append(q * (lax.rsqrt(jnp.sum(q * q, axis=-1, keepdims=True) + NORM_EPS) * (HEAD_DIM ** -0.5)))
        kn.append(k * lax.rsqrt(jnp.sum(k * k, axis=-1, keepdims=True) + NORM_EPS))
    gcol = [gcs[c][:, h:h + 1] for c, h in probs]
    bcol = [beta_all[rsl(c), HEADS + h:HEADS + h + 1] for c, h in probs]
    glast = [gcs[c][C - 1:C, h:h + 1] for c, h in probs]
    decay = [jnp.where(tril, jnp.exp(jnp.where(tril, gcol[i] - gcts[c][h:h + 1, :], 0.0)), 0.0)
             for i, (c, h) in enumerate(probs)]
    kb = [k * b for k, b in zip(kn, bcol)]
    k16 = [k.astype(BF16) for k in kn]
    qk_kk = [_dot_nt(jnp.concatenate([kbi, q], axis=0).astype(BF16), k) for kbi, q, k in zip(kb, qn, k16)]
    nmat = [jnp.where(strict, m[:C] * d, 0.0) for m, d in zip(qk_kk, decay)]
    a_qk = [(m[C:] * d).astype(BF16) for m, d in zip(qk_kk, decay)]
    eg = [jnp.exp(g) for g in gcol]
    rhs = [jnp.concatenate([vs[rsl(c), hsl(h)] * bcol[i], kb[i] * eg[i]], axis=-1).astype(BF16)
           for i, (c, h) in enumerate(probs)]
    xinv = _unit_lower_inverse(nmat, inv_masks)
    sol = [_dot(x.astype(BF16), r) for x, r in zip(xinv, rhs)]
    u = [s[:, :HEAD_DIM] for s in sol]
    wq = [jnp.concatenate([s[:, HEAD_DIM:], q * e], axis=0).astype(BF16) for s, q, e in zip(sol, qn, eg)]
    kdt = [(k * jnp.exp(gl - g)).T.astype(BF16) for k, gl, g in zip(kn, glast, gcol)]
    egl = [jnp.exp(gl) for gl in glast]

    st = [state[h] for h in range(HEADS)]
    for c in range(NC):
        idx = [c * HEADS + h for h in range(HEADS)]
        r = [_dot(wq[i], st[h].astype(BF16)) for h, i in enumerate(idx)]
        v16 = [(u[i] - r[h][:C]).astype(BF16) for h, i in enumerate(idx)]
        o = [r[h][C:] + _dot(a_qk[i], v16[h]) for h, i in enumerate(idx)]
        st = [st[h] * egl[i] + _dot(kdt[i], v16[h]) for h, i in enumerate(idx)]
        for h in range(HEADS):
            oh = o[h] * lax.rsqrt(jnp.mean(o[h] * o[h], axis=-1, keepdims=True) + NORM_EPS) * og
            z = z_ref[rsl(c), hsl(h)].astype(F32)
            o_ref[rsl(c), hsl(h)] = (oh * (z * _sigmoid(z))).astype(BF16)
    for h in range(HEADS):
        state[h] = st[h]


def _deltanet(proj, small, conv_w, a_log_row, dt_row, out_g, layer, B, S, rows):
    T = B * S
    W = HEADS * HEAD_DIM
    nb = S // rows
    blk = lambda col: pl.BlockSpec((rows, W), lambda b, i, col=col: (b * nb + i, col))
    prow = pl.BlockSpec((None, 1, LANES), lambda b, i: (layer, 0, 0))
    return pl.pallas_call(
        functools.partial(_deltanet_kernel, rows=rows),
        out_shape=jax.ShapeDtypeStruct((T, W), BF16),
        grid=(B, nb),
        in_specs=[
            blk(PROJ_DN_Q // W), blk(PROJ_DN_Q // W + 1), blk(PROJ_DN_Q // W + 2), blk(PROJ_DN_Q // W + 3),
            pl.BlockSpec((rows, LANES), lambda b, i: (b * nb + i, 0)),
            pl.BlockSpec((None, DN_CONV, 3 * W), lambda b, i: (layer, 0, 0)),
            prow, prow, prow,
        ],
        out_specs=pl.BlockSpec((rows, W), lambda b, i: (b * nb + i, 0)),
        scratch_shapes=[pltpu.VMEM((rows + 8, W), F32)] * 3 + [pltpu.VMEM((rows, W), F32)] * 3
        + [pltpu.VMEM((HEADS, HEAD_DIM, HEAD_DIM), F32)],
        compiler_params=_cparams(("parallel", "arbitrary")),
        name="deltanet",
    )(proj, proj, proj, proj, small, conv_w, a_log_row, dt_row, out_g)


VT_ONES = 16
VT_ROWS = HEAD_DIM + VT_ONES


def _store_vt(vt_out, h, v16):
    r = lax.broadcasted_iota(jnp.int32, (HEAD_DIM, HEAD_DIM), 0)
    c = lax.broadcasted_iota(jnp.int32, (HEAD_DIM, HEAD_DIM), 1)
    eye = jnp.where(r == c, 1.0, 0.0).astype(BF16)
    base = h * VT_ROWS
    vt_out[base:base + HEAD_DIM, :] = _dot_nt(eye, v16).astype(BF16)
    vt_out[base + HEAD_DIM:base + VT_ROWS, :] = jnp.ones((VT_ONES, v16.shape[0]), BF16)


def _mla_prep_kernel(cq_ref, ckv_ref, kr_ref, cos_ref, sin_ref, qn_ref, kvn_ref, wq_ref, wkv_ref,
                     gq_ref, gk_ref, off_ref, q_out, k_out, vt_out):
    cq = cq_ref[...].astype(F32)
    ckv = ckv_ref[...].astype(F32)
    cq2 = cq * cq
    cq_ms = jnp.sum(cq2[:, :LANES] + cq2[:, LANES:], axis=-1, keepdims=True) * (1.0 / MLA_Q_RANK)
    cq = cq * lax.rsqrt(cq_ms + NORM_EPS) * qn_ref[...]
    ckv = ckv * lax.rsqrt(jnp.mean(ckv * ckv, axis=-1, keepdims=True) + NORM_EPS) * kvn_ref[...]
    q_all = _dot(cq.astype(BF16), wq_ref[...])
    kv_all = _dot(ckv.astype(BF16), wkv_ref[...])
    cos = cos_ref[...]
    sin = sin_ref[...]
    gq = gq_ref[...]
    gk = gk_ref[...]
    lane = lax.broadcasted_iota(jnp.int32, cos.shape, 1)
    live = lane < MLA_ROPE
    scale = MLA_QK_DIM ** -0.5 * LOG2E
    off = off_ref[...]

    def rope(xr):
        return xr * cos + pltpu.roll(xr, MLA_ROPE, 1) * sin

    kr = kr_ref[...].astype(F32)
    kr_sq = jnp.where(live, kr * kr, 0.0)
    kr_rot = rope(kr * gk[:, MLA_NOPE:])
    for h in range(HEADS):
        base = h * QK_PAD
        qn = q_all[:, base:base + MLA_NOPE]
        qr = q_all[:, base + MLA_NOPE:base + QK_PAD]
        ss = jnp.sum(qn * qn + jnp.where(live, qr * qr, 0.0), axis=-1, keepdims=True)
        rinv = lax.rsqrt(ss * (1.0 / MLA_QK_DIM) + NORM_EPS) * scale
        q_out[:, base:base + MLA_NOPE] = (qn * rinv * gq[:, :MLA_NOPE]).astype(BF16)
        q_rot = jnp.where(lane == MLA_ROPE, 1.0, rope(qr * rinv * gq[:, MLA_NOPE:]))
        q_out[:, base + MLA_NOPE:base + QK_PAD] = q_rot.astype(BF16)

        kn = kv_all[:, base:base + MLA_NOPE]
        v = kv_all[:, base + MLA_NOPE:base + QK_PAD]
        ssk = jnp.sum(kn * kn + kr_sq, axis=-1, keepdims=True)
        rk = lax.rsqrt(ssk * (1.0 / MLA_QK_DIM) + NORM_EPS)
        k_out[:, base:base + MLA_NOPE] = (kn * rk * gk[:, :MLA_NOPE]).astype(BF16)
        k_rot = jnp.where(lane == MLA_ROPE, off, kr_rot * rk)
        k_out[:, base + MLA_NOPE:base + QK_PAD] = k_rot.astype(BF16)
        _store_vt(vt_out, h, v.astype(BF16))


def _split3(c):
    hi = c.astype(BF16).astype(F32)
    r = c - hi
    mid = r.astype(BF16).astype(F32)
    lo = (r - mid).astype(BF16).astype(F32)
    return hi, mid, lo


def _fox_selectors():
    sel_q = np.zeros((3 * LANES, HEADS * LANES), np.float32)
    sel_k = np.zeros((3 * LANES, HEADS * LANES), np.float32)
    for h in range(HEADS):
        for piece in range(3):
            sel_q[piece * LANES + 2 * HEADS + h, h * LANES + piece] = 1.0
            sel_k[piece * LANES + 2 * HEADS + h, h * LANES + 3 + piece] = -1.0
    return jnp.asarray(sel_q, BF16), jnp.asarray(sel_k, BF16)


def _fox_prep_kernel(q_ref, k_ref, v_ref, small_ref, fb_ref, gq_ref, gk_ref, off_ref, selq_ref, selk_ref,
                     q_out, k_out, vt_out, carry, *, tm):
    C = LANES
    tril, _ = _tri_masks(C)
    tril_f = tril.astype(F32)
    logf = -_softplus(-(small_ref[...] + fb_ref[...]))
    gq = gq_ref[...] * (HEAD_DIM ** -0.5 * LOG2E)
    gk = gk_ref[...]
    lane = lax.broadcasted_iota(jnp.int32, (1, LANES), 1)
    q_const = jnp.where((lane >= 3) & (lane < 7), 1.0, 0.0)
    k_const = jnp.where(lane < 3, 1.0, jnp.where(lane == 6, off_ref[...], 0.0))
    for c in range(tm // C):
        rs = slice(c * C, (c + 1) * C)
        cum = _dot_f32(tril_f, logf[rs, :]) + carry[...]
        carry[...] = cum[C - 1:C, :]
        pieces = jnp.concatenate([p.astype(BF16) for p in _split3(cum * LOG2E)], axis=-1)
        q_aug = _dot(pieces, selq_ref[...])
        k_aug = _dot(pieces, selk_ref[...])
        for h in range(HEADS):
            hs = slice(h * HEAD_DIM, (h + 1) * HEAD_DIM)
            q = q_ref[rs, hs].astype(F32)
            k = k_ref[rs, hs].astype(F32)
            q = q * lax.rsqrt(jnp.mean(q * q, axis=-1, keepdims=True) + NORM_EPS) * gq
            k = k * lax.rsqrt(jnp.mean(k * k, axis=-1, keepdims=True) + NORM_EPS) * gk
            base = h * QK_PAD
            q_out[rs, base:base + HEAD_DIM] = q.astype(BF16)
            q_out[rs, base + HEAD_DIM:base + QK_PAD] = (q_aug[:, hs] + q_const).astype(BF16)
            k_out[rs, base:base + HEAD_DIM] = k.astype(BF16)
            k_out[rs, base + HEAD_DIM:base + QK_PAD] = (k_aug[:, hs] + k_const).astype(BF16)
    for h in range(HEADS):
        _store_vt(vt_out, h, v_ref[:, h * HEAD_DIM:(h + 1) * HEAD_DIM])


def _attn_kernel(q_ref, k_ref, vt_ref, o_ref, m_scr, acc_scr, *, blk):
    i = pl.program_id(1)
    heads = range(HEADS)
    qsl = lambda h: slice(h * QK_PAD, (h + 1) * QK_PAD)
    vsl = lambda h: slice(h * VT_ROWS, (h + 1) * VT_ROWS)
    osl = lambda h: slice(h * HEAD_DIM, (h + 1) * HEAD_DIM)
    q = [q_ref[:, qsl(h)] for h in heads]
    m_scr[...] = jnp.full_like(m_scr, NEG_BIG)
    acc_scr[...] = jnp.zeros_like(acc_scr)

    def scores(j):
        rows = pl.ds(pl.multiple_of(j * blk, blk), blk)
        return [_dot_nt(k_ref[rows, qsl(h)], q[h]) for h in heads]

    def accumulate(j, st, masked):
        if masked:
            r = lax.broadcasted_iota(jnp.int32, (blk, blk), 0)
            c = lax.broadcasted_iota(jnp.int32, (blk, blk), 1)
            keep = r <= c
            st = [jnp.where(keep, s, NEG_BIG) for s in st]
        for grp in ((0, 1), (2, 3)):
            m_old = [m_scr[h] for h in grp]
            m_new = [jnp.maximum(mo, jnp.max(st[h], axis=0, keepdims=True)) for mo, h in zip(m_old, grp)]
            alpha = [jnp.exp2(mo - mn) for mo, mn in zip(m_old, m_new)]
            p = [jnp.exp2(st[h] - mn).astype(BF16) for h, mn in zip(grp, m_new)]
            pv = [_dot(vt_ref[j, vsl(h), :], pp) for h, pp in zip(grp, p)]
            for n, h in enumerate(grp):
                m_scr[h] = m_new[n]
                acc_scr[h] = alpha[n] * acc_scr[h] + pv[n]

    def pair(t, carry):
        j = 2 * t
        st_a = scores(j)
        st_b = scores(j + 1)
        accumulate(j, st_a, False)
        accumulate(j + 1, st_b, False)
        return carry

    lax.fori_loop(0, i // 2, pair, 0)

    @pl.when(i % 2 == 1)
    def _():
        accumulate(i - 1, scores(i - 1), False)

    accumulate(i, scores(i), True)
    for h in heads:
        acc = acc_scr[h]
        o_ref[:, osl(h)] = (acc[:HEAD_DIM] / acc[HEAD_DIM:HEAD_DIM + 1]).T.astype(BF16)


def _attn_shifted_kernel(q_ref, k_ref, vt_ref, o_ref, acc_scr, *, blk):
    i = pl.program_id(1)
    heads = range(HEADS)
    qsl = lambda h: slice(h * QK_PAD, (h + 1) * QK_PAD)
    vsl = lambda h: slice(h * VT_ROWS, (h + 1) * VT_ROWS)
    osl = lambda h: slice(h * HEAD_DIM, (h + 1) * HEAD_DIM)
    q = [q_ref[:, qsl(h)] for h in heads]
    acc_scr[...] = jnp.zeros_like(acc_scr)

    def block(j):
        rows = pl.ds(pl.multiple_of(j * blk, blk), blk)
        st = [_dot_nt(k_ref[rows, qsl(h)], q[h]) for h in heads]
        pv = [_dot(vt_ref[j, vsl(h), :], jnp.exp2(st[h]).astype(BF16)) for h in heads]
        for h in heads:
            acc_scr[h] += pv[h]

    def diagonal_block():
        half = blk // 2
        lo = pl.ds(pl.multiple_of(i * blk, blk), half)
        hi = pl.ds(pl.multiple_of(i * blk + half, half), half)
        keep_lo = (lax.broadcasted_iota(jnp.int32, (half, blk), 0)
                   <= lax.broadcasted_iota(jnp.int32, (half, blk), 1))
        keep_hi = (lax.broadcasted_iota(jnp.int32, (half, half), 0)
                   <= lax.broadcasted_iota(jnp.int32, (half, half), 1))
        s_lo = [jnp.where(keep_lo, _dot_nt(k_ref[lo, qsl(h)], q[h]), NEG_BIG) for h in heads]
        s_hi = [jnp.where(keep_hi, _dot_nt(k_ref[hi, qsl(h)], q_ref[half:, qsl(h)]), NEG_BIG) for h in heads]
        pv_lo = [_dot(vt_ref[i, vsl(h), :half], jnp.exp2(s_lo[h]).astype(BF16)) for h in heads]
        pv_hi = [_dot(vt_ref[i, vsl(h), half:], jnp.exp2(s_hi[h]).astype(BF16)) for h in heads]
        for h in heads:
            acc_scr[h] += pv_lo[h]
            acc_scr[h, :, half:] += pv_hi[h]

    def pair(t, carry):
        j = 2 * t
        rows = pl.ds(pl.multiple_of(j * blk, 2 * blk), 2 * blk)
        st = [_dot_nt(k_ref[rows, qsl(h)], q[h]) for h in heads]
        pv = [_dot(jnp.concatenate([vt_ref[j, vsl(h), :], vt_ref[j + 1, vsl(h), :]], axis=1),
                   jnp.exp2(st[h]).astype(BF16)) for h in heads]
        for h in heads:
            acc_scr[h] += pv[h]
        return carry

    lax.fori_loop(0, i // 2, pair, 0)

    @pl.when(i % 2 == 1)
    def _():
        block(i - 1)

    diagonal_block()
    for h in heads:
        acc = acc_scr[h]
        o_ref[:, osl(h)] = (acc[:HEAD_DIM] / acc[HEAD_DIM:HEAD_DIM + 1]).T.astype(BF16)


def _attention(q, k, vt, shifted, B, S, blk):
    T = B * S
    nb = S // blk
    W = HEADS * HEAD_DIM
    HW = HEADS * QK_PAD
    common = dict(
        out_shape=jax.ShapeDtypeStruct((T, W), BF16),
        grid=(B, nb),
        in_specs=[
            pl.BlockSpec((blk, HW), lambda b, i: (b * nb + i, 0)),
            pl.BlockSpec((S, HW), lambda b, i: (b, 0)),
            pl.BlockSpec((None, nb, HEADS * VT_ROWS, blk), lambda b, i: (b, 0, 0, 0)),
        ],
        out_specs=pl.BlockSpec((blk, W), lambda b, i: (b * nb + i, 0)),
        compiler_params=_cparams(("parallel", "arbitrary")),
    )
    acc = pltpu.VMEM((HEADS, VT_ROWS, blk), F32)
    online = pl.pallas_call(functools.partial(_attn_kernel, blk=blk), name="causal_attention",
                            scratch_shapes=[pltpu.VMEM((HEADS, 1, blk), F32), acc], **common)
    fast = pl.pallas_call(functools.partial(_attn_shifted_kernel, blk=blk), name="causal_attention_shifted",
                          scratch_shapes=[acc], **common)
    return lax.cond(shifted, fast, online, q, k, vt)


SHIFT_MAX = 40.0


def _logit_shift(gq, gk, dim):
    L = gq.shape[0]
    bound = (dim ** 0.5) * LOG2E * 1.02 * jnp.max(jnp.abs(gq.reshape(L, -1)), axis=-1) \
        * jnp.max(jnp.abs(gk.reshape(L, -1)), axis=-1)
    ok = bound <= SHIFT_MAX
    shift = jnp.where(ok, -bound, 0.0).astype(F32)
    return jnp.broadcast_to(shift[:, None, None], (L, 1, LANES)), ok


def _sg_kernel(u_ref, v_ref, g_ref, ws_ref, b_ref, o_ref, *, rows):
    Tn = SG_CHUNK
    tril, _ = _tri_masks(Tn)
    g = g_ref[...]
    bias = b_ref[...]
    for gi in range(HEADS):
        hs = slice(gi * HEAD_DIM, (gi + 1) * HEAD_DIM)
        w = jnp.where(tril, ws_ref[gi], 0.0).astype(BF16)
        for n in range(rows // Tn):
            rs = slice(n * Tn, (n + 1) * Tn)
            v = _gelu_tanh(v_ref[rs, hs].astype(F32))
            v = v * lax.rsqrt(jnp.mean(v * v, axis=-1, keepdims=True) + NORM_EPS) * g[:, hs]
            mixed = _dot(w, v.astype(BF16)) + bias[:, hs]
            u = _gelu_tanh(u_ref[rs, hs].astype(F32))
            o_ref[rs, hs] = (u * mixed).astype(BF16)


N_MLA_IN, N_FOX_IN, N_SG_IN = 12, 10, 5


def _mixer_prep_kernel(*refs, tm):
    mla_in = refs[:N_MLA_IN]
    fox_in = refs[N_MLA_IN:N_MLA_IN + N_FOX_IN]
    sg_in = refs[N_MLA_IN + N_FOX_IN:N_MLA_IN + N_FOX_IN + N_SG_IN]
    mq, mk, mvt, fq, fk, fvt, sg_out, carry = refs[N_MLA_IN + N_FOX_IN + N_SG_IN:]

    @pl.when(pl.program_id(1) == 0)
    def _():
        carry[...] = jnp.zeros_like(carry)

    _mla_prep_kernel(*mla_in, mq, mk, mvt)
    _fox_prep_kernel(*fox_in, fq, fk, fvt, carry, tm=tm)
    _sg_kernel(*sg_in, sg_out, rows=tm)


def _mixer_prep(proj, small, cos_t, sin_t, p, mla_shift, fox_shift, sg_w_s, layer, B, S, tm):
    T = B * S
    nb = S // tm
    W = HEADS * HEAD_DIM
    HW = HEADS * QK_PAD
    rows = lambda width, col=0: pl.BlockSpec((tm, width), lambda b, i, col=col: (b * nb + i, col))
    par = lambda *shape: pl.BlockSpec((None,) + shape, lambda b, i: (layer,) + (0,) * len(shape))
    vt = pl.BlockSpec((None, None, HEADS * VT_ROWS, tm), lambda b, i: (b, i, 0, 0))
    vt_shape = jax.ShapeDtypeStruct((B, nb, HEADS * VT_ROWS, tm), BF16)
    qk_shape = jax.ShapeDtypeStruct((T, HW), BF16)
    mla_specs = [rows(MLA_Q_RANK, PROJ_MLA_CQ // MLA_Q_RANK), rows(MLA_KV_RANK, PROJ_MLA_CKV // MLA_KV_RANK),
                 rows(LANES, PROJ_MLA_KR // LANES), rows(LANES), rows(LANES),
                 par(1, MLA_Q_RANK), par(1, MLA_KV_RANK), par(MLA_Q_RANK, HW), par(MLA_KV_RANK, HW),
                 par(1, QK_PAD), par(1, QK_PAD), par(1, LANES)]
    mla_args = [proj, proj, proj, cos_t, sin_t, p["mla_qn"], p["mla_kvn"], p["w_uq_p"], p["w_ukv"],
                p["mla_gq"], p["mla_gk"], mla_shift]
    fox_specs = [rows(W, PROJ_FOX_Q // W), rows(W, PROJ_FOX_Q // W + 1), rows(W, PROJ_FOX_Q // W + 2), rows(LANES),
                 par(1, LANES), par(1, LANES), par(1, LANES), par(1, LANES)]
    fox_specs += [pl.BlockSpec((3 * LANES, HEADS * LANES), lambda b, i: (0, 0))] * 2
    fox_args = [proj, proj, proj, small, p["fox_fb"], p["fox_gq"], p["fox_gk"], fox_shift, *_fox_selectors()]
    sg_specs = [rows(W, PROJ_SG_U // W), rows(W, PROJ_SG_V // W), par(1, W), par(HEADS, SG_CHUNK, SG_CHUNK),
                par(SG_CHUNK, W)]
    sg_args = [proj, proj, p["sg_g"], sg_w_s, p["sg_bias"]]
    assert (len(mla_specs), len(fox_specs), len(sg_specs)) == (N_MLA_IN, N_FOX_IN, N_SG_IN)
    out = pl.pallas_call(
        functools.partial(_mixer_prep_kernel, tm=tm),
        out_shape=(qk_shape, qk_shape, vt_shape, qk_shape, qk_shape, vt_shape, jax.ShapeDtypeStruct((T, W), BF16)),
        grid=(B, nb),
        in_specs=mla_specs + fox_specs + sg_specs,
        out_specs=(rows(HW), rows(HW), vt, rows(HW), rows(HW), vt, rows(W)),
        scratch_shapes=[pltpu.VMEM((1, LANES), F32)],
        compiler_params=_cparams(("parallel", "arbitrary")),
        name="mixer_prep",
    )(*mla_args, *fox_args, *sg_args)
    return out[0:3], out[3:6], out[6]


def _merge_kernel(oa, ob, oc, od, g0, g1, g2, g3, x_ref, wb_ref, wo_ref, n2_ref, out_ref, h_ref):
    merged = None
    for i, (o, g) in enumerate(((oa, g0), (ob, g1), (oc, g2), (od, g3))):
        term = _sigmoid(g[...].astype(F32)) * _dot(o[...], wb_ref[i])
        merged = term if merged is None else merged + term
    x1 = x_ref[...] + _dot(merged.astype(BF16), wo_ref[...])
    out_ref[...] = x1
    ms = jnp.mean(x1 * x1, axis=-1, keepdims=True)
    h_ref[...] = (x1 * lax.rsqrt(ms + NORM_EPS) * n2_ref[...]).astype(BF16)


def _merge(branches, proj, x2, w_branch, w_out, norm2_g, layer, tm):
    T = x2.shape[0]
    ob = pl.BlockSpec((tm, BRANCH_WIDTH), lambda i: (i, 0))
    gate = lambda n: pl.BlockSpec((tm, D_MODEL), lambda i, n=n: (i, n))
    xs = pl.BlockSpec((tm, D_MODEL), lambda i: (i, 0))
    return pl.pallas_call(
        _merge_kernel,
        out_shape=(jax.ShapeDtypeStruct((T, D_MODEL), F32), jax.ShapeDtypeStruct((T, D_MODEL), BF16)),
        grid=(T // tm,),
        in_specs=[ob, ob, ob, ob, gate(0), gate(1), gate(2), gate(3), xs,
                  pl.BlockSpec((None, N_BRANCHES, BRANCH_WIDTH, D_MODEL), lambda i: (layer, 0, 0, 0)),
                  pl.BlockSpec((None, D_MODEL, D_MODEL), lambda i: (layer, 0, 0)),
                  pl.BlockSpec((None, 1, D_MODEL), lambda i: (layer, 0, 0))],
        out_specs=(xs, xs),
        compiler_params=_cparams(("parallel",)),
        name="merge_out_proj",
    )(*branches, proj, proj, proj, proj, x2, w_branch, w_out, norm2_g)


def _ffn_kernel(x_ref, h_ref, w1_ref, w2_ref, out_ref):
    def contribution():
        a = jnp.maximum(_dot(h_ref[...], w1_ref[...]), 0.0)
        return _dot((a * a).astype(BF16), w2_ref[...])

    @pl.when(pl.program_id(1) == 0)
    def _():
        out_ref[...] = x_ref[...] + contribution()

    @pl.when(pl.program_id(1) != 0)
    def _():
        out_ref[...] += contribution()


def _ffn(x2, h2, w1, w2, layer, tm, tf):
    T = x2.shape[0]
    xs = pl.BlockSpec((tm, D_MODEL), lambda i, f: (i, 0))
    return pl.pallas_call(
        _ffn_kernel,
        out_shape=jax.ShapeDtypeStruct((T, D_MODEL), F32),
        grid=(T // tm, D_FF // tf),
        in_specs=[xs, xs,
                  pl.BlockSpec((None, D_MODEL, tf), lambda i, f: (layer, 0, f)),
                  pl.BlockSpec((None, tf, D_MODEL), lambda i, f: (layer, f, 0))],
        out_specs=xs,
        compiler_params=_cparams(("parallel", "arbitrary")),
        name="relu2_mlp",
    )(x2, h2, w1, w2)


def _pad_lanes(a, width):
    return jnp.pad(a, [(0, 0)] * (a.ndim - 1) + [(0, width - a.shape[-1])])


def _swap_halves(a):
    h = a.shape[-1] // 2
    return jnp.concatenate([a[..., h:], a[..., :h]], axis=-1)


def _prep_params(norm1_g, w_in, dn_a_log, dn_dt_bias, dn_out_norm_g, mla_q_norm_g, mla_kv_norm_g, mla_w_uq,
                 mla_w_ukv, mla_qk_q_g, mla_qk_k_g, sg_v_norm_g, sg_b_s, fox_q_norm_g, fox_k_norm_g, fox_f_bias,
                 w_branch, w_out, norm2_g, w_ff1, w_ff2):
    L = w_in.shape[0]
    col = lambda a, b: w_in[:, :, a:b]
    kr = col(O_MLA_KR, O_MLA_KR + MLA_ROPE)
    w_p = jnp.concatenate([
        col(O_GATES, O_END), col(O_DN_QKV, O_DN_Z), col(O_DN_Z, O_DN_A), col(O_SG_U, O_SG_V),
        col(O_SG_V, O_FOX_QKV), col(O_FOX_QKV, O_FOX_F), col(O_MLA_CQ, O_MLA_CKV), col(O_MLA_CKV, O_MLA_KR),
        kr, _swap_halves(kr)], axis=-1).astype(BF16)
    assert w_p.shape[-1] == PROJ_WIDTH
    w_small = _pad_lanes(jnp.concatenate([col(O_DN_A, O_DN_B), col(O_DN_B, O_MLA_CQ), col(O_FOX_F, O_GATES)],
                                         axis=-1), LANES).astype(BF16)

    def row(a, width=LANES, offset=0):
        a = a.reshape(L, 1, -1)
        return jnp.pad(a, ((0, 0), (0, 0), (offset, width - offset - a.shape[-1])))

    wq = mla_w_uq.reshape(L, MLA_Q_RANK, HEADS, MLA_QK_DIM)
    wq_r = wq[..., MLA_NOPE:]
    w_uq_p = jnp.concatenate([wq, _swap_halves(wq_r)], axis=-1).reshape(L, MLA_Q_RANK, HEADS * QK_PAD).astype(BF16)

    def qk_gain(g):
        return jnp.concatenate([g, _swap_halves(g[:, MLA_NOPE:])], axis=-1).reshape(L, 1, QK_PAD)

    bias_full = jnp.repeat(jnp.swapaxes(sg_b_s, 1, 2), HEAD_DIM, axis=-1)
    return dict(
        norm1_g=norm1_g.reshape(L, 1, D_MODEL), w_p=w_p, w_small=w_small,
        a_log=row(dn_a_log), dt=row(dn_dt_bias), dn_og=dn_out_norm_g.reshape(L, 1, HEAD_DIM),
        mla_qn=mla_q_norm_g.reshape(L, 1, MLA_Q_RANK), mla_kvn=mla_kv_norm_g.reshape(L, 1, MLA_KV_RANK),
        w_uq_p=w_uq_p, w_ukv=mla_w_ukv.astype(BF16), mla_gq=qk_gain(mla_qk_q_g), mla_gk=qk_gain(mla_qk_k_g),
        sg_g=sg_v_norm_g.reshape(L, 1, HEADS * HEAD_DIM), sg_bias=bias_full,
        fox_fb=row(fox_f_bias, offset=2 * HEADS), fox_gq=fox_q_norm_g.reshape(L, 1, HEAD_DIM),
        fox_gk=fox_k_norm_g.reshape(L, 1, HEAD_DIM),
        w_branch=w_branch.astype(BF16), w_out=w_out.astype(BF16), norm2_g=norm2_g.reshape(L, 1, D_MODEL),
        w_ff1=w_ff1.astype(BF16), w_ff2=w_ff2.astype(BF16),
    )


def _tiles(S):
    pick = lambda want: min(want, S)
    return dict(in_tm=pick(2048), in_tn=1536, dn_rows=pick(1024), prep_tm=pick(512), attn_blk=pick(512),
                merge_tm=pick(512), ffn_tm=pick(1024), ffn_tf=2048, rope_tm=pick(1024))


def kernel(x, positions, norm1_g, w_in, dn_conv_w, dn_a_log, dn_dt_bias, dn_out_norm_g, mla_q_norm_g, mla_kv_norm_g, mla_w_uq, mla_w_ukv, mla_qk_q_g, mla_qk_k_g, sg_v_norm_g, sg_w_s, sg_b_s, fox_q_norm_g, fox_k_norm_g, fox_f_bias, w_branch, w_out, norm2_g, w_ff1, w_ff2):
    B, S, D = x.shape
    assert D == D_MODEL and S % LANES == 0
    T = B * S
    depth = w_in.shape[0]
    t = _tiles(S)
    p = _prep_params(norm1_g, w_in, dn_a_log, dn_dt_bias, dn_out_norm_g, mla_q_norm_g, mla_kv_norm_g, mla_w_uq,
                     mla_w_ukv, mla_qk_q_g, mla_qk_k_g, sg_v_norm_g, sg_b_s, fox_q_norm_g, fox_k_norm_g,
                     fox_f_bias, w_branch, w_out, norm2_g, w_ff1, w_ff2)
    cos_t, sin_t = _rope_tables(positions, t["rope_tm"])
    mla_shift, mla_ok = _logit_shift(mla_qk_q_g, mla_qk_k_g, MLA_QK_DIM)
    fox_shift, fox_ok = _logit_shift(fox_q_norm_g, fox_k_norm_g, HEAD_DIM)
    x2 = x.reshape(T, D)
    for l in range(depth):
        proj, small = _in_proj(x2, p["norm1_g"], p["w_p"], p["w_small"], l, t["in_tm"], t["in_tn"])
        o_a = _deltanet(proj, small, dn_conv_w, p["a_log"], p["dt"], p["dn_og"], l, B, S, t["dn_rows"])
        (qb, kb, vtb), (qd, kd, vtd), o_c = _mixer_prep(proj, small, cos_t, sin_t, p, mla_shift, fox_shift, sg_w_s,
                                                          l, B, S, t["prep_tm"])
        o_b = _attention(qb, kb, vtb, mla_ok[l], B, S, t["attn_blk"])
        o_d = _attention(qd, kd, vtd, fox_ok[l], B, S, t["attn_blk"])
        x2, h2 = _merge((o_a, o_b, o_c, o_d), proj, x2, p["w_branch"], p["w_out"], p["norm2_g"], l, t["merge_tm"])
        x2 = _ffn(x2, h2, p["w_ff1"], p["w_ff2"], l, t["ffn_tm"], t["ffn_tf"])
    return x2.reshape(B, S, D)
```

```python
import functools
import math

import jax
import numpy as np
import jax.numpy as jnp
from jax import lax
from jax.experimental import pallas as pl
from jax.experimental.pallas import tpu as pltpu

F32 = jnp.float32
BF16 = jnp.bfloat16

D_MODEL = 1024
NORM_EPS = 1e-6
N_BRANCHES = 4
BRANCH_WIDTH = 512
D_FF = 4 * D_MODEL
HEADS = 4
HEAD_DIM = 128
DN_CONV = 4
DN_CHUNK = 128
MLA_Q_RANK = 256
MLA_KV_RANK = 128
MLA_NOPE = 128
MLA_ROPE = 64
MLA_QK_DIM = MLA_NOPE + MLA_ROPE
ROPE_THETA = 10000.0
SG_CHUNK = 128
QK_PAD = 256
LANES = 128
NEG_BIG = -1e30
LOG2E = math.log2(math.e)

PROJ_GATES = 0
PROJ_DN_Q = 4096
PROJ_SG_U = 6144
PROJ_SG_V = 6656
PROJ_FOX_Q = 7168
PROJ_MLA_CQ = 8704
PROJ_MLA_CKV = 8960
PROJ_MLA_KR = 9088
PROJ_WIDTH = 9216

_SPLITS = (1536, 512, 4, 4, 256, 128, 64, 512, 512, 1536, 4, 4096)
_OFF = [0]
for _s in _SPLITS:
    _OFF.append(_OFF[-1] + _s)
(O_DN_QKV, O_DN_Z, O_DN_A, O_DN_B, O_MLA_CQ, O_MLA_CKV, O_MLA_KR, O_SG_U, O_SG_V, O_FOX_QKV, O_FOX_F,
 O_GATES, O_END) = _OFF

VMEM_LIMIT = 56 * 1024 * 1024


def _cparams(sem):
    return pltpu.CompilerParams(dimension_semantics=sem, vmem_limit_bytes=VMEM_LIMIT)


def _dot(a, b):
    return jnp.dot(a, b, preferred_element_type=F32)


def _dot_nt(a, b):
    return lax.dot_general(a, b, (((1,), (1,)), ((), ())), preferred_element_type=F32)


def _dot_f32(a, b):
    return jnp.dot(a, b, preferred_element_type=F32, precision=lax.Precision.HIGHEST)


def _sigmoid(x):
    return 0.5 * jnp.tanh(0.5 * x) + 0.5


def _softplus(x):
    return jnp.maximum(x, 0.0) + jnp.log1p(jnp.exp(-jnp.abs(x)))


def _gelu_tanh(x):
    c = math.sqrt(2.0 / math.pi)
    half = 0.5 * x
    return half + half * jnp.tanh(x * (c + (c * 0.044715) * (x * x)))


def _tri_masks(n):
    r = lax.broadcasted_iota(jnp.int32, (n, n), 0)
    c = lax.broadcasted_iota(jnp.int32, (n, n), 1)
    return r >= c, r > c


def _rope_kernel(pos_ref, freq_ref, sign_ref, cos_ref, sin_ref):
    ang = pos_ref[...] * freq_ref[...]
    lane = lax.broadcasted_iota(jnp.int32, ang.shape, 1)
    live = lane < MLA_ROPE
    cos_ref[...] = jnp.where(live, jnp.cos(ang), 0.0)
    sin_ref[...] = jnp.where(live, jnp.sin(ang) * sign_ref[...], 0.0)


def _rope_tables(positions, tm):
    T = positions.size
    half = MLA_ROPE // 2
    inv_freq = ROPE_THETA ** (-jnp.arange(0, MLA_ROPE, 2, dtype=F32) / MLA_ROPE)
    zeros = jnp.zeros((LANES - MLA_ROPE,), F32)
    freq = jnp.concatenate([inv_freq, inv_freq, zeros]).reshape(1, LANES)
    sign = jnp.concatenate([-jnp.ones((half,), F32), jnp.ones((half,), F32), zeros]).reshape(1, LANES)
    row = pl.BlockSpec((1, LANES), lambda i: (0, 0))
    tab = pl.BlockSpec((tm, LANES), lambda i: (i, 0))
    return pl.pallas_call(
        _rope_kernel,
        out_shape=(jax.ShapeDtypeStruct((T, LANES), F32), jax.ShapeDtypeStruct((T, LANES), F32)),
        grid=(T // tm,),
        in_specs=[tab, row, row],
        out_specs=(tab, tab),
        compiler_params=_cparams(("parallel",)),
        name="rope_tables",
    )(jnp.broadcast_to(positions.reshape(T, 1).astype(F32), (T, LANES)), freq, sign)


def _in_proj_kernel(x_ref, g_ref, w_ref, ws_ref, proj_ref, small_ref, h_scr):
    @pl.when(pl.program_id(1) == 0)
    def _():
        x = x_ref[...]
        ms = jnp.mean(x * x, axis=-1, keepdims=True)
        h = (x * lax.rsqrt(ms + NORM_EPS) * g_ref[...]).astype(BF16)
        h_scr[...] = h
        small_ref[...] = _dot(h, ws_ref[...])
        proj_ref[...] = _dot(h, w_ref[...]).astype(BF16)

    @pl.when(pl.program_id(1) != 0)
    def _():
        proj_ref[...] = _dot(h_scr[...], w_ref[...]).astype(BF16)


def _in_proj(x2, g, w_p, w_small, layer, tm, tn):
    T = x2.shape[0]
    return pl.pallas_call(
        _in_proj_kernel,
        out_shape=(jax.ShapeDtypeStruct((T, PROJ_WIDTH), BF16), jax.ShapeDtypeStruct((T, LANES), F32)),
        grid=(T // tm, PROJ_WIDTH // tn),
        in_specs=[
            pl.BlockSpec((tm, D_MODEL), lambda i, j: (i, 0)),
            pl.BlockSpec((None, 1, D_MODEL), lambda i, j: (layer, 0, 0)),
            pl.BlockSpec((None, D_MODEL, tn), lambda i, j: (layer, 0, j)),
            pl.BlockSpec((None, D_MODEL, LANES), lambda i, j: (layer, 0, 0)),
        ],
        out_specs=(pl.BlockSpec((tm, tn), lambda i, j: (i, j)),
                   pl.BlockSpec((tm, LANES), lambda i, j: (i, 0))),
        scratch_shapes=[pltpu.VMEM((tm, D_MODEL), BF16)],
        compiler_params=_cparams(("parallel", "arbitrary")),
        name="in_proj",
    )(x2, g, w_p, w_small)


INV_BASE = 8


def _inverse_masks(n):
    r = lax.broadcasted_iota(jnp.int32, (n, n), 0)
    c = lax.broadcasted_iota(jnp.int32, (n, n), 1)
    same = lambda s: (r // s) == (c // s)
    levels = []
    s = INV_BASE
    while s < n:
        levels.append(same(2 * s) & jnp.logical_not(same(s)))
        s *= 2
    as16 = lambda m: jnp.where(m, 1.0, 0.0).astype(BF16)
    return (r == c).astype(F32), as16(same(INV_BASE)), [as16(m) for m in levels]


def _unit_lower_inverse(nmats, masks):
    eye, base, levels = masks
    n16 = [n.astype(BF16) for n in nmats]
    ps = [n * base for n in n16]
    xs = [eye - p.astype(F32) for p in ps]
    for _ in range(int(math.log2(INV_BASE)) - 1):
        ps = [_dot(p, p).astype(BF16) for p in ps]
        xs = [x + _dot(x.astype(BF16), p) for x, p in zip(xs, ps)]
    for off in levels:
        x16 = [x.astype(BF16) for x in xs]
        ts = [_dot(n * off, xb).astype(BF16) for n, xb in zip(n16, x16)]
        xs = [x - _dot(xb, t) for x, xb, t in zip(xs, x16, ts)]
    return xs


def _deltanet_kernel(q_ref, k_ref, v_ref, z_ref, small_ref, cw_ref, alog_ref, dt_ref, og_ref, o_ref,
                     qbuf, kbuf, vbuf, qs, ks, vs, state, *, rows):
    C = DN_CHUNK
    W = HEADS * HEAD_DIM
    NC = rows // C

    @pl.when(pl.program_id(1) == 0)
    def _():
        state[...] = jnp.zeros_like(state)
        for buf in (qbuf, kbuf, vbuf):
            buf[0:8, :] = jnp.zeros((8, W), F32)

    def conv_silu(buf, x_ref, w, dst):
        buf[8:8 + rows, :] = x_ref[...].astype(F32)
        acc = buf[8:8 + rows, :] * w[DN_CONV - 1:DN_CONV, :]
        for s in range(1, DN_CONV):
            acc = acc + buf[8 - s:8 - s + rows, :] * w[DN_CONV - 1 - s:DN_CONV - s, :]
        buf[0:8, :] = buf[rows:rows + 8, :]
        dst[...] = acc * _sigmoid(acc)

    cw = cw_ref[...]
    conv_silu(qbuf, q_ref, cw[:, 0:W], qs)
    conv_silu(kbuf, k_ref, cw[:, W:2 * W], ks)
    conv_silu(vbuf, v_ref, cw[:, 2 * W:3 * W], vs)

    small = small_ref[...]
    g_all = -jnp.exp(alog_ref[...]) * _softplus(small + dt_ref[...])
    beta_all = _sigmoid(small)

    tril, strict = _tri_masks(C)
    tril_f = tril.astype(F32)
    inv_masks = _inverse_masks(C)
    og = og_ref[...]

    probs = [(c, h) for c in range(NC) for h in range(HEADS)]
    gcs = [_dot_f32(tril_f, g_all[c * C:(c + 1) * C, :]) for c in range(NC)]
    gcts = [gc.T for gc in gcs]
    rsl = lambda c: slice(c * C, (c + 1) * C)
    hsl = lambda h: slice(h * HEAD_DIM, (h + 1) * HEAD_DIM)
    qn, kn = [], []
    for c, h in probs:
        q = qs[rsl(c), hsl(h)]
        k = ks[rsl(c), hsl(h)]
        qn.append(q * (lax.rsqrt(jnp.sum(q * q, axis=-1, keepdims=True) + NORM_EPS) * (HEAD_DIM ** -0.5)))
        kn.append(k * lax.rsqrt(jnp.sum(k * k, axis=-1, keepdims=True) + NORM_EPS))
    gcol = [gcs[c][:, h:h + 1] for c, h in probs]
    bcol = [beta_all[rsl(c), HEADS + h:HEADS + h + 1] for c, h in probs]
    glast = [gcs[c][C - 1:C, h:h + 1] for c, h in probs]
    decay = [jnp.where(tril, jnp.exp(jnp.where(tril, gcol[i] - gcts[c][h:h + 1, :], 0.0)), 0.0)
             for i, (c, h) in enumerate(probs)]
    kb = [k * b for k, b in zip(kn, bcol)]
    k16 = [k.astype(BF16) for k in kn]
    qk_kk = [_dot_nt(jnp.concatenate([kbi, q], axis=0).astype(BF16), k) for kbi, q, k in zip(kb, qn, k16)]
    nmat = [jnp.where(strict, m[:C] * d, 0.0) for m, d in zip(qk_kk, decay)]
    a_qk = [(m[C:] * d).astype(BF16) for m, d in zip(qk_kk, decay)]
    eg = [jnp.exp(g) for g in gcol]
    rhs = [jnp.concatenate([vs[rsl(c), hsl(h)] * bcol[i], kb[i] * eg[i]], axis=-1).astype(BF16)
           for i, (c, h) in enumerate(probs)]
    xinv = _unit_lower_inverse(nmat, inv_masks)
    sol = [_dot(x.astype(BF16), r) for x, r in zip(xinv, rhs)]
    u = [s[:, :HEAD_DIM] for s in sol]
    wq = [jnp.concatenate([s[:, HEAD_DIM:], q * e], axis=0).astype(BF16) for s, q, e in zip(sol, qn, eg)]
    kdt = [(k * jnp.exp(gl - g)).T.astype(BF16) for k, gl, g in zip(kn, glast, gcol)]
    egl = [jnp.exp(gl) for gl in glast]

    st = [state[h] for h in range(HEADS)]
    for c in range(NC):
        idx = [c * HEADS + h for h in range(HEADS)]
        r = [_dot(wq[i], st[h].astype(BF16)) for h, i in enumerate(idx)]
        v16 = [(u[i] - r[h][:C]).astype(BF16) for h, i in enumerate(idx)]
        o = [r[h][C:] + _dot(a_qk[i], v16[h]) for h, i in enumerate(idx)]
        st = [st[h] * egl[i] + _dot(kdt[i], v16[h]) for h, i in enumerate(idx)]
        for h in range(HEADS):
            oh = o[h] * lax.rsqrt(jnp.mean(o[h] * o[h], axis=-1, keepdims=True) + NORM_EPS) * og
            z = z_ref[rsl(c), hsl(h)].astype(F32)
            o_ref[rsl(c), hsl(h)] = (oh * (z * _sigmoid(z))).astype(BF16)
    for h in range(HEADS):
        state[h] = st[h]


def _deltanet(proj, small, conv_w, a_log_row, dt_row, out_g, layer, B, S, rows):
    T = B * S
    W = HEADS * HEAD_DIM
    nb = S // rows
    blk = lambda col: pl.BlockSpec((rows, W), lambda b, i, col=col: (b * nb + i, col))
    prow = pl.BlockSpec((None, 1, LANES), lambda b, i: (layer, 0, 0))
    return pl.pallas_call(
        functools.partial(_deltanet_kernel, rows=rows),
        out_shape=jax.ShapeDtypeStruct((T, W), BF16),
        grid=(B, nb),
        in_specs=[
            blk(PROJ_DN_Q // W), blk(PROJ_DN_Q // W + 1), blk(PROJ_DN_Q // W + 2), blk(PROJ_DN_Q // W + 3),
            pl.BlockSpec((rows, LANES), lambda b, i: (b * nb + i, 0)),
            pl.BlockSpec((None, DN_CONV, 3 * W), lambda b, i: (layer, 0, 0)),
            prow, prow, prow,
        ],
        out_specs=pl.BlockSpec((rows, W), lambda b, i: (b * nb + i, 0)),
        scratch_shapes=[pltpu.VMEM((rows + 8, W), F32)] * 3 + [pltpu.VMEM((rows, W), F32)] * 3
        + [pltpu.VMEM((HEADS, HEAD_DIM, HEAD_DIM), F32)],
        compiler_params=_cparams(("parallel", "arbitrary")),
        name="deltanet",
    )(proj, proj, proj, proj, small, conv_w, a_log_row, dt_row, out_g)


VT_ONES = 16
VT_ROWS = HEAD_DIM + VT_ONES


def _store_vt(vt_out, h, v16):
    r = lax.broadcasted_iota(jnp.int32, (HEAD_DIM, HEAD_DIM), 0)
    c = lax.broadcasted_iota(jnp.int32, (HEAD_DIM, HEAD_DIM), 1)
    eye = jnp.where(r == c, 1.0, 0.0).astype(BF16)
    base = h * VT_ROWS
    vt_out[base:base + HEAD_DIM, :] = _dot_nt(eye, v16).astype(BF16)
    vt_out[base + HEAD_DIM:base + VT_ROWS, :] = jnp.ones((VT_ONES, v16.shape[0]), BF16)


def _mla_prep_kernel(cq_ref, ckv_ref, kr_ref, cos_ref, sin_ref, qn_ref, kvn_ref, wq_ref, wkv_ref,
                     gq_ref, gk_ref, off_ref, q_out, k_out, vt_out):
    cq = cq_ref[...].astype(F32)
    ckv = ckv_ref[...].astype(F32)
    cq2 = cq * cq
    cq_ms = jnp.sum(cq2[:, :LANES] + cq2[:, LANES:], axis=-1, keepdims=True) * (1.0 / MLA_Q_RANK)
    cq = cq * lax.rsqrt(cq_ms + NORM_EPS) * qn_ref[...]
    ckv = ckv * lax.rsqrt(jnp.mean(ckv * ckv, axis=-1, keepdims=True) + NORM_EPS) * kvn_ref[...]
    q_all = _dot(cq.astype(BF16), wq_ref[...])
    kv_all = _dot(ckv.astype(BF16), wkv_ref[...])
    cos = cos_ref[...]
    sin = sin_ref[...]
    gq = gq_ref[...]
    gk = gk_ref[...]
    lane = lax.broadcasted_iota(jnp.int32, cos.shape, 1)
    live = lane < MLA_ROPE
    scale = MLA_QK_DIM ** -0.5 * LOG2E
    off = off_ref[...]

    def rope(xr):
        return xr * cos + pltpu.roll(xr, MLA_ROPE, 1) * sin

    kr = kr_ref[...].astype(F32)
    kr_sq = jnp.where(live, kr * kr, 0.0)
    kr_rot = rope(kr * gk[:, MLA_NOPE:])
    for h in range(HEADS):
        base = h * QK_PAD
        qn = q_all[:, base:base + MLA_NOPE]
        qr = q_all[:, base + MLA_NOPE:base + QK_PAD]
        ss = jnp.sum(qn * qn + jnp.where(live, qr * qr, 0.0), axis=-1, keepdims=True)
        rinv = lax.rsqrt(ss * (1.0 / MLA_QK_DIM) + NORM_EPS) * scale
        q_out[:, base:base + MLA_NOPE] = (qn * rinv * gq[:, :MLA_NOPE]).astype(BF16)
        q_rot = jnp.where(lane == MLA_ROPE, 1.0, rope(qr * rinv * gq[:, MLA_NOPE:]))
        q_out[:, base + MLA_NOPE:base + QK_PAD] = q_rot.astype(BF16)

        kn = kv_all[:, base:base + MLA_NOPE]
        v = kv_all[:, base + MLA_NOPE:base + QK_PAD]
        ssk = jnp.sum(kn * kn + kr_sq, axis=-1, keepdims=True)
        rk = lax.rsqrt(ssk * (1.0 / MLA_QK_DIM) + NORM_EPS)
        k_out[:, base:base + MLA_NOPE] = (kn * rk * gk[:, :MLA_NOPE]).astype(BF16)
        k_rot = jnp.where(lane == MLA_ROPE, off, kr_rot * rk)
        k_out[:, base + MLA_NOPE:base + QK_PAD] = k_rot.astype(BF16)
        _store_vt(vt_out, h, v.astype(BF16))


def _split3(c):
    hi = c.astype(BF16).astype(F32)
    r = c - hi
    mid = r.astype(BF16).astype(F32)
    lo = (r - mid).astype(BF16).astype(F32)
    return hi, mid, lo


def _fox_selectors():
    sel_q = np.zeros((3 * LANES, HEADS * LANES), np.float32)
    sel_k = np.zeros((3 * LANES, HEADS * LANES), np.float32)
    for h in range(HEADS):
        for piece in range(3):
            sel_q[piece * LANES + 2 * HEADS + h, h * LANES + piece] = 1.0
            sel_k[piece * LANES + 2 * HEADS + h, h * LANES + 3 + piece] = -1.0
    return jnp.asarray(sel_q, BF16), jnp.asarray(sel_k, BF16)


def _fox_prep_kernel(q_ref, k_ref, v_ref, small_ref, fb_ref, gq_ref, gk_ref, off_ref, selq_ref, selk_ref,
                     q_out, k_out, vt_out, carry, *, tm):
    C = LANES
    tril, _ = _tri_masks(C)
    tril_f = tril.astype(F32)
    logf = -_softplus(-(small_ref[...] + fb_ref[...]))
    gq = gq_ref[...] * (HEAD_DIM ** -0.5 * LOG2E)
    gk = gk_ref[...]
    lane = lax.broadcasted_iota(jnp.int32, (1, LANES), 1)
    q_const = jnp.where((lane >= 3) & (lane < 7), 1.0, 0.0)
    k_const = jnp.where(lane < 3, 1.0, jnp.where(lane == 6, off_ref[...], 0.0))
    for c in range(tm // C):
        rs = slice(c * C, (c + 1) * C)
        cum = _dot_f32(tril_f, logf[rs, :]) + carry[...]
        carry[...] = cum[C - 1:C, :]
        pieces = jnp.concatenate([p.astype(BF16) for p in _split3(cum * LOG2E)], axis=-1)
        q_aug = _dot(pieces, selq_ref[...])
        k_aug = _dot(pieces, selk_ref[...])
        for h in range(HEADS):
            hs = slice(h * HEAD_DIM, (h + 1) * HEAD_DIM)
            q = q_ref[rs, hs].astype(F32)
            k = k_ref[rs, hs].astype(F32)
            q = q * lax.rsqrt(jnp.mean(q * q, axis=-1, keepdims=True) + NORM_EPS) * gq
            k = k * lax.rsqrt(jnp.mean(k * k, axis=-1, keepdims=True) + NORM_EPS) * gk
            base = h * QK_PAD
            q_out[rs, base:base + HEAD_DIM] = q.astype(BF16)
            q_out[rs, base + HEAD_DIM:base + QK_PAD] = (q_aug[:, hs] + q_const).astype(BF16)
            k_out[rs, base:base + HEAD_DIM] = k.astype(BF16)
            k_out[rs, base + HEAD_DIM:base + QK_PAD] = (k_aug[:, hs] + k_const).astype(BF16)
    for h in range(HEADS):
        _store_vt(vt_out, h, v_ref[:, h * HEAD_DIM:(h + 1) * HEAD_DIM])


def _attn_kernel(q_ref, k_ref, vt_ref, o_ref, m_scr, acc_scr, *, blk):
    i = pl.program_id(1)
    heads = range(HEADS)
    qsl = lambda h: slice(h * QK_PAD, (h + 1) * QK_PAD)
    vsl = lambda h: slice(h * VT_ROWS, (h + 1) * VT_ROWS)
    osl = lambda h: slice(h * HEAD_DIM, (h + 1) * HEAD_DIM)
    q = [q_ref[:, qsl(h)] for h in heads]
    m_scr[...] = jnp.full_like(m_scr, NEG_BIG)
    acc_scr[...] = jnp.zeros_like(acc_scr)

    def scores(j):
        rows = pl.ds(pl.multiple_of(j * blk, blk), blk)
        return [_dot_nt(k_ref[rows, qsl(h)], q[h]) for h in heads]

    def accumulate(j, st, masked):
        if masked:
            r = lax.broadcasted_iota(jnp.int32, (blk, blk), 0)
            c = lax.broadcasted_iota(jnp.int32, (blk, blk), 1)
            keep = r <= c
            st = [jnp.where(keep, s, NEG_BIG) for s in st]
        for grp in ((0, 1), (2, 3)):
            m_old = [m_scr[h] for h in grp]
            m_new = [jnp.maximum(mo, jnp.max(st[h], axis=0, keepdims=True)) for mo, h in zip(m_old, grp)]
            alpha = [jnp.exp2(mo - mn) for mo, mn in zip(m_old, m_new)]
            p = [jnp.exp2(st[h] - mn).astype(BF16) for h, mn in zip(grp, m_new)]
            pv = [_dot(vt_ref[j, vsl(h), :], pp) for h, pp in zip(grp, p)]
            for n, h in enumerate(grp):
                m_scr[h] = m_new[n]
                acc_scr[h] = alpha[n] * acc_scr[h] + pv[n]

    def pair(t, carry):
        j = 2 * t
        st_a = scores(j)
        st_b = scores(j + 1)
        accumulate(j, st_a, False)
        accumulate(j + 1, st_b, False)
        return carry

    lax.fori_loop(0, i // 2, pair, 0)

    @pl.when(i % 2 == 1)
    def _():
        accumulate(i - 1, scores(i - 1), False)

    accumulate(i, scores(i), True)
    for h in heads:
        acc = acc_scr[h]
        o_ref[:, osl(h)] = (acc[:HEAD_DIM] / acc[HEAD_DIM:HEAD_DIM + 1]).T.astype(BF16)


def _attn_shifted_kernel(q_ref, k_ref, vt_ref, o_ref, acc_scr, *, blk):
    i = pl.program_id(1)
    heads = range(HEADS)
    qsl = lambda h: slice(h * QK_PAD, (h + 1) * QK_PAD)
    vsl = lambda h: slice(h * VT_ROWS, (h + 1) * VT_ROWS)
    osl = lambda h: slice(h * HEAD_DIM, (h + 1) * HEAD_DIM)
    q = [q_ref[:, qsl(h)] for h in heads]
    acc_scr[...] = jnp.zeros_like(acc_scr)

    def block(j):
        rows = pl.ds(pl.multiple_of(j * blk, blk), blk)
        st = [_dot_nt(k_ref[rows, qsl(h)], q[h]) for h in heads]
        pv = [_dot(vt_ref[j, vsl(h), :], jnp.exp2(st[h]).astype(BF16)) for h in heads]
        for h in heads:
            acc_scr[h] += pv[h]

    def diagonal_block():
        half = blk // 2
        lo = pl.ds(pl.multiple_of(i * blk, blk), half)
        hi = pl.ds(pl.multiple_of(i * blk + half, half), half)
        keep_lo = (lax.broadcasted_iota(jnp.int32, (half, blk), 0)
                   <= lax.broadcasted_iota(jnp.int32, (half, blk), 1))
        keep_hi = (lax.broadcasted_iota(jnp.int32, (half, half), 0)
                   <= lax.broadcasted_iota(jnp.int32, (half, half), 1))
        s_lo = [jnp.where(keep_lo, _dot_nt(k_ref[lo, qsl(h)], q[h]), NEG_BIG) for h in heads]
        s_hi = [jnp.where(keep_hi, _dot_nt(k_ref[hi, qsl(h)], q_ref[half:, qsl(h)]), NEG_BIG) for h in heads]
        pv_lo = [_dot(vt_ref[i, vsl(h), :half], jnp.exp2(s_lo[h]).astype(BF16)) for h in heads]
        pv_hi = [_dot(vt_ref[i, vsl(h), half:], jnp.exp2(s_hi[h]).astype(BF16)) for h in heads]
        for h in heads:
            acc_scr[h] += pv_lo[h]
            acc_scr[h, :, half:] += pv_hi[h]

    GROUP = 3

    def group(j, n):
        rows = pl.ds(pl.multiple_of(j * blk, blk), n * blk)
        st = [_dot_nt(k_ref[rows, qsl(h)], q[h]) for h in heads]
        pv = [_dot(jnp.concatenate([vt_ref[j + m, vsl(h), :] for m in range(n)], axis=1),
                   jnp.exp2(st[h]).astype(BF16)) for h in heads]
        for h in heads:
            acc_scr[h] += pv[h]

    def full_group(t, carry):
        group(GROUP * t, GROUP)
        return carry

    lax.fori_loop(0, i // GROUP, full_group, 0)
    for rem in range(1, GROUP):
        @pl.when(i % GROUP == rem)
        def _(rem=rem):
            group(i - rem, rem)

    diagonal_block()
    for h in heads:
        acc = acc_scr[h]
        o_ref[:, osl(h)] = (acc[:HEAD_DIM] / acc[HEAD_DIM:HEAD_DIM + 1]).T.astype(BF16)


def _attention(q, k, vt, shifted, B, S, blk):
    T = B * S
    nb = S // blk
    W = HEADS * HEAD_DIM
    HW = HEADS * QK_PAD
    common = dict(
        out_shape=jax.ShapeDtypeStruct((T, W), BF16),
        grid=(B, nb),
        in_specs=[
            pl.BlockSpec((blk, HW), lambda b, i: (b * nb + i, 0)),
            pl.BlockSpec((S, HW), lambda b, i: (b, 0)),
            pl.BlockSpec((None, nb, HEADS * VT_ROWS, blk), lambda b, i: (b, 0, 0, 0)),
        ],
        out_specs=pl.BlockSpec((blk, W), lambda b, i: (b * nb + i, 0)),
        compiler_params=_cparams(("parallel", "arbitrary")),
    )
    acc = pltpu.VMEM((HEADS, VT_ROWS, blk), F32)
    online = pl.pallas_call(functools.partial(_attn_kernel, blk=blk), name="causal_attention",
                            scratch_shapes=[pltpu.VMEM((HEADS, 1, blk), F32), acc], **common)
    fast = pl.pallas_call(functools.partial(_attn_shifted_kernel, blk=blk), name="causal_attention_shifted",
                          scratch_shapes=[acc], **common)
    return lax.cond(shifted, fast, online, q, k, vt)


SHIFT_MAX = 40.0


def _logit_shift(gq, gk, dim):
    L = gq.shape[0]
    bound = (dim ** 0.5) * LOG2E * 1.02 * jnp.max(jnp.abs(gq.reshape(L, -1)), axis=-1) \
        * jnp.max(jnp.abs(gk.reshape(L, -1)), axis=-1)
    ok = bound <= SHIFT_MAX
    shift = jnp.where(ok, -bound, 0.0).astype(F32)
    return jnp.broadcast_to(shift[:, None, None], (L, 1, LANES)), ok


def _sg_kernel(u_ref, v_ref, g_ref, ws_ref, b_ref, o_ref, *, rows):
    Tn = SG_CHUNK
    tril, _ = _tri_masks(Tn)
    g = g_ref[...]
    bias = b_ref[...]
    for gi in range(HEADS):
        hs = slice(gi * HEAD_DIM, (gi + 1) * HEAD_DIM)
        w = jnp.where(tril, ws_ref[gi], 0.0).astype(BF16)
        for n in range(rows // Tn):
            rs = slice(n * Tn, (n + 1) * Tn)
            v = _gelu_tanh(v_ref[rs, hs].astype(F32))
            v = v * lax.rsqrt(jnp.mean(v * v, axis=-1, keepdims=True) + NORM_EPS) * g[:, hs]
            mixed = _dot(w, v.astype(BF16)) + bias[:, hs]
            u = _gelu_tanh(u_ref[rs, hs].astype(F32))
            o_ref[rs, hs] = (u * mixed).astype(BF16)


N_MLA_IN, N_FOX_IN, N_SG_IN = 12, 10, 5


def _mixer_prep_kernel(*refs, tm):
    mla_in = refs[:N_MLA_IN]
    fox_in = refs[N_MLA_IN:N_MLA_IN + N_FOX_IN]
    sg_in = refs[N_MLA_IN + N_FOX_IN:N_MLA_IN + N_FOX_IN + N_SG_IN]
    mq, mk, mvt, fq, fk, fvt, sg_out, carry = refs[N_MLA_IN + N_FOX_IN + N_SG_IN:]

    @pl.when(pl.program_id(1) == 0)
    def _():
        carry[...] = jnp.zeros_like(carry)

    _mla_prep_kernel(*mla_in, mq, mk, mvt)
    _fox_prep_kernel(*fox_in, fq, fk, fvt, carry, tm=tm)
    _sg_kernel(*sg_in, sg_out, rows=tm)


def _mixer_prep(proj, small, cos_t, sin_t, p, mla_shift, fox_shift, sg_w_s, layer, B, S, tm):
    T = B * S
    nb = S // tm
    W = HEADS * HEAD_DIM
    HW = HEADS * QK_PAD
    rows = lambda width, col=0: pl.BlockSpec((tm, width), lambda b, i, col=col: (b * nb + i, col))
    par = lambda *shape: pl.BlockSpec((None,) + shape, lambda b, i: (layer,) + (0,) * len(shape))
    vt = pl.BlockSpec((None, None, HEADS * VT_ROWS, tm), lambda b, i: (b, i, 0, 0))
    vt_shape = jax.ShapeDtypeStruct((B, nb, HEADS * VT_ROWS, tm), BF16)
    qk_shape = jax.ShapeDtypeStruct((T, HW), BF16)
    mla_specs = [rows(MLA_Q_RANK, PROJ_MLA_CQ // MLA_Q_RANK), rows(MLA_KV_RANK, PROJ_MLA_CKV // MLA_KV_RANK),
                 rows(LANES, PROJ_MLA_KR // LANES), rows(LANES), rows(LANES),
                 par(1, MLA_Q_RANK), par(1, MLA_KV_RANK), par(MLA_Q_RANK, HW), par(MLA_KV_RANK, HW),
                 par(1, QK_PAD), par(1, QK_PAD), par(1, LANES)]
    mla_args = [proj, proj, proj, cos_t, sin_t, p["mla_qn"], p["mla_kvn"], p["w_uq_p"], p["w_ukv"],
                p["mla_gq"], p["mla_gk"], mla_shift]
    fox_specs = [rows(W, PROJ_FOX_Q // W), rows(W, PROJ_FOX_Q // W + 1), rows(W, PROJ_FOX_Q // W + 2), rows(LANES),
                 par(1, LANES), par(1, LANES), par(1, LANES), par(1, LANES)]
    fox_specs += [pl.BlockSpec((3 * LANES, HEADS * LANES), lambda b, i: (0, 0))] * 2
    fox_args = [proj, proj, proj, small, p["fox_fb"], p["fox_gq"], p["fox_gk"], fox_shift, *_fox_selectors()]
    sg_specs = [rows(W, PROJ_SG_U // W), rows(W, PROJ_SG_V // W), par(1, W), par(HEADS, SG_CHUNK, SG_CHUNK),
                par(SG_CHUNK, W)]
    sg_args = [proj, proj, p["sg_g"], sg_w_s, p["sg_bias"]]
    assert (len(mla_specs), len(fox_specs), len(sg_specs)) == (N_MLA_IN, N_FOX_IN, N_SG_IN)
    out = pl.pallas_call(
        functools.partial(_mixer_prep_kernel, tm=tm),
        out_shape=(qk_shape, qk_shape, vt_shape, qk_shape, qk_shape, vt_shape, jax.ShapeDtypeStruct((T, W), BF16)),
        grid=(B, nb),
        in_specs=mla_specs + fox_specs + sg_specs,
        out_specs=(rows(HW), rows(HW), vt, rows(HW), rows(HW), vt, rows(W)),
        scratch_shapes=[pltpu.VMEM((1, LANES), F32)],
        compiler_params=_cparams(("parallel", "arbitrary")),
        name="mixer_prep",
    )(*mla_args, *fox_args, *sg_args)
    return out[0:3], out[3:6], out[6]


def _merge_kernel(oa, ob, oc, od, g0, g1, g2, g3, x_ref, wb_ref, wo_ref, n2_ref, out_ref, h_ref):
    merged = None
    for i, (o, g) in enumerate(((oa, g0), (ob, g1), (oc, g2), (od, g3))):
        term = _sigmoid(g[...].astype(F32)) * _dot(o[...], wb_ref[i])
        merged = term if merged is None else merged + term
    x1 = x_ref[...] + _dot(merged.astype(BF16), wo_ref[...])
    out_ref[...] = x1
    ms = jnp.mean(x1 * x1, axis=-1, keepdims=True)
    h_ref[...] = (x1 * lax.rsqrt(ms + NORM_EPS) * n2_ref[...]).astype(BF16)


def _merge(branches, proj, x2, w_branch, w_out, norm2_g, layer, tm):
    T = x2.shape[0]
    ob = pl.BlockSpec((tm, BRANCH_WIDTH), lambda i: (i, 0))
    gate = lambda n: pl.BlockSpec((tm, D_MODEL), lambda i, n=n: (i, n))
    xs = pl.BlockSpec((tm, D_MODEL), lambda i: (i, 0))
    return pl.pallas_call(
        _merge_kernel,
        out_shape=(jax.ShapeDtypeStruct((T, D_MODEL), F32), jax.ShapeDtypeStruct((T, D_MODEL), BF16)),
        grid=(T // tm,),
        in_specs=[ob, ob, ob, ob, gate(0), gate(1), gate(2), gate(3), xs,
                  pl.BlockSpec((None, N_BRANCHES, BRANCH_WIDTH, D_MODEL), lambda i: (layer, 0, 0, 0)),
                  pl.BlockSpec((None, D_MODEL, D_MODEL), lambda i: (layer, 0, 0)),
                  pl.BlockSpec((None, 1, D_MODEL), lambda i: (layer, 0, 0))],
        out_specs=(xs, xs),
        compiler_params=_cparams(("parallel",)),
        name="merge_out_proj",
    )(*branches, proj, proj, proj, proj, x2, w_branch, w_out, norm2_g)


def _ffn_kernel(x_ref, h_ref, w1_ref, w2_ref, out_ref):
    def contribution():
        a = jnp.maximum(_dot(h_ref[...], w1_ref[...]), 0.0)
        return _dot((a * a).astype(BF16), w2_ref[...])

    @pl.when(pl.program_id(1) == 0)
    def _():
        out_ref[...] = x_ref[...] + contribution()

    @pl.when(pl.program_id(1) != 0)
    def _():
        out_ref[...] += contribution()


def _ffn(x2, h2, w1, w2, layer, tm, tf):
    T = x2.shape[0]
    xs = pl.BlockSpec((tm, D_MODEL), lambda i, f: (i, 0))
    return pl.pallas_call(
        _ffn_kernel,
        out_shape=jax.ShapeDtypeStruct((T, D_MODEL), F32),
        grid=(T // tm, D_FF // tf),
        in_specs=[xs, xs,
                  pl.BlockSpec((None, D_MODEL, tf), lambda i, f: (layer, 0, f)),
                  pl.BlockSpec((None, tf, D_MODEL), lambda i, f: (layer, f, 0))],
        out_specs=xs,
        compiler_params=_cparams(("parallel", "arbitrary")),
        name="relu2_mlp",
    )(x2, h2, w1, w2)


def _pad_lanes(a, width):
    return jnp.pad(a, [(0, 0)] * (a.ndim - 1) + [(0, width - a.shape[-1])])


def _swap_halves(a):
    h = a.shape[-1] // 2
    return jnp.concatenate([a[..., h:], a[..., :h]], axis=-1)


def _prep_params(norm1_g, w_in, dn_a_log, dn_dt_bias, dn_out_norm_g, mla_q_norm_g, mla_kv_norm_g, mla_w_uq,
                 mla_w_ukv, mla_qk_q_g, mla_qk_k_g, sg_v_norm_g, sg_b_s, fox_q_norm_g, fox_k_norm_g, fox_f_bias,
                 w_branch, w_out, norm2_g, w_ff1, w_ff2):
    L = w_in.shape[0]
    col = lambda a, b: w_in[:, :, a:b]
    kr = col(O_MLA_KR, O_MLA_KR + MLA_ROPE)
    w_p = jnp.concatenate([
        col(O_GATES, O_END), col(O_DN_QKV, O_DN_Z), col(O_DN_Z, O_DN_A), col(O_SG_U, O_SG_V),
        col(O_SG_V, O_FOX_QKV), col(O_FOX_QKV, O_FOX_F), col(O_MLA_CQ, O_MLA_CKV), col(O_MLA_CKV, O_MLA_KR),
        kr, _swap_halves(kr)], axis=-1).astype(BF16)
    assert w_p.shape[-1] == PROJ_WIDTH
    w_small = _pad_lanes(jnp.concatenate([col(O_DN_A, O_DN_B), col(O_DN_B, O_MLA_CQ), col(O_FOX_F, O_GATES)],
                                         axis=-1), LANES).astype(BF16)

    def row(a, width=LANES, offset=0):
        a = a.reshape(L, 1, -1)
        return jnp.pad(a, ((0, 0), (0, 0), (offset, width - offset - a.shape[-1])))

    wq = mla_w_uq.reshape(L, MLA_Q_RANK, HEADS, MLA_QK_DIM)
    wq_r = wq[..., MLA_NOPE:]
    w_uq_p = jnp.concatenate([wq, _swap_halves(wq_r)], axis=-1).reshape(L, MLA_Q_RANK, HEADS * QK_PAD).astype(BF16)

    def qk_gain(g):
        return jnp.concatenate([g, _swap_halves(g[:, MLA_NOPE:])], axis=-1).reshape(L, 1, QK_PAD)

    bias_full = jnp.repeat(jnp.swapaxes(sg_b_s, 1, 2), HEAD_DIM, axis=-1)
    return dict(
        norm1_g=norm1_g.reshape(L, 1, D_MODEL), w_p=w_p, w_small=w_small,
        a_log=row(dn_a_log), dt=row(dn_dt_bias), dn_og=dn_out_norm_g.reshape(L, 1, HEAD_DIM),
        mla_qn=mla_q_norm_g.reshape(L, 1, MLA_Q_RANK), mla_kvn=mla_kv_norm_g.reshape(L, 1, MLA_KV_RANK),
        w_uq_p=w_uq_p, w_ukv=mla_w_ukv.astype(BF16), mla_gq=qk_gain(mla_qk_q_g), mla_gk=qk_gain(mla_qk_k_g),
        sg_g=sg_v_norm_g.reshape(L, 1, HEADS * HEAD_DIM), sg_bias=bias_full,
        fox_fb=row(fox_f_bias, offset=2 * HEADS), fox_gq=fox_q_norm_g.reshape(L, 1, HEAD_DIM),
        fox_gk=fox_k_norm_g.reshape(L, 1, HEAD_DIM),
        w_branch=w_branch.astype(BF16), w_out=w_out.astype(BF16), norm2_g=norm2_g.reshape(L, 1, D_MODEL),
        w_ff1=w_ff1.astype(BF16), w_ff2=w_ff2.astype(BF16),
    )


def _tiles(S):
    pick = lambda want: min(want, S)
    return dict(in_tm=pick(2048), in_tn=1536, dn_rows=pick(1024), prep_tm=pick(512), attn_blk=pick(512),
                merge_tm=pick(512), ffn_tm=pick(1024), ffn_tf=2048, rope_tm=pick(1024))


def kernel(x, positions, norm1_g, w_in, dn_conv_w, dn_a_log, dn_dt_bias, dn_out_norm_g, mla_q_norm_g, mla_kv_norm_g, mla_w_uq, mla_w_ukv, mla_qk_q_g, mla_qk_k_g, sg_v_norm_g, sg_w_s, sg_b_s, fox_q_norm_g, fox_k_norm_g, fox_f_bias, w_branch, w_out, norm2_g, w_ff1, w_ff2):
    B, S, D = x.shape
    assert D == D_MODEL and S % LANES == 0
    T = B * S
    depth = w_in.shape[0]
    t = _tiles(S)
    p = _prep_params(norm1_g, w_in, dn_a_log, dn_dt_bias, dn_out_norm_g, mla_q_norm_g, mla_kv_norm_g, mla_w_uq,
                     mla_w_ukv, mla_qk_q_g, mla_qk_k_g, sg_v_norm_g, sg_b_s, fox_q_norm_g, fox_k_norm_g,
                     fox_f_bias, w_branch, w_out, norm2_g, w_ff1, w_ff2)
    cos_t, sin_t = _rope_tables(positions, t["rope_tm"])
    mla_shift, mla_ok = _logit_shift(mla_qk_q_g, mla_qk_k_g, MLA_QK_DIM)
    fox_shift, fox_ok = _logit_shift(fox_q_norm_g, fox_k_norm_g, HEAD_DIM)
    x2 = x.reshape(T, D)
    for l in range(depth):
        proj, small = _in_proj(x2, p["norm1_g"], p["w_p"], p["w_small"], l, t["in_tm"], t["in_tn"])
        o_a = _deltanet(proj, small, dn_conv_w, p["a_log"], p["dt"], p["dn_og"], l, B, S, t["dn_rows"])
        (qb, kb, vtb), (qd, kd, vtd), o_c = _mixer_prep(proj, small, cos_t, sin_t, p, mla_shift, fox_shift, sg_w_s,
                                                          l, B, S, t["prep_tm"])
        o_b = _attention(qb, kb, vtb, mla_ok[l], B, S, t["attn_blk"])
        o_d = _attention(qd, kd, vtd, fox_ok[l], B, S, t["attn_blk"])
        x2, h2 = _merge((o_a, o_b, o_c, o_d), proj, x2, p["w_branch"], p["w_out"], p["norm2_g"], l, t["merge_tm"])
        x2 = _ffn(x2, h2, p["w_ff1"], p["w_ff2"], l, t["ffn_tm"], t["ffn_tf"])
    return x2.reshape(B, S, D)
```

```python
import functools
import math

import jax
import numpy as np
import jax.numpy as jnp
from jax import lax
from jax.experimental import pallas as pl
from jax.experimental.pallas import tpu as pltpu

F32 = jnp.float32
BF16 = jnp.bfloat16

D_MODEL = 1024
NORM_EPS = 1e-6
N_BRANCHES = 4
BRANCH_WIDTH = 512
D_FF = 4 * D_MODEL
HEADS = 4
HEAD_DIM = 128
DN_CONV = 4
DN_CHUNK = 128
MLA_Q_RANK = 256
MLA_KV_RANK = 128
MLA_NOPE = 128
MLA_ROPE = 64
MLA_QK_DIM = MLA_NOPE + MLA_ROPE
ROPE_THETA = 10000.0
SG_CHUNK = 128
QK_PAD = 256
LANES = 128
NEG_BIG = -1e30
LOG2E = math.log2(math.e)

PROJ_GATES = 0
PROJ_DN_Q = 4096
PROJ_SG_U = 6144
PROJ_SG_V = 6656
PROJ_FOX_Q = 7168
PROJ_MLA_CQ = 8704
PROJ_MLA_CKV = 8960
PROJ_MLA_KR = 9088
PROJ_WIDTH = 9216

_SPLITS = (1536, 512, 4, 4, 256, 128, 64, 512, 512, 1536, 4, 4096)
_OFF = [0]
for _s in _SPLITS:
    _OFF.append(_OFF[-1] + _s)
(O_DN_QKV, O_DN_Z, O_DN_A, O_DN_B, O_MLA_CQ, O_MLA_CKV, O_MLA_KR, O_SG_U, O_SG_V, O_FOX_QKV, O_FOX_F,
 O_GATES, O_END) = _OFF

VMEM_LIMIT = 56 * 1024 * 1024


def _cparams(sem):
    return pltpu.CompilerParams(dimension_semantics=sem, vmem_limit_bytes=VMEM_LIMIT)


def _dot(a, b):
    return jnp.dot(a, b, preferred_element_type=F32)


def _dot_nt(a, b):
    return lax.dot_general(a, b, (((1,), (1,)), ((), ())), preferred_element_type=F32)


def _dot_f32(a, b):
    return jnp.dot(a, b, preferred_element_type=F32, precision=lax.Precision.HIGHEST)


def _sigmoid(x):
    return 0.5 * jnp.tanh(0.5 * x) + 0.5


def _softplus(x):
    return jnp.maximum(x, 0.0) + jnp.log1p(jnp.exp(-jnp.abs(x)))


def _gelu_tanh(x):
    c = math.sqrt(2.0 / math.pi)
    half = 0.5 * x
    return half + half * jnp.tanh(x * (c + (c * 0.044715) * (x * x)))


def _tri_masks(n):
    r = lax.broadcasted_iota(jnp.int32, (n, n), 0)
    c = lax.broadcasted_iota(jnp.int32, (n, n), 1)
    return r >= c, r > c


def _rope_kernel(pos_ref, freq_ref, sign_ref, cos_ref, sin_ref):
    ang = pos_ref[...] * freq_ref[...]
    lane = lax.broadcasted_iota(jnp.int32, ang.shape, 1)
    live = lane < MLA_ROPE
    cos_ref[...] = jnp.where(live, jnp.cos(ang), 0.0)
    sin_ref[...] = jnp.where(live, jnp.sin(ang) * sign_ref[...], 0.0)


def _rope_tables(positions, tm):
    T = positions.size
    half = MLA_ROPE // 2
    inv_freq = ROPE_THETA ** (-jnp.arange(0, MLA_ROPE, 2, dtype=F32) / MLA_ROPE)
    zeros = jnp.zeros((LANES - MLA_ROPE,), F32)
    freq = jnp.concatenate([inv_freq, inv_freq, zeros]).reshape(1, LANES)
    sign = jnp.concatenate([-jnp.ones((half,), F32), jnp.ones((half,), F32), zeros]).reshape(1, LANES)
    row = pl.BlockSpec((1, LANES), lambda i: (0, 0))
    tab = pl.BlockSpec((tm, LANES), lambda i: (i, 0))
    return pl.pallas_call(
        _rope_kernel,
        out_shape=(jax.ShapeDtypeStruct((T, LANES), F32), jax.ShapeDtypeStruct((T, LANES), F32)),
        grid=(T // tm,),
        in_specs=[tab, row, row],
        out_specs=(tab, tab),
        compiler_params=_cparams(("parallel",)),
        name="rope_tables",
    )(jnp.broadcast_to(positions.reshape(T, 1).astype(F32), (T, LANES)), freq, sign)


def _in_proj_kernel(x_ref, g_ref, w_ref, ws_ref, proj_ref, small_ref, h_scr):
    @pl.when(pl.program_id(1) == 0)
    def _():
        x = x_ref[...]
        ms = jnp.mean(x * x, axis=-1, keepdims=True)
        h = (x * lax.rsqrt(ms + NORM_EPS) * g_ref[...]).astype(BF16)
        h_scr[...] = h
        small_ref[...] = _dot(h, ws_ref[...])
        proj_ref[...] = _dot(h, w_ref[...]).astype(BF16)

    @pl.when(pl.program_id(1) != 0)
    def _():
        proj_ref[...] = _dot(h_scr[...], w_ref[...]).astype(BF16)


def _in_proj(x2, g, w_p, w_small, layer, tm, tn):
    T = x2.shape[0]
    return pl.pallas_call(
        _in_proj_kernel,
        out_shape=(jax.ShapeDtypeStruct((T, PROJ_WIDTH), BF16), jax.ShapeDtypeStruct((T, LANES), F32)),
        grid=(T // tm, PROJ_WIDTH // tn),
        in_specs=[
            pl.BlockSpec((tm, D_MODEL), lambda i, j: (i, 0)),
            pl.BlockSpec((None, 1, D_MODEL), lambda i, j: (layer, 0, 0)),
            pl.BlockSpec((None, D_MODEL, tn), lambda i, j: (layer, 0, j)),
            pl.BlockSpec((None, D_MODEL, LANES), lambda i, j: (layer, 0, 0)),
        ],
        out_specs=(pl.BlockSpec((tm, tn), lambda i, j: (i, j)),
                   pl.BlockSpec((tm, LANES), lambda i, j: (i, 0))),
        scratch_shapes=[pltpu.VMEM((tm, D_MODEL), BF16)],
        compiler_params=_cparams(("parallel", "arbitrary")),
        name="in_proj",
    )(x2, g, w_p, w_small)


INV_BASE = 8


def _inverse_masks(n):
    r = lax.broadcasted_iota(jnp.int32, (n, n), 0)
    c = lax.broadcasted_iota(jnp.int32, (n, n), 1)
    same = lambda s: (r // s) == (c // s)
    levels = []
    s = INV_BASE
    while s < n:
        levels.append(same(2 * s) & jnp.logical_not(same(s)))
        s *= 2
    as16 = lambda m: jnp.where(m, 1.0, 0.0).astype(BF16)
    return (r == c).astype(F32), as16(same(INV_BASE)), [as16(m) for m in levels]


def _unit_lower_inverse(nmats, masks):
    eye, base, levels = masks
    n16 = [n.astype(BF16) for n in nmats]
    ps = [n * base for n in n16]
    xs = [eye - p.astype(F32) for p in ps]
    for _ in range(int(math.log2(INV_BASE)) - 1):
        ps = [_dot(p, p).astype(BF16) for p in ps]
        xs = [x + _dot(x.astype(BF16), p) for x, p in zip(xs, ps)]
    for off in levels:
        x16 = [x.astype(BF16) for x in xs]
        ts = [_dot(n * off, xb).astype(BF16) for n, xb in zip(n16, x16)]
        xs = [x - _dot(xb, t) for x, xb, t in zip(xs, x16, ts)]
    return xs


def _deltanet_kernel(q_ref, k_ref, v_ref, z_ref, small_ref, cw_ref, alog_ref, dt_ref, og_ref, o_ref,
                     qbuf, kbuf, vbuf, qs, ks, vs, state, *, rows):
    C = DN_CHUNK
    W = HEADS * HEAD_DIM
    NC = rows // C

    @pl.when(pl.program_id(1) == 0)
    def _():
        state[...] = jnp.zeros_like(state)
        for buf in (qbuf, kbuf, vbuf):
            buf[0:8, :] = jnp.zeros((8, W), F32)

    def conv_silu(buf, x_ref, w, dst):
        buf[8:8 + rows, :] = x_ref[...].astype(F32)
        acc = buf[8:8 + rows, :] * w[DN_CONV - 1:DN_CONV, :]
        for s in range(1, DN_CONV):
            acc = acc + buf[8 - s:8 - s + rows, :] * w[DN_CONV - 1 - s:DN_CONV - s, :]
        buf[0:8, :] = buf[rows:rows + 8, :]
        dst[...] = acc * _sigmoid(acc)

    cw = cw_ref[...]
    conv_silu(qbuf, q_ref, cw[:, 0:W], qs)
    conv_silu(kbuf, k_ref, cw[:, W:2 * W], ks)
    conv_silu(vbuf, v_ref, cw[:, 2 * W:3 * W], vs)

    small = small_ref[...]
    g_all = -jnp.exp(alog_ref[...]) * _softplus(small + dt_ref[...])
    beta_all = _sigmoid(small)

    tril, strict = _tri_masks(C)
    tril_f = tril.astype(F32)
    inv_masks = _inverse_masks(C)
    og = og_ref[...]

    probs = [(c, h) for c in range(NC) for h in range(HEADS)]
    gcs = [_dot_f32(tril_f, g_all[c * C:(c + 1) * C, :]) for c in range(NC)]
    gcts = [gc.T for gc in gcs]
    rsl = lambda c: slice(c * C, (c + 1) * C)
    hsl = lambda h: slice(h * HEAD_DIM, (h + 1) * HEAD_DIM)
    qn, kn = [], []
    for c, h in probs:
        q = qs[rsl(c), hsl(h)]
        k = ks[rsl(c), hsl(h)]
        qn.append(q * (lax.rsqrt(jnp.sum(q * q, axis=-1, keepdims=True) + NORM_EPS) * (HEAD_DIM ** -0.5)))
        kn.append(k * lax.rsqrt(jnp.sum(k * k, axis=-1, keepdims=True) + NORM_EPS))
    gcol = [gcs[c][:, h:h + 1] for c, h in probs]
    bcol = [beta_all[rsl(c), HEADS + h:HEADS + h + 1] for c, h in probs]
    glast = [gcs[c][C - 1:C, h:h + 1] for c, h in probs]
    decay = [jnp.where(tril, jnp.exp(jnp.where(tril, gcol[i] - gcts[c][h:h + 1, :], 0.0)), 0.0)
             for i, (c, h) in enumerate(probs)]
    kb = [k * b for k, b in zip(kn, bcol)]
    k16 = [k.astype(BF16) for k in kn]
    qk_kk = [_dot_nt(jnp.concatenate([kbi, q], axis=0).astype(BF16), k) for kbi, q, k in zip(kb, qn, k16)]
    nmat = [jnp.where(strict, m[:C] * d, 0.0) for m, d in zip(qk_kk, decay)]
    a_qk = [(m[C:] * d).astype(BF16) for m, d in zip(qk_kk, decay)]
    eg = [jnp.exp(g) for g in gcol]
    rhs = [jnp.concatenate([vs[rsl(c), hsl(h)] * bcol[i], kb[i] * eg[i]], axis=-1).astype(BF16)
           for i, (c, h) in enumerate(probs)]
    xinv = _unit_lower_inverse(nmat, inv_masks)
    sol = [_dot(x.astype(BF16), r) for x, r in zip(xinv, rhs)]
    u = [s[:, :HEAD_DIM] for s in sol]
    wq = [jnp.concatenate([s[:, HEAD_DIM:], q * e], axis=0).astype(BF16) for s, q, e in zip(sol, qn, eg)]
    kdt = [(k * jnp.exp(gl - g)).T.astype(BF16) for k, gl, g in zip(kn, glast, gcol)]
    egl = [jnp.exp(gl) for gl in glast]

    st = [state[h] for h in range(HEADS)]
    for c in range(NC):
        idx = [c * HEADS + h for h in range(HEADS)]
        r = [_dot(wq[i], st[h].astype(BF16)) for h, i in enumerate(idx)]
        v16 = [(u[i] - r[h][:C]).astype(BF16) for h, i in enumerate(idx)]
        o = [r[h][C:] + _dot(a_qk[i], v16[h]) for h, i in enumerate(idx)]
        st = [st[h] * egl[i] + _dot(kdt[i], v16[h]) for h, i in enumerate(idx)]
        for h in range(HEADS):
            oh = o[h] * lax.rsqrt(jnp.mean(o[h] * o[h], axis=-1, keepdims=True) + NORM_EPS) * og
            z = z_ref[rsl(c), hsl(h)].astype(F32)
            o_ref[rsl(c), hsl(h)] = (oh * (z * _sigmoid(z))).astype(BF16)
    for h in range(HEADS):
        state[h] = st[h]


def _deltanet(proj, small, conv_w, a_log_row, dt_row, out_g, layer, B, S, rows):
    T = B * S
    W = HEADS * HEAD_DIM
    nb = S // rows
    blk = lambda col: pl.BlockSpec((rows, W), lambda b, i, col=col: (b * nb + i, col))
    prow = pl.BlockSpec((None, 1, LANES), lambda b, i: (layer, 0, 0))
    return pl.pallas_call(
        functools.partial(_deltanet_kernel, rows=rows),
        out_shape=jax.ShapeDtypeStruct((T, W), BF16),
        grid=(B, nb),
        in_specs=[
            blk(PROJ_DN_Q // W), blk(PROJ_DN_Q // W + 1), blk(PROJ_DN_Q // W + 2), blk(PROJ_DN_Q // W + 3),
            pl.BlockSpec((rows, LANES), lambda b, i: (b * nb + i, 0)),
            pl.BlockSpec((None, DN_CONV, 3 * W), lambda b, i: (layer, 0, 0)),
            prow, prow, prow,
        ],
        out_specs=pl.BlockSpec((rows, W), lambda b, i: (b * nb + i, 0)),
        scratch_shapes=[pltpu.VMEM((rows + 8, W), F32)] * 3 + [pltpu.VMEM((rows, W), F32)] * 3
        + [pltpu.VMEM((HEADS, HEAD_DIM, HEAD_DIM), F32)],
        compiler_params=_cparams(("parallel", "arbitrary")),
        name="deltanet",
    )(proj, proj, proj, proj, small, conv_w, a_log_row, dt_row, out_g)


VT_ONES = 16
VT_ROWS = HEAD_DIM + VT_ONES


def _store_vt(vt_out, h, v16):
    r = lax.broadcasted_iota(jnp.int32, (HEAD_DIM, HEAD_DIM), 0)
    c = lax.broadcasted_iota(jnp.int32, (HEAD_DIM, HEAD_DIM), 1)
    eye = jnp.where(r == c, 1.0, 0.0).astype(BF16)
    base = h * VT_ROWS
    vt_out[base:base + HEAD_DIM, :] = _dot_nt(eye, v16).astype(BF16)
    vt_out[base + HEAD_DIM:base + VT_ROWS, :] = jnp.ones((VT_ONES, v16.shape[0]), BF16)


def _mla_prep_kernel(cq_ref, ckv_ref, kr_ref, cos_ref, sin_ref, qn_ref, kvn_ref, wq_ref, wkv_ref,
                     gq_ref, gk_ref, off_ref, q_out, k_out, vt_out):
    cq = cq_ref[...].astype(F32)
    ckv = ckv_ref[...].astype(F32)
    cq2 = cq * cq
    cq_ms = jnp.sum(cq2[:, :LANES] + cq2[:, LANES:], axis=-1, keepdims=True) * (1.0 / MLA_Q_RANK)
    cq = cq * lax.rsqrt(cq_ms + NORM_EPS) * qn_ref[...]
    ckv = ckv * lax.rsqrt(jnp.mean(ckv * ckv, axis=-1, keepdims=True) + NORM_EPS) * kvn_ref[...]
    q_all = _dot(cq.astype(BF16), wq_ref[...])
    kv_all = _dot(ckv.astype(BF16), wkv_ref[...])
    cos = cos_ref[...]
    sin = sin_ref[...]
    gq = gq_ref[...]
    gk = gk_ref[...]
    lane = lax.broadcasted_iota(jnp.int32, cos.shape, 1)
    live = lane < MLA_ROPE
    scale = MLA_QK_DIM ** -0.5 * LOG2E
    off = off_ref[...]

    def rope(xr):
        return xr * cos + pltpu.roll(xr, MLA_ROPE, 1) * sin

    kr = kr_ref[...].astype(F32)
    kr_sq = jnp.where(live, kr * kr, 0.0)
    kr_rot = rope(kr * gk[:, MLA_NOPE:])
    for h in range(HEADS):
        base = h * QK_PAD
        qn = q_all[:, base:base + MLA_NOPE]
        qr = q_all[:, base + MLA_NOPE:base + QK_PAD]
        ss = jnp.sum(qn * qn + jnp.where(live, qr * qr, 0.0), axis=-1, keepdims=True)
        rinv = lax.rsqrt(ss * (1.0 / MLA_QK_DIM) + NORM_EPS) * scale
        q_out[:, base:base + MLA_NOPE] = (qn * rinv * gq[:, :MLA_NOPE]).astype(BF16)
        q_rot = jnp.where(lane == MLA_ROPE, 1.0, rope(qr * rinv * gq[:, MLA_NOPE:]))
        q_out[:, base + MLA_NOPE:base + QK_PAD] = q_rot.astype(BF16)

        kn = kv_all[:, base:base + MLA_NOPE]
        v = kv_all[:, base + MLA_NOPE:base + QK_PAD]
        ssk = jnp.sum(kn * kn + kr_sq, axis=-1, keepdims=True)
        rk = lax.rsqrt(ssk * (1.0 / MLA_QK_DIM) + NORM_EPS)
        k_out[:, base:base + MLA_NOPE] = (kn * rk * gk[:, :MLA_NOPE]).astype(BF16)
        k_rot = jnp.where(lane == MLA_ROPE, off, kr_rot * rk)
        k_out[:, base + MLA_NOPE:base + QK_PAD] = k_rot.astype(BF16)
        _store_vt(vt_out, h, v.astype(BF16))


def _split3(c):
    hi = c.astype(BF16).astype(F32)
    r = c - hi
    mid = r.astype(BF16).astype(F32)
    lo = (r - mid).astype(BF16).astype(F32)
    return hi, mid, lo


def _fox_selectors():
    sel_q = np.zeros((3 * LANES, HEADS * LANES), np.float32)
    sel_k = np.zeros((3 * LANES, HEADS * LANES), np.float32)
    for h in range(HEADS):
        for piece in range(3):
            sel_q[piece * LANES + 2 * HEADS + h, h * LANES + piece] = 1.0
            sel_k[piece * LANES + 2 * HEADS + h, h * LANES + 3 + piece] = -1.0
    return jnp.asarray(sel_q, BF16), jnp.asarray(sel_k, BF16)


def _fox_prep_kernel(q_ref, k_ref, v_ref, small_ref, fb_ref, gq_ref, gk_ref, off_ref, selq_ref, selk_ref,
                     q_out, k_out, vt_out, carry, *, tm):
    C = LANES
    tril, _ = _tri_masks(C)
    tril_f = tril.astype(F32)
    logf = -_softplus(-(small_ref[...] + fb_ref[...]))
    gq = gq_ref[...] * (HEAD_DIM ** -0.5 * LOG2E)
    gk = gk_ref[...]
    lane = lax.broadcasted_iota(jnp.int32, (1, LANES), 1)
    q_const = jnp.where((lane >= 3) & (lane < 7), 1.0, 0.0)
    k_const = jnp.where(lane < 3, 1.0, jnp.where(lane == 6, off_ref[...], 0.0))
    for c in range(tm // C):
        rs = slice(c * C, (c + 1) * C)
        cum = _dot_f32(tril_f, logf[rs, :]) + carry[...]
        carry[...] = cum[C - 1:C, :]
        pieces = jnp.concatenate([p.astype(BF16) for p in _split3(cum * LOG2E)], axis=-1)
        q_aug = _dot(pieces, selq_ref[...])
        k_aug = _dot(pieces, selk_ref[...])
        for h in range(HEADS):
            hs = slice(h * HEAD_DIM, (h + 1) * HEAD_DIM)
            q = q_ref[rs, hs].astype(F32)
            k = k_ref[rs, hs].astype(F32)
            q = q * lax.rsqrt(jnp.mean(q * q, axis=-1, keepdims=True) + NORM_EPS) * gq
            k = k * lax.rsqrt(jnp.mean(k * k, axis=-1, keepdims=True) + NORM_EPS) * gk
            base = h * QK_PAD
            q_out[rs, base:base + HEAD_DIM] = q.astype(BF16)
            q_out[rs, base + HEAD_DIM:base + QK_PAD] = (q_aug[:, hs] + q_const).astype(BF16)
            k_out[rs, base:base + HEAD_DIM] = k.astype(BF16)
            k_out[rs, base + HEAD_DIM:base + QK_PAD] = (k_aug[:, hs] + k_const).astype(BF16)
    for h in range(HEADS):
        _store_vt(vt_out, h, v_ref[:, h * HEAD_DIM:(h + 1) * HEAD_DIM])


def _attn_kernel(q_ref, k_ref, vt_ref, o_ref, m_scr, acc_scr, *, blk):
    i = pl.program_id(1)
    heads = range(HEADS)
    qsl = lambda h: slice(h * QK_PAD, (h + 1) * QK_PAD)
    vsl = lambda h: slice(h * VT_ROWS, (h + 1) * VT_ROWS)
    osl = lambda h: slice(h * HEAD_DIM, (h + 1) * HEAD_DIM)
    q = [q_ref[:, qsl(h)] for h in heads]
    m_scr[...] = jnp.full_like(m_scr, NEG_BIG)
    acc_scr[...] = jnp.zeros_like(acc_scr)

    def scores(j):
        rows = pl.ds(pl.multiple_of(j * blk, blk), blk)
        return [_dot_nt(k_ref[rows, qsl(h)], q[h]) for h in heads]

    def accumulate(j, st, masked):
        if masked:
            r = lax.broadcasted_iota(jnp.int32, (blk, blk), 0)
            c = lax.broadcasted_iota(jnp.int32, (blk, blk), 1)
            keep = r <= c
            st = [jnp.where(keep, s, NEG_BIG) for s in st]
        for grp in ((0, 1), (2, 3)):
            m_old = [m_scr[h] for h in grp]
            m_new = [jnp.maximum(mo, jnp.max(st[h], axis=0, keepdims=True)) for mo, h in zip(m_old, grp)]
            alpha = [jnp.exp2(mo - mn) for mo, mn in zip(m_old, m_new)]
            p = [jnp.exp2(st[h] - mn).astype(BF16) for h, mn in zip(grp, m_new)]
            pv = [_dot(vt_ref[j, vsl(h), :], pp) for h, pp in zip(grp, p)]
            for n, h in enumerate(grp):
                m_scr[h] = m_new[n]
                acc_scr[h] = alpha[n] * acc_scr[h] + pv[n]

    def pair(t, carry):
        j = 2 * t
        st_a = scores(j)
        st_b = scores(j + 1)
        accumulate(j, st_a, False)
        accumulate(j + 1, st_b, False)
        return carry

    lax.fori_loop(0, i // 2, pair, 0)

    @pl.when(i % 2 == 1)
    def _():
        accumulate(i - 1, scores(i - 1), False)

    accumulate(i, scores(i), True)
    for h in heads:
        acc = acc_scr[h]
        o_ref[:, osl(h)] = (acc[:HEAD_DIM] / acc[HEAD_DIM:HEAD_DIM + 1]).T.astype(BF16)


def _attn_shifted_kernel(q_ref, k_ref, vt_ref, o_ref, acc_scr, *, blk):
    i = pl.program_id(1)
    heads = range(HEADS)
    qsl = lambda h: slice(h * QK_PAD, (h + 1) * QK_PAD)
    vsl = lambda h: slice(h * VT_ROWS, (h + 1) * VT_ROWS)
    osl = lambda h: slice(h * HEAD_DIM, (h + 1) * HEAD_DIM)
    q = [q_ref[:, qsl(h)] for h in heads]
    acc_scr[...] = jnp.zeros_like(acc_scr)

    def block(j):
        rows = pl.ds(pl.multiple_of(j * blk, blk), blk)
        st = [_dot_nt(k_ref[rows, qsl(h)], q[h]) for h in heads]
        pv = [_dot(vt_ref[j, vsl(h), :], jnp.exp2(st[h]).astype(BF16)) for h in heads]
        for h in heads:
            acc_scr[h] += pv[h]

    def diagonal_block():
        half = blk // 2
        lo = pl.ds(pl.multiple_of(i * blk, blk), half)
        hi = pl.ds(pl.multiple_of(i * blk + half, half), half)
        keep_lo = (lax.broadcasted_iota(jnp.int32, (half, blk), 0)
                   <= lax.broadcasted_iota(jnp.int32, (half, blk), 1))
        keep_hi = (lax.broadcasted_iota(jnp.int32, (half, half), 0)
                   <= lax.broadcasted_iota(jnp.int32, (half, half), 1))
        s_lo = [jnp.where(keep_lo, _dot_nt(k_ref[lo, qsl(h)], q[h]), NEG_BIG) for h in heads]
        s_hi = [jnp.where(keep_hi, _dot_nt(k_ref[hi, qsl(h)], q_ref[half:, qsl(h)]), NEG_BIG) for h in heads]
        pv_lo = [_dot(vt_ref[i, vsl(h), :half], jnp.exp2(s_lo[h]).astype(BF16)) for h in heads]
        pv_hi = [_dot(vt_ref[i, vsl(h), half:], jnp.exp2(s_hi[h]).astype(BF16)) for h in heads]
        for h in heads:
            acc_scr[h] += pv_lo[h]
            acc_scr[h, :, half:] += pv_hi[h]

    GROUP = 4

    def group(j, n):
        rows = pl.ds(pl.multiple_of(j * blk, blk), n * blk)
        st = [_dot_nt(k_ref[rows, qsl(h)], q[h]) for h in heads]
        pv = [_dot(jnp.concatenate([vt_ref[j + m, vsl(h), :] for m in range(n)], axis=1),
                   jnp.exp2(st[h]).astype(BF16)) for h in heads]
        for h in heads:
            acc_scr[h] += pv[h]

    def full_group(t, carry):
        group(GROUP * t, GROUP)
        return carry

    lax.fori_loop(0, i // GROUP, full_group, 0)
    for rem in range(1, GROUP):
        @pl.when(i % GROUP == rem)
        def _(rem=rem):
            group(i - rem, rem)

    diagonal_block()
    for h in heads:
        acc = acc_scr[h]
        o_ref[:, osl(h)] = (acc[:HEAD_DIM] / acc[HEAD_DIM:HEAD_DIM + 1]).T.astype(BF16)


def _attention(q, k, vt, shifted, B, S, blk):
    T = B * S
    nb = S // blk
    W = HEADS * HEAD_DIM
    HW = HEADS * QK_PAD
    common = dict(
        out_shape=jax.ShapeDtypeStruct((T, W), BF16),
        grid=(B, nb),
        in_specs=[
            pl.BlockSpec((blk, HW), lambda b, i: (b * nb + i, 0)),
            pl.BlockSpec((S, HW), lambda b, i: (b, 0)),
            pl.BlockSpec((None, nb, HEADS * VT_ROWS, blk), lambda b, i: (b, 0, 0, 0)),
        ],
        out_specs=pl.BlockSpec((blk, W), lambda b, i: (b * nb + i, 0)),
        compiler_params=_cparams(("parallel", "arbitrary")),
    )
    acc = pltpu.VMEM((HEADS, VT_ROWS, blk), F32)
    online = pl.pallas_call(functools.partial(_attn_kernel, blk=blk), name="causal_attention",
                            scratch_shapes=[pltpu.VMEM((HEADS, 1, blk), F32), acc], **common)
    fast = pl.pallas_call(functools.partial(_attn_shifted_kernel, blk=blk), name="causal_attention_shifted",
                          scratch_shapes=[acc], **common)
    return lax.cond(shifted, fast, online, q, k, vt)


SHIFT_MAX = 40.0


def _logit_shift(gq, gk, dim):
    L = gq.shape[0]
    bound = (dim ** 0.5) * LOG2E * 1.02 * jnp.max(jnp.abs(gq.reshape(L, -1)), axis=-1) \
        * jnp.max(jnp.abs(gk.reshape(L, -1)), axis=-1)
    ok = bound <= SHIFT_MAX
    shift = jnp.where(ok, -bound, 0.0).astype(F32)
    return jnp.broadcast_to(shift[:, None, None], (L, 1, LANES)), ok


def _sg_kernel(u_ref, v_ref, g_ref, ws_ref, b_ref, o_ref, *, rows):
    Tn = SG_CHUNK
    tril, _ = _tri_masks(Tn)
    g = g_ref[...]
    bias = b_ref[...]
    for gi in range(HEADS):
        hs = slice(gi * HEAD_DIM, (gi + 1) * HEAD_DIM)
        w = jnp.where(tril, ws_ref[gi], 0.0).astype(BF16)
        for n in range(rows // Tn):
            rs = slice(n * Tn, (n + 1) * Tn)
            v = _gelu_tanh(v_ref[rs, hs].astype(F32))
            v = v * lax.rsqrt(jnp.mean(v * v, axis=-1, keepdims=True) + NORM_EPS) * g[:, hs]
            mixed = _dot(w, v.astype(BF16)) + bias[:, hs]
            u = _gelu_tanh(u_ref[rs, hs].astype(F32))
            o_ref[rs, hs] = (u * mixed).astype(BF16)


N_MLA_IN, N_FOX_IN, N_SG_IN = 12, 10, 5


def _mixer_prep_kernel(*refs, tm):
    mla_in = refs[:N_MLA_IN]
    fox_in = refs[N_MLA_IN:N_MLA_IN + N_FOX_IN]
    sg_in = refs[N_MLA_IN + N_FOX_IN:N_MLA_IN + N_FOX_IN + N_SG_IN]
    mq, mk, mvt, fq, fk, fvt, sg_out, carry = refs[N_MLA_IN + N_FOX_IN + N_SG_IN:]

    @pl.when(pl.program_id(1) == 0)
    def _():
        carry[...] = jnp.zeros_like(carry)

    _mla_prep_kernel(*mla_in, mq, mk, mvt)
    _fox_prep_kernel(*fox_in, fq, fk, fvt, carry, tm=tm)
    _sg_kernel(*sg_in, sg_out, rows=tm)


def _mixer_prep(proj, small, cos_t, sin_t, p, mla_shift, fox_shift, sg_w_s, layer, B, S, tm):
    T = B * S
    nb = S // tm
    W = HEADS * HEAD_DIM
    HW = HEADS * QK_PAD
    rows = lambda width, col=0: pl.BlockSpec((tm, width), lambda b, i, col=col: (b * nb + i, col))
    par = lambda *shape: pl.BlockSpec((None,) + shape, lambda b, i: (layer,) + (0,) * len(shape))
    vt = pl.BlockSpec((None, None, HEADS * VT_ROWS, tm), lambda b, i: (b, i, 0, 0))
    vt_shape = jax.ShapeDtypeStruct((B, nb, HEADS * VT_ROWS, tm), BF16)
    qk_shape = jax.ShapeDtypeStruct((T, HW), BF16)
    mla_specs = [rows(MLA_Q_RANK, PROJ_MLA_CQ // MLA_Q_RANK), rows(MLA_KV_RANK, PROJ_MLA_CKV // MLA_KV_RANK),
                 rows(LANES, PROJ_MLA_KR // LANES), rows(LANES), rows(LANES),
                 par(1, MLA_Q_RANK), par(1, MLA_KV_RANK), par(MLA_Q_RANK, HW), par(MLA_KV_RANK, HW),
                 par(1, QK_PAD), par(1, QK_PAD), par(1, LANES)]
    mla_args = [proj, proj, proj, cos_t, sin_t, p["mla_qn"], p["mla_kvn"], p["w_uq_p"], p["w_ukv"],
                p["mla_gq"], p["mla_gk"], mla_shift]
    fox_specs = [rows(W, PROJ_FOX_Q // W), rows(W, PROJ_FOX_Q // W + 1), rows(W, PROJ_FOX_Q // W + 2), rows(LANES),
                 par(1, LANES), par(1, LANES), par(1, LANES), par(1, LANES)]
    fox_specs += [pl.BlockSpec((3 * LANES, HEADS * LANES), lambda b, i: (0, 0))] * 2
    fox_args = [proj, proj, proj, small, p["fox_fb"], p["fox_gq"], p["fox_gk"], fox_shift, *_fox_selectors()]
    sg_specs = [rows(W, PROJ_SG_U // W), rows(W, PROJ_SG_V // W), par(1, W), par(HEADS, SG_CHUNK, SG_CHUNK),
                par(SG_CHUNK, W)]
    sg_args = [proj, proj, p["sg_g"], sg_w_s, p["sg_bias"]]
    assert (len(mla_specs), len(fox_specs), len(sg_specs)) == (N_MLA_IN, N_FOX_IN, N_SG_IN)
    out = pl.pallas_call(
        functools.partial(_mixer_prep_kernel, tm=tm),
        out_shape=(qk_shape, qk_shape, vt_shape, qk_shape, qk_shape, vt_shape, jax.ShapeDtypeStruct((T, W), BF16)),
        grid=(B, nb),
        in_specs=mla_specs + fox_specs + sg_specs,
        out_specs=(rows(HW), rows(HW), vt, rows(HW), rows(HW), vt, rows(W)),
        scratch_shapes=[pltpu.VMEM((1, LANES), F32)],
        compiler_params=_cparams(("parallel", "arbitrary")),
        name="mixer_prep",
    )(*mla_args, *fox_args, *sg_args)
    return out[0:3], out[3:6], out[6]


def _merge_kernel(oa, ob, oc, od, g0, g1, g2, g3, x_ref, wb_ref, wo_ref, n2_ref, out_ref, h_ref):
    merged = None
    for i, (o, g) in enumerate(((oa, g0), (ob, g1), (oc, g2), (od, g3))):
        term = _sigmoid(g[...].astype(F32)) * _dot(o[...], wb_ref[i])
        merged = term if merged is None else merged + term
    x1 = x_ref[...] + _dot(merged.astype(BF16), wo_ref[...])
    out_ref[...] = x1
    ms = jnp.mean(x1 * x1, axis=-1, keepdims=True)
    h_ref[...] = (x1 * lax.rsqrt(ms + NORM_EPS) * n2_ref[...]).astype(BF16)


def _merge(branches, proj, x2, w_branch, w_out, norm2_g, layer, tm):
    T = x2.shape[0]
    ob = pl.BlockSpec((tm, BRANCH_WIDTH), lambda i: (i, 0))
    gate = lambda n: pl.BlockSpec((tm, D_MODEL), lambda i, n=n: (i, n))
    xs = pl.BlockSpec((tm, D_MODEL), lambda i: (i, 0))
    return pl.pallas_call(
        _merge_kernel,
        out_shape=(jax.ShapeDtypeStruct((T, D_MODEL), F32), jax.ShapeDtypeStruct((T, D_MODEL), BF16)),
        grid=(T // tm,),
        in_specs=[ob, ob, ob, ob, gate(0), gate(1), gate(2), gate(3), xs,
                  pl.BlockSpec((None, N_BRANCHES, BRANCH_WIDTH, D_MODEL), lambda i: (layer, 0, 0, 0)),
                  pl.BlockSpec((None, D_MODEL, D_MODEL), lambda i: (layer, 0, 0)),
                  pl.BlockSpec((None, 1, D_MODEL), lambda i: (layer, 0, 0))],
        out_specs=(xs, xs),
        compiler_params=_cparams(("parallel",)),
        name="merge_out_proj",
    )(*branches, proj, proj, proj, proj, x2, w_branch, w_out, norm2_g)


def _ffn_kernel(x_ref, h_ref, w1_ref, w2_ref, out_ref):
    def contribution():
        a = jnp.maximum(_dot(h_ref[...], w1_ref[...]), 0.0)
        return _dot((a * a).astype(BF16), w2_ref[...])

    @pl.when(pl.program_id(1) == 0)
    def _():
        out_ref[...] = x_ref[...] + contribution()

    @pl.when(pl.program_id(1) != 0)
    def _():
        out_ref[...] += contribution()


def _ffn(x2, h2, w1, w2, layer, tm, tf):
    T = x2.shape[0]
    xs = pl.BlockSpec((tm, D_MODEL), lambda i, f: (i, 0))
    return pl.pallas_call(
        _ffn_kernel,
        out_shape=jax.ShapeDtypeStruct((T, D_MODEL), F32),
        grid=(T // tm, D_FF // tf),
        in_specs=[xs, xs,
                  pl.BlockSpec((None, D_MODEL, tf), lambda i, f: (layer, 0, f)),
                  pl.BlockSpec((None, tf, D_MODEL), lambda i, f: (layer, f, 0))],
        out_specs=xs,
        compiler_params=_cparams(("parallel", "arbitrary")),
        name="relu2_mlp",
    )(x2, h2, w1, w2)


def _pad_lanes(a, width):
    return jnp.pad(a, [(0, 0)] * (a.ndim - 1) + [(0, width - a.shape[-1])])


def _swap_halves(a):
    h = a.shape[-1] // 2
    return jnp.concatenate([a[..., h:], a[..., :h]], axis=-1)


def _prep_params(norm1_g, w_in, dn_a_log, dn_dt_bias, dn_out_norm_g, mla_q_norm_g, mla_kv_norm_g, mla_w_uq,
                 mla_w_ukv, mla_qk_q_g, mla_qk_k_g, sg_v_norm_g, sg_b_s, fox_q_norm_g, fox_k_norm_g, fox_f_bias,
                 w_branch, w_out, norm2_g, w_ff1, w_ff2):
    L = w_in.shape[0]
    col = lambda a, b: w_in[:, :, a:b]
    kr = col(O_MLA_KR, O_MLA_KR + MLA_ROPE)
    w_p = jnp.concatenate([
        col(O_GATES, O_END), col(O_DN_QKV, O_DN_Z), col(O_DN_Z, O_DN_A), col(O_SG_U, O_SG_V),
        col(O_SG_V, O_FOX_QKV), col(O_FOX_QKV, O_FOX_F), col(O_MLA_CQ, O_MLA_CKV), col(O_MLA_CKV, O_MLA_KR),
        kr, _swap_halves(kr)], axis=-1).astype(BF16)
    assert w_p.shape[-1] == PROJ_WIDTH
    w_small = _pad_lanes(jnp.concatenate([col(O_DN_A, O_DN_B), col(O_DN_B, O_MLA_CQ), col(O_FOX_F, O_GATES)],
                                         axis=-1), LANES).astype(BF16)

    def row(a, width=LANES, offset=0):
        a = a.reshape(L, 1, -1)
        return jnp.pad(a, ((0, 0), (0, 0), (offset, width - offset - a.shape[-1])))

    wq = mla_w_uq.reshape(L, MLA_Q_RANK, HEADS, MLA_QK_DIM)
    wq_r = wq[..., MLA_NOPE:]
    w_uq_p = jnp.concatenate([wq, _swap_halves(wq_r)], axis=-1).reshape(L, MLA_Q_RANK, HEADS * QK_PAD).astype(BF16)

    def qk_gain(g):
        return jnp.concatenate([g, _swap_halves(g[:, MLA_NOPE:])], axis=-1).reshape(L, 1, QK_PAD)

    bias_full = jnp.repeat(jnp.swapaxes(sg_b_s, 1, 2), HEAD_DIM, axis=-1)
    return dict(
        norm1_g=norm1_g.reshape(L, 1, D_MODEL), w_p=w_p, w_small=w_small,
        a_log=row(dn_a_log), dt=row(dn_dt_bias), dn_og=dn_out_norm_g.reshape(L, 1, HEAD_DIM),
        mla_qn=mla_q_norm_g.reshape(L, 1, MLA_Q_RANK), mla_kvn=mla_kv_norm_g.reshape(L, 1, MLA_KV_RANK),
        w_uq_p=w_uq_p, w_ukv=mla_w_ukv.astype(BF16), mla_gq=qk_gain(mla_qk_q_g), mla_gk=qk_gain(mla_qk_k_g),
        sg_g=sg_v_norm_g.reshape(L, 1, HEADS * HEAD_DIM), sg_bias=bias_full,
        fox_fb=row(fox_f_bias, offset=2 * HEADS), fox_gq=fox_q_norm_g.reshape(L, 1, HEAD_DIM),
        fox_gk=fox_k_norm_g.reshape(L, 1, HEAD_DIM),
        w_branch=w_branch.astype(BF16), w_out=w_out.astype(BF16), norm2_g=norm2_g.reshape(L, 1, D_MODEL),
        w_ff1=w_ff1.astype(BF16), w_ff2=w_ff2.astype(BF16),
    )


def _tiles(S):
    pick = lambda want: min(want, S)
    return dict(in_tm=pick(2048), in_tn=1536, dn_rows=pick(1024), prep_tm=pick(512), attn_blk=pick(512),
                merge_tm=pick(512), ffn_tm=pick(1024), ffn_tf=2048, rope_tm=pick(1024))


def kernel(x, positions, norm1_g, w_in, dn_conv_w, dn_a_log, dn_dt_bias, dn_out_norm_g, mla_q_norm_g, mla_kv_norm_g, mla_w_uq, mla_w_ukv, mla_qk_q_g, mla_qk_k_g, sg_v_norm_g, sg_w_s, sg_b_s, fox_q_norm_g, fox_k_norm_g, fox_f_bias, w_branch, w_out, norm2_g, w_ff1, w_ff2):
    B, S, D = x.shape
    assert D == D_MODEL and S % LANES == 0
    T = B * S
    depth = w_in.shape[0]
    t = _tiles(S)
    p = _prep_params(norm1_g, w_in, dn_a_log, dn_dt_bias, dn_out_norm_g, mla_q_norm_g, mla_kv_norm_g, mla_w_uq,
                     mla_w_ukv, mla_qk_q_g, mla_qk_k_g, sg_v_norm_g, sg_b_s, fox_q_norm_g, fox_k_norm_g,
                     fox_f_bias, w_branch, w_out, norm2_g, w_ff1, w_ff2)
    cos_t, sin_t = _rope_tables(positions, t["rope_tm"])
    mla_shift, mla_ok = _logit_shift(mla_qk_q_g, mla_qk_k_g, MLA_QK_DIM)
    fox_shift, fox_ok = _logit_shift(fox_q_norm_g, fox_k_norm_g, HEAD_DIM)
    x2 = x.reshape(T, D)
    for l in range(depth):
        proj, small = _in_proj(x2, p["norm1_g"], p["w_p"], p["w_small"], l, t["in_tm"], t["in_tn"])
        o_a = _deltanet(proj, small, dn_conv_w, p["a_log"], p["dt"], p["dn_og"], l, B, S, t["dn_rows"])
        (qb, kb, vtb), (qd, kd, vtd), o_c = _mixer_prep(proj, small, cos_t, sin_t, p, mla_shift, fox_shift, sg_w_s,
                                                          l, B, S, t["prep_tm"])
        o_b = _attention(qb, kb, vtb, mla_ok[l], B, S, t["attn_blk"])
        o_d = _attention(qd, kd, vtd, fox_ok[l], B, S, t["attn_blk"])
        x2, h2 = _merge((o_a, o_b, o_c, o_d), proj, x2, p["w_branch"], p["w_out"], p["norm2_g"], l, t["merge_tm"])
        x2 = _ffn(x2, h2, p["w_ff1"], p["w_ff2"], l, t["ffn_tm"], t["ffn_tf"])
    return x2.reshape(B, S, D)
```
